```python
import math
import jax, jax.numpy as jnp
from jax import lax
import numpy as np

D_MODEL = 1024
BATCH = 8
SEQ = 4096
DEPTH = 2

N_META = 16
S5_GROUP = 16
S5_GROUPS = D_MODEL // S5_GROUP
S5_STATE = 64
N_HEADS = 16
HEAD_DIM = D_MODEL // N_HEADS
D_FF = ((8 * D_MODEL + 3 * 256 - 1) // (3 * 256)) * 256
Q_BLOCK = 128
N_A_LAYERS = DEPTH // 2
N_B_LAYERS = DEPTH - N_A_LAYERS
RMS_EPS = 1e-6
DT_MIN = 1e-3
DT_MAX = 1e-1

kernel_name = "s5_yoco_stickbreaking_hybrid"


def rmsnorm(x, gain):
    x32 = x.astype(jnp.float32)
    y = x32 * lax.rsqrt(jnp.mean(x32 * x32, axis=-1, keepdims=True) + RMS_EPS)
    return (y * gain.astype(jnp.float32)).astype(x.dtype)


def s5_mixer(u, a_re, a_im, log_dt, b_re, b_im, c_re, c_im, d_skip, w_glu):
    bsz, L, _ = u.shape
    f32 = jnp.float32
    u32 = u.astype(f32).reshape(bsz, L, S5_GROUPS, S5_GROUP)
    a_re = a_re.astype(f32)
    a_im = a_im.astype(f32)
    dt = jnp.exp(log_dt.astype(f32))[:, None]
    mag = jnp.exp(dt * a_re)
    ang = dt * a_im
    abar_re = mag * jnp.cos(ang)
    abar_im = mag * jnp.sin(ang)
    den = a_re * a_re + a_im * a_im
    coef_re = ((abar_re - 1.0) * a_re + abar_im * a_im) / den
    coef_im = (abar_im * a_re - (abar_re - 1.0) * a_im) / den
    b_re = b_re.astype(f32)
    b_im = b_im.astype(f32)
    bbar_re = coef_re[..., None] * b_re - coef_im[..., None] * b_im
    bbar_im = coef_re[..., None] * b_im + coef_im[..., None] * b_re
    bu_re = jnp.einsum('blgc,gpc->blgp', u32, bbar_re)
    bu_im = jnp.einsum('blgc,gpc->blgp', u32, bbar_im)
    a_seq_re = jnp.broadcast_to(abar_re, (1, L, S5_GROUPS, S5_STATE))
    a_seq_im = jnp.broadcast_to(abar_im, (1, L, S5_GROUPS, S5_STATE))

    def combine(e1, e2):
        a1r, a1i, b1r, b1i = e1
        a2r, a2i, b2r, b2i = e2
        return (a2r * a1r - a2i * a1i,
                a2r * a1i + a2i * a1r,
                a2r * b1r - a2i * b1i + b2r,
                a2r * b1i + a2i * b1r + b2i)

    _, _, x_re, x_im = lax.associative_scan(combine, (a_seq_re, a_seq_im, bu_re, bu_im), axis=1)
    y = (jnp.einsum('blgp,gcp->blgc', x_re, c_re.astype(f32))
         - jnp.einsum('blgp,gcp->blgc', x_im, c_im.astype(f32)))
    y = (y + d_skip.astype(f32).reshape(S5_GROUPS, S5_GROUP) * u32).reshape(bsz, L, D_MODEL)
    z = jax.nn.gelu(y)
    vg = jnp.einsum('bld,de->ble', z, w_glu.astype(f32))
    val, gate = jnp.split(vg, 2, axis=-1)
    return (val * jax.nn.sigmoid(gate)).astype(u.dtype)


def stick_breaking_attention(q, k, v):
    f32 = jnp.float32
    q = q.astype(f32)
    k = k.astype(f32)
    v = v.astype(f32)
    L = q.shape[1]
    scale = 1.0 / math.sqrt(HEAD_DIM)
    n_real_blocks = (L - N_META) // Q_BLOCK
    bounds = [(0, N_META)] + [(N_META + i * Q_BLOCK, N_META + (i + 1) * Q_BLOCK) for i in range(n_real_blocks)]
    outs = []
    for q0, q1 in bounds:
        qb = q[:, q0:q1]
        kb = k[:, :q1]
        vb = v[:, :q1]
        z = jnp.einsum('bqhd,bkhd->bhqk', qb, kb) * scale
        t_idx = jnp.arange(q0, q1)[:, None]
        s_idx = jnp.arange(q1)[None, :]
        strict = s_idx < t_idx
        log_beta = jax.nn.log_sigmoid(z)
        log_1m = jnp.where(strict, jax.nn.log_sigmoid(-z), 0.0)
        log_remain = lax.cumsum(log_1m, axis=3, reverse=True) - log_1m
        w = jnp.where(strict, jnp.exp(log_beta + log_remain), 0.0)
        outs.append(jnp.einsum('bhqk,bkhd->bqhd', w, vb))
    return jnp.concatenate(outs, axis=1)


def swiglu_ffn(h, w_in, w_out):
    gu = jnp.einsum('bld,df->blf', h, w_in)
    g, u = jnp.split(gu, 2, axis=-1)
    return jnp.einsum('blf,fd->bld', jax.nn.silu(g) * u, w_out)


def _fwd_setup_inputs(seed: int = 0) -> dict:
    key = jax.random.key(seed)
    ks = jax.random.split(key, 24)
    f32 = jnp.float32
    G, P, C = S5_GROUPS, S5_STATE, S5_GROUP
    HD = N_HEADS * HEAD_DIM
    x = jax.random.normal(ks[0], (BATCH, SEQ, D_MODEL), f32)
    meta_tokens = jax.random.normal(ks[1], (N_META, D_MODEL), f32)
    norm_mix = 1.0 + 0.02 * jax.random.normal(ks[2], (DEPTH, D_MODEL), f32)
    norm_ffn = 1.0 + 0.02 * jax.random.normal(ks[3], (DEPTH, D_MODEL), f32)
    n_idx = jnp.arange(P, dtype=f32)
    s5_a_re = -0.5 + 0.01 * jax.random.normal(ks[4], (N_A_LAYERS, G, P), f32)
    s5_a_im = math.pi * n_idx + 0.01 * jax.random.normal(ks[5], (N_A_LAYERS, G, P), f32)
    s5_log_dt = jax.random.uniform(ks[6], (N_A_LAYERS, G), f32, math.log(DT_MIN), math.log(DT_MAX))
    s5_b_re = jax.random.normal(ks[7], (N_A_LAYERS, G, P, C), f32) * (2 * C) ** -0.5
    s5_b_im = jax.random.normal(ks[8], (N_A_LAYERS, G, P, C), f32) * (2 * C) ** -0.5
    s5_c_re = jax.random.normal(ks[9], (N_A_LAYERS, G, C, P), f32) * P ** -0.5
    s5_c_im = jax.random.normal(ks[10], (N_A_LAYERS, G, C, P), f32) * P ** -0.5
    s5_d = jax.random.normal(ks[11], (N_A_LAYERS, D_MODEL), f32)
    s5_w_glu = jax.random.normal(ks[12], (N_A_LAYERS, D_MODEL, 2 * D_MODEL), f32) * D_MODEL ** -0.5
    norm_kv = 1.0 + 0.02 * jax.random.normal(ks[13], (D_MODEL,), f32)
    w_kv = jax.random.normal(ks[14], (D_MODEL, 2 * HD), f32) * D_MODEL ** -0.5
    w_q = jax.random.normal(ks[15], (N_B_LAYERS, D_MODEL, HD), f32) * D_MODEL ** -0.5
    w_o = jax.random.normal(ks[16], (N_B_LAYERS, HD, D_MODEL), f32) * HD ** -0.5
    w_ffn_in = jax.random.normal(ks[17], (DEPTH, D_MODEL, 2 * D_FF), f32) * D_MODEL ** -0.5
    w_ffn_out = jax.random.normal(ks[18], (DEPTH, D_FF, D_MODEL), f32) * D_FF ** -0.5
    norm_final = 1.0 + 0.02 * jax.random.normal(ks[19], (D_MODEL,), f32)
    return {"x": x, "meta_tokens": meta_tokens, "norm_mix": norm_mix, "norm_ffn": norm_ffn,
            "s5_a_re": s5_a_re, "s5_a_im": s5_a_im, "s5_log_dt": s5_log_dt,
            "s5_b_re": s5_b_re, "s5_b_im": s5_b_im, "s5_c_re": s5_c_re, "s5_c_im": s5_c_im,
            "s5_d": s5_d, "s5_w_glu": s5_w_glu, "norm_kv": norm_kv, "w_kv": w_kv,
            "w_q": w_q, "w_o": w_o, "w_ffn_in": w_ffn_in, "w_ffn_out": w_ffn_out,
            "norm_final": norm_final}


def _fwd_reference(x, meta_tokens, norm_mix, norm_ffn, s5_a_re, s5_a_im, s5_log_dt,
              s5_b_re, s5_b_im, s5_c_re, s5_c_im, s5_d, s5_w_glu, norm_kv, w_kv,
              w_q, w_o, w_ffn_in, w_ffn_out, norm_final):
    bsz = x.shape[0]
    meta = jnp.broadcast_to(meta_tokens.astype(x.dtype)[None], (bsz, N_META, D_MODEL))
    h = jnp.concatenate([meta, x], axis=1)
    L = h.shape[1]
    k_shared = None
    v_shared = None
    for i in range(DEPTH):
        if i < N_A_LAYERS:
            a = i
            h = h + s5_mixer(rmsnorm(h, norm_mix[i]), s5_a_re[a], s5_a_im[a], s5_log_dt[a],
                             s5_b_re[a], s5_b_im[a], s5_c_re[a], s5_c_im[a], s5_d[a], s5_w_glu[a])
        else:
            j = i - N_A_LAYERS
            q = jnp.einsum('bld,de->ble', rmsnorm(h, norm_mix[i]), w_q[j]).reshape(bsz, L, N_HEADS, HEAD_DIM)
            o = stick_breaking_attention(q, k_shared, v_shared).astype(h.dtype)
            h = h + jnp.einsum('ble,ed->bld', o.reshape(bsz, L, N_HEADS * HEAD_DIM), w_o[j])
        h = h + swiglu_ffn(rmsnorm(h, norm_ffn[i]), w_ffn_in[i], w_ffn_out[i])
        if i == N_A_LAYERS - 1:
            kv = jnp.einsum('bld,de->ble', rmsnorm(h, norm_kv), w_kv).reshape(bsz, L, 2, N_HEADS, HEAD_DIM)
            k_shared = kv[:, :, 0]
            v_shared = kv[:, :, 1]
    out = rmsnorm(h, norm_final)
    return out[:, N_META:]


import jax as _jax
import jax.numpy as _jnp

TWIN_FORMAT = 'train_step'
FWD_PARAMS = ['x', 'meta_tokens', 'norm_mix', 'norm_ffn', 's5_a_re', 's5_a_im', 's5_log_dt', 's5_b_re', 's5_b_im', 's5_c_re', 's5_c_im', 's5_d', 's5_w_glu', 'norm_kv', 'w_kv', 'w_q', 'w_o', 'w_ffn_in', 'w_ffn_out', 'norm_final']
TWIN_WEIGHTS = ['meta_tokens', 'norm_mix', 'norm_ffn', 's5_a_re', 's5_a_im', 's5_log_dt', 's5_b_re', 's5_b_im', 's5_c_re', 's5_c_im', 's5_d', 's5_w_glu', 'norm_kv', 'w_kv', 'w_q', 'w_o', 'w_ffn_in', 'w_ffn_out', 'norm_final']
TWIN_DIFF_INPUT = 'x'
TWIN_INPUTS = ['x', 'meta_tokens', 'norm_mix', 'norm_ffn', 's5_a_re', 's5_a_im', 's5_log_dt', 's5_b_re', 's5_b_im', 's5_c_re', 's5_c_im', 's5_d', 's5_w_glu', 'norm_kv', 'w_kv', 'w_q', 'w_o', 'w_ffn_in', 'w_ffn_out', 'norm_final', 'loss_target', 'm_meta_tokens', 'm_norm_mix', 'm_norm_ffn', 'm_s5_a_re', 'm_s5_a_im', 'm_s5_log_dt', 'm_s5_b_re', 'm_s5_b_im', 'm_s5_c_re', 'm_s5_c_im', 'm_s5_d', 'm_s5_w_glu', 'm_norm_kv', 'm_w_kv', 'm_w_q', 'm_w_o', 'm_w_ffn_in', 'm_w_ffn_out', 'm_norm_final', 'v_meta_tokens', 'v_norm_mix', 'v_norm_ffn', 'v_s5_a_re', 'v_s5_a_im', 'v_s5_log_dt', 'v_s5_b_re', 'v_s5_b_im', 'v_s5_c_re', 'v_s5_c_im', 'v_s5_d', 'v_s5_w_glu', 'v_norm_kv', 'v_w_kv', 'v_w_q', 'v_w_o', 'v_w_ffn_in', 'v_w_ffn_out', 'v_norm_final']
TWIN_OUTPUTS = ['loss', 'grad_x', 'grad_meta_tokens', 'grad_norm_mix', 'grad_norm_ffn', 'grad_s5_a_re', 'grad_s5_a_im', 'grad_s5_log_dt', 'grad_s5_b_re', 'grad_s5_b_im', 'grad_s5_c_re', 'grad_s5_c_im', 'grad_s5_d', 'grad_s5_w_glu', 'grad_norm_kv', 'grad_w_kv', 'grad_w_q', 'grad_w_o', 'grad_w_ffn_in', 'grad_w_ffn_out', 'grad_norm_final', 'delta_meta_tokens', 'delta_norm_mix', 'delta_norm_ffn', 'delta_s5_a_re', 'delta_s5_a_im', 'delta_s5_log_dt', 'delta_s5_b_re', 'delta_s5_b_im', 'delta_s5_c_re', 'delta_s5_c_im', 'delta_s5_d', 'delta_s5_w_glu', 'delta_norm_kv', 'delta_w_kv', 'delta_w_q', 'delta_w_o', 'delta_w_ffn_in', 'delta_w_ffn_out', 'delta_norm_final', 'new_m_meta_tokens', 'new_m_norm_mix', 'new_m_norm_ffn', 'new_m_s5_a_re', 'new_m_s5_a_im', 'new_m_s5_log_dt', 'new_m_s5_b_re', 'new_m_s5_b_im', 'new_m_s5_c_re', 'new_m_s5_c_im', 'new_m_s5_d', 'new_m_s5_w_glu', 'new_m_norm_kv', 'new_m_w_kv', 'new_m_w_q', 'new_m_w_o', 'new_m_w_ffn_in', 'new_m_w_ffn_out', 'new_m_norm_final', 'new_v_meta_tokens', 'new_v_norm_mix', 'new_v_norm_ffn', 'new_v_s5_a_re', 'new_v_s5_a_im', 'new_v_s5_log_dt', 'new_v_s5_b_re', 'new_v_s5_b_im', 'new_v_s5_c_re', 'new_v_s5_c_im', 'new_v_s5_d', 'new_v_s5_w_glu', 'new_v_norm_kv', 'new_v_w_kv', 'new_v_w_q', 'new_v_w_o', 'new_v_w_ffn_in', 'new_v_w_ffn_out', 'new_v_norm_final']
TWIN_LEAF_KINDS = {'loss': 'loss', 'grad_x': 'grad_x', 'grad_meta_tokens': 'grad_w', 'grad_norm_mix': 'grad_w', 'grad_norm_ffn': 'grad_w', 'grad_s5_a_re': 'grad_w', 'grad_s5_a_im': 'grad_w', 'grad_s5_log_dt': 'grad_w', 'grad_s5_b_re': 'grad_w', 'grad_s5_b_im': 'grad_w', 'grad_s5_c_re': 'grad_w', 'grad_s5_c_im': 'grad_w', 'grad_s5_d': 'grad_w', 'grad_s5_w_glu': 'grad_w', 'grad_norm_kv': 'grad_w', 'grad_w_kv': 'grad_w', 'grad_w_q': 'grad_w', 'grad_w_o': 'grad_w', 'grad_w_ffn_in': 'grad_w', 'grad_w_ffn_out': 'grad_w', 'grad_norm_final': 'grad_w', 'delta_meta_tokens': 'delta_w', 'delta_norm_mix': 'delta_w', 'delta_norm_ffn': 'delta_w', 'delta_s5_a_re': 'delta_w', 'delta_s5_a_im': 'delta_w', 'delta_s5_log_dt': 'delta_w', 'delta_s5_b_re': 'delta_w', 'delta_s5_b_im': 'delta_w', 'delta_s5_c_re': 'delta_w', 'delta_s5_c_im': 'delta_w', 'delta_s5_d': 'delta_w', 'delta_s5_w_glu': 'delta_w', 'delta_norm_kv': 'delta_w', 'delta_w_kv': 'delta_w', 'delta_w_q': 'delta_w', 'delta_w_o': 'delta_w', 'delta_w_ffn_in': 'delta_w', 'delta_w_ffn_out': 'delta_w', 'delta_norm_final': 'delta_w', 'new_m_meta_tokens': 'new_m', 'new_m_norm_mix': 'new_m', 'new_m_norm_ffn': 'new_m', 'new_m_s5_a_re': 'new_m', 'new_m_s5_a_im': 'new_m', 'new_m_s5_log_dt': 'new_m', 'new_m_s5_b_re': 'new_m', 'new_m_s5_b_im': 'new_m', 'new_m_s5_c_re': 'new_m', 'new_m_s5_c_im': 'new_m', 'new_m_s5_d': 'new_m', 'new_m_s5_w_glu': 'new_m', 'new_m_norm_kv': 'new_m', 'new_m_w_kv': 'new_m', 'new_m_w_q': 'new_m', 'new_m_w_o': 'new_m', 'new_m_w_ffn_in': 'new_m', 'new_m_w_ffn_out': 'new_m', 'new_m_norm_final': 'new_m', 'new_v_meta_tokens': 'new_v', 'new_v_norm_mix': 'new_v', 'new_v_norm_ffn': 'new_v', 'new_v_s5_a_re': 'new_v', 'new_v_s5_a_im': 'new_v', 'new_v_s5_log_dt': 'new_v', 'new_v_s5_b_re': 'new_v', 'new_v_s5_b_im': 'new_v', 'new_v_s5_c_re': 'new_v', 'new_v_s5_c_im': 'new_v', 'new_v_s5_d': 'new_v', 'new_v_s5_w_glu': 'new_v', 'new_v_norm_kv': 'new_v', 'new_v_w_kv': 'new_v', 'new_v_w_q': 'new_v', 'new_v_w_o': 'new_v', 'new_v_w_ffn_in': 'new_v', 'new_v_w_ffn_out': 'new_v', 'new_v_norm_final': 'new_v'}


def _forward(args):
    return _fwd_reference(*[args[k] for k in FWD_PARAMS])


def _output_shape():
    out = _jax.eval_shape(lambda: _forward(_fwd_setup_inputs(0)))
    return out.shape, out.dtype

N_MICROBATCH = 1
ADAM_LR = 0.001
ADAM_B1 = 0.9
ADAM_B2 = 0.999
ADAM_EPS = 1e-08
ADAM_WD = 0.01
ADAM_STEP = 10
PER_EXAMPLE_BATCH_AXIS = {'x': 0, 'loss_target': 0}
SHARED_INPUTS = []
_WEIGHT_DTYPES = {'meta_tokens': _jnp.float32, 'norm_mix': _jnp.float32, 'norm_ffn': _jnp.float32, 's5_a_re': _jnp.float32, 's5_a_im': _jnp.float32, 's5_log_dt': _jnp.float32, 's5_b_re': _jnp.float32, 's5_b_im': _jnp.float32, 's5_c_re': _jnp.float32, 's5_c_im': _jnp.float32, 's5_d': _jnp.float32, 's5_w_glu': _jnp.float32, 'norm_kv': _jnp.float32, 'w_kv': _jnp.float32, 'w_q': _jnp.float32, 'w_o': _jnp.float32, 'w_ffn_in': _jnp.float32, 'w_ffn_out': _jnp.float32, 'norm_final': _jnp.float32}
MOMENT_SCALE = {'meta_tokens': 2.111437e-03, 'norm_mix': 7.566879e-02, 'norm_ffn': 1.239801e-01, 's5_a_re': 5.267875e-03, 's5_a_im': 5.488730e-03, 's5_log_dt': 4.016008e+00, 's5_b_re': 3.632105e-03, 's5_b_im': 3.595035e-03, 's5_c_re': 5.296095e-03, 's5_c_im': 5.091471e-03, 's5_d': 7.933504e-02, 's5_w_glu': 5.520817e-02, 'norm_kv': 1.015353e-01, 'w_kv': 7.059651e-02, 'w_q': 4.137331e-02, 'w_o': 9.156745e-02, 'w_ffn_in': 5.227377e-02, 'w_ffn_out': 8.516257e-02, 'norm_final': 3.195592e+01}


def _to_microbatches(a, axis):
    t = _jnp.moveaxis(a, axis, 0)
    t = t.reshape((N_MICROBATCH, t.shape[0] // N_MICROBATCH) + t.shape[1:])
    return _jnp.moveaxis(t, 1, axis + 1)


def setup_inputs(seed: int = 0) -> dict:
    inp = _fwd_setup_inputs(seed)
    key = _jax.random.fold_in(_jax.random.key(seed), 7919)
    shape, _ = _output_shape()
    out = dict(inp)
    out["loss_target"] = _jax.random.normal(_jax.random.fold_in(key, 0), shape, _jnp.float32)
    for i, name in enumerate(TWIN_WEIGHTS):
        w = inp[name].astype(_jnp.float32)
        if MOMENT_SCALE is None:
            s = _jnp.sqrt(_jnp.mean(_jnp.square(w)) + 1e-30)
        else:
            s = MOMENT_SCALE[name]
        km, kv = _jax.random.split(_jax.random.fold_in(key, i + 1))
        out[name] = w
        out["m_" + name] = s * _jax.random.normal(km, w.shape, _jnp.float32)
        out["v_" + name] = (s * s) * _jax.random.uniform(kv, w.shape, _jnp.float32, 0.5, 1.5)
    if N_MICROBATCH > 1:
        for name, axis in PER_EXAMPLE_BATCH_AXIS.items():
            out[name] = _to_microbatches(out[name], axis)
    return {'x': out['x'], 'meta_tokens': out['meta_tokens'], 'norm_mix': out['norm_mix'], 'norm_ffn': out['norm_ffn'], 's5_a_re': out['s5_a_re'], 's5_a_im': out['s5_a_im'], 's5_log_dt': out['s5_log_dt'], 's5_b_re': out['s5_b_re'], 's5_b_im': out['s5_b_im'], 's5_c_re': out['s5_c_re'], 's5_c_im': out['s5_c_im'], 's5_d': out['s5_d'], 's5_w_glu': out['s5_w_glu'], 'norm_kv': out['norm_kv'], 'w_kv': out['w_kv'], 'w_q': out['w_q'], 'w_o': out['w_o'], 'w_ffn_in': out['w_ffn_in'], 'w_ffn_out': out['w_ffn_out'], 'norm_final': out['norm_final'], 'loss_target': out['loss_target'], 'm_meta_tokens': out['m_meta_tokens'], 'm_norm_mix': out['m_norm_mix'], 'm_norm_ffn': out['m_norm_ffn'], 'm_s5_a_re': out['m_s5_a_re'], 'm_s5_a_im': out['m_s5_a_im'], 'm_s5_log_dt': out['m_s5_log_dt'], 'm_s5_b_re': out['m_s5_b_re'], 'm_s5_b_im': out['m_s5_b_im'], 'm_s5_c_re': out['m_s5_c_re'], 'm_s5_c_im': out['m_s5_c_im'], 'm_s5_d': out['m_s5_d'], 'm_s5_w_glu': out['m_s5_w_glu'], 'm_norm_kv': out['m_norm_kv'], 'm_w_kv': out['m_w_kv'], 'm_w_q': out['m_w_q'], 'm_w_o': out['m_w_o'], 'm_w_ffn_in': out['m_w_ffn_in'], 'm_w_ffn_out': out['m_w_ffn_out'], 'm_norm_final': out['m_norm_final'], 'v_meta_tokens': out['v_meta_tokens'], 'v_norm_mix': out['v_norm_mix'], 'v_norm_ffn': out['v_norm_ffn'], 'v_s5_a_re': out['v_s5_a_re'], 'v_s5_a_im': out['v_s5_a_im'], 'v_s5_log_dt': out['v_s5_log_dt'], 'v_s5_b_re': out['v_s5_b_re'], 'v_s5_b_im': out['v_s5_b_im'], 'v_s5_c_re': out['v_s5_c_re'], 'v_s5_c_im': out['v_s5_c_im'], 'v_s5_d': out['v_s5_d'], 'v_s5_w_glu': out['v_s5_w_glu'], 'v_norm_kv': out['v_norm_kv'], 'v_w_kv': out['v_w_kv'], 'v_w_q': out['v_w_q'], 'v_w_o': out['v_w_o'], 'v_w_ffn_in': out['v_w_ffn_in'], 'v_w_ffn_out': out['v_w_ffn_out'], 'v_norm_final': out['v_norm_final']}


def _loss(weights, diff, rest, loss_target):
    with _jax.named_scope("forward"):
        args = {**rest, TWIN_DIFF_INPUT: diff, **{k: w.astype(_WEIGHT_DTYPES[k]) for k, w in weights.items()}}
        y = _forward(args)
    with _jax.named_scope("loss_head"):
        err = _jnp.square(y.astype(_jnp.float32) - loss_target)
        return 0.5 * _jnp.sum(_jnp.mean(err, axis=-1)) if err.ndim else 0.5 * err


def _adamw(w, g, m, v):
    m = ADAM_B1 * m + (1.0 - ADAM_B1) * g
    v = ADAM_B2 * v + (1.0 - ADAM_B2) * _jnp.square(g)
    m_hat = m / (1.0 - ADAM_B1 ** ADAM_STEP)
    v_hat = v / (1.0 - ADAM_B2 ** ADAM_STEP)
    delta = -ADAM_LR * (m_hat / (_jnp.sqrt(v_hat) + ADAM_EPS) + ADAM_WD * w)
    return delta, m, v


def reference(x, meta_tokens, norm_mix, norm_ffn, s5_a_re, s5_a_im, s5_log_dt, s5_b_re, s5_b_im, s5_c_re, s5_c_im, s5_d, s5_w_glu, norm_kv, w_kv, w_q, w_o, w_ffn_in, w_ffn_out, norm_final, loss_target, m_meta_tokens, m_norm_mix, m_norm_ffn, m_s5_a_re, m_s5_a_im, m_s5_log_dt, m_s5_b_re, m_s5_b_im, m_s5_c_re, m_s5_c_im, m_s5_d, m_s5_w_glu, m_norm_kv, m_w_kv, m_w_q, m_w_o, m_w_ffn_in, m_w_ffn_out, m_norm_final, v_meta_tokens, v_norm_mix, v_norm_ffn, v_s5_a_re, v_s5_a_im, v_s5_log_dt, v_s5_b_re, v_s5_b_im, v_s5_c_re, v_s5_c_im, v_s5_d, v_s5_w_glu, v_norm_kv, v_w_kv, v_w_q, v_w_o, v_w_ffn_in, v_w_ffn_out, v_norm_final):
    given = dict(x=x, meta_tokens=meta_tokens, norm_mix=norm_mix, norm_ffn=norm_ffn, s5_a_re=s5_a_re, s5_a_im=s5_a_im, s5_log_dt=s5_log_dt, s5_b_re=s5_b_re, s5_b_im=s5_b_im, s5_c_re=s5_c_re, s5_c_im=s5_c_im, s5_d=s5_d, s5_w_glu=s5_w_glu, norm_kv=norm_kv, w_kv=w_kv, w_q=w_q, w_o=w_o, w_ffn_in=w_ffn_in, w_ffn_out=w_ffn_out, norm_final=norm_final, loss_target=loss_target, m_meta_tokens=m_meta_tokens, m_norm_mix=m_norm_mix, m_norm_ffn=m_norm_ffn, m_s5_a_re=m_s5_a_re, m_s5_a_im=m_s5_a_im, m_s5_log_dt=m_s5_log_dt, m_s5_b_re=m_s5_b_re, m_s5_b_im=m_s5_b_im, m_s5_c_re=m_s5_c_re, m_s5_c_im=m_s5_c_im, m_s5_d=m_s5_d, m_s5_w_glu=m_s5_w_glu, m_norm_kv=m_norm_kv, m_w_kv=m_w_kv, m_w_q=m_w_q, m_w_o=m_w_o, m_w_ffn_in=m_w_ffn_in, m_w_ffn_out=m_w_ffn_out, m_norm_final=m_norm_final, v_meta_tokens=v_meta_tokens, v_norm_mix=v_norm_mix, v_norm_ffn=v_norm_ffn, v_s5_a_re=v_s5_a_re, v_s5_a_im=v_s5_a_im, v_s5_log_dt=v_s5_log_dt, v_s5_b_re=v_s5_b_re, v_s5_b_im=v_s5_b_im, v_s5_c_re=v_s5_c_re, v_s5_c_im=v_s5_c_im, v_s5_d=v_s5_d, v_s5_w_glu=v_s5_w_glu, v_norm_kv=v_norm_kv, v_w_kv=v_w_kv, v_w_q=v_w_q, v_w_o=v_w_o, v_w_ffn_in=v_w_ffn_in, v_w_ffn_out=v_w_ffn_out, v_norm_final=v_norm_final)
    weights = {n: given[n] for n in TWIN_WEIGHTS}
    shared = {n: given[n] for n in SHARED_INPUTS}
    per_example = {n: given[n] for n in ['x']}
    grad_fn = _jax.value_and_grad(_loss, argnums=(0, 1))

    def one_microbatch(ex, loss_target):
        ex = dict(ex)
        diff = ex.pop(TWIN_DIFF_INPUT)
        return grad_fn(weights, diff, {**shared, **ex}, loss_target)

    if N_MICROBATCH == 1:
        loss, (grad_w, grad_x) = one_microbatch(per_example, given["loss_target"])
    else:
        def body(carry, xs):
            loss_sum, grad_sum = carry
            l_k, (gw_k, gx_k) = one_microbatch(xs[0], xs[1])
            with _jax.named_scope("update"):
                return (loss_sum + l_k, _jax.tree.map(_jnp.add, grad_sum, gw_k)), gx_k

        init = (_jnp.zeros((), _jnp.float32), _jax.tree.map(_jnp.zeros_like, weights))
        (loss, grad_w), grad_x = _jax.lax.scan(body, init, (per_example, given["loss_target"]))
    with _jax.named_scope("update"):
        delta_w, new_m, new_v = {}, {}, {}
        for n in TWIN_WEIGHTS:
            delta_w[n], new_m[n], new_v[n] = _adamw(weights[n], grad_w[n], given["m_" + n], given["v_" + n])
    return (loss, grad_x, *[grad_w[n] for n in TWIN_WEIGHTS], *[delta_w[n] for n in TWIN_WEIGHTS],
            *[new_m[n] for n in TWIN_WEIGHTS], *[new_v[n] for n in TWIN_WEIGHTS])
```

```python
import functools
import math

import jax
import jax.numpy as jnp
from jax import lax
from jax.experimental import pallas as pl
from jax.experimental.pallas import tpu as pltpu

F32 = jnp.float32
BF16 = jnp.bfloat16

N_META = 16
X_START = 128
META_START = X_START - N_META
S5_GROUP = 16
S5_STATE = 64
HEAD_DIM = 64
KEY_BLOCK = 128
STATE_TILE = 512
CH_TILE = 128
RMS_EPS = 1e-6
ADAM_LR, ADAM_B1, ADAM_B2, ADAM_EPS, ADAM_WD, ADAM_STEP = 0.001, 0.9, 0.999, 1e-08, 0.01, 10
VMEM_LIMIT_BYTES = 48 * 1024 * 1024
MESH = pl.DeviceIdType.MESH


def _cparams(*sem):
    return pltpu.CompilerParams(dimension_semantics=sem, vmem_limit_bytes=VMEM_LIMIT_BYTES)


def _row_tile(rows, cap):
    for unit in (128, 8):
        best = 0
        for t in range(unit, min(rows, cap) + 1, unit):
            if rows % t == 0:
                best = t
        if best:
            return best
    return rows


def _col_tile(cols, cap):
    best = 0
    for t in range(128, min(cols, cap) + 1, 128):
        if cols % t == 0:
            best = t
    return best if best else cols


def _gelu(x):
    k = math.sqrt(2.0 / math.pi)
    return 0.5 * x * (1.0 + jnp.tanh(k * (x + 0.044715 * x * x * x)))


def _gelu_grad(x):
    k = math.sqrt(2.0 / math.pi)
    t = jnp.tanh(k * (x + 0.044715 * x * x * x))
    return 0.5 * (1.0 + t) + 0.5 * x * (1.0 - t * t) * k * (1.0 + 3.0 * 0.044715 * x * x)


def _sigmoid(x):
    return 1.0 / (1.0 + jnp.exp(-x))


def _rmsnorm_fwd(name, x, gains, out_dtypes):
    T, D = x.shape
    tm = _row_tile(T, 512)
    n = len(gains)

    def body(x_ref, *refs):
        xv = x_ref[...]
        xh = xv * lax.rsqrt(jnp.mean(xv * xv, axis=-1, keepdims=True) + RMS_EPS)
        for g_ref, o_ref in zip(refs[:n], refs[n:]):
            o_ref[...] = (xh * g_ref[...]).astype(o_ref.dtype)

    row = pl.BlockSpec((tm, D), lambda i: (i, 0))
    vec = pl.BlockSpec((1, D), lambda i: (0, 0))
    return pl.pallas_call(
        body, name=name, grid=(T // tm,),
        in_specs=[row] + [vec] * n, out_specs=[row] * n,
        out_shape=[jax.ShapeDtypeStruct((T, D), dt) for dt in out_dtypes],
        compiler_params=_cparams("parallel"),
    )(x, *gains)


def _rmsnorm_bwd(name, x, pairs, dres):
    T, D = x.shape
    tm = _row_tile(T, 256)
    n = len(pairs)

    def body(x_ref, dres_ref, *refs):
        g_refs, dy_refs = refs[:n], refs[n:2 * n]
        dx_ref, dg_refs = refs[2 * n], refs[2 * n + 1:]
        i = pl.program_id(0)
        xv = x_ref[...]
        r = lax.rsqrt(jnp.mean(xv * xv, axis=-1, keepdims=True) + RMS_EPS)
        xh = xv * r
        dxh = jnp.zeros_like(xv)
        for g_ref, dy_ref, dg_ref in zip(g_refs, dy_refs, dg_refs):
            dy = dy_ref[...].astype(F32)
            part = jnp.sum(dy * xh, axis=0, keepdims=True)

            @pl.when(i == 0)
            def _():
                dg_ref[...] = part

            @pl.when(i > 0)
            def _():
                dg_ref[...] += part

            dxh = dxh + dy * g_ref[...]
        dx = r * (dxh - xh * jnp.mean(dxh * xh, axis=-1, keepdims=True))
        dx_ref[...] = dres_ref[...] + dx

    row = pl.BlockSpec((tm, D), lambda i: (i, 0))
    vec = pl.BlockSpec((1, D), lambda i: (0, 0))
    outs = pl.pallas_call(
        body, name=name, grid=(T // tm,),
        in_specs=[row, row] + [vec] * n + [row] * n,
        out_specs=[row] + [vec] * n,
        out_shape=[jax.ShapeDtypeStruct((T, D), F32)] + [jax.ShapeDtypeStruct((1, D), F32)] * n,
        compiler_params=_cparams("arbitrary"),
    )(x, dres, *[g for g, _ in pairs], *[dy for _, dy in pairs])
    return outs[0], outs[1:]


def _mm_nn(name, a, w, k_blk=0, res=None, out_dtype=F32):
    M, K = a.shape
    S, _, Ns = w.shape
    tm = _row_tile(M, 512)
    tn = _col_tile(Ns, 1408)
    nt = Ns // tn

    def body(a_ref, w_ref, *refs):
        o_ref = refs[-1]
        acc = jnp.dot(a_ref[...].astype(BF16), w_ref[...], preferred_element_type=F32)
        if res is not None:
            acc = acc + refs[0][...]
        o_ref[...] = acc.astype(o_ref.dtype)

    in_specs = [pl.BlockSpec((tm, K), lambda j, i: (i, 0)),
                pl.BlockSpec((None, K, tn), lambda j, i: (j // nt, k_blk, j % nt))]
    args = [a, w]
    if res is not None:
        in_specs.append(pl.BlockSpec((tm, tn), lambda j, i: (i, j)))
        args.append(res)
    return pl.pallas_call(
        body, name=name, grid=(S * nt, M // tm),
        in_specs=in_specs, out_specs=pl.BlockSpec((tm, tn), lambda j, i: (i, j)),
        out_shape=jax.ShapeDtypeStruct((M, S * Ns), out_dtype),
        compiler_params=_cparams("parallel", "parallel"),
    )(*args)


def _mm_nt_k(name, dy, w, K, k_blk=0, out_dtype=F32):
    M = dy.shape[0]
    S, _, Ns = w.shape
    tm = _row_tile(M, 512)
    tn = _col_tile(Ns, 1408)
    nt = Ns // tn
    steps = S * nt

    def body(dy_ref, w_ref, o_ref, acc_ref):
        j = pl.program_id(1)
        part = lax.dot_general(dy_ref[...].astype(BF16), w_ref[...], (((1,), (1,)), ((), ())),
                               preferred_element_type=F32)

        @pl.when(j == 0)
        def _():
            acc_ref[...] = part

        @pl.when(j > 0)
        def _():
            acc_ref[...] += part

        @pl.when(j == steps - 1)
        def _():
            o_ref[...] = acc_ref[...].astype(o_ref.dtype)

    return pl.pallas_call(
        body, name=name, grid=(M // tm, steps),
        in_specs=[pl.BlockSpec((tm, tn), lambda i, j: (i, j)),
                  pl.BlockSpec((None, K, tn), lambda i, j: (j // nt, k_blk, j % nt))],
        out_specs=pl.BlockSpec((tm, K), lambda i, j: (i, 0)),
        out_shape=jax.ShapeDtypeStruct((M, K), out_dtype),
        scratch_shapes=[pltpu.VMEM((tm, K), F32)],
        compiler_params=_cparams("parallel", "arbitrary"),
    )(dy, w)


def _mm_tn(name, a, dy, S, out_dtype=BF16):
    T, K = a.shape
    Ns = dy.shape[1] // S
    tn = _col_tile(Ns, max(128, (6 * 1024 * 1024) // (4 * K) // 128 * 128))
    nt = Ns // tn
    tt = _row_tile(T, 512)
    steps = T // tt

    def body(a_ref, dy_ref, o_ref, acc_ref):
        t = pl.program_id(1)
        part = lax.dot_general(a_ref[...].astype(BF16), dy_ref[...].astype(BF16), (((0,), (0,)), ((), ())),
                               preferred_element_type=F32)

        @pl.when(t == 0)
        def _():
            acc_ref[...] = part

        @pl.when(t > 0)
        def _():
            acc_ref[...] += part

        @pl.when(t == steps - 1)
        def _():
            o_ref[...] = acc_ref[...].astype(o_ref.dtype)

    return pl.pallas_call(
        body, name=name, grid=(S * nt, steps),
        in_specs=[pl.BlockSpec((tt, K), lambda j, t: (t, 0)),
                  pl.BlockSpec((tt, tn), lambda j, t: (t, j))],
        out_specs=pl.BlockSpec((None, K, tn), lambda j, t: (j // nt, 0, j % nt)),
        out_shape=jax.ShapeDtypeStruct((S, K, Ns), out_dtype),
        scratch_shapes=[pltpu.VMEM((K, tn), F32)],
        compiler_params=_cparams("parallel", "arbitrary"),
    )(a, dy)


def _gated_tile(T, width):
    return _row_tile(T, max(8, (2 * 1024 * 1024) // (4 * width) // 8 * 8))


def _glu_fwd(name, vg, h):
    T, D = h.shape
    tm = _gated_tile(T, 2 * D)

    def body(vg_ref, h_ref, o_ref):
        o_ref[...] = h_ref[...] + vg_ref[:, :D] * _sigmoid(vg_ref[:, D:])

    return pl.pallas_call(
        body, name=name, grid=(T // tm,),
        in_specs=[pl.BlockSpec((tm, 2 * D), lambda i: (i, 0)), pl.BlockSpec((tm, D), lambda i: (i, 0))],
        out_specs=pl.BlockSpec((tm, D), lambda i: (i, 0)),
        out_shape=jax.ShapeDtypeStruct((T, D), F32),
        compiler_params=_cparams("parallel"),
    )(vg, h)


def _glu_bwd(name, vg, dout):
    T, D = dout.shape
    tm = _gated_tile(T, 2 * D)

    def body(vg_ref, d_ref, o_ref):
        s = _sigmoid(vg_ref[:, D:])
        d = d_ref[...]
        o_ref[:, :D] = (d * s).astype(o_ref.dtype)
        o_ref[:, D:] = (d * vg_ref[:, :D] * s * (1.0 - s)).astype(o_ref.dtype)

    return pl.pallas_call(
        body, name=name, grid=(T // tm,),
        in_specs=[pl.BlockSpec((tm, 2 * D), lambda i: (i, 0)), pl.BlockSpec((tm, D), lambda i: (i, 0))],
        out_specs=pl.BlockSpec((tm, 2 * D), lambda i: (i, 0)),
        out_shape=jax.ShapeDtypeStruct((T, 2 * D), BF16),
        compiler_params=_cparams("parallel"),
    )(vg, dout)


def _swiglu_fwd(name, gu):
    T, W = gu.shape
    H = W // 2
    tm = _gated_tile(T, W)

    def body(gu_ref, o_ref):
        g = gu_ref[:, :H]
        o_ref[...] = (g * _sigmoid(g) * gu_ref[:, H:]).astype(o_ref.dtype)

    return pl.pallas_call(
        body, name=name, grid=(T // tm,),
        in_specs=[pl.BlockSpec((tm, W), lambda i: (i, 0))],
        out_specs=pl.BlockSpec((tm, H), lambda i: (i, 0)),
        out_shape=jax.ShapeDtypeStruct((T, H), BF16),
        compiler_params=_cparams("parallel"),
    )(gu)


def _swiglu_bwd(name, gu, dmid):
    T, W = gu.shape
    H = W // 2
    tm = _gated_tile(T, W)

    def body(gu_ref, d_ref, o_ref):
        g = gu_ref[:, :H]
        u = gu_ref[:, H:]
        d = d_ref[...]
        s = _sigmoid(g)
        o_ref[:, :H] = (d * u * s * (1.0 + g * (1.0 - s))).astype(o_ref.dtype)
        o_ref[:, H:] = (d * g * s).astype(o_ref.dtype)

    return pl.pallas_call(
        body, name=name, grid=(T // tm,),
        in_specs=[pl.BlockSpec((tm, W), lambda i: (i, 0)), pl.BlockSpec((tm, H), lambda i: (i, 0))],
        out_specs=pl.BlockSpec((tm, W), lambda i: (i, 0)),
        out_shape=jax.ShapeDtypeStruct((T, W), BF16),
        compiler_params=_cparams("parallel"),
    )(gu, dmid)


def _final_loss(name, h, gain, target):
    T, D = h.shape
    tm = X_START
    lead = X_START // tm

    def body(h_ref, g_ref, t_ref, loss_ref, dh_ref, dg_ref):
        i = pl.program_id(0)

        @pl.when(i == 0)
        def _():
            loss_ref[...] = jnp.zeros_like(loss_ref)
            dg_ref[...] = jnp.zeros_like(dg_ref)
            dh_ref[...] = jnp.zeros_like(dh_ref)

        @pl.when(i >= lead)
        def _():
            xv = h_ref[...]
            r = lax.rsqrt(jnp.mean(xv * xv, axis=-1, keepdims=True) + RMS_EPS)
            xh = xv * r
            g = g_ref[...]
            diff = xh * g - t_ref[...]
            loss_ref[...] += 0.5 * jnp.sum(jnp.mean(diff * diff, axis=-1, keepdims=True), axis=0, keepdims=True)
            dout = diff * (1.0 / D)
            dg_ref[...] += jnp.sum(dout * xh, axis=0, keepdims=True)
            dxh = dout * g
            dh_ref[...] = r * (dxh - xh * jnp.mean(dxh * xh, axis=-1, keepdims=True))

    return pl.pallas_call(
        body, name=name, grid=(T // tm,),
        in_specs=[pl.BlockSpec((tm, D), lambda i: (i, 0)), pl.BlockSpec((1, D), lambda i: (0, 0)),
                  pl.BlockSpec((tm, D), lambda i: (jnp.maximum(i - lead, 0), 0))],
        out_specs=[pl.BlockSpec((1, 128), lambda i: (0, 0)), pl.BlockSpec((tm, D), lambda i: (i, 0)),
                   pl.BlockSpec((1, D), lambda i: (0, 0))],
        out_shape=[jax.ShapeDtypeStruct((1, 128), F32), jax.ShapeDtypeStruct((T, D), F32),
                   jax.ShapeDtypeStruct((1, D), F32)],
        compiler_params=_cparams("arbitrary"),
    )(h, gain, target)


def _s5_discretise(a_re, a_im, log_dt, bt_re, bt_im):
    dt = jnp.exp(log_dt)
    mag = jnp.exp(dt * a_re)
    ang = dt * a_im
    abar_re = mag * jnp.cos(ang)
    abar_im = mag * jnp.sin(ang)
    den = a_re * a_re + a_im * a_im
    coef_re = ((abar_re - 1.0) * a_re + abar_im * a_im) / den
    coef_im = (abar_im * a_re - (abar_re - 1.0) * a_im) / den
    bbar_re = coef_re * bt_re - coef_im * bt_im
    bbar_im = coef_re * bt_im + coef_im * bt_re
    return abar_re, abar_im, bbar_re, bbar_im


def _s5_prep_fwd(name, a_re, a_im, log_dt, bt_re, bt_im):
    G, _, P = a_re.shape
    C = bt_re.shape[1]

    def body(ar, ai, ld, br, bi, o_ar, o_ai, o_br, o_bi):
        outs = _s5_discretise(ar[...], ai[...], ld[...], br[...], bi[...])
        for o, v in zip((o_ar, o_ai, o_br, o_bi), outs):
            o[...] = v

    return pl.pallas_call(
        body, name=name,
        out_shape=[jax.ShapeDtypeStruct((G, 1, P), F32)] * 2 + [jax.ShapeDtypeStruct((G, C, P), F32)] * 2,
    )(a_re, a_im, log_dt, bt_re, bt_im)


def _s5_prep_bwd(name, a_re, a_im, log_dt, bt_re, bt_im, d_ar, d_ai, d_br, d_bi):
    G, _, P = a_re.shape
    C = bt_re.shape[1]

    def body(ar, ai, ld, br, bi, gar, gai, gbr, gbi, o_ar, o_ai, o_ld, o_br, o_bi):
        _, vjp = jax.vjp(_s5_discretise, ar[...], ai[...], ld[...], br[...], bi[...])
        grads = vjp((gar[...], gai[...], gbr[...], gbi[...]))
        for o, v in zip((o_ar, o_ai, o_ld, o_br, o_bi), grads):
            o[...] = v

    return pl.pallas_call(
        body, name=name,
        out_shape=[jax.ShapeDtypeStruct((G, 1, P), F32)] * 2 + [jax.ShapeDtypeStruct((G, 1, 1), F32)]
        + [jax.ShapeDtypeStruct((G, C, P), F32)] * 2,
    )(a_re, a_im, log_dt, bt_re, bt_im, d_ar, d_ai, d_br, d_bi)


def _cmul(ar, ai, br, bi):
    return ar * br - ai * bi, ar * bi + ai * br


def _power_table(a_re, a_im):
    rows_re, rows_im = [a_re], [a_im]
    for _ in range(7):
        r, m = _cmul(rows_re[-1], rows_im[-1], a_re, a_im)
        rows_re.append(r)
        rows_im.append(m)
    row = lax.broadcasted_iota(jnp.int32, (8, a_re.shape[1]), 0)
    t_re = jnp.zeros((8, a_re.shape[1]), F32)
    t_im = jnp.zeros((8, a_re.shape[1]), F32)
    for k in range(8):
        t_re = jnp.where(row == k, rows_re[k], t_re)
        t_im = jnp.where(row == k, rows_im[k], t_im)
    return t_re, t_im, rows_re, rows_im


def _s5_fwd(name, u, b_re, b_im, ct_re, ct_im, abar_re, abar_im, d_skip):
    T, D = u.shape
    n_st = D // CH_TILE
    W = STATE_TILE
    tc = _row_tile(T, 512)
    n_tiles = tc // 8

    def body(u_ref, bre_ref, bim_ref, cre_ref, cim_ref, ar_ref, ai_ref, d_ref,
             y_ref, z_ref, xr_ref, xi_ref, carry_re, carry_im, pw_re, pw_im, sh_re, sh_im):
        c = pl.program_id(1)

        @pl.when(c == 0)
        def _():
            t_re, t_im, rows_re, rows_im = _power_table(ar_ref[...], ai_ref[...])
            pw_re[...] = t_re
            pw_im[...] = t_im
            for n, k in enumerate((0, 1, 3)):
                sh_re[n] = jnp.broadcast_to(rows_re[k], (8, W))
                sh_im[n] = jnp.broadcast_to(rows_im[k], (8, W))
            carry_re[...] = jnp.zeros_like(carry_re)
            carry_im[...] = jnp.zeros_like(carry_im)

        ub = u_ref[...].astype(BF16)
        xr_ref[...] = jnp.dot(ub, bre_ref[...], preferred_element_type=F32)
        xi_ref[...] = jnp.dot(ub, bim_ref[...], preferred_element_type=F32)
        row = lax.broadcasted_iota(jnp.int32, (8, W), 0)

        def tile(i, carry):
            c_re, c_im = carry
            rows = pl.ds(pl.multiple_of(i * 8, 8), 8)
            r = xr_ref[rows, :]
            m = xi_ref[rows, :]
            for n, d in enumerate((1, 2, 4)):
                pr, pm = _cmul(sh_re[n], sh_im[n], pltpu.roll(r, d, 0), pltpu.roll(m, d, 0))
                r = r + jnp.where(row >= d, pr, 0.0)
                m = m + jnp.where(row >= d, pm, 0.0)
            pr, pm = _cmul(pw_re[...], pw_im[...], c_re, c_im)
            r = r + pr
            m = m + pm
            xr_ref[rows, :] = r
            xi_ref[rows, :] = m
            return jnp.broadcast_to(r[7:8, :], (8, W)), jnp.broadcast_to(m[7:8, :], (8, W))

        c_re, c_im = lax.fori_loop(0, n_tiles, tile, (carry_re[...], carry_im[...]))
        carry_re[...] = c_re
        carry_im[...] = c_im
        y = (jnp.dot(xr_ref[...].astype(BF16), cre_ref[...], preferred_element_type=F32)
             - jnp.dot(xi_ref[...].astype(BF16), cim_ref[...], preferred_element_type=F32)
             + d_ref[...] * u_ref[...])
        y_ref[...] = y
        z_ref[...] = _gelu(y).astype(z_ref.dtype)

    ch = pl.BlockSpec((tc, CH_TILE), lambda s, c: (c, s))
    st = pl.BlockSpec((tc, W), lambda s, c: (c, s))
    return pl.pallas_call(
        body, name=name, grid=(n_st, T // tc),
        in_specs=[ch,
                  pl.BlockSpec((None, CH_TILE, W), lambda s, c: (s, 0, 0)),
                  pl.BlockSpec((None, CH_TILE, W), lambda s, c: (s, 0, 0)),
                  pl.BlockSpec((None, W, CH_TILE), lambda s, c: (s, 0, 0)),
                  pl.BlockSpec((None, W, CH_TILE), lambda s, c: (s, 0, 0)),
                  pl.BlockSpec((1, W), lambda s, c: (0, s)),
                  pl.BlockSpec((1, W), lambda s, c: (0, s)),
                  pl.BlockSpec((1, CH_TILE), lambda s, c: (0, s))],
        out_specs=[ch, ch, st, st],
        out_shape=[jax.ShapeDtypeStruct((T, D), F32), jax.ShapeDtypeStruct((T, D), BF16),
                   jax.ShapeDtypeStruct((T, 4 * D), F32), jax.ShapeDtypeStruct((T, 4 * D), F32)],
        scratch_shapes=[pltpu.VMEM((8, W), F32), pltpu.VMEM((8, W), F32),
                        pltpu.VMEM((8, W), F32), pltpu.VMEM((8, W), F32),
                        pltpu.VMEM((3, 8, W), F32), pltpu.VMEM((3, 8, W), F32)],
        compiler_params=_cparams("parallel", "arbitrary"),
    )(u, b_re, b_im, ct_re, ct_im, abar_re, abar_im, d_skip)


def _s5_bwd(name, dz, y, u, x_re, x_im, c_re, c_im, bt_re, bt_im, abar_re, abar_im, d_skip):
    T, D = u.shape
    n_st = D // CH_TILE
    W = STATE_TILE
    tc = _row_tile(T, 512)
    n_chunks = T // tc
    n_tiles = tc // 8
    tiles_per_chunk = tc // 8

    def body(dz_ref, y_ref, u_ref, xr_ref, xi_ref, xpr_ref, xpi_ref, cre_ref, cim_ref, btr_ref, bti_ref,
             ar_ref, ai_ref, d_ref,
             du_ref, dd_ref, dbr_ref, dbi_ref, dcr_ref, dci_ref, dar_ref, dai_ref,
             lam_re, lam_im, xe_re, xe_im, carry_re, carry_im, pw_re, pw_im, sh_re, sh_im, acc_ar, acc_ai):
        k = pl.program_id(1)
        first_chunk = k == n_chunks - 1

        @pl.when(k == 0)
        def _():
            t_re, t_im, rows_re, rows_im = _power_table(ar_ref[...], -ai_ref[...])
            row = lax.broadcasted_iota(jnp.int32, (8, W), 0)
            r_re = jnp.zeros((8, W), F32)
            r_im = jnp.zeros((8, W), F32)
            for j in range(8):
                r_re = jnp.where(row == j, rows_re[7 - j], r_re)
                r_im = jnp.where(row == j, rows_im[7 - j], r_im)
            pw_re[...] = r_re
            pw_im[...] = r_im
            for n, j in enumerate((0, 1, 3)):
                sh_re[n] = jnp.broadcast_to(rows_re[j], (8, W))
                sh_im[n] = jnp.broadcast_to(rows_im[j], (8, W))
            carry_re[...] = jnp.zeros_like(carry_re)
            carry_im[...] = jnp.zeros_like(carry_im)
            acc_ar[...] = jnp.zeros_like(acc_ar)
            acc_ai[...] = jnp.zeros_like(acc_ai)
            dd_ref[...] = jnp.zeros_like(dd_ref)
            dbr_ref[...] = jnp.zeros_like(dbr_ref)
            dbi_ref[...] = jnp.zeros_like(dbi_ref)
            dcr_ref[...] = jnp.zeros_like(dcr_ref)
            dci_ref[...] = jnp.zeros_like(dci_ref)

        uv = u_ref[...]
        dy = dz_ref[...] * _gelu_grad(y_ref[...])
        dyb = dy.astype(BF16)
        lam_re[...] = jnp.dot(dyb, cre_ref[...], preferred_element_type=F32)
        lam_im[...] = -jnp.dot(dyb, cim_ref[...], preferred_element_type=F32)
        keep = jnp.where(first_chunk, 0.0, 1.0)
        xe_re[pl.ds(0, 8), :] = xpr_ref[...] * keep
        xe_im[pl.ds(0, 8), :] = xpi_ref[...] * keep
        xe_re[pl.ds(8, tc), :] = xr_ref[...]
        xe_im[pl.ds(8, tc), :] = xi_ref[...]
        row = lax.broadcasted_iota(jnp.int32, (8, W), 0)

        def tile(n, carry):
            c_re, c_im, s_ar, s_ai = carry
            i = n_tiles - 1 - n
            rows = pl.ds(pl.multiple_of(i * 8, 8), 8)
            r = lam_re[rows, :]
            m = lam_im[rows, :]
            for q, d in enumerate((1, 2, 4)):
                pr, pm = _cmul(sh_re[q], sh_im[q], pltpu.roll(r, 8 - d, 0), pltpu.roll(m, 8 - d, 0))
                r = r + jnp.where(row < 8 - d, pr, 0.0)
                m = m + jnp.where(row < 8 - d, pm, 0.0)
            pr, pm = _cmul(pw_re[...], pw_im[...], c_re, c_im)
            r = r + pr
            m = m + pm
            lam_re[rows, :] = r
            lam_im[rows, :] = m
            cur_re = xe_re[pl.ds(pl.multiple_of(i * 8 + 8, 8), 8), :]
            cur_im = xe_im[pl.ds(pl.multiple_of(i * 8 + 8, 8), 8), :]
            bef_re = xe_re[rows, :]
            bef_im = xe_im[rows, :]
            xp_re = jnp.where(row == 0, jnp.broadcast_to(bef_re[7:8, :], (8, W)), pltpu.roll(cur_re, 1, 0))
            xp_im = jnp.where(row == 0, jnp.broadcast_to(bef_im[7:8, :], (8, W)), pltpu.roll(cur_im, 1, 0))
            s_ar = s_ar + r * xp_re + m * xp_im
            s_ai = s_ai + m * xp_re - r * xp_im
            return jnp.broadcast_to(r[0:1, :], (8, W)), jnp.broadcast_to(m[0:1, :], (8, W)), s_ar, s_ai

        c_re, c_im, s_ar, s_ai = lax.fori_loop(
            0, n_tiles, tile, (carry_re[...], carry_im[...], acc_ar[...], acc_ai[...]))
        carry_re[...] = c_re
        carry_im[...] = c_im
        acc_ar[...] = s_ar
        acc_ai[...] = s_ai
        lr = lam_re[...].astype(BF16)
        li = lam_im[...].astype(BF16)
        du_ref[...] = (dy * d_ref[...] + jnp.dot(lr, btr_ref[...], preferred_element_type=F32)
                       + jnp.dot(li, bti_ref[...], preferred_element_type=F32))
        dd_ref[...] += jnp.sum(dy * uv, axis=0, keepdims=True)
        tn_dims = (((0,), (0,)), ((), ()))
        ub = uv.astype(BF16)
        dbr_ref[...] += lax.dot_general(ub, lr, tn_dims, preferred_element_type=F32)
        dbi_ref[...] += lax.dot_general(ub, li, tn_dims, preferred_element_type=F32)
        dcr_ref[...] += lax.dot_general(dyb, xr_ref[...].astype(BF16), tn_dims, preferred_element_type=F32)
        dci_ref[...] -= lax.dot_general(dyb, xi_ref[...].astype(BF16), tn_dims, preferred_element_type=F32)

        @pl.when(first_chunk)
        def _():
            dar_ref[...] = jnp.sum(acc_ar[...], axis=0, keepdims=True)
            dai_ref[...] = jnp.sum(acc_ai[...], axis=0, keepdims=True)

    rev = lambda k: n_chunks - 1 - k
    ch = pl.BlockSpec((tc, CH_TILE), lambda s, k: (rev(k), s))
    st = pl.BlockSpec((tc, W), lambda s, k: (rev(k), s))
    prev = pl.BlockSpec((8, W), lambda s, k: (jnp.maximum(rev(k) * tiles_per_chunk - 1, 0), s))
    mat_cw = pl.BlockSpec((None, CH_TILE, W), lambda s, k: (s, 0, 0))
    mat_wc = pl.BlockSpec((None, W, CH_TILE), lambda s, k: (s, 0, 0))
    vec_w = pl.BlockSpec((1, W), lambda s, k: (0, s))
    vec_c = pl.BlockSpec((1, CH_TILE), lambda s, k: (0, s))
    dense = jax.ShapeDtypeStruct((n_st, CH_TILE, W), F32)
    return pl.pallas_call(
        body, name=name, grid=(n_st, n_chunks),
        in_specs=[ch, ch, ch, st, st, prev, prev, mat_cw, mat_cw, mat_wc, mat_wc, vec_w, vec_w, vec_c],
        out_specs=[ch, vec_c, mat_cw, mat_cw, mat_cw, mat_cw, vec_w, vec_w],
        out_shape=[jax.ShapeDtypeStruct((T, D), F32), jax.ShapeDtypeStruct((1, D), F32), dense, dense, dense, dense,
                   jax.ShapeDtypeStruct((1, 4 * D), F32), jax.ShapeDtypeStruct((1, 4 * D), F32)],
        scratch_shapes=[pltpu.VMEM((tc, W), F32), pltpu.VMEM((tc, W), F32),
                        pltpu.VMEM((tc + 8, W), F32), pltpu.VMEM((tc + 8, W), F32),
                        pltpu.VMEM((8, W), F32), pltpu.VMEM((8, W), F32),
                        pltpu.VMEM((8, W), F32), pltpu.VMEM((8, W), F32),
                        pltpu.VMEM((3, 8, W), F32), pltpu.VMEM((3, 8, W), F32),
                        pltpu.VMEM((8, W), F32), pltpu.VMEM((8, W), F32)],
        compiler_params=_cparams("parallel", "arbitrary"),
    )(dz, y, u, x_re, x_im, x_re, x_im, c_re, c_im, bt_re, bt_im, abar_re, abar_im, d_skip)


def _sb_block(q_h, kblk, q_row0, k_row0, tq):
    z = lax.dot_general(q_h, kblk, (((1,), (1,)), ((), ())), preferred_element_type=F32)
    lb = jnp.minimum(z, 0.0) - jnp.log1p(jnp.exp(-jnp.abs(z)))
    tpos = q_row0 + lax.broadcasted_iota(jnp.int32, (tq, KEY_BLOCK), 0)
    spos = k_row0 + lax.broadcasted_iota(jnp.int32, (tq, KEY_BLOCK), 1)
    mask = (spos < tpos) & (spos >= META_START)
    lm = jnp.where(mask, lb - z, 0.0)
    hi = lm.astype(BF16)
    lo = (lm - hi.astype(F32)).astype(BF16)
    jj = lax.broadcasted_iota(jnp.int32, (KEY_BLOCK, KEY_BLOCK), 0)
    ss = lax.broadcasted_iota(jnp.int32, (KEY_BLOCK, KEY_BLOCK), 1)
    later = (jj > ss).astype(BF16)
    cum = jnp.dot(hi, later, preferred_element_type=F32) + jnp.dot(lo, later, preferred_element_type=F32)
    return lb, lm, cum, mask


def _attn_fwd(name, q, kv):
    T, D = q.shape
    n_hp = D // 128
    tq = _row_tile(T, 256)
    kb_per_q = tq // KEY_BLOCK
    scale = 1.0 / math.sqrt(HEAD_DIM)

    def body(q_ref, k_ref, v_ref, o_ref, l_ref):
        iq = pl.program_id(1)
        lane = lax.broadcasted_iota(jnp.int32, (tq, 128), 1)
        qs = q_ref[...] * jnp.asarray(scale, BF16)
        n_kb = (iq + 1) * kb_per_q
        accs, sums = [], []
        for head in range(2):
            in_head = (lane < HEAD_DIM) if head == 0 else (lane >= HEAD_DIM)
            q_h = jnp.where(in_head, qs, jnp.zeros_like(qs))

            def step(n, carry, q_h=q_h):
                acc, rsum = carry
                kb = n_kb - 1 - n
                rows = pl.ds(pl.multiple_of(kb * KEY_BLOCK, KEY_BLOCK), KEY_BLOCK)
                lb, lm, cum, mask = _sb_block(q_h, k_ref[rows, :], iq * tq, kb * KEY_BLOCK, tq)
                w = jnp.where(mask, jnp.exp(lb + cum + rsum), 0.0)
                acc = acc + jnp.dot(w.astype(BF16), v_ref[rows, :], preferred_element_type=F32)
                return acc, rsum + jnp.sum(lm, axis=1, keepdims=True)

            acc, rsum = lax.fori_loop(0, n_kb, step, (jnp.zeros((tq, 128), F32), jnp.zeros((tq, 1), F32)))
            accs.append(acc)
            sums.append(rsum)
        o_ref[...] = jnp.where(lane < HEAD_DIM, accs[0], accs[1]).astype(o_ref.dtype)
        l_ref[...] = jnp.where(lane < HEAD_DIM, sums[0], sums[1])

    blk = pl.BlockSpec((tq, 128), lambda h, i: (i, h))
    return pl.pallas_call(
        body, name=name, grid=(n_hp, T // tq),
        in_specs=[blk, pl.BlockSpec((T, 128), lambda h, i: (0, h)), pl.BlockSpec((T, 128), lambda h, i: (0, n_hp + h))],
        out_specs=[blk, blk],
        out_shape=[jax.ShapeDtypeStruct((T, D), BF16), jax.ShapeDtypeStruct((T, D), F32)],
        compiler_params=_cparams("parallel", "arbitrary"),
    )(q, kv, kv)


def _attn_bwd(name, q, kv, do, ltot):
    T, D = q.shape
    n_hp = D // 128
    tq = _row_tile(T, 256)
    n_q = T // tq
    kb_per_q = tq // KEY_BLOCK
    scale = 1.0 / math.sqrt(HEAD_DIM)

    def body(q_ref, k_ref, v_ref, do_ref, l_ref, dq_ref, dk_ref, dv_ref, dk_acc, dv_acc):
        iq = pl.program_id(1)

        @pl.when(iq == 0)
        def _():
            dk_acc[...] = jnp.zeros_like(dk_acc)
            dv_acc[...] = jnp.zeros_like(dv_acc)

        lane = lax.broadcasted_iota(jnp.int32, (tq, 128), 1)
        qs = q_ref[...] * jnp.asarray(scale, BF16)
        dov = do_ref[...]
        ltv = l_ref[...]
        n_kb = (iq + 1) * kb_per_q
        jj = lax.broadcasted_iota(jnp.int32, (KEY_BLOCK, KEY_BLOCK), 0)
        ss = lax.broadcasted_iota(jnp.int32, (KEY_BLOCK, KEY_BLOCK), 1)
        earlier = (jj < ss).astype(BF16)
        tn_dims = (((0,), (0,)), ((), ()))
        dqs = []
        for head in range(2):
            in_head = (lane < HEAD_DIM) if head == 0 else (lane >= HEAD_DIM)
            q_h = jnp.where(in_head, qs, jnp.zeros_like(qs))
            do_h = jnp.where(in_head, dov, jnp.zeros_like(dov))
            ltot_h = ltv[:, 0:1] if head == 0 else ltv[:, HEAD_DIM:HEAD_DIM + 1]

            def step(kb, carry, q_h=q_h, do_h=do_h, ltot_h=ltot_h):
                dq, lpre, cpre = carry
                rows = pl.ds(pl.multiple_of(kb * KEY_BLOCK, KEY_BLOCK), KEY_BLOCK)
                kblk = k_ref[rows, :]
                vblk = v_ref[rows, :]
                lb, lm, cum, mask = _sb_block(q_h, kblk, iq * tq, kb * KEY_BLOCK, tq)
                rs = jnp.sum(lm, axis=1, keepdims=True)
                w = jnp.where(mask, jnp.exp(lb + cum + (ltot_h - lpre - rs)), 0.0)
                dw = lax.dot_general(do_h, vblk, (((1,), (1,)), ((), ())), preferred_element_type=F32)
                da = w * dw
                pre = jnp.dot(da.astype(BF16), earlier, preferred_element_type=F32) + cpre
                sig = jnp.exp(lb)
                dz = (da * (1.0 - sig) - jnp.where(mask, sig * pre, 0.0)).astype(BF16)
                dq = dq + jnp.dot(dz, kblk, preferred_element_type=F32)
                dk_acc[rows, :] += lax.dot_general(dz, q_h, tn_dims, preferred_element_type=F32)
                dv_acc[rows, :] += lax.dot_general(w.astype(BF16), do_h, tn_dims, preferred_element_type=F32)
                return dq, lpre + rs, cpre + jnp.sum(da, axis=1, keepdims=True)

            dq, _, _ = lax.fori_loop(
                0, n_kb, step, (jnp.zeros((tq, 128), F32), jnp.zeros((tq, 1), F32), jnp.zeros((tq, 1), F32)))
            dqs.append(dq)
        dq_ref[...] = (jnp.where(lane < HEAD_DIM, dqs[0], dqs[1]) * scale).astype(dq_ref.dtype)

        @pl.when(iq == n_q - 1)
        def _():
            dk_ref[...] = dk_acc[...].astype(dk_ref.dtype)
            dv_ref[...] = dv_acc[...].astype(dv_ref.dtype)

    blk = pl.BlockSpec((tq, 128), lambda h, i: (i, h))
    full = pl.BlockSpec((T, 128), lambda h, i: (0, h))
    return pl.pallas_call(
        body, name=name, grid=(n_hp, n_q),
        in_specs=[blk, full, pl.BlockSpec((T, 128), lambda h, i: (0, n_hp + h)), blk, blk],
        out_specs=[blk, full, full],
        out_shape=[jax.ShapeDtypeStruct((T, D), BF16)] * 3,
        scratch_shapes=[pltpu.VMEM((T, 128), F32), pltpu.VMEM((T, 128), F32)],
        compiler_params=_cparams("parallel", "arbitrary"),
    )(q, kv, kv, do, ltot)


def _adamw(name, w, g, m, v):
    shape = w.shape
    size = w.size
    cols = 1024 if size % 1024 == 0 else shape[-1]
    rows = size // cols
    tm = _row_tile(rows, 512) if rows % 8 == 0 else rows
    c1 = 1.0 / (1.0 - ADAM_B1 ** ADAM_STEP)
    c2 = 1.0 / (1.0 - ADAM_B2 ** ADAM_STEP)

    def body(w_ref, g_ref, m_ref, v_ref, d_ref, nm_ref, nv_ref):
        gv = g_ref[...]
        nm = ADAM_B1 * m_ref[...] + (1.0 - ADAM_B1) * gv
        nv = ADAM_B2 * v_ref[...] + (1.0 - ADAM_B2) * (gv * gv)
        d_ref[...] = -ADAM_LR * ((nm * c1) / (jnp.sqrt(nv * c2) + ADAM_EPS) + ADAM_WD * w_ref[...])
        nm_ref[...] = nm
        nv_ref[...] = nv

    blk = pl.BlockSpec((tm, cols), lambda i: (i, 0))
    outs = pl.pallas_call(
        body, name=name, grid=(rows // tm,),
        in_specs=[blk] * 4, out_specs=[blk] * 3,
        out_shape=[jax.ShapeDtypeStruct((rows, cols), F32)] * 3,
        compiler_params=_cparams("parallel"),
    )(*[t.reshape(rows, cols) for t in (w, g, m, v)])
    return tuple(o.reshape(shape) for o in outs)


def _any_specs(n):
    return [pl.BlockSpec(memory_space=pl.ANY)] * n


def _place():
    x, y, c = lax.axis_index("x"), lax.axis_index("y"), lax.axis_index("c")
    chips = [(1 - x, y), (x, 1 - y), (1 - x, 1 - y)]
    return x, y, c, chips


def _all_gather_chips(name, shards):
    n = len(shards)

    def body(*refs):
        x_refs, o_refs = refs[:n], refs[n:2 * n]
        send_sems, recv_sems, local_sems = refs[2 * n:]
        x, y, c, chips = _place()
        me = 2 * x + y
        sibling = (x, y, 1 - c)

        def half(ref, i, which):
            h = shards[i].shape[0] // 2
            return ref.at[pl.ds(which * h, h)]

        def remote(k, i, src, dst, to):
            return pltpu.make_async_remote_copy(src_ref=src, dst_ref=dst, send_sem=send_sems.at[k, i],
                                                recv_sem=recv_sems.at[k, i], device_id=to, device_id_type=MESH)

        local = [pltpu.make_async_copy(x_refs[i], o_refs[i].at[me], local_sems.at[i]) for i in range(n)]
        for cp in local:
            cp.start()
        sent = []
        for j, chip in enumerate(chips):
            for i in range(n):
                cp = remote(j, i, half(x_refs[i], i, c), half(o_refs[i].at[me], i, c), (*chip, c))
                cp.start()
                sent.append(cp)
        for j, chip in enumerate(chips):
            pj = 2 * chip[0] + chip[1]
            for i in range(n):
                landed = half(o_refs[i].at[pj], i, c)
                remote(j, i, landed, landed, (*chip, c)).wait_recv()
                cp = remote(3 + j, i, landed, landed, sibling)
                cp.start()
                sent.append(cp)
        for j, chip in enumerate(chips):
            pj = 2 * chip[0] + chip[1]
            for i in range(n):
                got = half(o_refs[i].at[pj], i, 1 - c)
                remote(3 + j, i, got, got, sibling).wait_recv()
        for cp in sent:
            cp.wait_send()
        for cp in local:
            cp.wait()

    return pl.pallas_call(
        body, name=name,
        in_specs=_any_specs(n), out_specs=_any_specs(n),
        out_shape=[jax.ShapeDtypeStruct((4,) + s.shape, s.dtype) for s in shards],
        scratch_shapes=[pltpu.SemaphoreType.DMA((6, n)), pltpu.SemaphoreType.DMA((6, n)),
                        pltpu.SemaphoreType.DMA((n,))],
    )(*shards)


def _pair_split(name, grads):
    n = len(grads)

    def body(*refs):
        g_refs, kept_refs, got_refs = refs[:n], refs[n:2 * n], refs[2 * n:3 * n]
        send_sems, recv_sems, local_sems = refs[3 * n:]
        x, y, c, _ = _place()
        sibling = (x, y, 1 - c)
        local, sent = [], []
        for i in range(n):
            h = grads[i].shape[1] // 2
            cp = pltpu.make_async_copy(g_refs[i].at[:, pl.ds(c * h, h)], kept_refs[i], local_sems.at[i])
            cp.start()
            local.append(cp)
            rc = pltpu.make_async_remote_copy(
                src_ref=g_refs[i].at[:, pl.ds((1 - c) * h, h)], dst_ref=got_refs[i],
                send_sem=send_sems.at[i], recv_sem=recv_sems.at[i], device_id=sibling, device_id_type=MESH)
            rc.start()
            sent.append(rc)
        for rc in sent:
            rc.wait()
        for cp in local:
            cp.wait()

    halves = [jax.ShapeDtypeStruct((4, g.shape[1] // 2, g.shape[2]), g.dtype) for g in grads]
    outs = pl.pallas_call(
        body, name=name,
        in_specs=_any_specs(n), out_specs=_any_specs(2 * n), out_shape=halves + halves,
        scratch_shapes=[pltpu.SemaphoreType.DMA((n,)), pltpu.SemaphoreType.DMA((n,)), pltpu.SemaphoreType.DMA((n,))],
    )(*grads)
    return outs[:n], outs[n:]


def _chip_exchange(name, sums):
    n = len(sums)

    def body(*refs):
        s_refs, o_refs = refs[:n], refs[n:2 * n]
        send_sems, recv_sems, local_sems = refs[2 * n:]
        x, y, c, chips = _place()
        me = 2 * x + y
        local, sent = [], []
        for i in range(n):
            cp = pltpu.make_async_copy(s_refs[i].at[me], o_refs[i].at[me], local_sems.at[i])
            cp.start()
            local.append(cp)
        for j, chip in enumerate(chips):
            pj = 2 * chip[0] + chip[1]
            for i in range(n):
                rc = pltpu.make_async_remote_copy(
                    src_ref=s_refs[i].at[pj], dst_ref=o_refs[i].at[me],
                    send_sem=send_sems.at[j, i], recv_sem=recv_sems.at[j, i],
                    device_id=(*chip, c), device_id_type=MESH)
                rc.start()
                sent.append(rc)
        for j, chip in enumerate(chips):
            pj = 2 * chip[0] + chip[1]
            for i in range(n):
                pltpu.make_async_remote_copy(
                    src_ref=s_refs[i].at[pj], dst_ref=o_refs[i].at[pj],
                    send_sem=send_sems.at[j, i], recv_sem=recv_sems.at[j, i],
                    device_id=(*chip, c), device_id_type=MESH).wait_recv()
        for rc in sent:
            rc.wait_send()
        for cp in local:
            cp.wait()

    return pl.pallas_call(
        body, name=name,
        in_specs=_any_specs(n), out_specs=_any_specs(n),
        out_shape=[jax.ShapeDtypeStruct(s.shape, s.dtype) for s in sums],
        scratch_shapes=[pltpu.SemaphoreType.DMA((3, n)), pltpu.SemaphoreType.DMA((3, n)),
                        pltpu.SemaphoreType.DMA((n,))],
    )(*sums)


def _pair_join(name, halves):
    n = len(halves)

    def body(*refs):
        h_refs, o_refs = refs[:n], refs[n:2 * n]
        send_sems, recv_sems, local_sems = refs[2 * n:]
        x, y, c, _ = _place()
        sibling = (x, y, 1 - c)
        local, sent = [], []
        for i in range(n):
            h = halves[i].shape[0]
            mine = o_refs[i].at[pl.ds(c * h, h)]
            cp = pltpu.make_async_copy(h_refs[i], mine, local_sems.at[i])
            cp.start()
            local.append(cp)
            rc = pltpu.make_async_remote_copy(
                src_ref=h_refs[i], dst_ref=mine, send_sem=send_sems.at[i], recv_sem=recv_sems.at[i],
                device_id=sibling, device_id_type=MESH)
            rc.start()
            sent.append(rc)
        for i in range(n):
            h = halves[i].shape[0]
            theirs = o_refs[i].at[pl.ds((1 - c) * h, h)]
            pltpu.make_async_remote_copy(
                src_ref=h_refs[i], dst_ref=theirs, send_sem=send_sems.at[i], recv_sem=recv_sems.at[i],
                device_id=sibling, device_id_type=MESH).wait_recv()
        for rc in sent:
            rc.wait_send()
        for cp in local:
            cp.wait()

    return pl.pallas_call(
        body, name=name,
        in_specs=_any_specs(n), out_specs=_any_specs(n),
        out_shape=[jax.ShapeDtypeStruct((2 * s.shape[0], s.shape[1]), s.dtype) for s in halves],
        scratch_shapes=[pltpu.SemaphoreType.DMA((n,)), pltpu.SemaphoreType.DMA((n,)), pltpu.SemaphoreType.DMA((n,))],
    )(*halves)


def _add_pair(name, a, b):
    _, H, C = a.shape
    th = _row_tile(H, max(8, (1024 * 1024) // (4 * C) // 8 * 8))

    def body(a_ref, b_ref, o_ref):
        o_ref[...] = (a_ref[...].astype(F32) + b_ref[...].astype(F32)).astype(o_ref.dtype)

    blk = pl.BlockSpec((None, th, C), lambda q, i: (q, i, 0))
    return pl.pallas_call(
        body, name=name, grid=(4, H // th), in_specs=[blk, blk], out_specs=blk,
        out_shape=jax.ShapeDtypeStruct(a.shape, a.dtype), compiler_params=_cparams("parallel", "parallel"),
    )(a, b)


def _add_chips(name, parts):
    _, H, C = parts.shape
    th = _row_tile(H, max(8, (1024 * 1024) // (4 * C) // 8 * 8))

    def body(p_ref, o_ref):
        acc = p_ref[0].astype(F32)
        for q in range(1, 4):
            acc = acc + p_ref[q].astype(F32)
        o_ref[...] = acc

    return pl.pallas_call(
        body, name=name, grid=(H // th,),
        in_specs=[pl.BlockSpec((4, th, C), lambda i: (0, i, 0))], out_specs=pl.BlockSpec((th, C), lambda i: (i, 0)),
        out_shape=jax.ShapeDtypeStruct((H, C), F32), compiler_params=_cparams("parallel"),
    )(parts)


def _reduce_scatter(grads):
    n = len(grads)
    kept, got = _pair_split("rs_pair_split", grads)
    pair = [_add_pair(f"rs_add_pair_{i}", kept[i], got[i]) for i in range(n)]
    parts = _chip_exchange("rs_chip_exchange", pair)
    red = [_add_chips(f"rs_add_chips_{i}", parts[i]) for i in range(n)]
    return _pair_join("rs_pair_join", red)


def _block_diag(t):
    G, A, B = t.shape
    eye = jnp.eye(8, dtype=t.dtype)
    return jnp.einsum("sgab,gh->sgahb", t.reshape(G // 8, 8, A, B), eye).reshape(G // 8, 8 * A, 8 * B)


def _block_diag_extract(m, A, B):
    n = m.shape[0]
    eye = jnp.eye(8, dtype=m.dtype)
    return jnp.einsum("sgahb,gh->sgab", m.reshape(n, 8, A, 8, B), eye).reshape(8 * n, A, B)


def kernel(x, meta_tokens, norm_mix, norm_ffn, s5_a_re, s5_a_im, s5_log_dt, s5_b_re, s5_b_im, s5_c_re, s5_c_im, s5_d, s5_w_glu, norm_kv, w_kv, w_q, w_o, w_ffn_in, w_ffn_out, norm_final, loss_target, m_meta_tokens, m_norm_mix, m_norm_ffn, m_s5_a_re, m_s5_a_im, m_s5_log_dt, m_s5_b_re, m_s5_b_im, m_s5_c_re, m_s5_c_im, m_s5_d, m_s5_w_glu, m_norm_kv, m_w_kv, m_w_q, m_w_o, m_w_ffn_in, m_w_ffn_out, m_norm_final, v_meta_tokens, v_norm_mix, v_norm_ffn, v_s5_a_re, v_s5_a_im, v_s5_log_dt, v_s5_b_re, v_s5_b_im, v_s5_c_re, v_s5_c_im, v_s5_d, v_s5_w_glu, v_norm_kv, v_w_kv, v_w_q, v_w_o, v_w_ffn_in, v_w_ffn_out, v_norm_final):
    seq, D = x.shape[1], x.shape[2]
    T = X_START + seq
    G, P, C = s5_a_re.shape[1], S5_STATE, S5_GROUP
    d_ff = w_ffn_out.shape[1] * 4
    dq4 = D // 4
    chip = 2 * lax.axis_index("x") + lax.axis_index("y")

    small_in = jnp.concatenate([meta_tokens, jnp.pad(s5_d, ((0, 15), (0, 0)))], axis=0)
    (small_all,) = _all_gather_chips("ag_small", [small_in])
    meta_full = small_all[:, :N_META, :].transpose(1, 0, 2).reshape(N_META, D)
    d_skip = small_all[:, N_META, :].reshape(1, D)
    shards = [s5_w_glu[0], w_kv, w_q[0], w_o[0],
              w_ffn_in.reshape(2 * D, -1), w_ffn_out.reshape(-1, D)]
    wg_glu, wg_kv, wg_q, wg_o, wg_in, wg_out = _all_gather_chips("ag_weights", [s.astype(BF16) for s in shards])
    wg_q = wg_q.reshape(1, D, D)
    wg_o = wg_o.reshape(1, D, D)
    rows_out = d_ff // 4
    wg_out = [wg_out[:, l * rows_out:(l + 1) * rows_out, :].reshape(1, d_ff, D) for l in range(2)]

    row = lambda v: v.reshape(1, -1)
    g_mix0, g_mix1 = row(norm_mix[0]), row(norm_mix[1])
    g_ffn = [row(norm_ffn[0]), row(norm_ffn[1])]
    g_kv, g_final = row(norm_kv), row(norm_final)

    a_re3 = s5_a_re[0].reshape(G, 1, P)
    a_im3 = s5_a_im[0].reshape(G, 1, P)
    log_dt3 = s5_log_dt[0].reshape(G, 1, 1)
    bt_re = s5_b_re[0].transpose(0, 2, 1)
    bt_im = s5_b_im[0].transpose(0, 2, 1)
    ab_re, ab_im, bb_re, bb_im = _s5_prep_fwd("s5_prep", a_re3, a_im3, log_dt3, bt_re, bt_im)
    abar_re, abar_im = ab_re.reshape(1, G * P), ab_im.reshape(1, G * P)
    bd_b_re = _block_diag(bb_re).astype(BF16)
    bd_b_im = _block_diag(bb_im).astype(BF16)
    bd_bt_re = bd_b_re.transpose(0, 2, 1)
    bd_bt_im = bd_b_im.transpose(0, 2, 1)
    bd_c_re = _block_diag(s5_c_re[0]).astype(BF16)
    bd_c_im = _block_diag(s5_c_im[0]).astype(BF16)
    bd_ct_re = bd_c_re.transpose(0, 2, 1)
    bd_ct_im = bd_c_im.transpose(0, 2, 1)

    h0 = jnp.concatenate([jnp.zeros((META_START, D), F32), meta_full, x[0]], axis=0)
    (u,) = _rmsnorm_fwd("norm_mix0", h0, [g_mix0], [F32])
    y, z, x_re, x_im = _s5_fwd("s5_scan", u, bd_b_re, bd_b_im, bd_ct_re, bd_ct_im, abar_re, abar_im, d_skip)
    vg = _mm_nn("glu_proj", z, wg_glu)
    h1 = _glu_fwd("glu", vg, h0)

    def ffn_fwd(l, h):
        (n,) = _rmsnorm_fwd(f"norm_ffn{l}", h, [g_ffn[l]], [BF16])
        gu = _mm_nn(f"ffn_in{l}", n, wg_in, k_blk=l)
        mid = _swiglu_fwd(f"swiglu{l}", gu)
        return n, gu, mid, _mm_nn(f"ffn_out{l}", mid, wg_out[l], res=h)

    n1, gu0, mid0, h2 = ffn_fwd(0, h1)
    nk, nq = _rmsnorm_fwd("norm_kv_q", h2, [g_kv, g_mix1], [BF16, BF16])
    kv = _mm_nn("kv_proj", nk, wg_kv, out_dtype=BF16)
    q = _mm_nn("q_proj", nq, wg_q, out_dtype=BF16)
    o, ltot = _attn_fwd("attn_fwd", q, kv)
    h3 = _mm_nn("o_proj", o, wg_o, res=h2)
    n3, gu1, mid1, h4 = ffn_fwd(1, h3)
    loss_part, dh4, dg_final = _final_loss("final_loss", h4, g_final, loss_target[0])
    loss = lax.psum(loss_part[0, 0], ("x", "y", "c"))

    def ffn_bwd(l, dh, h, n, gu, mid):
        dmid = _mm_nt_k(f"ffn_out{l}_dx", dh, wg_out[l], d_ff)
        dgu = _swiglu_bwd(f"swiglu{l}_bwd", gu, dmid)
        dw_out = _mm_tn(f"ffn_out{l}_dw", mid, dh, 1)
        dw_in = _mm_tn(f"ffn_in{l}_dw", n, dgu, 4)
        dn = _mm_nt_k(f"ffn_in{l}_dx", dgu, wg_in, D, k_blk=l)
        dh_prev, (dg,) = _rmsnorm_bwd(f"norm_ffn{l}_bwd", h, [(g_ffn[l], dn)], dh)
        return dh_prev, dg, dw_in, dw_out

    dh3, dg_ffn1, dw_in1, dw_out1 = ffn_bwd(1, dh4, h3, n3, gu1, mid1)
    d_o = _mm_nt_k("o_proj_dx", dh3, wg_o, D, out_dtype=BF16)
    dw_o = _mm_tn("o_proj_dw", o, dh3, 1)
    dq, dk, dv = _attn_bwd("attn_bwd", q, kv, d_o, ltot)
    dkv = jnp.concatenate([dk, dv], axis=1)
    dw_q = _mm_tn("q_proj_dw", nq, dq, 1)
    dnq = _mm_nt_k("q_proj_dx", dq, wg_q, D)
    dw_kv = _mm_tn("kv_proj_dw", nk, dkv, 4)
    dnk = _mm_nt_k("kv_proj_dx", dkv, wg_kv, D)
    dh2, (dg_mix1, dg_kv) = _rmsnorm_bwd("norm_kv_q_bwd", h2, [(g_mix1, dnq), (g_kv, dnk)], dh3)
    dh1, dg_ffn0, dw_in0, dw_out0 = ffn_bwd(0, dh2, h1, n1, gu0, mid0)
    dvg = _glu_bwd("glu_bwd", vg, dh1)
    dw_glu = _mm_tn("glu_proj_dw", z, dvg, 4)
    dz = _mm_nt_k("glu_proj_dx", dvg, wg_glu, D)
    du, dd, dbd_b_re, dbd_b_im, dbd_c_re, dbd_c_im, dab_re, dab_im = _s5_bwd(
        "s5_scan_bwd", dz, y, u, x_re, x_im, bd_c_re, bd_c_im, bd_bt_re, bd_bt_im, abar_re, abar_im, d_skip)
    dh0, (dg_mix0,) = _rmsnorm_bwd("norm_mix0_bwd", h0, [(g_mix0, du)], dh1)
    da_re, da_im, dlog_dt, dbt_re, dbt_im = _s5_prep_bwd(
        "s5_prep_bwd", a_re3, a_im3, log_dt3, bt_re, bt_im,
        dab_re.reshape(G, 1, P), dab_im.reshape(G, 1, P),
        _block_diag_extract(dbd_b_re, C, P), _block_diag_extract(dbd_b_im, C, P))
    grad_x = dh0[X_START:][None]

    small_parts = [
        dg_mix0, dg_mix1, dg_ffn0, dg_ffn1, da_re, da_im,
        dbt_re.transpose(0, 2, 1), dbt_im.transpose(0, 2, 1),
        _block_diag_extract(dbd_c_re, C, P), _block_diag_extract(dbd_c_im, C, P),
        dg_kv, dg_final, dh0[META_START:X_START], dd, dlog_dt]
    small_sizes = [p.size for p in small_parts]
    unit = 4 * 2 * 8 * 128
    padded = -(-sum(small_sizes) // unit) * unit
    small_flat = jnp.concatenate(
        [p.reshape(-1) for p in small_parts[:-1]]
        + [jnp.pad(dlog_dt.reshape(-1), (0, padded - sum(small_sizes)))])
    small_blocks = small_flat.reshape(4, padded // (4 * 128), 128)
    big = [dw_glu, dw_kv, dw_q.reshape(4, D // 4, D), dw_o.reshape(4, D // 4, D),
           jnp.stack([dw_in0, dw_in1], axis=1).reshape(4, 2 * D, -1),
           jnp.stack([dw_out0.reshape(4, rows_out, D), dw_out1.reshape(4, rows_out, D)], axis=1).reshape(4, 2 * rows_out, D)]
    red = _reduce_scatter(big + [small_blocks])
    gw_glu, gw_kv, gw_q, gw_o, gw_in, gw_out, small_mine = red
    (small_red,) = _all_gather_chips("ag_small_grads", [small_mine])
    small_red = small_red.reshape(-1)
    pieces, at = [], 0
    for p, size in zip(small_parts, small_sizes):
        pieces.append(small_red[at:at + size].reshape(p.shape))
        at += size
    (gn_mix0, gn_mix1, gn_ffn0, gn_ffn1, ga_re, ga_im, gb_re, gb_im, gc_re, gc_im, gn_kv, gn_final,
     gmeta_full, gd_full, glog_dt) = pieces
    gn_mix = small_red[:2 * D].reshape(2, D)
    gn_ffn = small_red[2 * D:4 * D].reshape(2, D)
    gmeta = lax.dynamic_slice_in_dim(gmeta_full, chip * dq4, dq4, axis=1)
    gd = lax.dynamic_slice_in_dim(gd_full, chip * dq4, dq4, axis=1)

    grads = {
        "meta_tokens": gmeta, "norm_mix": gn_mix, "norm_ffn": gn_ffn,
        "s5_a_re": ga_re.reshape(s5_a_re.shape), "s5_a_im": ga_im.reshape(s5_a_im.shape),
        "s5_log_dt": glog_dt.reshape(s5_log_dt.shape),
        "s5_b_re": gb_re.reshape(s5_b_re.shape), "s5_b_im": gb_im.reshape(s5_b_im.shape),
        "s5_c_re": gc_re.reshape(s5_c_re.shape), "s5_c_im": gc_im.reshape(s5_c_im.shape),
        "s5_d": gd, "s5_w_glu": gw_glu.reshape(s5_w_glu.shape), "norm_kv": gn_kv.reshape(norm_kv.shape),
        "w_kv": gw_kv, "w_q": gw_q.reshape(w_q.shape), "w_o": gw_o.reshape(w_o.shape),
        "w_ffn_in": gw_in.reshape(w_ffn_in.shape), "w_ffn_out": gw_out.reshape(w_ffn_out.shape),
        "norm_final": gn_final.reshape(norm_final.shape),
    }
    weights = {
        "meta_tokens": (meta_tokens, m_meta_tokens, v_meta_tokens), "norm_mix": (norm_mix, m_norm_mix, v_norm_mix),
        "norm_ffn": (norm_ffn, m_norm_ffn, v_norm_ffn), "s5_a_re": (s5_a_re, m_s5_a_re, v_s5_a_re),
        "s5_a_im": (s5_a_im, m_s5_a_im, v_s5_a_im), "s5_log_dt": (s5_log_dt, m_s5_log_dt, v_s5_log_dt),
        "s5_b_re": (s5_b_re, m_s5_b_re, v_s5_b_re), "s5_b_im": (s5_b_im, m_s5_b_im, v_s5_b_im),
        "s5_c_re": (s5_c_re, m_s5_c_re, v_s5_c_re), "s5_c_im": (s5_c_im, m_s5_c_im, v_s5_c_im),
        "s5_d": (s5_d, m_s5_d, v_s5_d), "s5_w_glu": (s5_w_glu, m_s5_w_glu, v_s5_w_glu),
        "norm_kv": (norm_kv, m_norm_kv, v_norm_kv), "w_kv": (w_kv, m_w_kv, v_w_kv), "w_q": (w_q, m_w_q, v_w_q),
        "w_o": (w_o, m_w_o, v_w_o), "w_ffn_in": (w_ffn_in, m_w_ffn_in, v_w_ffn_in),
        "w_ffn_out": (w_ffn_out, m_w_ffn_out, v_w_ffn_out), "norm_final": (norm_final, m_norm_final, v_norm_final),
    }
    names = list(weights)
    deltas, new_m, new_v = [], [], []
    for name in names:
        w, m, v = weights[name]
        d, nm, nv = _adamw(f"adamw_{name}", w, grads[name], m, v)
        deltas.append(d)
        new_m.append(nm)
        new_v.append(nv)
    return (loss, grad_x, *[grads[n] for n in names], *deltas, *new_m, *new_v)
```

```python
import functools
import math

import jax
import jax.numpy as jnp
from jax import lax
from jax.experimental import pallas as pl
from jax.experimental.pallas import tpu as pltpu

F32 = jnp.float32
BF16 = jnp.bfloat16

N_META = 16
X_START = 128
META_START = X_START - N_META
S5_GROUP = 16
S5_STATE = 64
HEAD_DIM = 64
KEY_BLOCK = 128
STATE_TILE = 512
CH_TILE = 128
RMS_EPS = 1e-6
ADAM_LR, ADAM_B1, ADAM_B2, ADAM_EPS, ADAM_WD, ADAM_STEP = 0.001, 0.9, 0.999, 1e-08, 0.01, 10
VMEM_LIMIT_BYTES = 48 * 1024 * 1024
MESH = pl.DeviceIdType.MESH


def _cparams(*sem):
    return pltpu.CompilerParams(dimension_semantics=sem, vmem_limit_bytes=VMEM_LIMIT_BYTES)


def _row_tile(rows, cap):
    for unit in (128, 8):
        best = 0
        for t in range(unit, min(rows, cap) + 1, unit):
            if rows % t == 0:
                best = t
        if best:
            return best
    return rows


def _col_tile(cols, cap):
    best = 0
    for t in range(128, min(cols, cap) + 1, 128):
        if cols % t == 0:
            best = t
    return best if best else cols


def _gelu(x):
    k = math.sqrt(2.0 / math.pi)
    return 0.5 * x * (1.0 + jnp.tanh(k * (x + 0.044715 * x * x * x)))


def _gelu_grad(x):
    k = math.sqrt(2.0 / math.pi)
    t = jnp.tanh(k * (x + 0.044715 * x * x * x))
    return 0.5 * (1.0 + t) + 0.5 * x * (1.0 - t * t) * k * (1.0 + 3.0 * 0.044715 * x * x)


def _sigmoid(x):
    return 1.0 / (1.0 + jnp.exp(-x))


def _rmsnorm_fwd(name, x, gains, out_dtypes):
    T, D = x.shape
    tm = _row_tile(T, 512)
    n = len(gains)

    def body(x_ref, *refs):
        xv = x_ref[...]
        xh = xv * lax.rsqrt(jnp.mean(xv * xv, axis=-1, keepdims=True) + RMS_EPS)
        for g_ref, o_ref in zip(refs[:n], refs[n:]):
            o_ref[...] = (xh * g_ref[...]).astype(o_ref.dtype)

    row = pl.BlockSpec((tm, D), lambda i: (i, 0))
    vec = pl.BlockSpec((1, D), lambda i: (0, 0))
    return pl.pallas_call(
        body, name=name, grid=(T // tm,),
        in_specs=[row] + [vec] * n, out_specs=[row] * n,
        out_shape=[jax.ShapeDtypeStruct((T, D), dt) for dt in out_dtypes],
        compiler_params=_cparams("parallel"),
    )(x, *gains)


def _rmsnorm_bwd(name, x, pairs, dres):
    T, D = x.shape
    tm = _row_tile(T, 256)
    n = len(pairs)

    def body(x_ref, dres_ref, *refs):
        g_refs, dy_refs = refs[:n], refs[n:2 * n]
        dx_ref, dg_refs = refs[2 * n], refs[2 * n + 1:]
        i = pl.program_id(0)
        xv = x_ref[...]
        r = lax.rsqrt(jnp.mean(xv * xv, axis=-1, keepdims=True) + RMS_EPS)
        xh = xv * r
        dxh = jnp.zeros_like(xv)
        for g_ref, dy_ref, dg_ref in zip(g_refs, dy_refs, dg_refs):
            dy = dy_ref[...].astype(F32)
            part = jnp.sum(dy * xh, axis=0, keepdims=True)

            @pl.when(i == 0)
            def _():
                dg_ref[...] = part

            @pl.when(i > 0)
            def _():
                dg_ref[...] += part

            dxh = dxh + dy * g_ref[...]
        dx = r * (dxh - xh * jnp.mean(dxh * xh, axis=-1, keepdims=True))
        dx_ref[...] = dres_ref[...] + dx

    row = pl.BlockSpec((tm, D), lambda i: (i, 0))
    vec = pl.BlockSpec((1, D), lambda i: (0, 0))
    outs = pl.pallas_call(
        body, name=name, grid=(T // tm,),
        in_specs=[row, row] + [vec] * n + [row] * n,
        out_specs=[row] + [vec] * n,
        out_shape=[jax.ShapeDtypeStruct((T, D), F32)] + [jax.ShapeDtypeStruct((1, D), F32)] * n,
        compiler_params=_cparams("arbitrary"),
    )(x, dres, *[g for g, _ in pairs], *[dy for _, dy in pairs])
    return outs[0], outs[1:]


def _mm_nn(name, a, w, k_blk=0, res=None, out_dtype=F32):
    M, K = a.shape
    S, _, Ns = w.shape
    tm = _row_tile(M, 512)
    tn = _col_tile(Ns, 1408)
    nt = Ns // tn

    def body(a_ref, w_ref, *refs):
        o_ref = refs[-1]
        acc = jnp.dot(a_ref[...].astype(BF16), w_ref[...], preferred_element_type=F32)
        if res is not None:
            acc = acc + refs[0][...]
        o_ref[...] = acc.astype(o_ref.dtype)

    in_specs = [pl.BlockSpec((tm, K), lambda j, i: (i, 0)),
                pl.BlockSpec((None, K, tn), lambda j, i: (j // nt, k_blk, j % nt))]
    args = [a, w]
    if res is not None:
        in_specs.append(pl.BlockSpec((tm, tn), lambda j, i: (i, j)))
        args.append(res)
    return pl.pallas_call(
        body, name=name, grid=(S * nt, M // tm),
        in_specs=in_specs, out_specs=pl.BlockSpec((tm, tn), lambda j, i: (i, j)),
        out_shape=jax.ShapeDtypeStruct((M, S * Ns), out_dtype),
        compiler_params=_cparams("parallel", "parallel"),
    )(*args)


def _mm_nt_k(name, dy, w, K, k_blk=0, out_dtype=F32):
    M = dy.shape[0]
    S, _, Ns = w.shape
    tm = _row_tile(M, 512)
    tn = _col_tile(Ns, 1408)
    nt = Ns // tn
    steps = S * nt

    def body(dy_ref, w_ref, o_ref, acc_ref):
        j = pl.program_id(1)
        part = lax.dot_general(dy_ref[...].astype(BF16), w_ref[...], (((1,), (1,)), ((), ())),
                               preferred_element_type=F32)

        @pl.when(j == 0)
        def _():
            acc_ref[...] = part

        @pl.when(j > 0)
        def _():
            acc_ref[...] += part

        @pl.when(j == steps - 1)
        def _():
            o_ref[...] = acc_ref[...].astype(o_ref.dtype)

    return pl.pallas_call(
        body, name=name, grid=(M // tm, steps),
        in_specs=[pl.BlockSpec((tm, tn), lambda i, j: (i, j)),
                  pl.BlockSpec((None, K, tn), lambda i, j: (j // nt, k_blk, j % nt))],
        out_specs=pl.BlockSpec((tm, K), lambda i, j: (i, 0)),
        out_shape=jax.ShapeDtypeStruct((M, K), out_dtype),
        scratch_shapes=[pltpu.VMEM((tm, K), F32)],
        compiler_params=_cparams("parallel", "arbitrary"),
    )(dy, w)


def _mm_tn(name, a, dy, S, out_dtype=BF16):
    T, K = a.shape
    Ns = dy.shape[1] // S
    tn = _col_tile(Ns, max(128, (6 * 1024 * 1024) // (4 * K) // 128 * 128))
    nt = Ns // tn
    tt = _row_tile(T, 512)
    steps = T // tt

    def body(a_ref, dy_ref, o_ref, acc_ref):
        t = pl.program_id(1)
        part = lax.dot_general(a_ref[...].astype(BF16), dy_ref[...].astype(BF16), (((0,), (0,)), ((), ())),
                               preferred_element_type=F32)

        @pl.when(t == 0)
        def _():
            acc_ref[...] = part

        @pl.when(t > 0)
        def _():
            acc_ref[...] += part

        @pl.when(t == steps - 1)
        def _():
            o_ref[...] = acc_ref[...].astype(o_ref.dtype)

    return pl.pallas_call(
        body, name=name, grid=(S * nt, steps),
        in_specs=[pl.BlockSpec((tt, K), lambda j, t: (t, 0)),
                  pl.BlockSpec((tt, tn), lambda j, t: (t, j))],
        out_specs=pl.BlockSpec((None, K, tn), lambda j, t: (j // nt, 0, j % nt)),
        out_shape=jax.ShapeDtypeStruct((S, K, Ns), out_dtype),
        scratch_shapes=[pltpu.VMEM((K, tn), F32)],
        compiler_params=_cparams("parallel", "arbitrary"),
    )(a, dy)


def _gated_tile(T, width):
    return _row_tile(T, max(8, (2 * 1024 * 1024) // (4 * width) // 8 * 8))


def _glu_fwd(name, vg, h):
    T, D = h.shape
    tm = _gated_tile(T, 2 * D)

    def body(vg_ref, h_ref, o_ref):
        o_ref[...] = h_ref[...] + vg_ref[:, :D] * _sigmoid(vg_ref[:, D:])

    return pl.pallas_call(
        body, name=name, grid=(T // tm,),
        in_specs=[pl.BlockSpec((tm, 2 * D), lambda i: (i, 0)), pl.BlockSpec((tm, D), lambda i: (i, 0))],
        out_specs=pl.BlockSpec((tm, D), lambda i: (i, 0)),
        out_shape=jax.ShapeDtypeStruct((T, D), F32),
        compiler_params=_cparams("parallel"),
    )(vg, h)


def _glu_bwd(name, vg, dout):
    T, D = dout.shape
    tm = _gated_tile(T, 2 * D)

    def body(vg_ref, d_ref, o_ref):
        s = _sigmoid(vg_ref[:, D:])
        d = d_ref[...]
        o_ref[:, :D] = (d * s).astype(o_ref.dtype)
        o_ref[:, D:] = (d * vg_ref[:, :D] * s * (1.0 - s)).astype(o_ref.dtype)

    return pl.pallas_call(
        body, name=name, grid=(T // tm,),
        in_specs=[pl.BlockSpec((tm, 2 * D), lambda i: (i, 0)), pl.BlockSpec((tm, D), lambda i: (i, 0))],
        out_specs=pl.BlockSpec((tm, 2 * D), lambda i: (i, 0)),
        out_shape=jax.ShapeDtypeStruct((T, 2 * D), BF16),
        compiler_params=_cparams("parallel"),
    )(vg, dout)


def _swiglu_fwd(name, gu):
    T, W = gu.shape
    H = W // 2
    tm = _gated_tile(T, W)

    def body(gu_ref, o_ref):
        g = gu_ref[:, :H]
        o_ref[...] = (g * _sigmoid(g) * gu_ref[:, H:]).astype(o_ref.dtype)

    return pl.pallas_call(
        body, name=name, grid=(T // tm,),
        in_specs=[pl.BlockSpec((tm, W), lambda i: (i, 0))],
        out_specs=pl.BlockSpec((tm, H), lambda i: (i, 0)),
        out_shape=jax.ShapeDtypeStruct((T, H), BF16),
        compiler_params=_cparams("parallel"),
    )(gu)


def _swiglu_bwd(name, gu, dmid):
    T, W = gu.shape
    H = W // 2
    tm = _gated_tile(T, W)

    def body(gu_ref, d_ref, o_ref):
        g = gu_ref[:, :H]
        u = gu_ref[:, H:]
        d = d_ref[...]
        s = _sigmoid(g)
        o_ref[:, :H] = (d * u * s * (1.0 + g * (1.0 - s))).astype(o_ref.dtype)
        o_ref[:, H:] = (d * g * s).astype(o_ref.dtype)

    return pl.pallas_call(
        body, name=name, grid=(T // tm,),
        in_specs=[pl.BlockSpec((tm, W), lambda i: (i, 0)), pl.BlockSpec((tm, H), lambda i: (i, 0))],
        out_specs=pl.BlockSpec((tm, W), lambda i: (i, 0)),
        out_shape=jax.ShapeDtypeStruct((T, W), BF16),
        compiler_params=_cparams("parallel"),
    )(gu, dmid)


def _final_loss(name, h, gain, target):
    T, D = h.shape
    tm = X_START
    lead = X_START // tm

    def body(h_ref, g_ref, t_ref, loss_ref, dh_ref, dg_ref):
        i = pl.program_id(0)

        @pl.when(i == 0)
        def _():
            loss_ref[...] = jnp.zeros_like(loss_ref)
            dg_ref[...] = jnp.zeros_like(dg_ref)
            dh_ref[...] = jnp.zeros_like(dh_ref)

        @pl.when(i >= lead)
        def _():
            xv = h_ref[...]
            r = lax.rsqrt(jnp.mean(xv * xv, axis=-1, keepdims=True) + RMS_EPS)
            xh = xv * r
            g = g_ref[...]
            diff = xh * g - t_ref[...]
            loss_ref[...] += 0.5 * jnp.sum(jnp.mean(diff * diff, axis=-1, keepdims=True), axis=0, keepdims=True)
            dout = diff * (1.0 / D)
            dg_ref[...] += jnp.sum(dout * xh, axis=0, keepdims=True)
            dxh = dout * g
            dh_ref[...] = r * (dxh - xh * jnp.mean(dxh * xh, axis=-1, keepdims=True))

    return pl.pallas_call(
        body, name=name, grid=(T // tm,),
        in_specs=[pl.BlockSpec((tm, D), lambda i: (i, 0)), pl.BlockSpec((1, D), lambda i: (0, 0)),
                  pl.BlockSpec((tm, D), lambda i: (jnp.maximum(i - lead, 0), 0))],
        out_specs=[pl.BlockSpec((1, 128), lambda i: (0, 0)), pl.BlockSpec((tm, D), lambda i: (i, 0)),
                   pl.BlockSpec((1, D), lambda i: (0, 0))],
        out_shape=[jax.ShapeDtypeStruct((1, 128), F32), jax.ShapeDtypeStruct((T, D), F32),
                   jax.ShapeDtypeStruct((1, D), F32)],
        compiler_params=_cparams("arbitrary"),
    )(h, gain, target)


def _s5_discretise(a_re, a_im, log_dt, bt_re, bt_im):
    dt = jnp.exp(log_dt)
    mag = jnp.exp(dt * a_re)
    ang = dt * a_im
    abar_re = mag * jnp.cos(ang)
    abar_im = mag * jnp.sin(ang)
    den = a_re * a_re + a_im * a_im
    coef_re = ((abar_re - 1.0) * a_re + abar_im * a_im) / den
    coef_im = (abar_im * a_re - (abar_re - 1.0) * a_im) / den
    bbar_re = coef_re * bt_re - coef_im * bt_im
    bbar_im = coef_re * bt_im + coef_im * bt_re
    return abar_re, abar_im, bbar_re, bbar_im


def _s5_prep_fwd(name, a_re, a_im, log_dt, bt_re, bt_im):
    G, _, P = a_re.shape
    C = bt_re.shape[1]

    def body(ar, ai, ld, br, bi, o_ar, o_ai, o_br, o_bi):
        outs = _s5_discretise(ar[...], ai[...], ld[...], br[...], bi[...])
        for o, v in zip((o_ar, o_ai, o_br, o_bi), outs):
            o[...] = v

    return pl.pallas_call(
        body, name=name,
        out_shape=[jax.ShapeDtypeStruct((G, 1, P), F32)] * 2 + [jax.ShapeDtypeStruct((G, C, P), F32)] * 2,
    )(a_re, a_im, log_dt, bt_re, bt_im)


def _s5_prep_bwd(name, a_re, a_im, log_dt, bt_re, bt_im, d_ar, d_ai, d_br, d_bi):
    G, _, P = a_re.shape
    C = bt_re.shape[1]

    def body(ar, ai, ld, br, bi, gar, gai, gbr, gbi, o_ar, o_ai, o_ld, o_br, o_bi):
        _, vjp = jax.vjp(_s5_discretise, ar[...], ai[...], ld[...], br[...], bi[...])
        grads = vjp((gar[...], gai[...], gbr[...], gbi[...]))
        for o, v in zip((o_ar, o_ai, o_ld, o_br, o_bi), grads):
            o[...] = v

    return pl.pallas_call(
        body, name=name,
        out_shape=[jax.ShapeDtypeStruct((G, 1, P), F32)] * 2 + [jax.ShapeDtypeStruct((G, 1, 1), F32)]
        + [jax.ShapeDtypeStruct((G, C, P), F32)] * 2,
    )(a_re, a_im, log_dt, bt_re, bt_im, d_ar, d_ai, d_br, d_bi)


def _cmul(ar, ai, br, bi):
    return ar * br - ai * bi, ar * bi + ai * br


def _power_table(a_re, a_im):
    rows_re, rows_im = [a_re], [a_im]
    for _ in range(7):
        r, m = _cmul(rows_re[-1], rows_im[-1], a_re, a_im)
        rows_re.append(r)
        rows_im.append(m)
    row = lax.broadcasted_iota(jnp.int32, (8, a_re.shape[1]), 0)
    t_re = jnp.zeros((8, a_re.shape[1]), F32)
    t_im = jnp.zeros((8, a_re.shape[1]), F32)
    for k in range(8):
        t_re = jnp.where(row == k, rows_re[k], t_re)
        t_im = jnp.where(row == k, rows_im[k], t_im)
    return t_re, t_im, rows_re, rows_im


def _s5_fwd(name, u, b_re, b_im, ct_re, ct_im, abar_re, abar_im, d_skip):
    T, D = u.shape
    n_st = D // CH_TILE
    W = STATE_TILE
    tc = _row_tile(T, 512)
    n_tiles = tc // 8

    def body(u_ref, bre_ref, bim_ref, cre_ref, cim_ref, ar_ref, ai_ref, d_ref,
             y_ref, z_ref, xr_ref, xi_ref, carry_re, carry_im, pw_re, pw_im, sh_re, sh_im):
        c = pl.program_id(1)

        @pl.when(c == 0)
        def _():
            t_re, t_im, rows_re, rows_im = _power_table(ar_ref[...], ai_ref[...])
            pw_re[...] = t_re
            pw_im[...] = t_im
            for n, k in enumerate((0, 1, 3)):
                sh_re[n] = jnp.broadcast_to(rows_re[k], (8, W))
                sh_im[n] = jnp.broadcast_to(rows_im[k], (8, W))
            carry_re[...] = jnp.zeros_like(carry_re)
            carry_im[...] = jnp.zeros_like(carry_im)

        ub = u_ref[...].astype(BF16)
        xr_ref[...] = jnp.dot(ub, bre_ref[...], preferred_element_type=F32)
        xi_ref[...] = jnp.dot(ub, bim_ref[...], preferred_element_type=F32)
        row = lax.broadcasted_iota(jnp.int32, (8, W), 0)

        def tile(i, carry):
            c_re, c_im = carry
            rows = pl.ds(pl.multiple_of(i * 8, 8), 8)
            r = xr_ref[rows, :]
            m = xi_ref[rows, :]
            for n, d in enumerate((1, 2, 4)):
                pr, pm = _cmul(sh_re[n], sh_im[n], pltpu.roll(r, d, 0), pltpu.roll(m, d, 0))
                r = r + jnp.where(row >= d, pr, 0.0)
                m = m + jnp.where(row >= d, pm, 0.0)
            pr, pm = _cmul(pw_re[...], pw_im[...], c_re, c_im)
            r = r + pr
            m = m + pm
            xr_ref[rows, :] = r
            xi_ref[rows, :] = m
            return jnp.broadcast_to(r[7:8, :], (8, W)), jnp.broadcast_to(m[7:8, :], (8, W))

        c_re, c_im = lax.fori_loop(0, n_tiles, tile, (carry_re[...], carry_im[...]))
        carry_re[...] = c_re
        carry_im[...] = c_im
        y = (jnp.dot(xr_ref[...].astype(BF16), cre_ref[...], preferred_element_type=F32)
             - jnp.dot(xi_ref[...].astype(BF16), cim_ref[...], preferred_element_type=F32)
             + d_ref[...] * u_ref[...])
        y_ref[...] = y
        z_ref[...] = _gelu(y).astype(z_ref.dtype)

    ch = pl.BlockSpec((tc, CH_TILE), lambda s, c: (c, s))
    st = pl.BlockSpec((tc, W), lambda s, c: (c, s))
    return pl.pallas_call(
        body, name=name, grid=(n_st, T // tc),
        in_specs=[ch,
                  pl.BlockSpec((None, CH_TILE, W), lambda s, c: (s, 0, 0)),
                  pl.BlockSpec((None, CH_TILE, W), lambda s, c: (s, 0, 0)),
                  pl.BlockSpec((None, W, CH_TILE), lambda s, c: (s, 0, 0)),
                  pl.BlockSpec((None, W, CH_TILE), lambda s, c: (s, 0, 0)),
                  pl.BlockSpec((1, W), lambda s, c: (0, s)),
                  pl.BlockSpec((1, W), lambda s, c: (0, s)),
                  pl.BlockSpec((1, CH_TILE), lambda s, c: (0, s))],
        out_specs=[ch, ch, st, st],
        out_shape=[jax.ShapeDtypeStruct((T, D), F32), jax.ShapeDtypeStruct((T, D), BF16),
                   jax.ShapeDtypeStruct((T, 4 * D), F32), jax.ShapeDtypeStruct((T, 4 * D), F32)],
        scratch_shapes=[pltpu.VMEM((8, W), F32), pltpu.VMEM((8, W), F32),
                        pltpu.VMEM((8, W), F32), pltpu.VMEM((8, W), F32),
                        pltpu.VMEM((3, 8, W), F32), pltpu.VMEM((3, 8, W), F32)],
        compiler_params=_cparams("parallel", "arbitrary"),
    )(u, b_re, b_im, ct_re, ct_im, abar_re, abar_im, d_skip)


def _s5_bwd(name, dz, y, u, x_re, x_im, c_re, c_im, bt_re, bt_im, abar_re, abar_im, d_skip):
    T, D = u.shape
    n_st = D // CH_TILE
    W = STATE_TILE
    tc = _row_tile(T, 512)
    n_chunks = T // tc
    n_tiles = tc // 8
    tiles_per_chunk = tc // 8

    def body(dz_ref, y_ref, u_ref, xr_ref, xi_ref, xpr_ref, xpi_ref, cre_ref, cim_ref, btr_ref, bti_ref,
             ar_ref, ai_ref, d_ref,
             du_ref, dd_ref, dbr_ref, dbi_ref, dcr_ref, dci_ref, dar_ref, dai_ref,
             lam_re, lam_im, xe_re, xe_im, carry_re, carry_im, pw_re, pw_im, sh_re, sh_im, acc_ar, acc_ai):
        k = pl.program_id(1)
        first_chunk = k == n_chunks - 1

        @pl.when(k == 0)
        def _():
            t_re, t_im, rows_re, rows_im = _power_table(ar_ref[...], -ai_ref[...])
            row = lax.broadcasted_iota(jnp.int32, (8, W), 0)
            r_re = jnp.zeros((8, W), F32)
            r_im = jnp.zeros((8, W), F32)
            for j in range(8):
                r_re = jnp.where(row == j, rows_re[7 - j], r_re)
                r_im = jnp.where(row == j, rows_im[7 - j], r_im)
            pw_re[...] = r_re
            pw_im[...] = r_im
            for n, j in enumerate((0, 1, 3)):
                sh_re[n] = jnp.broadcast_to(rows_re[j], (8, W))
                sh_im[n] = jnp.broadcast_to(rows_im[j], (8, W))
            carry_re[...] = jnp.zeros_like(carry_re)
            carry_im[...] = jnp.zeros_like(carry_im)
            acc_ar[...] = jnp.zeros_like(acc_ar)
            acc_ai[...] = jnp.zeros_like(acc_ai)
            dd_ref[...] = jnp.zeros_like(dd_ref)
            dbr_ref[...] = jnp.zeros_like(dbr_ref)
            dbi_ref[...] = jnp.zeros_like(dbi_ref)
            dcr_ref[...] = jnp.zeros_like(dcr_ref)
            dci_ref[...] = jnp.zeros_like(dci_ref)

        uv = u_ref[...]
        dy = dz_ref[...] * _gelu_grad(y_ref[...])
        dyb = dy.astype(BF16)
        lam_re[...] = jnp.dot(dyb, cre_ref[...], preferred_element_type=F32)
        lam_im[...] = -jnp.dot(dyb, cim_ref[...], preferred_element_type=F32)
        keep = jnp.where(first_chunk, 0.0, 1.0)
        xe_re[pl.ds(0, 8), :] = xpr_ref[...] * keep
        xe_im[pl.ds(0, 8), :] = xpi_ref[...] * keep
        xe_re[pl.ds(8, tc), :] = xr_ref[...]
        xe_im[pl.ds(8, tc), :] = xi_ref[...]
        row = lax.broadcasted_iota(jnp.int32, (8, W), 0)

        def tile(n, carry):
            c_re, c_im, s_ar, s_ai = carry
            i = n_tiles - 1 - n
            rows = pl.ds(pl.multiple_of(i * 8, 8), 8)
            r = lam_re[rows, :]
            m = lam_im[rows, :]
            for q, d in enumerate((1, 2, 4)):
                pr, pm = _cmul(sh_re[q], sh_im[q], pltpu.roll(r, 8 - d, 0), pltpu.roll(m, 8 - d, 0))
                r = r + jnp.where(row < 8 - d, pr, 0.0)
                m = m + jnp.where(row < 8 - d, pm, 0.0)
            pr, pm = _cmul(pw_re[...], pw_im[...], c_re, c_im)
            r = r + pr
            m = m + pm
            lam_re[rows, :] = r
            lam_im[rows, :] = m
            cur_re = xe_re[pl.ds(pl.multiple_of(i * 8 + 8, 8), 8), :]
            cur_im = xe_im[pl.ds(pl.multiple_of(i * 8 + 8, 8), 8), :]
            bef_re = xe_re[rows, :]
            bef_im = xe_im[rows, :]
            xp_re = jnp.where(row == 0, jnp.broadcast_to(bef_re[7:8, :], (8, W)), pltpu.roll(cur_re, 1, 0))
            xp_im = jnp.where(row == 0, jnp.broadcast_to(bef_im[7:8, :], (8, W)), pltpu.roll(cur_im, 1, 0))
            s_ar = s_ar + r * xp_re + m * xp_im
            s_ai = s_ai + m * xp_re - r * xp_im
            return jnp.broadcast_to(r[0:1, :], (8, W)), jnp.broadcast_to(m[0:1, :], (8, W)), s_ar, s_ai

        c_re, c_im, s_ar, s_ai = lax.fori_loop(
            0, n_tiles, tile, (carry_re[...], carry_im[...], acc_ar[...], acc_ai[...]))
        carry_re[...] = c_re
        carry_im[...] = c_im
        acc_ar[...] = s_ar
        acc_ai[...] = s_ai
        lr = lam_re[...].astype(BF16)
        li = lam_im[...].astype(BF16)
        du_ref[...] = (dy * d_ref[...] + jnp.dot(lr, btr_ref[...], preferred_element_type=F32)
                       + jnp.dot(li, bti_ref[...], preferred_element_type=F32))
        dd_ref[...] += jnp.sum(dy * uv, axis=0, keepdims=True)
        tn_dims = (((0,), (0,)), ((), ()))
        ub = uv.astype(BF16)
        dbr_ref[...] += lax.dot_general(ub, lr, tn_dims, preferred_element_type=F32)
        dbi_ref[...] += lax.dot_general(ub, li, tn_dims, preferred_element_type=F32)
        dcr_ref[...] += lax.dot_general(dyb, xr_ref[...].astype(BF16), tn_dims, preferred_element_type=F32)
        dci_ref[...] -= lax.dot_general(dyb, xi_ref[...].astype(BF16), tn_dims, preferred_element_type=F32)

        @pl.when(first_chunk)
        def _():
            dar_ref[...] = jnp.sum(acc_ar[...], axis=0, keepdims=True)
            dai_ref[...] = jnp.sum(acc_ai[...], axis=0, keepdims=True)

    rev = lambda k: n_chunks - 1 - k
    ch = pl.BlockSpec((tc, CH_TILE), lambda s, k: (rev(k), s))
    st = pl.BlockSpec((tc, W), lambda s, k: (rev(k), s))
    prev = pl.BlockSpec((8, W), lambda s, k: (jnp.maximum(rev(k) * tiles_per_chunk - 1, 0), s))
    mat_cw = pl.BlockSpec((None, CH_TILE, W), lambda s, k: (s, 0, 0))
    mat_wc = pl.BlockSpec((None, W, CH_TILE), lambda s, k: (s, 0, 0))
    vec_w = pl.BlockSpec((1, W), lambda s, k: (0, s))
    vec_c = pl.BlockSpec((1, CH_TILE), lambda s, k: (0, s))
    dense = jax.ShapeDtypeStruct((n_st, CH_TILE, W), F32)
    return pl.pallas_call(
        body, name=name, grid=(n_st, n_chunks),
        in_specs=[ch, ch, ch, st, st, prev, prev, mat_cw, mat_cw, mat_wc, mat_wc, vec_w, vec_w, vec_c],
        out_specs=[ch, vec_c, mat_cw, mat_cw, mat_cw, mat_cw, vec_w, vec_w],
        out_shape=[jax.ShapeDtypeStruct((T, D), F32), jax.ShapeDtypeStruct((1, D), F32), dense, dense, dense, dense,
                   jax.ShapeDtypeStruct((1, 4 * D), F32), jax.ShapeDtypeStruct((1, 4 * D), F32)],
        scratch_shapes=[pltpu.VMEM((tc, W), F32), pltpu.VMEM((tc, W), F32),
                        pltpu.VMEM((tc + 8, W), F32), pltpu.VMEM((tc + 8, W), F32),
                        pltpu.VMEM((8, W), F32), pltpu.VMEM((8, W), F32),
                        pltpu.VMEM((8, W), F32), pltpu.VMEM((8, W), F32),
                        pltpu.VMEM((3, 8, W), F32), pltpu.VMEM((3, 8, W), F32),
                        pltpu.VMEM((8, W), F32), pltpu.VMEM((8, W), F32)],
        compiler_params=_cparams("parallel", "arbitrary"),
    )(dz, y, u, x_re, x_im, x_re, x_im, c_re, c_im, bt_re, bt_im, abar_re, abar_im, d_skip)


NT_DIMS = (((1,), (1,)), ((), ()))
TN_DIMS = (((0,), (0,)), ((), ()))


def _sums_matrix(strictly_later):
    jj = lax.broadcasted_iota(jnp.int32, (KEY_BLOCK, 2 * KEY_BLOCK), 0)
    ss = lax.broadcasted_iota(jnp.int32, (KEY_BLOCK, 2 * KEY_BLOCK), 1)
    tri = (jj > ss) if strictly_later else (jj < ss)
    return (tri | (ss >= KEY_BLOCK)).astype(BF16)


def _sb_scores(q_h, kblk, mask, later):
    z = lax.dot_general(q_h, kblk, NT_DIMS, preferred_element_type=F32)
    lb = jnp.minimum(z, 0.0) - jnp.log1p(jnp.exp(-jnp.abs(z)))
    lm = lb - z
    if mask is not None:
        lm = jnp.where(mask, lm, 0.0)
    hi = lm.astype(BF16)
    lo = (lm - hi.astype(F32)).astype(BF16)
    sums = jnp.dot(hi, later, preferred_element_type=F32) + jnp.dot(lo, later, preferred_element_type=F32)
    return lb, sums


def _sb_mask(q_row0, k_row0, tq):
    tpos = q_row0 + lax.broadcasted_iota(jnp.int32, (tq, KEY_BLOCK), 0)
    spos = k_row0 + lax.broadcasted_iota(jnp.int32, (tq, KEY_BLOCK), 1)
    return (spos < tpos) & (spos >= META_START)


def _key_block_phases(iq, per_q, block, ascending):
    first_diag = iq * per_q

    def run(lo, n, masked):
        def step(i, carry):
            block(lo + i if ascending else lo + n - 1 - i, masked)
            return carry
        lax.fori_loop(0, n, step, 0)

    phases = [(0, jnp.minimum(iq, 1), True), (1, jnp.maximum(first_diag - 1, 0), False), (first_diag, per_q, True)]
    for lo, n, masked in (phases if ascending else phases[::-1]):
        run(lo, n, masked)


def _attn_fwd(name, q, kv):
    T, D = q.shape
    n_hp = D // 128
    tq = _row_tile(T, 512)
    per_q = tq // KEY_BLOCK
    scale = 1.0 / math.sqrt(HEAD_DIM)

    def body(q_ref, k_ref, v_ref, o_ref, l_ref, acc_ref, run_ref):
        iq = pl.program_id(1)
        lane = lax.broadcasted_iota(jnp.int32, (tq, 128), 1)
        qs = q_ref[...] * jnp.asarray(scale, BF16)
        q_heads = [jnp.where(lane < HEAD_DIM, qs, jnp.zeros_like(qs)), jnp.where(lane >= HEAD_DIM, qs, jnp.zeros_like(qs))]
        later = _sums_matrix(True)
        acc_ref[...] = jnp.zeros_like(acc_ref)
        run_ref[...] = jnp.zeros_like(run_ref)

        def block(kb, masked):
            rows = pl.ds(pl.multiple_of(kb * KEY_BLOCK, KEY_BLOCK), KEY_BLOCK)
            kblk = k_ref[rows, :]
            vblk = v_ref[rows, :]
            mask = _sb_mask(iq * tq, kb * KEY_BLOCK, tq) if masked else None
            for head in range(2):
                lb, sums = _sb_scores(q_heads[head], kblk, mask, later)
                run = run_ref[head]
                w = jnp.exp(lb + sums[:, :KEY_BLOCK] + run)
                if masked:
                    w = jnp.where(mask, w, 0.0)
                acc_ref[head] += jnp.dot(w.astype(BF16), vblk, preferred_element_type=F32)
                run_ref[head] = run + sums[:, KEY_BLOCK:]

        _key_block_phases(iq, per_q, block, ascending=False)
        o_ref[...] = jnp.where(lane < HEAD_DIM, acc_ref[0], acc_ref[1]).astype(o_ref.dtype)
        l_ref[...] = jnp.where(lane < HEAD_DIM, run_ref[0], run_ref[1])

    blk = pl.BlockSpec((tq, 128), lambda h, i: (i, h))
    return pl.pallas_call(
        body, name=name, grid=(n_hp, T // tq),
        in_specs=[blk, pl.BlockSpec((T, 128), lambda h, i: (0, h)), pl.BlockSpec((T, 128), lambda h, i: (0, n_hp + h))],
        out_specs=[blk, blk],
        out_shape=[jax.ShapeDtypeStruct((T, D), BF16), jax.ShapeDtypeStruct((T, D), F32)],
        scratch_shapes=[pltpu.VMEM((2, tq, 128), F32), pltpu.VMEM((2, tq, 128), F32)],
        compiler_params=_cparams("parallel", "arbitrary"),
    )(q, kv, kv)


def _attn_bwd(name, q, kv, do, ltot):
    T, D = q.shape
    n_hp = D // 128
    tq = _row_tile(T, 512)
    n_q = T // tq
    per_q = tq // KEY_BLOCK
    scale = 1.0 / math.sqrt(HEAD_DIM)

    def body(q_ref, k_ref, v_ref, do_ref, l_ref, dq_ref, dk_ref, dv_ref, dk_acc, dv_acc, dq_acc, lpre_ref, cpre_ref):
        iq = pl.program_id(1)

        @pl.when(iq == 0)
        def _():
            dk_acc[...] = jnp.zeros_like(dk_acc)
            dv_acc[...] = jnp.zeros_like(dv_acc)

        lane = lax.broadcasted_iota(jnp.int32, (tq, 128), 1)
        first = lane < HEAD_DIM
        qs = q_ref[...] * jnp.asarray(scale, BF16)
        dov = do_ref[...]
        ltv = l_ref[...]
        swapped = pltpu.roll(ltv, HEAD_DIM, 1)
        q_heads = [jnp.where(first, qs, jnp.zeros_like(qs)), jnp.where(first, jnp.zeros_like(qs), qs)]
        do_heads = [jnp.where(first, dov, jnp.zeros_like(dov)), jnp.where(first, jnp.zeros_like(dov), dov)]
        ltot = [jnp.where(first, ltv, swapped), jnp.where(first, swapped, ltv)]
        later = _sums_matrix(True)
        earlier = _sums_matrix(False)
        dq_acc[...] = jnp.zeros_like(dq_acc)
        lpre_ref[...] = jnp.zeros_like(lpre_ref)
        cpre_ref[...] = jnp.zeros_like(cpre_ref)

        def block(kb, masked):
            rows = pl.ds(pl.multiple_of(kb * KEY_BLOCK, KEY_BLOCK), KEY_BLOCK)
            kblk = k_ref[rows, :]
            vblk = v_ref[rows, :]
            mask = _sb_mask(iq * tq, kb * KEY_BLOCK, tq) if masked else None
            for head in range(2):
                lb, sums = _sb_scores(q_heads[head], kblk, mask, later)
                lpre = lpre_ref[head]
                after = ltot[head] - lpre - sums[:, KEY_BLOCK:]
                w = jnp.exp(lb + sums[:, :KEY_BLOCK] + after)
                if masked:
                    w = jnp.where(mask, w, 0.0)
                dw = lax.dot_general(do_heads[head], vblk, NT_DIMS, preferred_element_type=F32)
                da = w * dw
                dsums = jnp.dot(da.astype(BF16), earlier, preferred_element_type=F32)
                cpre = cpre_ref[head]
                sig = jnp.exp(lb)
                through_later = sig * (dsums[:, :KEY_BLOCK] + cpre)
                if masked:
                    through_later = jnp.where(mask, through_later, 0.0)
                dz = (da * (1.0 - sig) - through_later).astype(BF16)
                dq_acc[head] += jnp.dot(dz, kblk, preferred_element_type=F32)
                dk_acc[rows, :] += lax.dot_general(dz, q_heads[head], TN_DIMS, preferred_element_type=F32)
                dv_acc[rows, :] += lax.dot_general(w.astype(BF16), do_heads[head], TN_DIMS, preferred_element_type=F32)
                lpre_ref[head] = lpre + sums[:, KEY_BLOCK:]
                cpre_ref[head] = cpre + dsums[:, KEY_BLOCK:]

        _key_block_phases(iq, per_q, block, ascending=True)
        dq_ref[...] = (jnp.where(first, dq_acc[0], dq_acc[1]) * scale).astype(dq_ref.dtype)

        @pl.when(iq == n_q - 1)
        def _():
            dk_ref[...] = dk_acc[...].astype(dk_ref.dtype)
            dv_ref[...] = dv_acc[...].astype(dv_ref.dtype)

    blk = pl.BlockSpec((tq, 128), lambda h, i: (i, h))
    full = pl.BlockSpec((T, 128), lambda h, i: (0, h))
    return pl.pallas_call(
        body, name=name, grid=(n_hp, n_q),
        in_specs=[blk, full, pl.BlockSpec((T, 128), lambda h, i: (0, n_hp + h)), blk, blk],
        out_specs=[blk, full, full],
        out_shape=[jax.ShapeDtypeStruct((T, D), BF16)] * 3,
        scratch_shapes=[pltpu.VMEM((T, 128), F32), pltpu.VMEM((T, 128), F32), pltpu.VMEM((2, tq, 128), F32),
                        pltpu.VMEM((2, tq, 128), F32), pltpu.VMEM((2, tq, 128), F32)],
        compiler_params=_cparams("parallel", "arbitrary"),
    )(q, kv, kv, do, ltot)


def _adamw(name, w, g, m, v):
    shape = w.shape
    size = w.size
    cols = 1024 if size % 1024 == 0 else shape[-1]
    rows = size // cols
    tm = _row_tile(rows, 512) if rows % 8 == 0 else rows
    c1 = 1.0 / (1.0 - ADAM_B1 ** ADAM_STEP)
    c2 = 1.0 / (1.0 - ADAM_B2 ** ADAM_STEP)

    def body(w_ref, g_ref, m_ref, v_ref, d_ref, nm_ref, nv_ref):
        gv = g_ref[...]
        nm = ADAM_B1 * m_ref[...] + (1.0 - ADAM_B1) * gv
        nv = ADAM_B2 * v_ref[...] + (1.0 - ADAM_B2) * (gv * gv)
        d_ref[...] = -ADAM_LR * ((nm * c1) / (jnp.sqrt(nv * c2) + ADAM_EPS) + ADAM_WD * w_ref[...])
        nm_ref[...] = nm
        nv_ref[...] = nv

    blk = pl.BlockSpec((tm, cols), lambda i: (i, 0))
    outs = pl.pallas_call(
        body, name=name, grid=(rows // tm,),
        in_specs=[blk] * 4, out_specs=[blk] * 3,
        out_shape=[jax.ShapeDtypeStruct((rows, cols), F32)] * 3,
        compiler_params=_cparams("parallel"),
    )(*[t.reshape(rows, cols) for t in (w, g, m, v)])
    return tuple(o.reshape(shape) for o in outs)


def _any_specs(n):
    return [pl.BlockSpec(memory_space=pl.ANY)] * n


def _place():
    x, y, c = lax.axis_index("x"), lax.axis_index("y"), lax.axis_index("c")
    chips = [(1 - x, y), (x, 1 - y), (1 - x, 1 - y)]
    return x, y, c, chips


def _all_gather_chips(name, shards):
    n = len(shards)

    def body(*refs):
        x_refs, o_refs = refs[:n], refs[n:2 * n]
        send_sems, recv_sems, local_sems = refs[2 * n:]
        x, y, c, chips = _place()
        me = 2 * x + y
        sibling = (x, y, 1 - c)

        def half(ref, i, which):
            h = shards[i].shape[0] // 2
            return ref.at[pl.ds(which * h, h)]

        def remote(k, i, src, dst, to):
            return pltpu.make_async_remote_copy(src_ref=src, dst_ref=dst, send_sem=send_sems.at[k, i],
                                                recv_sem=recv_sems.at[k, i], device_id=to, device_id_type=MESH)

        local = [pltpu.make_async_copy(x_refs[i], o_refs[i].at[me], local_sems.at[i]) for i in range(n)]
        for cp in local:
            cp.start()
        sent = []
        for j, chip in enumerate(chips):
            for i in range(n):
                cp = remote(j, i, half(x_refs[i], i, c), half(o_refs[i].at[me], i, c), (*chip, c))
                cp.start()
                sent.append(cp)
        for j, chip in enumerate(chips):
            pj = 2 * chip[0] + chip[1]
            for i in range(n):
                landed = half(o_refs[i].at[pj], i, c)
                remote(j, i, landed, landed, (*chip, c)).wait_recv()
                cp = remote(3 + j, i, landed, landed, sibling)
                cp.start()
                sent.append(cp)
        for j, chip in enumerate(chips):
            pj = 2 * chip[0] + chip[1]
            for i in range(n):
                got = half(o_refs[i].at[pj], i, 1 - c)
                remote(3 + j, i, got, got, sibling).wait_recv()
        for cp in sent:
            cp.wait_send()
        for cp in local:
            cp.wait()

    return pl.pallas_call(
        body, name=name,
        in_specs=_any_specs(n), out_specs=_any_specs(n),
        out_shape=[jax.ShapeDtypeStruct((4,) + s.shape, s.dtype) for s in shards],
        scratch_shapes=[pltpu.SemaphoreType.DMA((6, n)), pltpu.SemaphoreType.DMA((6, n)),
                        pltpu.SemaphoreType.DMA((n,))],
    )(*shards)


def _pair_split(name, grads):
    n = len(grads)

    def body(*refs):
        g_refs, kept_refs, got_refs = refs[:n], refs[n:2 * n], refs[2 * n:3 * n]
        send_sems, recv_sems, local_sems = refs[3 * n:]
        x, y, c, _ = _place()
        sibling = (x, y, 1 - c)
        local, sent = [], []
        for i in range(n):
            h = grads[i].shape[1] // 2
            cp = pltpu.make_async_copy(g_refs[i].at[:, pl.ds(c * h, h)], kept_refs[i], local_sems.at[i])
            cp.start()
            local.append(cp)
            rc = pltpu.make_async_remote_copy(
                src_ref=g_refs[i].at[:, pl.ds((1 - c) * h, h)], dst_ref=got_refs[i],
                send_sem=send_sems.at[i], recv_sem=recv_sems.at[i], device_id=sibling, device_id_type=MESH)
            rc.start()
            sent.append(rc)
        for rc in sent:
            rc.wait()
        for cp in local:
            cp.wait()

    halves = [jax.ShapeDtypeStruct((4, g.shape[1] // 2, g.shape[2]), g.dtype) for g in grads]
    outs = pl.pallas_call(
        body, name=name,
        in_specs=_any_specs(n), out_specs=_any_specs(2 * n), out_shape=halves + halves,
        scratch_shapes=[pltpu.SemaphoreType.DMA((n,)), pltpu.SemaphoreType.DMA((n,)), pltpu.SemaphoreType.DMA((n,))],
    )(*grads)
    return outs[:n], outs[n:]


def _chip_exchange(name, sums):
    n = len(sums)

    def body(*refs):
        s_refs, o_refs = refs[:n], refs[n:2 * n]
        send_sems, recv_sems, local_sems = refs[2 * n:]
        x, y, c, chips = _place()
        me = 2 * x + y
        local, sent = [], []
        for i in range(n):
            cp = pltpu.make_async_copy(s_refs[i].at[me], o_refs[i].at[me], local_sems.at[i])
            cp.start()
            local.append(cp)
        for j, chip in enumerate(chips):
            pj = 2 * chip[0] + chip[1]
            for i in range(n):
                rc = pltpu.make_async_remote_copy(
                    src_ref=s_refs[i].at[pj], dst_ref=o_refs[i].at[me],
                    send_sem=send_sems.at[j, i], recv_sem=recv_sems.at[j, i],
                    device_id=(*chip, c), device_id_type=MESH)
                rc.start()
                sent.append(rc)
        for j, chip in enumerate(chips):
            pj = 2 * chip[0] + chip[1]
            for i in range(n):
                pltpu.make_async_remote_copy(
                    src_ref=s_refs[i].at[pj], dst_ref=o_refs[i].at[pj],
                    send_sem=send_sems.at[j, i], recv_sem=recv_sems.at[j, i],
                    device_id=(*chip, c), device_id_type=MESH).wait_recv()
        for rc in sent:
            rc.wait_send()
        for cp in local:
            cp.wait()

    return pl.pallas_call(
        body, name=name,
        in_specs=_any_specs(n), out_specs=_any_specs(n),
        out_shape=[jax.ShapeDtypeStruct(s.shape, s.dtype) for s in sums],
        scratch_shapes=[pltpu.SemaphoreType.DMA((3, n)), pltpu.SemaphoreType.DMA((3, n)),
                        pltpu.SemaphoreType.DMA((n,))],
    )(*sums)


def _pair_join(name, halves):
    n = len(halves)

    def body(*refs):
        h_refs, o_refs = refs[:n], refs[n:2 * n]
        send_sems, recv_sems, local_sems = refs[2 * n:]
        x, y, c, _ = _place()
        sibling = (x, y, 1 - c)
        local, sent = [], []
        for i in range(n):
            h = halves[i].shape[0]
            mine = o_refs[i].at[pl.ds(c * h, h)]
            cp = pltpu.make_async_copy(h_refs[i], mine, local_sems.at[i])
            cp.start()
            local.append(cp)
            rc = pltpu.make_async_remote_copy(
                src_ref=h_refs[i], dst_ref=mine, send_sem=send_sems.at[i], recv_sem=recv_sems.at[i],
                device_id=sibling, device_id_type=MESH)
            rc.start()
            sent.append(rc)
        for i in range(n):
            h = halves[i].shape[0]
            theirs = o_refs[i].at[pl.ds((1 - c) * h, h)]
            pltpu.make_async_remote_copy(
                src_ref=h_refs[i], dst_ref=theirs, send_sem=send_sems.at[i], recv_sem=recv_sems.at[i],
                device_id=sibling, device_id_type=MESH).wait_recv()
        for rc in sent:
            rc.wait_send()
        for cp in local:
            cp.wait()

    return pl.pallas_call(
        body, name=name,
        in_specs=_any_specs(n), out_specs=_any_specs(n),
        out_shape=[jax.ShapeDtypeStruct((2 * s.shape[0], s.shape[1]), s.dtype) for s in halves],
        scratch_shapes=[pltpu.SemaphoreType.DMA((n,)), pltpu.SemaphoreType.DMA((n,)), pltpu.SemaphoreType.DMA((n,))],
    )(*halves)


def _add_pair(name, a, b):
    _, H, C = a.shape
    th = _row_tile(H, max(8, (1024 * 1024) // (4 * C) // 8 * 8))

    def body(a_ref, b_ref, o_ref):
        o_ref[...] = (a_ref[...].astype(F32) + b_ref[...].astype(F32)).astype(o_ref.dtype)

    blk = pl.BlockSpec((None, th, C), lambda q, i: (q, i, 0))
    return pl.pallas_call(
        body, name=name, grid=(4, H // th), in_specs=[blk, blk], out_specs=blk,
        out_shape=jax.ShapeDtypeStruct(a.shape, a.dtype), compiler_params=_cparams("parallel", "parallel"),
    )(a, b)


def _add_chips(name, parts):
    _, H, C = parts.shape
    th = _row_tile(H, max(8, (1024 * 1024) // (4 * C) // 8 * 8))

    def body(p_ref, o_ref):
        acc = p_ref[0].astype(F32)
        for q in range(1, 4):
            acc = acc + p_ref[q].astype(F32)
        o_ref[...] = acc

    return pl.pallas_call(
        body, name=name, grid=(H // th,),
        in_specs=[pl.BlockSpec((4, th, C), lambda i: (0, i, 0))], out_specs=pl.BlockSpec((th, C), lambda i: (i, 0)),
        out_shape=jax.ShapeDtypeStruct((H, C), F32), compiler_params=_cparams("parallel"),
    )(parts)


def _reduce_scatter(grads):
    n = len(grads)
    kept, got = _pair_split("rs_pair_split", grads)
    pair = [_add_pair(f"rs_add_pair_{i}", kept[i], got[i]) for i in range(n)]
    parts = _chip_exchange("rs_chip_exchange", pair)
    red = [_add_chips(f"rs_add_chips_{i}", parts[i]) for i in range(n)]
    return _pair_join("rs_pair_join", red)


def _block_diag(t):
    G, A, B = t.shape
    eye = jnp.eye(8, dtype=t.dtype)
    return jnp.einsum("sgab,gh->sgahb", t.reshape(G // 8, 8, A, B), eye).reshape(G // 8, 8 * A, 8 * B)


def _block_diag_extract(m, A, B):
    n = m.shape[0]
    eye = jnp.eye(8, dtype=m.dtype)
    return jnp.einsum("sgahb,gh->sgab", m.reshape(n, 8, A, 8, B), eye).reshape(8 * n, A, B)


def kernel(x, meta_tokens, norm_mix, norm_ffn, s5_a_re, s5_a_im, s5_log_dt, s5_b_re, s5_b_im, s5_c_re, s5_c_im, s5_d, s5_w_glu, norm_kv, w_kv, w_q, w_o, w_ffn_in, w_ffn_out, norm_final, loss_target, m_meta_tokens, m_norm_mix, m_norm_ffn, m_s5_a_re, m_s5_a_im, m_s5_log_dt, m_s5_b_re, m_s5_b_im, m_s5_c_re, m_s5_c_im, m_s5_d, m_s5_w_glu, m_norm_kv, m_w_kv, m_w_q, m_w_o, m_w_ffn_in, m_w_ffn_out, m_norm_final, v_meta_tokens, v_norm_mix, v_norm_ffn, v_s5_a_re, v_s5_a_im, v_s5_log_dt, v_s5_b_re, v_s5_b_im, v_s5_c_re, v_s5_c_im, v_s5_d, v_s5_w_glu, v_norm_kv, v_w_kv, v_w_q, v_w_o, v_w_ffn_in, v_w_ffn_out, v_norm_final):
    seq, D = x.shape[1], x.shape[2]
    T = X_START + seq
    G, P, C = s5_a_re.shape[1], S5_STATE, S5_GROUP
    d_ff = w_ffn_out.shape[1] * 4
    dq4 = D // 4
    chip = 2 * lax.axis_index("x") + lax.axis_index("y")

    small_in = jnp.concatenate([meta_tokens, jnp.pad(s5_d, ((0, 15), (0, 0)))], axis=0)
    (small_all,) = _all_gather_chips("ag_small", [small_in])
    meta_full = small_all[:, :N_META, :].transpose(1, 0, 2).reshape(N_META, D)
    d_skip = small_all[:, N_META, :].reshape(1, D)
    shards = [s5_w_glu[0], w_kv, w_q[0], w_o[0],
              w_ffn_in.reshape(2 * D, -1), w_ffn_out.reshape(-1, D)]
    wg_glu, wg_kv, wg_q, wg_o, wg_in, wg_out = _all_gather_chips("ag_weights", [s.astype(BF16) for s in shards])
    wg_q = wg_q.reshape(1, D, D)
    wg_o = wg_o.reshape(1, D, D)
    rows_out = d_ff // 4
    wg_out = [wg_out[:, l * rows_out:(l + 1) * rows_out, :].reshape(1, d_ff, D) for l in range(2)]

    row = lambda v: v.reshape(1, -1)
    g_mix0, g_mix1 = row(norm_mix[0]), row(norm_mix[1])
    g_ffn = [row(norm_ffn[0]), row(norm_ffn[1])]
    g_kv, g_final = row(norm_kv), row(norm_final)

    a_re3 = s5_a_re[0].reshape(G, 1, P)
    a_im3 = s5_a_im[0].reshape(G, 1, P)
    log_dt3 = s5_log_dt[0].reshape(G, 1, 1)
    bt_re = s5_b_re[0].transpose(0, 2, 1)
    bt_im = s5_b_im[0].transpose(0, 2, 1)
    ab_re, ab_im, bb_re, bb_im = _s5_prep_fwd("s5_prep", a_re3, a_im3, log_dt3, bt_re, bt_im)
    abar_re, abar_im = ab_re.reshape(1, G * P), ab_im.reshape(1, G * P)
    bd_b_re = _block_diag(bb_re).astype(BF16)
    bd_b_im = _block_diag(bb_im).astype(BF16)
    bd_bt_re = bd_b_re.transpose(0, 2, 1)
    bd_bt_im = bd_b_im.transpose(0, 2, 1)
    bd_c_re = _block_diag(s5_c_re[0]).astype(BF16)
    bd_c_im = _block_diag(s5_c_im[0]).astype(BF16)
    bd_ct_re = bd_c_re.transpose(0, 2, 1)
    bd_ct_im = bd_c_im.transpose(0, 2, 1)

    h0 = jnp.concatenate([jnp.zeros((META_START, D), F32), meta_full, x[0]], axis=0)
    (u,) = _rmsnorm_fwd("norm_mix0", h0, [g_mix0], [F32])
    y, z, x_re, x_im = _s5_fwd("s5_scan", u, bd_b_re, bd_b_im, bd_ct_re, bd_ct_im, abar_re, abar_im, d_skip)
    vg = _mm_nn("glu_proj", z, wg_glu)
    h1 = _glu_fwd("glu", vg, h0)

    def ffn_fwd(l, h):
        (n,) = _rmsnorm_fwd(f"norm_ffn{l}", h, [g_ffn[l]], [BF16])
        gu = _mm_nn(f"ffn_in{l}", n, wg_in, k_blk=l)
        mid = _swiglu_fwd(f"swiglu{l}", gu)
        return n, gu, mid, _mm_nn(f"ffn_out{l}", mid, wg_out[l], res=h)

    n1, gu0, mid0, h2 = ffn_fwd(0, h1)
    nk, nq = _rmsnorm_fwd("norm_kv_q", h2, [g_kv, g_mix1], [BF16, BF16])
    kv = _mm_nn("kv_proj", nk, wg_kv, out_dtype=BF16)
    q = _mm_nn("q_proj", nq, wg_q, out_dtype=BF16)
    o, ltot = _attn_fwd("attn_fwd", q, kv)
    h3 = _mm_nn("o_proj", o, wg_o, res=h2)
    n3, gu1, mid1, h4 = ffn_fwd(1, h3)
    loss_part, dh4, dg_final = _final_loss("final_loss", h4, g_final, loss_target[0])
    loss = lax.psum(loss_part[0, 0], ("x", "y", "c"))

    def ffn_bwd(l, dh, h, n, gu, mid):
        dmid = _mm_nt_k(f"ffn_out{l}_dx", dh, wg_out[l], d_ff)
        dgu = _swiglu_bwd(f"swiglu{l}_bwd", gu, dmid)
        dw_out = _mm_tn(f"ffn_out{l}_dw", mid, dh, 1)
        dw_in = _mm_tn(f"ffn_in{l}_dw", n, dgu, 4)
        dn = _mm_nt_k(f"ffn_in{l}_dx", dgu, wg_in, D, k_blk=l)
        dh_prev, (dg,) = _rmsnorm_bwd(f"norm_ffn{l}_bwd", h, [(g_ffn[l], dn)], dh)
        return dh_prev, dg, dw_in, dw_out

    dh3, dg_ffn1, dw_in1, dw_out1 = ffn_bwd(1, dh4, h3, n3, gu1, mid1)
    d_o = _mm_nt_k("o_proj_dx", dh3, wg_o, D, out_dtype=BF16)
    dw_o = _mm_tn("o_proj_dw", o, dh3, 1)
    dq, dk, dv = _attn_bwd("attn_bwd", q, kv, d_o, ltot)
    dkv = jnp.concatenate([dk, dv], axis=1)
    dw_q = _mm_tn("q_proj_dw", nq, dq, 1)
    dnq = _mm_nt_k("q_proj_dx", dq, wg_q, D)
    dw_kv = _mm_tn("kv_proj_dw", nk, dkv, 4)
    dnk = _mm_nt_k("kv_proj_dx", dkv, wg_kv, D)
    dh2, (dg_mix1, dg_kv) = _rmsnorm_bwd("norm_kv_q_bwd", h2, [(g_mix1, dnq), (g_kv, dnk)], dh3)
    dh1, dg_ffn0, dw_in0, dw_out0 = ffn_bwd(0, dh2, h1, n1, gu0, mid0)
    dvg = _glu_bwd("glu_bwd", vg, dh1)
    dw_glu = _mm_tn("glu_proj_dw", z, dvg, 4)
    dz = _mm_nt_k("glu_proj_dx", dvg, wg_glu, D)
    du, dd, dbd_b_re, dbd_b_im, dbd_c_re, dbd_c_im, dab_re, dab_im = _s5_bwd(
        "s5_scan_bwd", dz, y, u, x_re, x_im, bd_c_re, bd_c_im, bd_bt_re, bd_bt_im, abar_re, abar_im, d_skip)
    dh0, (dg_mix0,) = _rmsnorm_bwd("norm_mix0_bwd", h0, [(g_mix0, du)], dh1)
    da_re, da_im, dlog_dt, dbt_re, dbt_im = _s5_prep_bwd(
        "s5_prep_bwd", a_re3, a_im3, log_dt3, bt_re, bt_im,
        dab_re.reshape(G, 1, P), dab_im.reshape(G, 1, P),
        _block_diag_extract(dbd_b_re, C, P), _block_diag_extract(dbd_b_im, C, P))
    grad_x = dh0[X_START:][None]

    small_parts = [
        dg_mix0, dg_mix1, dg_ffn0, dg_ffn1, da_re, da_im,
        dbt_re.transpose(0, 2, 1), dbt_im.transpose(0, 2, 1),
        _block_diag_extract(dbd_c_re, C, P), _block_diag_extract(dbd_c_im, C, P),
        dg_kv, dg_final, dh0[META_START:X_START], dd, dlog_dt]
    small_sizes = [p.size for p in small_parts]
    unit = 4 * 2 * 8 * 128
    padded = -(-sum(small_sizes) // unit) * unit
    small_flat = jnp.concatenate(
        [p.reshape(-1) for p in small_parts[:-1]]
        + [jnp.pad(dlog_dt.reshape(-1), (0, padded - sum(small_sizes)))])
    small_blocks = small_flat.reshape(4, padded // (4 * 128), 128)
    big = [dw_glu, dw_kv, dw_q.reshape(4, D // 4, D), dw_o.reshape(4, D // 4, D),
           jnp.stack([dw_in0, dw_in1], axis=1).reshape(4, 2 * D, -1),
           jnp.stack([dw_out0.reshape(4, rows_out, D), dw_out1.reshape(4, rows_out, D)], axis=1).reshape(4, 2 * rows_out, D)]
    red = _reduce_scatter(big + [small_blocks])
    gw_glu, gw_kv, gw_q, gw_o, gw_in, gw_out, small_mine = red
    (small_red,) = _all_gather_chips("ag_small_grads", [small_mine])
    small_red = small_red.reshape(-1)
    pieces, at = [], 0
    for p, size in zip(small_parts, small_sizes):
        pieces.append(small_red[at:at + size].reshape(p.shape))
        at += size
    (gn_mix0, gn_mix1, gn_ffn0, gn_ffn1, ga_re, ga_im, gb_re, gb_im, gc_re, gc_im, gn_kv, gn_final,
     gmeta_full, gd_full, glog_dt) = pieces
    gn_mix = small_red[:2 * D].reshape(2, D)
    gn_ffn = small_red[2 * D:4 * D].reshape(2, D)
    gmeta = lax.dynamic_slice_in_dim(gmeta_full, chip * dq4, dq4, axis=1)
    gd = lax.dynamic_slice_in_dim(gd_full, chip * dq4, dq4, axis=1)

    grads = {
        "meta_tokens": gmeta, "norm_mix": gn_mix, "norm_ffn": gn_ffn,
        "s5_a_re": ga_re.reshape(s5_a_re.shape), "s5_a_im": ga_im.reshape(s5_a_im.shape),
        "s5_log_dt": glog_dt.reshape(s5_log_dt.shape),
        "s5_b_re": gb_re.reshape(s5_b_re.shape), "s5_b_im": gb_im.reshape(s5_b_im.shape),
        "s5_c_re": gc_re.reshape(s5_c_re.shape), "s5_c_im": gc_im.reshape(s5_c_im.shape),
        "s5_d": gd, "s5_w_glu": gw_glu.reshape(s5_w_glu.shape), "norm_kv": gn_kv.reshape(norm_kv.shape),
        "w_kv": gw_kv, "w_q": gw_q.reshape(w_q.shape), "w_o": gw_o.reshape(w_o.shape),
        "w_ffn_in": gw_in.reshape(w_ffn_in.shape), "w_ffn_out": gw_out.reshape(w_ffn_out.shape),
        "norm_final": gn_final.reshape(norm_final.shape),
    }
    weights = {
        "meta_tokens": (meta_tokens, m_meta_tokens, v_meta_tokens), "norm_mix": (norm_mix, m_norm_mix, v_norm_mix),
        "norm_ffn": (norm_ffn, m_norm_ffn, v_norm_ffn), "s5_a_re": (s5_a_re, m_s5_a_re, v_s5_a_re),
        "s5_a_im": (s5_a_im, m_s5_a_im, v_s5_a_im), "s5_log_dt": (s5_log_dt, m_s5_log_dt, v_s5_log_dt),
        "s5_b_re": (s5_b_re, m_s5_b_re, v_s5_b_re), "s5_b_im": (s5_b_im, m_s5_b_im, v_s5_b_im),
        "s5_c_re": (s5_c_re, m_s5_c_re, v_s5_c_re), "s5_c_im": (s5_c_im, m_s5_c_im, v_s5_c_im),
        "s5_d": (s5_d, m_s5_d, v_s5_d), "s5_w_glu": (s5_w_glu, m_s5_w_glu, v_s5_w_glu),
        "norm_kv": (norm_kv, m_norm_kv, v_norm_kv), "w_kv": (w_kv, m_w_kv, v_w_kv), "w_q": (w_q, m_w_q, v_w_q),
        "w_o": (w_o, m_w_o, v_w_o), "w_ffn_in": (w_ffn_in, m_w_ffn_in, v_w_ffn_in),
        "w_ffn_out": (w_ffn_out, m_w_ffn_out, v_w_ffn_out), "norm_final": (norm_final, m_norm_final, v_norm_final),
    }
    names = list(weights)
    deltas, new_m, new_v = [], [], []
    for name in names:
        w, m, v = weights[name]
        d, nm, nv = _adamw(f"adamw_{name}", w, grads[name], m, v)
        deltas.append(d)
        new_m.append(nm)
        new_v.append(nv)
    return (loss, grad_x, *[grads[n] for n in names], *deltas, *new_m, *new_v)
```

```python
import functools
import math

import jax
import jax.numpy as jnp
from jax import lax
from jax.experimental import pallas as pl
from jax.experimental.pallas import tpu as pltpu

F32 = jnp.float32
BF16 = jnp.bfloat16

N_META = 16
X_START = 128
META_START = X_START - N_META
S5_GROUP = 16
S5_STATE = 64
HEAD_DIM = 64
KEY_BLOCK = 128
STATE_TILE = 512
CH_TILE = 128
RMS_EPS = 1e-6
ADAM_LR, ADAM_B1, ADAM_B2, ADAM_EPS, ADAM_WD, ADAM_STEP = 0.001, 0.9, 0.999, 1e-08, 0.01, 10
VMEM_LIMIT_BYTES = 48 * 1024 * 1024
MESH = pl.DeviceIdType.MESH


def _cparams(*sem):
    return pltpu.CompilerParams(dimension_semantics=sem, vmem_limit_bytes=VMEM_LIMIT_BYTES)


def _row_tile(rows, cap):
    for unit in (128, 8):
        best = 0
        for t in range(unit, min(rows, cap) + 1, unit):
            if rows % t == 0:
                best = t
        if best:
            return best
    return rows


def _col_tile(cols, cap):
    best = 0
    for t in range(128, min(cols, cap) + 1, 128):
        if cols % t == 0:
            best = t
    return best if best else cols


def _gelu(x):
    k = math.sqrt(2.0 / math.pi)
    return 0.5 * x * (1.0 + jnp.tanh(k * (x + 0.044715 * x * x * x)))


def _gelu_grad(x):
    k = math.sqrt(2.0 / math.pi)
    t = jnp.tanh(k * (x + 0.044715 * x * x * x))
    return 0.5 * (1.0 + t) + 0.5 * x * (1.0 - t * t) * k * (1.0 + 3.0 * 0.044715 * x * x)


def _sigmoid(x):
    return 1.0 / (1.0 + jnp.exp(-x))


def _rmsnorm_fwd(name, x, gains, out_dtypes):
    T, D = x.shape
    tm = _row_tile(T, 512)
    n = len(gains)

    def body(x_ref, *refs):
        xv = x_ref[...]
        xh = xv * lax.rsqrt(jnp.mean(xv * xv, axis=-1, keepdims=True) + RMS_EPS)
        for g_ref, o_ref in zip(refs[:n], refs[n:]):
            o_ref[...] = (xh * g_ref[...]).astype(o_ref.dtype)

    row = pl.BlockSpec((tm, D), lambda i: (i, 0))
    vec = pl.BlockSpec((1, D), lambda i: (0, 0))
    return pl.pallas_call(
        body, name=name, grid=(T // tm,),
        in_specs=[row] + [vec] * n, out_specs=[row] * n,
        out_shape=[jax.ShapeDtypeStruct((T, D), dt) for dt in out_dtypes],
        compiler_params=_cparams("parallel"),
    )(x, *gains)


def _rmsnorm_bwd(name, x, pairs, dres):
    T, D = x.shape
    tm = _row_tile(T, 256)
    n = len(pairs)

    def body(x_ref, dres_ref, *refs):
        g_refs, dy_refs = refs[:n], refs[n:2 * n]
        dx_ref, dg_refs = refs[2 * n], refs[2 * n + 1:]
        i = pl.program_id(0)
        xv = x_ref[...]
        r = lax.rsqrt(jnp.mean(xv * xv, axis=-1, keepdims=True) + RMS_EPS)
        xh = xv * r
        dxh = jnp.zeros_like(xv)
        for g_ref, dy_ref, dg_ref in zip(g_refs, dy_refs, dg_refs):
            dy = dy_ref[...].astype(F32)
            part = jnp.sum(dy * xh, axis=0, keepdims=True)

            @pl.when(i == 0)
            def _():
                dg_ref[...] = part

            @pl.when(i > 0)
            def _():
                dg_ref[...] += part

            dxh = dxh + dy * g_ref[...]
        dx = r * (dxh - xh * jnp.mean(dxh * xh, axis=-1, keepdims=True))
        dx_ref[...] = dres_ref[...] + dx

    row = pl.BlockSpec((tm, D), lambda i: (i, 0))
    vec = pl.BlockSpec((1, D), lambda i: (0, 0))
    outs = pl.pallas_call(
        body, name=name, grid=(T // tm,),
        in_specs=[row, row] + [vec] * n + [row] * n,
        out_specs=[row] + [vec] * n,
        out_shape=[jax.ShapeDtypeStruct((T, D), F32)] + [jax.ShapeDtypeStruct((1, D), F32)] * n,
        compiler_params=_cparams("arbitrary"),
    )(x, dres, *[g for g, _ in pairs], *[dy for _, dy in pairs])
    return outs[0], outs[1:]


def _mm_nn(name, a, w, k_blk=0, res=None, out_dtype=F32):
    M, K = a.shape
    S, _, Ns = w.shape
    tm = _row_tile(M, 512)
    tn = _col_tile(Ns, 1408)
    nt = Ns // tn

    def body(a_ref, w_ref, *refs):
        o_ref = refs[-1]
        acc = jnp.dot(a_ref[...].astype(BF16), w_ref[...], preferred_element_type=F32)
        if res is not None:
            acc = acc + refs[0][...]
        o_ref[...] = acc.astype(o_ref.dtype)

    in_specs = [pl.BlockSpec((tm, K), lambda j, i: (i, 0)),
                pl.BlockSpec((None, K, tn), lambda j, i: (j // nt, k_blk, j % nt))]
    args = [a, w]
    if res is not None:
        in_specs.append(pl.BlockSpec((tm, tn), lambda j, i: (i, j)))
        args.append(res)
    return pl.pallas_call(
        body, name=name, grid=(S * nt, M // tm),
        in_specs=in_specs, out_specs=pl.BlockSpec((tm, tn), lambda j, i: (i, j)),
        out_shape=jax.ShapeDtypeStruct((M, S * Ns), out_dtype),
        compiler_params=_cparams("parallel", "parallel"),
    )(*args)


def _mm_nt_k(name, dy, w, K, k_blk=0, out_dtype=F32):
    M = dy.shape[0]
    S, _, Ns = w.shape
    tm = _row_tile(M, 512)
    tn = _col_tile(Ns, 1408)
    nt = Ns // tn
    steps = S * nt

    def body(dy_ref, w_ref, o_ref, acc_ref):
        j = pl.program_id(1)
        part = lax.dot_general(dy_ref[...].astype(BF16), w_ref[...], (((1,), (1,)), ((), ())),
                               preferred_element_type=F32)

        @pl.when(j == 0)
        def _():
            acc_ref[...] = part

        @pl.when(j > 0)
        def _():
            acc_ref[...] += part

        @pl.when(j == steps - 1)
        def _():
            o_ref[...] = acc_ref[...].astype(o_ref.dtype)

    return pl.pallas_call(
        body, name=name, grid=(M // tm, steps),
        in_specs=[pl.BlockSpec((tm, tn), lambda i, j: (i, j)),
                  pl.BlockSpec((None, K, tn), lambda i, j: (j // nt, k_blk, j % nt))],
        out_specs=pl.BlockSpec((tm, K), lambda i, j: (i, 0)),
        out_shape=jax.ShapeDtypeStruct((M, K), out_dtype),
        scratch_shapes=[pltpu.VMEM((tm, K), F32)],
        compiler_params=_cparams("parallel", "arbitrary"),
    )(dy, w)


def _mm_tn(name, a, dy, S, out_dtype=BF16):
    T, K = a.shape
    Ns = dy.shape[1] // S
    tn = _col_tile(Ns, max(128, (6 * 1024 * 1024) // (4 * K) // 128 * 128))
    nt = Ns // tn
    tt = _row_tile(T, 512)
    steps = T // tt

    def body(a_ref, dy_ref, o_ref, acc_ref):
        t = pl.program_id(1)
        part = lax.dot_general(a_ref[...].astype(BF16), dy_ref[...].astype(BF16), (((0,), (0,)), ((), ())),
                               preferred_element_type=F32)

        @pl.when(t == 0)
        def _():
            acc_ref[...] = part

        @pl.when(t > 0)
        def _():
            acc_ref[...] += part

        @pl.when(t == steps - 1)
        def _():
            o_ref[...] = acc_ref[...].astype(o_ref.dtype)

    return pl.pallas_call(
        body, name=name, grid=(S * nt, steps),
        in_specs=[pl.BlockSpec((tt, K), lambda j, t: (t, 0)),
                  pl.BlockSpec((tt, tn), lambda j, t: (t, j))],
        out_specs=pl.BlockSpec((None, K, tn), lambda j, t: (j // nt, 0, j % nt)),
        out_shape=jax.ShapeDtypeStruct((S, K, Ns), out_dtype),
        scratch_shapes=[pltpu.VMEM((K, tn), F32)],
        compiler_params=_cparams("parallel", "arbitrary"),
    )(a, dy)


def _gated_tile(T, width):
    return _row_tile(T, max(8, (2 * 1024 * 1024) // (4 * width) // 8 * 8))


def _glu_fwd(name, vg, h):
    T, D = h.shape
    tm = _gated_tile(T, 2 * D)

    def body(vg_ref, h_ref, o_ref):
        o_ref[...] = h_ref[...] + vg_ref[:, :D] * _sigmoid(vg_ref[:, D:])

    return pl.pallas_call(
        body, name=name, grid=(T // tm,),
        in_specs=[pl.BlockSpec((tm, 2 * D), lambda i: (i, 0)), pl.BlockSpec((tm, D), lambda i: (i, 0))],
        out_specs=pl.BlockSpec((tm, D), lambda i: (i, 0)),
        out_shape=jax.ShapeDtypeStruct((T, D), F32),
        compiler_params=_cparams("parallel"),
    )(vg, h)


def _glu_bwd(name, vg, dout):
    T, D = dout.shape
    tm = _gated_tile(T, 2 * D)

    def body(vg_ref, d_ref, o_ref):
        s = _sigmoid(vg_ref[:, D:])
        d = d_ref[...]
        o_ref[:, :D] = (d * s).astype(o_ref.dtype)
        o_ref[:, D:] = (d * vg_ref[:, :D] * s * (1.0 - s)).astype(o_ref.dtype)

    return pl.pallas_call(
        body, name=name, grid=(T // tm,),
        in_specs=[pl.BlockSpec((tm, 2 * D), lambda i: (i, 0)), pl.BlockSpec((tm, D), lambda i: (i, 0))],
        out_specs=pl.BlockSpec((tm, 2 * D), lambda i: (i, 0)),
        out_shape=jax.ShapeDtypeStruct((T, 2 * D), BF16),
        compiler_params=_cparams("parallel"),
    )(vg, dout)


def _swiglu_fwd(name, gu):
    T, W = gu.shape
    H = W // 2
    tm = _gated_tile(T, W)

    def body(gu_ref, o_ref):
        g = gu_ref[:, :H]
        o_ref[...] = (g * _sigmoid(g) * gu_ref[:, H:]).astype(o_ref.dtype)

    return pl.pallas_call(
        body, name=name, grid=(T // tm,),
        in_specs=[pl.BlockSpec((tm, W), lambda i: (i, 0))],
        out_specs=pl.BlockSpec((tm, H), lambda i: (i, 0)),
        out_shape=jax.ShapeDtypeStruct((T, H), BF16),
        compiler_params=_cparams("parallel"),
    )(gu)


def _swiglu_bwd(name, gu, dmid):
    T, W = gu.shape
    H = W // 2
    tm = _gated_tile(T, W)

    def body(gu_ref, d_ref, o_ref):
        g = gu_ref[:, :H]
        u = gu_ref[:, H:]
        d = d_ref[...]
        s = _sigmoid(g)
        o_ref[:, :H] = (d * u * s * (1.0 + g * (1.0 - s))).astype(o_ref.dtype)
        o_ref[:, H:] = (d * g * s).astype(o_ref.dtype)

    return pl.pallas_call(
        body, name=name, grid=(T // tm,),
        in_specs=[pl.BlockSpec((tm, W), lambda i: (i, 0)), pl.BlockSpec((tm, H), lambda i: (i, 0))],
        out_specs=pl.BlockSpec((tm, W), lambda i: (i, 0)),
        out_shape=jax.ShapeDtypeStruct((T, W), BF16),
        compiler_params=_cparams("parallel"),
    )(gu, dmid)


def _final_loss(name, h, gain, target):
    T, D = h.shape
    tm = X_START
    lead = X_START // tm

    def body(h_ref, g_ref, t_ref, loss_ref, dh_ref, dg_ref):
        i = pl.program_id(0)

        @pl.when(i == 0)
        def _():
            loss_ref[...] = jnp.zeros_like(loss_ref)
            dg_ref[...] = jnp.zeros_like(dg_ref)
            dh_ref[...] = jnp.zeros_like(dh_ref)

        @pl.when(i >= lead)
        def _():
            xv = h_ref[...]
            r = lax.rsqrt(jnp.mean(xv * xv, axis=-1, keepdims=True) + RMS_EPS)
            xh = xv * r
            g = g_ref[...]
            diff = xh * g - t_ref[...]
            loss_ref[...] += 0.5 * jnp.sum(jnp.mean(diff * diff, axis=-1, keepdims=True), axis=0, keepdims=True)
            dout = diff * (1.0 / D)
            dg_ref[...] += jnp.sum(dout * xh, axis=0, keepdims=True)
            dxh = dout * g
            dh_ref[...] = r * (dxh - xh * jnp.mean(dxh * xh, axis=-1, keepdims=True))

    return pl.pallas_call(
        body, name=name, grid=(T // tm,),
        in_specs=[pl.BlockSpec((tm, D), lambda i: (i, 0)), pl.BlockSpec((1, D), lambda i: (0, 0)),
                  pl.BlockSpec((tm, D), lambda i: (jnp.maximum(i - lead, 0), 0))],
        out_specs=[pl.BlockSpec((1, 128), lambda i: (0, 0)), pl.BlockSpec((tm, D), lambda i: (i, 0)),
                   pl.BlockSpec((1, D), lambda i: (0, 0))],
        out_shape=[jax.ShapeDtypeStruct((1, 128), F32), jax.ShapeDtypeStruct((T, D), F32),
                   jax.ShapeDtypeStruct((1, D), F32)],
        compiler_params=_cparams("arbitrary"),
    )(h, gain, target)


def _s5_discretise(a_re, a_im, log_dt, bt_re, bt_im):
    dt = jnp.exp(log_dt)
    mag = jnp.exp(dt * a_re)
    ang = dt * a_im
    abar_re = mag * jnp.cos(ang)
    abar_im = mag * jnp.sin(ang)
    den = a_re * a_re + a_im * a_im
    coef_re = ((abar_re - 1.0) * a_re + abar_im * a_im) / den
    coef_im = (abar_im * a_re - (abar_re - 1.0) * a_im) / den
    bbar_re = coef_re * bt_re - coef_im * bt_im
    bbar_im = coef_re * bt_im + coef_im * bt_re
    return abar_re, abar_im, bbar_re, bbar_im


def _s5_prep_fwd(name, a_re, a_im, log_dt, bt_re, bt_im):
    G, _, P = a_re.shape
    C = bt_re.shape[1]

    def body(ar, ai, ld, br, bi, o_ar, o_ai, o_br, o_bi):
        outs = _s5_discretise(ar[...], ai[...], ld[...], br[...], bi[...])
        for o, v in zip((o_ar, o_ai, o_br, o_bi), outs):
            o[...] = v

    return pl.pallas_call(
        body, name=name,
        out_shape=[jax.ShapeDtypeStruct((G, 1, P), F32)] * 2 + [jax.ShapeDtypeStruct((G, C, P), F32)] * 2,
    )(a_re, a_im, log_dt, bt_re, bt_im)


def _s5_prep_bwd(name, a_re, a_im, log_dt, bt_re, bt_im, d_ar, d_ai, d_br, d_bi):
    G, _, P = a_re.shape
    C = bt_re.shape[1]

    def body(ar, ai, ld, br, bi, gar, gai, gbr, gbi, o_ar, o_ai, o_ld, o_br, o_bi):
        _, vjp = jax.vjp(_s5_discretise, ar[...], ai[...], ld[...], br[...], bi[...])
        grads = vjp((gar[...], gai[...], gbr[...], gbi[...]))
        for o, v in zip((o_ar, o_ai, o_ld, o_br, o_bi), grads):
            o[...] = v

    return pl.pallas_call(
        body, name=name,
        out_shape=[jax.ShapeDtypeStruct((G, 1, P), F32)] * 2 + [jax.ShapeDtypeStruct((G, 1, 1), F32)]
        + [jax.ShapeDtypeStruct((G, C, P), F32)] * 2,
    )(a_re, a_im, log_dt, bt_re, bt_im, d_ar, d_ai, d_br, d_bi)


def _cmul(ar, ai, br, bi):
    return ar * br - ai * bi, ar * bi + ai * br


def _power_table(a_re, a_im):
    rows_re, rows_im = [a_re], [a_im]
    for _ in range(7):
        r, m = _cmul(rows_re[-1], rows_im[-1], a_re, a_im)
        rows_re.append(r)
        rows_im.append(m)
    row = lax.broadcasted_iota(jnp.int32, (8, a_re.shape[1]), 0)
    t_re = jnp.zeros((8, a_re.shape[1]), F32)
    t_im = jnp.zeros((8, a_re.shape[1]), F32)
    for k in range(8):
        t_re = jnp.where(row == k, rows_re[k], t_re)
        t_im = jnp.where(row == k, rows_im[k], t_im)
    return t_re, t_im, rows_re, rows_im


def _s5_fwd(name, u, b_re, b_im, ct_re, ct_im, abar_re, abar_im, d_skip):
    T, D = u.shape
    n_st = D // CH_TILE
    W = STATE_TILE
    tc = _row_tile(T, 512)
    n_tiles = tc // 8

    def body(u_ref, bre_ref, bim_ref, cre_ref, cim_ref, ar_ref, ai_ref, d_ref,
             y_ref, z_ref, xr_ref, xi_ref, carry_re, carry_im, pw_re, pw_im, sh_re, sh_im):
        c = pl.program_id(1)

        @pl.when(c == 0)
        def _():
            t_re, t_im, rows_re, rows_im = _power_table(ar_ref[...], ai_ref[...])
            pw_re[...] = t_re
            pw_im[...] = t_im
            for n, k in enumerate((0, 1, 3)):
                sh_re[n] = jnp.broadcast_to(rows_re[k], (8, W))
                sh_im[n] = jnp.broadcast_to(rows_im[k], (8, W))
            carry_re[...] = jnp.zeros_like(carry_re)
            carry_im[...] = jnp.zeros_like(carry_im)

        ub = u_ref[...].astype(BF16)
        xr_ref[...] = jnp.dot(ub, bre_ref[...], preferred_element_type=F32)
        xi_ref[...] = jnp.dot(ub, bim_ref[...], preferred_element_type=F32)
        row = lax.broadcasted_iota(jnp.int32, (8, W), 0)

        def tile(i, carry):
            c_re, c_im = carry
            rows = pl.ds(pl.multiple_of(i * 8, 8), 8)
            r = xr_ref[rows, :]
            m = xi_ref[rows, :]
            for n, d in enumerate((1, 2, 4)):
                pr, pm = _cmul(sh_re[n], sh_im[n], pltpu.roll(r, d, 0), pltpu.roll(m, d, 0))
                r = r + jnp.where(row >= d, pr, 0.0)
                m = m + jnp.where(row >= d, pm, 0.0)
            pr, pm = _cmul(pw_re[...], pw_im[...], c_re, c_im)
            r = r + pr
            m = m + pm
            xr_ref[rows, :] = r
            xi_ref[rows, :] = m
            return jnp.broadcast_to(r[7:8, :], (8, W)), jnp.broadcast_to(m[7:8, :], (8, W))

        c_re, c_im = lax.fori_loop(0, n_tiles, tile, (carry_re[...], carry_im[...]))
        carry_re[...] = c_re
        carry_im[...] = c_im
        y = (jnp.dot(xr_ref[...].astype(BF16), cre_ref[...], preferred_element_type=F32)
             - jnp.dot(xi_ref[...].astype(BF16), cim_ref[...], preferred_element_type=F32)
             + d_ref[...] * u_ref[...])
        y_ref[...] = y
        z_ref[...] = _gelu(y).astype(z_ref.dtype)

    ch = pl.BlockSpec((tc, CH_TILE), lambda s, c: (c, s))
    st = pl.BlockSpec((tc, W), lambda s, c: (c, s))
    return pl.pallas_call(
        body, name=name, grid=(n_st, T // tc),
        in_specs=[ch,
                  pl.BlockSpec((None, CH_TILE, W), lambda s, c: (s, 0, 0)),
                  pl.BlockSpec((None, CH_TILE, W), lambda s, c: (s, 0, 0)),
                  pl.BlockSpec((None, W, CH_TILE), lambda s, c: (s, 0, 0)),
                  pl.BlockSpec((None, W, CH_TILE), lambda s, c: (s, 0, 0)),
                  pl.BlockSpec((1, W), lambda s, c: (0, s)),
                  pl.BlockSpec((1, W), lambda s, c: (0, s)),
                  pl.BlockSpec((1, CH_TILE), lambda s, c: (0, s))],
        out_specs=[ch, ch, st, st],
        out_shape=[jax.ShapeDtypeStruct((T, D), F32), jax.ShapeDtypeStruct((T, D), BF16),
                   jax.ShapeDtypeStruct((T, 4 * D), F32), jax.ShapeDtypeStruct((T, 4 * D), F32)],
        scratch_shapes=[pltpu.VMEM((8, W), F32), pltpu.VMEM((8, W), F32),
                        pltpu.VMEM((8, W), F32), pltpu.VMEM((8, W), F32),
                        pltpu.VMEM((3, 8, W), F32), pltpu.VMEM((3, 8, W), F32)],
        compiler_params=_cparams("parallel", "arbitrary"),
    )(u, b_re, b_im, ct_re, ct_im, abar_re, abar_im, d_skip)


def _s5_bwd(name, dz, y, u, x_re, x_im, c_re, c_im, bt_re, bt_im, abar_re, abar_im, d_skip):
    T, D = u.shape
    n_st = D // CH_TILE
    W = STATE_TILE
    tc = _row_tile(T, 512)
    n_chunks = T // tc
    n_tiles = tc // 8
    tiles_per_chunk = tc // 8

    def body(dz_ref, y_ref, u_ref, xr_ref, xi_ref, xpr_ref, xpi_ref, cre_ref, cim_ref, btr_ref, bti_ref,
             ar_ref, ai_ref, d_ref,
             du_ref, dd_ref, dbr_ref, dbi_ref, dcr_ref, dci_ref, dar_ref, dai_ref,
             lam_re, lam_im, xe_re, xe_im, carry_re, carry_im, pw_re, pw_im, sh_re, sh_im, acc_ar, acc_ai):
        k = pl.program_id(1)
        first_chunk = k == n_chunks - 1

        @pl.when(k == 0)
        def _():
            t_re, t_im, rows_re, rows_im = _power_table(ar_ref[...], -ai_ref[...])
            row = lax.broadcasted_iota(jnp.int32, (8, W), 0)
            r_re = jnp.zeros((8, W), F32)
            r_im = jnp.zeros((8, W), F32)
            for j in range(8):
                r_re = jnp.where(row == j, rows_re[7 - j], r_re)
                r_im = jnp.where(row == j, rows_im[7 - j], r_im)
            pw_re[...] = r_re
            pw_im[...] = r_im
            for n, j in enumerate((0, 1, 3)):
                sh_re[n] = jnp.broadcast_to(rows_re[j], (8, W))
                sh_im[n] = jnp.broadcast_to(rows_im[j], (8, W))
            carry_re[...] = jnp.zeros_like(carry_re)
            carry_im[...] = jnp.zeros_like(carry_im)
            acc_ar[...] = jnp.zeros_like(acc_ar)
            acc_ai[...] = jnp.zeros_like(acc_ai)
            dd_ref[...] = jnp.zeros_like(dd_ref)
            dbr_ref[...] = jnp.zeros_like(dbr_ref)
            dbi_ref[...] = jnp.zeros_like(dbi_ref)
            dcr_ref[...] = jnp.zeros_like(dcr_ref)
            dci_ref[...] = jnp.zeros_like(dci_ref)

        uv = u_ref[...]
        dy = dz_ref[...] * _gelu_grad(y_ref[...])
        dyb = dy.astype(BF16)
        lam_re[...] = jnp.dot(dyb, cre_ref[...], preferred_element_type=F32)
        lam_im[...] = -jnp.dot(dyb, cim_ref[...], preferred_element_type=F32)
        keep = jnp.where(first_chunk, 0.0, 1.0)
        xe_re[pl.ds(0, 8), :] = xpr_ref[...] * keep
        xe_im[pl.ds(0, 8), :] = xpi_ref[...] * keep
        xe_re[pl.ds(8, tc), :] = xr_ref[...]
        xe_im[pl.ds(8, tc), :] = xi_ref[...]
        row = lax.broadcasted_iota(jnp.int32, (8, W), 0)

        def tile(n, carry):
            c_re, c_im, s_ar, s_ai = carry
            i = n_tiles - 1 - n
            rows = pl.ds(pl.multiple_of(i * 8, 8), 8)
            r = lam_re[rows, :]
            m = lam_im[rows, :]
            for q, d in enumerate((1, 2, 4)):
                pr, pm = _cmul(sh_re[q], sh_im[q], pltpu.roll(r, 8 - d, 0), pltpu.roll(m, 8 - d, 0))
                r = r + jnp.where(row < 8 - d, pr, 0.0)
                m = m + jnp.where(row < 8 - d, pm, 0.0)
            pr, pm = _cmul(pw_re[...], pw_im[...], c_re, c_im)
            r = r + pr
            m = m + pm
            lam_re[rows, :] = r
            lam_im[rows, :] = m
            cur_re = xe_re[pl.ds(pl.multiple_of(i * 8 + 8, 8), 8), :]
            cur_im = xe_im[pl.ds(pl.multiple_of(i * 8 + 8, 8), 8), :]
            bef_re = xe_re[rows, :]
            bef_im = xe_im[rows, :]
            xp_re = jnp.where(row == 0, jnp.broadcast_to(bef_re[7:8, :], (8, W)), pltpu.roll(cur_re, 1, 0))
            xp_im = jnp.where(row == 0, jnp.broadcast_to(bef_im[7:8, :], (8, W)), pltpu.roll(cur_im, 1, 0))
            s_ar = s_ar + r * xp_re + m * xp_im
            s_ai = s_ai + m * xp_re - r * xp_im
            return jnp.broadcast_to(r[0:1, :], (8, W)), jnp.broadcast_to(m[0:1, :], (8, W)), s_ar, s_ai

        c_re, c_im, s_ar, s_ai = lax.fori_loop(
            0, n_tiles, tile, (carry_re[...], carry_im[...], acc_ar[...], acc_ai[...]))
        carry_re[...] = c_re
        carry_im[...] = c_im
        acc_ar[...] = s_ar
        acc_ai[...] = s_ai
        lr = lam_re[...].astype(BF16)
        li = lam_im[...].astype(BF16)
        du_ref[...] = (dy * d_ref[...] + jnp.dot(lr, btr_ref[...], preferred_element_type=F32)
                       + jnp.dot(li, bti_ref[...], preferred_element_type=F32))
        dd_ref[...] += jnp.sum(dy * uv, axis=0, keepdims=True)
        tn_dims = (((0,), (0,)), ((), ()))
        ub = uv.astype(BF16)
        dbr_ref[...] += lax.dot_general(ub, lr, tn_dims, preferred_element_type=F32)
        dbi_ref[...] += lax.dot_general(ub, li, tn_dims, preferred_element_type=F32)
        dcr_ref[...] += lax.dot_general(dyb, xr_ref[...].astype(BF16), tn_dims, preferred_element_type=F32)
        dci_ref[...] -= lax.dot_general(dyb, xi_ref[...].astype(BF16), tn_dims, preferred_element_type=F32)

        @pl.when(first_chunk)
        def _():
            dar_ref[...] = jnp.sum(acc_ar[...], axis=0, keepdims=True)
            dai_ref[...] = jnp.sum(acc_ai[...], axis=0, keepdims=True)

    rev = lambda k: n_chunks - 1 - k
    ch = pl.BlockSpec((tc, CH_TILE), lambda s, k: (rev(k), s))
    st = pl.BlockSpec((tc, W), lambda s, k: (rev(k), s))
    prev = pl.BlockSpec((8, W), lambda s, k: (jnp.maximum(rev(k) * tiles_per_chunk - 1, 0), s))
    mat_cw = pl.BlockSpec((None, CH_TILE, W), lambda s, k: (s, 0, 0))
    mat_wc = pl.BlockSpec((None, W, CH_TILE), lambda s, k: (s, 0, 0))
    vec_w = pl.BlockSpec((1, W), lambda s, k: (0, s))
    vec_c = pl.BlockSpec((1, CH_TILE), lambda s, k: (0, s))
    dense = jax.ShapeDtypeStruct((n_st, CH_TILE, W), F32)
    return pl.pallas_call(
        body, name=name, grid=(n_st, n_chunks),
        in_specs=[ch, ch, ch, st, st, prev, prev, mat_cw, mat_cw, mat_wc, mat_wc, vec_w, vec_w, vec_c],
        out_specs=[ch, vec_c, mat_cw, mat_cw, mat_cw, mat_cw, vec_w, vec_w],
        out_shape=[jax.ShapeDtypeStruct((T, D), F32), jax.ShapeDtypeStruct((1, D), F32), dense, dense, dense, dense,
                   jax.ShapeDtypeStruct((1, 4 * D), F32), jax.ShapeDtypeStruct((1, 4 * D), F32)],
        scratch_shapes=[pltpu.VMEM((tc, W), F32), pltpu.VMEM((tc, W), F32),
                        pltpu.VMEM((tc + 8, W), F32), pltpu.VMEM((tc + 8, W), F32),
                        pltpu.VMEM((8, W), F32), pltpu.VMEM((8, W), F32),
                        pltpu.VMEM((8, W), F32), pltpu.VMEM((8, W), F32),
                        pltpu.VMEM((3, 8, W), F32), pltpu.VMEM((3, 8, W), F32),
                        pltpu.VMEM((8, W), F32), pltpu.VMEM((8, W), F32)],
        compiler_params=_cparams("parallel", "arbitrary"),
    )(dz, y, u, x_re, x_im, x_re, x_im, c_re, c_im, bt_re, bt_im, abar_re, abar_im, d_skip)


NT_DIMS = (((1,), (1,)), ((), ()))
TN_DIMS = (((0,), (0,)), ((), ()))


def _sums_matrix(strictly_later, copies):
    jj = lax.broadcasted_iota(jnp.int32, (copies * KEY_BLOCK, 2 * KEY_BLOCK), 0) & (KEY_BLOCK - 1)
    ss = lax.broadcasted_iota(jnp.int32, (copies * KEY_BLOCK, 2 * KEY_BLOCK), 1)
    tri = (jj > ss) if strictly_later else (jj < ss)
    return (tri | (ss >= KEY_BLOCK)).astype(BF16)


def _split_heads(blk):
    first = lax.broadcasted_iota(jnp.int32, blk.shape, 1) < HEAD_DIM
    zero = jnp.zeros_like(blk)
    return jnp.concatenate([jnp.where(first, blk, zero), jnp.where(first, zero, blk)], axis=0)


def _sb_scores(z, mask, later):
    lb = jnp.minimum(z, 0.0) - jnp.log1p(jnp.exp(-jnp.abs(z)))
    lm = lb - z
    if mask is not None:
        lm = jnp.where(mask, lm, 0.0)
    hi = lm.astype(BF16)
    lo = (lm - hi.astype(F32)).astype(BF16)
    return lb, jnp.dot(jnp.concatenate([hi, lo], axis=1), later, preferred_element_type=F32)


def _key_rows(kb):
    return pl.ds(pl.multiple_of(kb * KEY_BLOCK, KEY_BLOCK), KEY_BLOCK)


def _sb_mask(q_row0, k_row0, tq):
    tpos = q_row0 + lax.broadcasted_iota(jnp.int32, (tq, KEY_BLOCK), 0)
    spos = k_row0 + lax.broadcasted_iota(jnp.int32, (tq, KEY_BLOCK), 1)
    return (spos < tpos) & (spos >= META_START)


def _key_block_phases(iq, per_q, block, ascending):
    first_diag = iq * per_q

    def run(lo, n, masked):
        def step(i, carry):
            block(lo + i if ascending else lo + n - 1 - i, masked)
            return carry
        lax.fori_loop(0, n, step, 0)

    phases = [(0, jnp.minimum(iq, 1), True), (1, jnp.maximum(first_diag - 1, 0), False), (first_diag, per_q, True)]
    for lo, n, masked in (phases if ascending else phases[::-1]):
        run(lo, n, masked)


def _attn_fwd(name, q, kv):
    T, D = q.shape
    n_hp = D // 128
    tq = _row_tile(T, 512)
    per_q = tq // KEY_BLOCK
    scale = 1.0 / math.sqrt(HEAD_DIM)

    def body(q_ref, k_ref, v_ref, o_ref, l_ref, z_buf, w_buf, acc_ref, run_ref):
        iq = pl.program_id(1)
        n_kb = (iq + 1) * per_q
        qs = q_ref[...] * jnp.asarray(scale, BF16)
        later = _sums_matrix(True, 2)

        def scores(kb):
            return lax.dot_general(qs, _split_heads(k_ref[_key_rows(kb), :]), NT_DIMS, preferred_element_type=F32)

        def weighted_values(kb):
            return jnp.dot(w_buf[...], _split_heads(v_ref[_key_rows(kb), :]), preferred_element_type=F32)

        acc_ref[...] = jnp.zeros_like(acc_ref)
        run_ref[...] = jnp.zeros_like(run_ref)
        w_buf[...] = jnp.zeros_like(w_buf)
        z_buf[...] = scores(n_kb - 1)

        def block(kb, masked):
            acc_ref[...] += weighted_values(jnp.minimum(kb + 1, n_kb - 1))
            z_next = scores(jnp.maximum(kb - 1, 0))
            mask = _sb_mask(iq * tq, kb * KEY_BLOCK, tq) if masked else None
            heads = [_sb_scores(z_buf[:, h * KEY_BLOCK:(h + 1) * KEY_BLOCK], mask, later) for h in range(2)]
            for h, (lb, sums) in enumerate(heads):
                run = run_ref[h]
                w = jnp.exp(lb + sums[:, :KEY_BLOCK] + run)
                if masked:
                    w = jnp.where(mask, w, 0.0)
                w_buf[:, h * KEY_BLOCK:(h + 1) * KEY_BLOCK] = w.astype(BF16)
                run_ref[h] = run + sums[:, KEY_BLOCK:]
            z_buf[...] = z_next

        _key_block_phases(iq, per_q, block, ascending=False)
        acc_ref[...] += weighted_values(0)
        lane = lax.broadcasted_iota(jnp.int32, (tq, 128), 1)
        o_ref[...] = acc_ref[...].astype(o_ref.dtype)
        l_ref[...] = jnp.where(lane < HEAD_DIM, run_ref[0], run_ref[1])

    blk = pl.BlockSpec((tq, 128), lambda h, i: (i, h))
    return pl.pallas_call(
        body, name=name, grid=(n_hp, T // tq),
        in_specs=[blk, pl.BlockSpec((T, 128), lambda h, i: (0, h)), pl.BlockSpec((T, 128), lambda h, i: (0, n_hp + h))],
        out_specs=[blk, blk],
        out_shape=[jax.ShapeDtypeStruct((T, D), BF16), jax.ShapeDtypeStruct((T, D), F32)],
        scratch_shapes=[pltpu.VMEM((tq, 2 * KEY_BLOCK), F32), pltpu.VMEM((tq, 2 * KEY_BLOCK), BF16),
                        pltpu.VMEM((tq, 128), F32), pltpu.VMEM((2, tq, 128), F32)],
        compiler_params=_cparams("parallel", "arbitrary"),
    )(q, kv, kv)


def _attn_bwd(name, q, kv, do, ltot):
    T, D = q.shape
    n_hp = D // 128
    tq = _row_tile(T, 512)
    n_q = T // tq
    per_q = tq // KEY_BLOCK
    scale = 1.0 / math.sqrt(HEAD_DIM)

    def body(q_ref, k_ref, v_ref, do_ref, l_ref, dq_ref, dk_ref, dv_ref,
             dk_acc, dv_acc, dq_acc, lpre_ref, cpre_ref, z_buf, dw_buf, dz_buf, w_buf):
        iq = pl.program_id(1)
        n_kb = (iq + 1) * per_q

        @pl.when(iq == 0)
        def _():
            dk_acc[...] = jnp.zeros_like(dk_acc)
            dv_acc[...] = jnp.zeros_like(dv_acc)

        first = lax.broadcasted_iota(jnp.int32, (tq, 128), 1) < HEAD_DIM
        qs = q_ref[...] * jnp.asarray(scale, BF16)
        dov = do_ref[...]
        ltv = l_ref[...]
        swapped = pltpu.roll(ltv, HEAD_DIM, 1)
        ltot = [jnp.where(first, ltv, swapped), jnp.where(first, swapped, ltv)]
        zero = jnp.zeros_like(qs)
        q_stack = jnp.concatenate([jnp.where(first, qs, zero), jnp.where(first, zero, qs)], axis=0)
        do_stack = jnp.concatenate([jnp.where(first, dov, zero), jnp.where(first, zero, dov)], axis=0)
        later = _sums_matrix(True, 2)
        earlier = _sums_matrix(False, 1)

        def scores(kb):
            rows = _key_rows(kb)
            return (lax.dot_general(qs, _split_heads(k_ref[rows, :]), NT_DIMS, preferred_element_type=F32),
                    lax.dot_general(dov, _split_heads(v_ref[rows, :]), NT_DIMS, preferred_element_type=F32))

        def flush(kb):
            rows = _key_rows(kb)
            k_heads = _split_heads(k_ref[rows, :])
            dq_acc[...] += (jnp.dot(dz_buf[:tq, :], k_heads[:KEY_BLOCK, :], preferred_element_type=F32)
                            + jnp.dot(dz_buf[tq:, :], k_heads[KEY_BLOCK:, :], preferred_element_type=F32))
            dk_acc[rows, :] += lax.dot_general(dz_buf[...], q_stack, TN_DIMS, preferred_element_type=F32)
            dv_acc[rows, :] += lax.dot_general(w_buf[...], do_stack, TN_DIMS, preferred_element_type=F32)

        dq_acc[...] = jnp.zeros_like(dq_acc)
        lpre_ref[...] = jnp.zeros_like(lpre_ref)
        cpre_ref[...] = jnp.zeros_like(cpre_ref)
        dz_buf[...] = jnp.zeros_like(dz_buf)
        w_buf[...] = jnp.zeros_like(w_buf)
        z_buf[...], dw_buf[...] = scores(0)

        def block(kb, masked):
            flush(jnp.maximum(kb - 1, 0))
            z_next, dw_next = scores(jnp.minimum(kb + 1, n_kb - 1))
            mask = _sb_mask(iq * tq, kb * KEY_BLOCK, tq) if masked else None
            heads = [_sb_scores(z_buf[:, h * KEY_BLOCK:(h + 1) * KEY_BLOCK], mask, later) for h in range(2)]
            grads = []
            for h, (lb, sums) in enumerate(heads):
                after = ltot[h] - lpre_ref[h] - sums[:, KEY_BLOCK:]
                w = jnp.exp(lb + sums[:, :KEY_BLOCK] + after)
                if masked:
                    w = jnp.where(mask, w, 0.0)
                da = w * dw_buf[:, h * KEY_BLOCK:(h + 1) * KEY_BLOCK]
                w_buf[h * tq:(h + 1) * tq, :] = w.astype(BF16)
                lpre_ref[h] += sums[:, KEY_BLOCK:]
                grads.append((da, jnp.dot(da.astype(BF16), earlier, preferred_element_type=F32)))
            for h, (da, dsums) in enumerate(grads):
                sig = jnp.exp(heads[h][0])
                through_later = sig * (dsums[:, :KEY_BLOCK] + cpre_ref[h])
                if masked:
                    through_later = jnp.where(mask, through_later, 0.0)
                dz_buf[h * tq:(h + 1) * tq, :] = (da * (1.0 - sig) - through_later).astype(BF16)
                cpre_ref[h] += dsums[:, KEY_BLOCK:]
            z_buf[...] = z_next
            dw_buf[...] = dw_next

        _key_block_phases(iq, per_q, block, ascending=True)
        flush(n_kb - 1)
        dq_ref[...] = (dq_acc[...] * scale).astype(dq_ref.dtype)

        @pl.when(iq == n_q - 1)
        def _():
            dk_ref[...] = dk_acc[...].astype(dk_ref.dtype)
            dv_ref[...] = dv_acc[...].astype(dv_ref.dtype)

    blk = pl.BlockSpec((tq, 128), lambda h, i: (i, h))
    full = pl.BlockSpec((T, 128), lambda h, i: (0, h))
    return pl.pallas_call(
        body, name=name, grid=(n_hp, n_q),
        in_specs=[blk, full, pl.BlockSpec((T, 128), lambda h, i: (0, n_hp + h)), blk, blk],
        out_specs=[blk, full, full],
        out_shape=[jax.ShapeDtypeStruct((T, D), BF16)] * 3,
        scratch_shapes=[pltpu.VMEM((T, 128), F32), pltpu.VMEM((T, 128), F32), pltpu.VMEM((tq, 128), F32),
                        pltpu.VMEM((2, tq, 128), F32), pltpu.VMEM((2, tq, 128), F32),
                        pltpu.VMEM((tq, 2 * KEY_BLOCK), F32), pltpu.VMEM((tq, 2 * KEY_BLOCK), F32),
                        pltpu.VMEM((2 * tq, 128), BF16), pltpu.VMEM((2 * tq, 128), BF16)],
        compiler_params=_cparams("parallel", "arbitrary"),
    )(q, kv, kv, do, ltot)


def _adamw(name, w, g, m, v):
    shape = w.shape
    size = w.size
    cols = 1024 if size % 1024 == 0 else shape[-1]
    rows = size // cols
    tm = _row_tile(rows, 512) if rows % 8 == 0 else rows
    c1 = 1.0 / (1.0 - ADAM_B1 ** ADAM_STEP)
    c2 = 1.0 / (1.0 - ADAM_B2 ** ADAM_STEP)

    def body(w_ref, g_ref, m_ref, v_ref, d_ref, nm_ref, nv_ref):
        gv = g_ref[...]
        nm = ADAM_B1 * m_ref[...] + (1.0 - ADAM_B1) * gv
        nv = ADAM_B2 * v_ref[...] + (1.0 - ADAM_B2) * (gv * gv)
        d_ref[...] = -ADAM_LR * ((nm * c1) / (jnp.sqrt(nv * c2) + ADAM_EPS) + ADAM_WD * w_ref[...])
        nm_ref[...] = nm
        nv_ref[...] = nv

    blk = pl.BlockSpec((tm, cols), lambda i: (i, 0))
    outs = pl.pallas_call(
        body, name=name, grid=(rows // tm,),
        in_specs=[blk] * 4, out_specs=[blk] * 3,
        out_shape=[jax.ShapeDtypeStruct((rows, cols), F32)] * 3,
        compiler_params=_cparams("parallel"),
    )(*[t.reshape(rows, cols) for t in (w, g, m, v)])
    return tuple(o.reshape(shape) for o in outs)


def _any_specs(n):
    return [pl.BlockSpec(memory_space=pl.ANY)] * n


def _chip_index():
    return 2 * lax.axis_index("x") + lax.axis_index("y")


def _place():
    x, y, c = lax.axis_index("x"), lax.axis_index("y"), lax.axis_index("c")
    chips = [(1 - x, y), (x, 1 - y), (1 - x, 1 - y)]
    return x, y, c, chips


def _all_gather_chips(name, shards):
    n = len(shards)

    def body(*refs):
        x_refs, o_refs = refs[:n], refs[n:2 * n]
        send_sems, recv_sems = refs[2 * n:]
        x, y, c, chips = _place()
        me = 2 * x + y
        sibling = (x, y, 1 - c)

        def half(ref, i, which):
            h = shards[i].shape[0] // 2
            return ref.at[pl.ds(which * h, h)]

        def remote(k, i, src, dst, to):
            return pltpu.make_async_remote_copy(src_ref=src, dst_ref=dst, send_sem=send_sems.at[k, i],
                                                recv_sem=recv_sems.at[k, i], device_id=to, device_id_type=MESH)

        sent = []
        for j, chip in enumerate(chips):
            for i in range(n):
                cp = remote(j, i, half(x_refs[i], i, c), half(o_refs[i].at[me], i, c), (*chip, c))
                cp.start()
                sent.append(cp)
        for j, chip in enumerate(chips):
            pj = 2 * chip[0] + chip[1]
            for i in range(n):
                landed = half(o_refs[i].at[pj], i, c)
                remote(j, i, landed, landed, (*chip, c)).wait_recv()
                cp = remote(3 + j, i, landed, landed, sibling)
                cp.start()
                sent.append(cp)
        for j, chip in enumerate(chips):
            pj = 2 * chip[0] + chip[1]
            for i in range(n):
                got = half(o_refs[i].at[pj], i, 1 - c)
                remote(3 + j, i, got, got, sibling).wait_recv()
        for cp in sent:
            cp.wait_send()

    outs = pl.pallas_call(
        body, name=name,
        in_specs=_any_specs(n), out_specs=_any_specs(n),
        out_shape=[jax.ShapeDtypeStruct((4,) + s.shape, s.dtype) for s in shards],
        scratch_shapes=[pltpu.SemaphoreType.DMA((6, n)), pltpu.SemaphoreType.DMA((6, n))],
    )(*shards)
    return [lax.dynamic_update_slice(o, s[None], (_chip_index(), 0, 0)) for o, s in zip(outs, shards)]


def _pair_split(name, grads):
    n = len(grads)

    def body(*refs):
        g_refs, got_refs = refs[:n], refs[n:2 * n]
        send_sems, recv_sems = refs[2 * n:]
        x, y, c, _ = _place()
        sibling = (x, y, 1 - c)
        sent = []
        for i in range(n):
            h = grads[i].shape[1] // 2
            rc = pltpu.make_async_remote_copy(
                src_ref=g_refs[i].at[:, pl.ds((1 - c) * h, h)], dst_ref=got_refs[i],
                send_sem=send_sems.at[i], recv_sem=recv_sems.at[i], device_id=sibling, device_id_type=MESH)
            rc.start()
            sent.append(rc)
        for rc in sent:
            rc.wait()

    return pl.pallas_call(
        body, name=name,
        in_specs=_any_specs(n), out_specs=_any_specs(n),
        out_shape=[jax.ShapeDtypeStruct((4, g.shape[1] // 2, g.shape[2]), g.dtype) for g in grads],
        scratch_shapes=[pltpu.SemaphoreType.DMA((n,)), pltpu.SemaphoreType.DMA((n,))],
    )(*grads)


def _chip_exchange(name, sums):
    n = len(sums)

    def body(*refs):
        s_refs, o_refs = refs[:n], refs[n:2 * n]
        send_sems, recv_sems = refs[2 * n:]
        x, y, c, chips = _place()
        me = 2 * x + y
        sent = []
        for j, chip in enumerate(chips):
            pj = 2 * chip[0] + chip[1]
            for i in range(n):
                rc = pltpu.make_async_remote_copy(
                    src_ref=s_refs[i].at[pj], dst_ref=o_refs[i].at[me],
                    send_sem=send_sems.at[j, i], recv_sem=recv_sems.at[j, i],
                    device_id=(*chip, c), device_id_type=MESH)
                rc.start()
                sent.append(rc)
        for j, chip in enumerate(chips):
            pj = 2 * chip[0] + chip[1]
            for i in range(n):
                pltpu.make_async_remote_copy(
                    src_ref=s_refs[i].at[pj], dst_ref=o_refs[i].at[pj],
                    send_sem=send_sems.at[j, i], recv_sem=recv_sems.at[j, i],
                    device_id=(*chip, c), device_id_type=MESH).wait_recv()
        for rc in sent:
            rc.wait_send()

    outs = pl.pallas_call(
        body, name=name,
        in_specs=_any_specs(n), out_specs=_any_specs(n),
        out_shape=[jax.ShapeDtypeStruct(s.shape, s.dtype) for s in sums],
        scratch_shapes=[pltpu.SemaphoreType.DMA((3, n)), pltpu.SemaphoreType.DMA((3, n))],
    )(*sums)
    me = _chip_index()
    return [lax.dynamic_update_slice(o, lax.dynamic_index_in_dim(s, me, 0, keepdims=True), (me, 0, 0))
            for o, s in zip(outs, sums)]


def _pair_join(name, halves):
    n = len(halves)

    def body(*refs):
        h_refs, o_refs = refs[:n], refs[n:2 * n]
        send_sems, recv_sems = refs[2 * n:]
        x, y, c, _ = _place()
        sibling = (x, y, 1 - c)
        sent = []
        for i in range(n):
            h = halves[i].shape[0]
            mine = o_refs[i].at[pl.ds(c * h, h)]
            rc = pltpu.make_async_remote_copy(
                src_ref=h_refs[i], dst_ref=mine, send_sem=send_sems.at[i], recv_sem=recv_sems.at[i],
                device_id=sibling, device_id_type=MESH)
            rc.start()
            sent.append(rc)
        for i in range(n):
            h = halves[i].shape[0]
            theirs = o_refs[i].at[pl.ds((1 - c) * h, h)]
            pltpu.make_async_remote_copy(
                src_ref=h_refs[i], dst_ref=theirs, send_sem=send_sems.at[i], recv_sem=recv_sems.at[i],
                device_id=sibling, device_id_type=MESH).wait_recv()
        for rc in sent:
            rc.wait_send()

    outs = pl.pallas_call(
        body, name=name,
        in_specs=_any_specs(n), out_specs=_any_specs(n),
        out_shape=[jax.ShapeDtypeStruct((2 * s.shape[0], s.shape[1]), s.dtype) for s in halves],
        scratch_shapes=[pltpu.SemaphoreType.DMA((n,)), pltpu.SemaphoreType.DMA((n,))],
    )(*halves)
    c = lax.axis_index("c")
    return [lax.dynamic_update_slice(o, s, (c * s.shape[0], 0)) for o, s in zip(outs, halves)]


def _add_pair(name, a, b):
    _, H, C = a.shape
    th = _row_tile(H, max(8, (1024 * 1024) // (4 * C) // 8 * 8))

    def body(a_ref, b_ref, o_ref):
        o_ref[...] = (a_ref[...].astype(F32) + b_ref[...].astype(F32)).astype(o_ref.dtype)

    blk = pl.BlockSpec((None, th, C), lambda q, i: (q, i, 0))
    return pl.pallas_call(
        body, name=name, grid=(4, H // th), in_specs=[blk, blk], out_specs=blk,
        out_shape=jax.ShapeDtypeStruct(a.shape, a.dtype), compiler_params=_cparams("parallel", "parallel"),
    )(a, b)


def _add_chips(name, parts):
    _, H, C = parts.shape
    th = _row_tile(H, max(8, (1024 * 1024) // (4 * C) // 8 * 8))

    def body(p_ref, o_ref):
        acc = p_ref[0].astype(F32)
        for q in range(1, 4):
            acc = acc + p_ref[q].astype(F32)
        o_ref[...] = acc

    return pl.pallas_call(
        body, name=name, grid=(H // th,),
        in_specs=[pl.BlockSpec((4, th, C), lambda i: (0, i, 0))], out_specs=pl.BlockSpec((th, C), lambda i: (i, 0)),
        out_shape=jax.ShapeDtypeStruct((H, C), F32), compiler_params=_cparams("parallel"),
    )(parts)


def _reduce_scatter(grads):
    n = len(grads)
    c = lax.axis_index("c")
    got = _pair_split("rs_pair_split", grads)
    kept = [lax.dynamic_slice_in_dim(g, c * (g.shape[1] // 2), g.shape[1] // 2, axis=1) for g in grads]
    pair = [_add_pair(f"rs_add_pair_{i}", kept[i], got[i]) for i in range(n)]
    parts = _chip_exchange("rs_chip_exchange", pair)
    red = [_add_chips(f"rs_add_chips_{i}", parts[i]) for i in range(n)]
    return _pair_join("rs_pair_join", red)


def _block_diag(t):
    G, A, B = t.shape
    eye = jnp.eye(8, dtype=t.dtype)
    return jnp.einsum("sgab,gh->sgahb", t.reshape(G // 8, 8, A, B), eye).reshape(G // 8, 8 * A, 8 * B)


def _block_diag_extract(m, A, B):
    n = m.shape[0]
    eye = jnp.eye(8, dtype=m.dtype)
    return jnp.einsum("sgahb,gh->sgab", m.reshape(n, 8, A, 8, B), eye).reshape(8 * n, A, B)


def kernel(x, meta_tokens, norm_mix, norm_ffn, s5_a_re, s5_a_im, s5_log_dt, s5_b_re, s5_b_im, s5_c_re, s5_c_im, s5_d, s5_w_glu, norm_kv, w_kv, w_q, w_o, w_ffn_in, w_ffn_out, norm_final, loss_target, m_meta_tokens, m_norm_mix, m_norm_ffn, m_s5_a_re, m_s5_a_im, m_s5_log_dt, m_s5_b_re, m_s5_b_im, m_s5_c_re, m_s5_c_im, m_s5_d, m_s5_w_glu, m_norm_kv, m_w_kv, m_w_q, m_w_o, m_w_ffn_in, m_w_ffn_out, m_norm_final, v_meta_tokens, v_norm_mix, v_norm_ffn, v_s5_a_re, v_s5_a_im, v_s5_log_dt, v_s5_b_re, v_s5_b_im, v_s5_c_re, v_s5_c_im, v_s5_d, v_s5_w_glu, v_norm_kv, v_w_kv, v_w_q, v_w_o, v_w_ffn_in, v_w_ffn_out, v_norm_final):
    seq, D = x.shape[1], x.shape[2]
    T = X_START + seq
    G, P, C = s5_a_re.shape[1], S5_STATE, S5_GROUP
    d_ff = w_ffn_out.shape[1] * 4
    dq4 = D // 4
    chip = 2 * lax.axis_index("x") + lax.axis_index("y")

    small_in = jnp.concatenate([meta_tokens, jnp.pad(s5_d, ((0, 15), (0, 0)))], axis=0)
    (small_all,) = _all_gather_chips("ag_small", [small_in])
    meta_full = small_all[:, :N_META, :].transpose(1, 0, 2).reshape(N_META, D)
    d_skip = small_all[:, N_META, :].reshape(1, D)
    shards = [s5_w_glu[0], w_kv, w_q[0], w_o[0],
              w_ffn_in.reshape(2 * D, -1), w_ffn_out.reshape(-1, D)]
    wg_glu, wg_kv, wg_q, wg_o, wg_in, wg_out = _all_gather_chips("ag_weights", [s.astype(BF16) for s in shards])
    wg_q = wg_q.reshape(1, D, D)
    wg_o = wg_o.reshape(1, D, D)
    rows_out = d_ff // 4
    wg_out = [wg_out[:, l * rows_out:(l + 1) * rows_out, :].reshape(1, d_ff, D) for l in range(2)]

    row = lambda v: v.reshape(1, -1)
    g_mix0, g_mix1 = row(norm_mix[0]), row(norm_mix[1])
    g_ffn = [row(norm_ffn[0]), row(norm_ffn[1])]
    g_kv, g_final = row(norm_kv), row(norm_final)

    a_re3 = s5_a_re[0].reshape(G, 1, P)
    a_im3 = s5_a_im[0].reshape(G, 1, P)
    log_dt3 = s5_log_dt[0].reshape(G, 1, 1)
    bt_re = s5_b_re[0].transpose(0, 2, 1)
    bt_im = s5_b_im[0].transpose(0, 2, 1)
    ab_re, ab_im, bb_re, bb_im = _s5_prep_fwd("s5_prep", a_re3, a_im3, log_dt3, bt_re, bt_im)
    abar_re, abar_im = ab_re.reshape(1, G * P), ab_im.reshape(1, G * P)
    bd_b_re = _block_diag(bb_re).astype(BF16)
    bd_b_im = _block_diag(bb_im).astype(BF16)
    bd_bt_re = bd_b_re.transpose(0, 2, 1)
    bd_bt_im = bd_b_im.transpose(0, 2, 1)
    bd_c_re = _block_diag(s5_c_re[0]).astype(BF16)
    bd_c_im = _block_diag(s5_c_im[0]).astype(BF16)
    bd_ct_re = bd_c_re.transpose(0, 2, 1)
    bd_ct_im = bd_c_im.transpose(0, 2, 1)

    h0 = jnp.concatenate([jnp.zeros((META_START, D), F32), meta_full, x[0]], axis=0)
    (u,) = _rmsnorm_fwd("norm_mix0", h0, [g_mix0], [F32])
    y, z, x_re, x_im = _s5_fwd("s5_scan", u, bd_b_re, bd_b_im, bd_ct_re, bd_ct_im, abar_re, abar_im, d_skip)
    vg = _mm_nn("glu_proj", z, wg_glu)
    h1 = _glu_fwd("glu", vg, h0)

    def ffn_fwd(l, h):
        (n,) = _rmsnorm_fwd(f"norm_ffn{l}", h, [g_ffn[l]], [BF16])
        gu = _mm_nn(f"ffn_in{l}", n, wg_in, k_blk=l)
        mid = _swiglu_fwd(f"swiglu{l}", gu)
        return n, gu, mid, _mm_nn(f"ffn_out{l}", mid, wg_out[l], res=h)

    n1, gu0, mid0, h2 = ffn_fwd(0, h1)
    nk, nq = _rmsnorm_fwd("norm_kv_q", h2, [g_kv, g_mix1], [BF16, BF16])
    kv = _mm_nn("kv_proj", nk, wg_kv, out_dtype=BF16)
    q = _mm_nn("q_proj", nq, wg_q, out_dtype=BF16)
    o, ltot = _attn_fwd("attn_fwd", q, kv)
    h3 = _mm_nn("o_proj", o, wg_o, res=h2)
    n3, gu1, mid1, h4 = ffn_fwd(1, h3)
    loss_part, dh4, dg_final = _final_loss("final_loss", h4, g_final, loss_target[0])
    loss = lax.psum(loss_part[0, 0], ("x", "y", "c"))

    def ffn_bwd(l, dh, h, n, gu, mid):
        dmid = _mm_nt_k(f"ffn_out{l}_dx", dh, wg_out[l], d_ff)
        dgu = _swiglu_bwd(f"swiglu{l}_bwd", gu, dmid)
        dw_out = _mm_tn(f"ffn_out{l}_dw", mid, dh, 1)
        dw_in = _mm_tn(f"ffn_in{l}_dw", n, dgu, 4)
        dn = _mm_nt_k(f"ffn_in{l}_dx", dgu, wg_in, D, k_blk=l)
        dh_prev, (dg,) = _rmsnorm_bwd(f"norm_ffn{l}_bwd", h, [(g_ffn[l], dn)], dh)
        return dh_prev, dg, dw_in, dw_out

    dh3, dg_ffn1, dw_in1, dw_out1 = ffn_bwd(1, dh4, h3, n3, gu1, mid1)
    d_o = _mm_nt_k("o_proj_dx", dh3, wg_o, D, out_dtype=BF16)
    dw_o = _mm_tn("o_proj_dw", o, dh3, 1)
    dq, dk, dv = _attn_bwd("attn_bwd", q, kv, d_o, ltot)
    dkv = jnp.concatenate([dk, dv], axis=1)
    dw_q = _mm_tn("q_proj_dw", nq, dq, 1)
    dnq = _mm_nt_k("q_proj_dx", dq, wg_q, D)
    dw_kv = _mm_tn("kv_proj_dw", nk, dkv, 4)
    dnk = _mm_nt_k("kv_proj_dx", dkv, wg_kv, D)
    dh2, (dg_mix1, dg_kv) = _rmsnorm_bwd("norm_kv_q_bwd", h2, [(g_mix1, dnq), (g_kv, dnk)], dh3)
    dh1, dg_ffn0, dw_in0, dw_out0 = ffn_bwd(0, dh2, h1, n1, gu0, mid0)
    dvg = _glu_bwd("glu_bwd", vg, dh1)
    dw_glu = _mm_tn("glu_proj_dw", z, dvg, 4)
    dz = _mm_nt_k("glu_proj_dx", dvg, wg_glu, D)
    du, dd, dbd_b_re, dbd_b_im, dbd_c_re, dbd_c_im, dab_re, dab_im = _s5_bwd(
        "s5_scan_bwd", dz, y, u, x_re, x_im, bd_c_re, bd_c_im, bd_bt_re, bd_bt_im, abar_re, abar_im, d_skip)
    dh0, (dg_mix0,) = _rmsnorm_bwd("norm_mix0_bwd", h0, [(g_mix0, du)], dh1)
    da_re, da_im, dlog_dt, dbt_re, dbt_im = _s5_prep_bwd(
        "s5_prep_bwd", a_re3, a_im3, log_dt3, bt_re, bt_im,
        dab_re.reshape(G, 1, P), dab_im.reshape(G, 1, P),
        _block_diag_extract(dbd_b_re, C, P), _block_diag_extract(dbd_b_im, C, P))
    grad_x = dh0[X_START:][None]

    small_parts = [
        dg_mix0, dg_mix1, dg_ffn0, dg_ffn1, da_re, da_im,
        dbt_re.transpose(0, 2, 1), dbt_im.transpose(0, 2, 1),
        _block_diag_extract(dbd_c_re, C, P), _block_diag_extract(dbd_c_im, C, P),
        dg_kv, dg_final, dh0[META_START:X_START], dd, dlog_dt]
    small_sizes = [p.size for p in small_parts]
    unit = 4 * 2 * 8 * 128
    padded = -(-sum(small_sizes) // unit) * unit
    small_flat = jnp.concatenate(
        [p.reshape(-1) for p in small_parts[:-1]]
        + [jnp.pad(dlog_dt.reshape(-1), (0, padded - sum(small_sizes)))])
    small_blocks = small_flat.reshape(4, padded // (4 * 128), 128)
    big = [dw_glu, dw_kv, dw_q.reshape(4, D // 4, D), dw_o.reshape(4, D // 4, D),
           jnp.stack([dw_in0, dw_in1], axis=1).reshape(4, 2 * D, -1),
           jnp.stack([dw_out0.reshape(4, rows_out, D), dw_out1.reshape(4, rows_out, D)], axis=1).reshape(4, 2 * rows_out, D)]
    red = _reduce_scatter(big + [small_blocks])
    gw_glu, gw_kv, gw_q, gw_o, gw_in, gw_out, small_mine = red
    (small_red,) = _all_gather_chips("ag_small_grads", [small_mine])
    small_red = small_red.reshape(-1)
    pieces, at = [], 0
    for p, size in zip(small_parts, small_sizes):
        pieces.append(small_red[at:at + size].reshape(p.shape))
        at += size
    (gn_mix0, gn_mix1, gn_ffn0, gn_ffn1, ga_re, ga_im, gb_re, gb_im, gc_re, gc_im, gn_kv, gn_final,
     gmeta_full, gd_full, glog_dt) = pieces
    gn_mix = small_red[:2 * D].reshape(2, D)
    gn_ffn = small_red[2 * D:4 * D].reshape(2, D)
    gmeta = lax.dynamic_slice_in_dim(gmeta_full, chip * dq4, dq4, axis=1)
    gd = lax.dynamic_slice_in_dim(gd_full, chip * dq4, dq4, axis=1)

    grads = {
        "meta_tokens": gmeta, "norm_mix": gn_mix, "norm_ffn": gn_ffn,
        "s5_a_re": ga_re.reshape(s5_a_re.shape), "s5_a_im": ga_im.reshape(s5_a_im.shape),
        "s5_log_dt": glog_dt.reshape(s5_log_dt.shape),
        "s5_b_re": gb_re.reshape(s5_b_re.shape), "s5_b_im": gb_im.reshape(s5_b_im.shape),
        "s5_c_re": gc_re.reshape(s5_c_re.shape), "s5_c_im": gc_im.reshape(s5_c_im.shape),
        "s5_d": gd, "s5_w_glu": gw_glu.reshape(s5_w_glu.shape), "norm_kv": gn_kv.reshape(norm_kv.shape),
        "w_kv": gw_kv, "w_q": gw_q.reshape(w_q.shape), "w_o": gw_o.reshape(w_o.shape),
        "w_ffn_in": gw_in.reshape(w_ffn_in.shape), "w_ffn_out": gw_out.reshape(w_ffn_out.shape),
        "norm_final": gn_final.reshape(norm_final.shape),
    }
    weights = {
        "meta_tokens": (meta_tokens, m_meta_tokens, v_meta_tokens), "norm_mix": (norm_mix, m_norm_mix, v_norm_mix),
        "norm_ffn": (norm_ffn, m_norm_ffn, v_norm_ffn), "s5_a_re": (s5_a_re, m_s5_a_re, v_s5_a_re),
        "s5_a_im": (s5_a_im, m_s5_a_im, v_s5_a_im), "s5_log_dt": (s5_log_dt, m_s5_log_dt, v_s5_log_dt),
        "s5_b_re": (s5_b_re, m_s5_b_re, v_s5_b_re), "s5_b_im": (s5_b_im, m_s5_b_im, v_s5_b_im),
        "s5_c_re": (s5_c_re, m_s5_c_re, v_s5_c_re), "s5_c_im": (s5_c_im, m_s5_c_im, v_s5_c_im),
        "s5_d": (s5_d, m_s5_d, v_s5_d), "s5_w_glu": (s5_w_glu, m_s5_w_glu, v_s5_w_glu),
        "norm_kv": (norm_kv, m_norm_kv, v_norm_kv), "w_kv": (w_kv, m_w_kv, v_w_kv), "w_q": (w_q, m_w_q, v_w_q),
        "w_o": (w_o, m_w_o, v_w_o), "w_ffn_in": (w_ffn_in, m_w_ffn_in, v_w_ffn_in),
        "w_ffn_out": (w_ffn_out, m_w_ffn_out, v_w_ffn_out), "norm_final": (norm_final, m_norm_final, v_norm_final),
    }
    names = list(weights)
    deltas, new_m, new_v = [], [], []
    for name in names:
        w, m, v = weights[name]
        d, nm, nv = _adamw(f"adamw_{name}", w, grads[name], m, v)
        deltas.append(d)
        new_m.append(nm)
        new_v.append(nv)
    return (loss, grad_x, *[grads[n] for n in names], *deltas, *new_m, *new_v)
```

```python
import functools
import math

import jax
import jax.numpy as jnp
from jax import lax
from jax.experimental import pallas as pl
from jax.experimental.pallas import tpu as pltpu

F32 = jnp.float32
BF16 = jnp.bfloat16

N_META = 16
X_START = 128
META_START = X_START - N_META
S5_GROUP = 16
S5_STATE = 64
HEAD_DIM = 64
KEY_BLOCK = 128
STATE_TILE = 512
CH_TILE = 128
RMS_EPS = 1e-6
ADAM_LR, ADAM_B1, ADAM_B2, ADAM_EPS, ADAM_WD, ADAM_STEP = 0.001, 0.9, 0.999, 1e-08, 0.01, 10
VMEM_LIMIT_BYTES = 48 * 1024 * 1024
MESH = pl.DeviceIdType.MESH


def _cparams(*sem):
    return pltpu.CompilerParams(dimension_semantics=sem, vmem_limit_bytes=VMEM_LIMIT_BYTES)


def _row_tile(rows, cap):
    for unit in (128, 8):
        best = 0
        for t in range(unit, min(rows, cap) + 1, unit):
            if rows % t == 0:
                best = t
        if best:
            return best
    return rows


def _col_tile(cols, cap):
    best = 0
    for t in range(128, min(cols, cap) + 1, 128):
        if cols % t == 0:
            best = t
    return best if best else cols


def _gelu(x):
    k = math.sqrt(2.0 / math.pi)
    return 0.5 * x * (1.0 + jnp.tanh(k * (x + 0.044715 * x * x * x)))


def _gelu_grad(x):
    k = math.sqrt(2.0 / math.pi)
    t = jnp.tanh(k * (x + 0.044715 * x * x * x))
    return 0.5 * (1.0 + t) + 0.5 * x * (1.0 - t * t) * k * (1.0 + 3.0 * 0.044715 * x * x)


def _sigmoid(x):
    return 1.0 / (1.0 + jnp.exp(-x))


def _rmsnorm_fwd(name, x, gains, out_dtypes):
    T, D = x.shape
    tm = _row_tile(T, 512)
    n = len(gains)

    def body(x_ref, *refs):
        xv = x_ref[...]
        xh = xv * lax.rsqrt(jnp.mean(xv * xv, axis=-1, keepdims=True) + RMS_EPS)
        for g_ref, o_ref in zip(refs[:n], refs[n:]):
            o_ref[...] = (xh * g_ref[...]).astype(o_ref.dtype)

    row = pl.BlockSpec((tm, D), lambda i: (i, 0))
    vec = pl.BlockSpec((1, D), lambda i: (0, 0))
    return pl.pallas_call(
        body, name=name, grid=(T // tm,),
        in_specs=[row] + [vec] * n, out_specs=[row] * n,
        out_shape=[jax.ShapeDtypeStruct((T, D), dt) for dt in out_dtypes],
        compiler_params=_cparams("parallel"),
    )(x, *gains)


def _rmsnorm_bwd(name, x, pairs, dres):
    T, D = x.shape
    tm = _row_tile(T, 256)
    n = len(pairs)

    def body(x_ref, dres_ref, *refs):
        g_refs, dy_refs = refs[:n], refs[n:2 * n]
        dx_ref, dg_refs = refs[2 * n], refs[2 * n + 1:]
        i = pl.program_id(0)
        xv = x_ref[...]
        r = lax.rsqrt(jnp.mean(xv * xv, axis=-1, keepdims=True) + RMS_EPS)
        xh = xv * r
        dxh = jnp.zeros_like(xv)
        for g_ref, dy_ref, dg_ref in zip(g_refs, dy_refs, dg_refs):
            dy = dy_ref[...].astype(F32)
            part = jnp.sum(dy * xh, axis=0, keepdims=True)

            @pl.when(i == 0)
            def _():
                dg_ref[...] = part

            @pl.when(i > 0)
            def _():
                dg_ref[...] += part

            dxh = dxh + dy * g_ref[...]
        dx = r * (dxh - xh * jnp.mean(dxh * xh, axis=-1, keepdims=True))
        dx_ref[...] = dres_ref[...] + dx

    row = pl.BlockSpec((tm, D), lambda i: (i, 0))
    vec = pl.BlockSpec((1, D), lambda i: (0, 0))
    outs = pl.pallas_call(
        body, name=name, grid=(T // tm,),
        in_specs=[row, row] + [vec] * n + [row] * n,
        out_specs=[row] + [vec] * n,
        out_shape=[jax.ShapeDtypeStruct((T, D), F32)] + [jax.ShapeDtypeStruct((1, D), F32)] * n,
        compiler_params=_cparams("arbitrary"),
    )(x, dres, *[g for g, _ in pairs], *[dy for _, dy in pairs])
    return outs[0], outs[1:]


def _mm_nn(name, a, w, k_blk=0, res=None, out_dtype=F32):
    M, K = a.shape
    S, _, Ns = w.shape
    tm = _row_tile(M, 512)
    tn = _col_tile(Ns, 1408)
    nt = Ns // tn

    def body(a_ref, w_ref, *refs):
        o_ref = refs[-1]
        acc = jnp.dot(a_ref[...].astype(BF16), w_ref[...], preferred_element_type=F32)
        if res is not None:
            acc = acc + refs[0][...]
        o_ref[...] = acc.astype(o_ref.dtype)

    in_specs = [pl.BlockSpec((tm, K), lambda j, i: (i, 0)),
                pl.BlockSpec((None, K, tn), lambda j, i: (j // nt, k_blk, j % nt))]
    args = [a, w]
    if res is not None:
        in_specs.append(pl.BlockSpec((tm, tn), lambda j, i: (i, j)))
        args.append(res)
    return pl.pallas_call(
        body, name=name, grid=(S * nt, M // tm),
        in_specs=in_specs, out_specs=pl.BlockSpec((tm, tn), lambda j, i: (i, j)),
        out_shape=jax.ShapeDtypeStruct((M, S * Ns), out_dtype),
        compiler_params=_cparams("parallel", "parallel"),
    )(*args)


def _mm_nt_k(name, dy, w, K, k_blk=0, out_dtype=F32):
    M = dy.shape[0]
    S, _, Ns = w.shape
    tm = _row_tile(M, 512)
    tn = _col_tile(Ns, 1408)
    nt = Ns // tn
    steps = S * nt

    def body(dy_ref, w_ref, o_ref, acc_ref):
        j = pl.program_id(1)
        part = lax.dot_general(dy_ref[...].astype(BF16), w_ref[...], (((1,), (1,)), ((), ())),
                               preferred_element_type=F32)

        @pl.when(j == 0)
        def _():
            acc_ref[...] = part

        @pl.when(j > 0)
        def _():
            acc_ref[...] += part

        @pl.when(j == steps - 1)
        def _():
            o_ref[...] = acc_ref[...].astype(o_ref.dtype)

    return pl.pallas_call(
        body, name=name, grid=(M // tm, steps),
        in_specs=[pl.BlockSpec((tm, tn), lambda i, j: (i, j)),
                  pl.BlockSpec((None, K, tn), lambda i, j: (j // nt, k_blk, j % nt))],
        out_specs=pl.BlockSpec((tm, K), lambda i, j: (i, 0)),
        out_shape=jax.ShapeDtypeStruct((M, K), out_dtype),
        scratch_shapes=[pltpu.VMEM((tm, K), F32)],
        compiler_params=_cparams("parallel", "arbitrary"),
    )(dy, w)


def _mm_tn(name, a, dy, S, out_dtype=BF16):
    T, K = a.shape
    Ns = dy.shape[1] // S
    tn = _col_tile(Ns, max(128, (6 * 1024 * 1024) // (4 * K) // 128 * 128))
    nt = Ns // tn
    tt = _row_tile(T, 512)
    steps = T // tt

    def body(a_ref, dy_ref, o_ref, acc_ref):
        t = pl.program_id(1)
        part = lax.dot_general(a_ref[...].astype(BF16), dy_ref[...].astype(BF16), (((0,), (0,)), ((), ())),
                               preferred_element_type=F32)

        @pl.when(t == 0)
        def _():
            acc_ref[...] = part

        @pl.when(t > 0)
        def _():
            acc_ref[...] += part

        @pl.when(t == steps - 1)
        def _():
            o_ref[...] = acc_ref[...].astype(o_ref.dtype)

    return pl.pallas_call(
        body, name=name, grid=(S * nt, steps),
        in_specs=[pl.BlockSpec((tt, K), lambda j, t: (t, 0)),
                  pl.BlockSpec((tt, tn), lambda j, t: (t, j))],
        out_specs=pl.BlockSpec((None, K, tn), lambda j, t: (j // nt, 0, j % nt)),
        out_shape=jax.ShapeDtypeStruct((S, K, Ns), out_dtype),
        scratch_shapes=[pltpu.VMEM((K, tn), F32)],
        compiler_params=_cparams("parallel", "arbitrary"),
    )(a, dy)


def _gated_tile(T, width):
    return _row_tile(T, max(8, (2 * 1024 * 1024) // (4 * width) // 8 * 8))


def _glu_fwd(name, vg, h):
    T, D = h.shape
    tm = _gated_tile(T, 2 * D)

    def body(vg_ref, h_ref, o_ref):
        o_ref[...] = h_ref[...] + vg_ref[:, :D] * _sigmoid(vg_ref[:, D:])

    return pl.pallas_call(
        body, name=name, grid=(T // tm,),
        in_specs=[pl.BlockSpec((tm, 2 * D), lambda i: (i, 0)), pl.BlockSpec((tm, D), lambda i: (i, 0))],
        out_specs=pl.BlockSpec((tm, D), lambda i: (i, 0)),
        out_shape=jax.ShapeDtypeStruct((T, D), F32),
        compiler_params=_cparams("parallel"),
    )(vg, h)


def _glu_bwd(name, vg, dout):
    T, D = dout.shape
    tm = _gated_tile(T, 2 * D)

    def body(vg_ref, d_ref, o_ref):
        s = _sigmoid(vg_ref[:, D:])
        d = d_ref[...]
        o_ref[:, :D] = (d * s).astype(o_ref.dtype)
        o_ref[:, D:] = (d * vg_ref[:, :D] * s * (1.0 - s)).astype(o_ref.dtype)

    return pl.pallas_call(
        body, name=name, grid=(T // tm,),
        in_specs=[pl.BlockSpec((tm, 2 * D), lambda i: (i, 0)), pl.BlockSpec((tm, D), lambda i: (i, 0))],
        out_specs=pl.BlockSpec((tm, 2 * D), lambda i: (i, 0)),
        out_shape=jax.ShapeDtypeStruct((T, 2 * D), BF16),
        compiler_params=_cparams("parallel"),
    )(vg, dout)


def _swiglu_fwd(name, gu):
    T, W = gu.shape
    H = W // 2
    tm = _gated_tile(T, W)

    def body(gu_ref, o_ref):
        g = gu_ref[:, :H]
        o_ref[...] = (g * _sigmoid(g) * gu_ref[:, H:]).astype(o_ref.dtype)

    return pl.pallas_call(
        body, name=name, grid=(T // tm,),
        in_specs=[pl.BlockSpec((tm, W), lambda i: (i, 0))],
        out_specs=pl.BlockSpec((tm, H), lambda i: (i, 0)),
        out_shape=jax.ShapeDtypeStruct((T, H), BF16),
        compiler_params=_cparams("parallel"),
    )(gu)


def _swiglu_bwd(name, gu, dmid):
    T, W = gu.shape
    H = W // 2
    tm = _gated_tile(T, W)

    def body(gu_ref, d_ref, o_ref):
        g = gu_ref[:, :H]
        u = gu_ref[:, H:]
        d = d_ref[...]
        s = _sigmoid(g)
        o_ref[:, :H] = (d * u * s * (1.0 + g * (1.0 - s))).astype(o_ref.dtype)
        o_ref[:, H:] = (d * g * s).astype(o_ref.dtype)

    return pl.pallas_call(
        body, name=name, grid=(T // tm,),
        in_specs=[pl.BlockSpec((tm, W), lambda i: (i, 0)), pl.BlockSpec((tm, H), lambda i: (i, 0))],
        out_specs=pl.BlockSpec((tm, W), lambda i: (i, 0)),
        out_shape=jax.ShapeDtypeStruct((T, W), BF16),
        compiler_params=_cparams("parallel"),
    )(gu, dmid)


def _final_loss(name, h, gain, target):
    T, D = h.shape
    tm = X_START
    lead = X_START // tm

    def body(h_ref, g_ref, t_ref, loss_ref, dh_ref, dg_ref):
        i = pl.program_id(0)

        @pl.when(i == 0)
        def _():
            loss_ref[...] = jnp.zeros_like(loss_ref)
            dg_ref[...] = jnp.zeros_like(dg_ref)
            dh_ref[...] = jnp.zeros_like(dh_ref)

        @pl.when(i >= lead)
        def _():
            xv = h_ref[...]
            r = lax.rsqrt(jnp.mean(xv * xv, axis=-1, keepdims=True) + RMS_EPS)
            xh = xv * r
            g = g_ref[...]
            diff = xh * g - t_ref[...]
            loss_ref[...] += 0.5 * jnp.sum(jnp.mean(diff * diff, axis=-1, keepdims=True), axis=0, keepdims=True)
            dout = diff * (1.0 / D)
            dg_ref[...] += jnp.sum(dout * xh, axis=0, keepdims=True)
            dxh = dout * g
            dh_ref[...] = r * (dxh - xh * jnp.mean(dxh * xh, axis=-1, keepdims=True))

    return pl.pallas_call(
        body, name=name, grid=(T // tm,),
        in_specs=[pl.BlockSpec((tm, D), lambda i: (i, 0)), pl.BlockSpec((1, D), lambda i: (0, 0)),
                  pl.BlockSpec((tm, D), lambda i: (jnp.maximum(i - lead, 0), 0))],
        out_specs=[pl.BlockSpec((1, 128), lambda i: (0, 0)), pl.BlockSpec((tm, D), lambda i: (i, 0)),
                   pl.BlockSpec((1, D), lambda i: (0, 0))],
        out_shape=[jax.ShapeDtypeStruct((1, 128), F32), jax.ShapeDtypeStruct((T, D), F32),
                   jax.ShapeDtypeStruct((1, D), F32)],
        compiler_params=_cparams("arbitrary"),
    )(h, gain, target)


def _grid_call(name, body, grid, in_specs, out_specs, out_shape, scratch_shapes, args, comm=None, comm_args=()):
    params = _cparams(*(("arbitrary",) * len(grid)))
    if comm is None:
        return pl.pallas_call(body, name=name, grid=grid, in_specs=in_specs, out_specs=out_specs,
                              out_shape=out_shape, scratch_shapes=scratch_shapes, compiler_params=params)(*args), []
    outs = pl.pallas_call(
        _embed_comm(comm, body, grid, len(in_specs), len(out_specs)), name=name, grid=grid,
        in_specs=list(in_specs) + _any_specs(comm.n), out_specs=list(out_specs) + _any_specs(comm.n),
        out_shape=list(out_shape) + comm.out_shape, scratch_shapes=list(scratch_shapes) + comm.scratch,
        compiler_params=params)(*args, *comm_args)
    return outs[:len(out_specs)], outs[len(out_specs):]


def _s5_discretise(a_re, a_im, log_dt, bt_re, bt_im):
    dt = jnp.exp(log_dt)
    mag = jnp.exp(dt * a_re)
    ang = dt * a_im
    abar_re = mag * jnp.cos(ang)
    abar_im = mag * jnp.sin(ang)
    den = a_re * a_re + a_im * a_im
    coef_re = ((abar_re - 1.0) * a_re + abar_im * a_im) / den
    coef_im = (abar_im * a_re - (abar_re - 1.0) * a_im) / den
    bbar_re = coef_re * bt_re - coef_im * bt_im
    bbar_im = coef_re * bt_im + coef_im * bt_re
    return abar_re, abar_im, bbar_re, bbar_im


def _s5_prep_fwd(name, a_re, a_im, log_dt, bt_re, bt_im):
    G, _, P = a_re.shape
    C = bt_re.shape[1]

    def body(ar, ai, ld, br, bi, o_ar, o_ai, o_br, o_bi):
        outs = _s5_discretise(ar[...], ai[...], ld[...], br[...], bi[...])
        for o, v in zip((o_ar, o_ai, o_br, o_bi), outs):
            o[...] = v

    return pl.pallas_call(
        body, name=name,
        out_shape=[jax.ShapeDtypeStruct((G, 1, P), F32)] * 2 + [jax.ShapeDtypeStruct((G, C, P), F32)] * 2,
    )(a_re, a_im, log_dt, bt_re, bt_im)


def _s5_prep_bwd(name, a_re, a_im, log_dt, bt_re, bt_im, d_ar, d_ai, d_br, d_bi):
    G, _, P = a_re.shape
    C = bt_re.shape[1]

    def body(ar, ai, ld, br, bi, gar, gai, gbr, gbi, o_ar, o_ai, o_ld, o_br, o_bi):
        _, vjp = jax.vjp(_s5_discretise, ar[...], ai[...], ld[...], br[...], bi[...])
        grads = vjp((gar[...], gai[...], gbr[...], gbi[...]))
        for o, v in zip((o_ar, o_ai, o_ld, o_br, o_bi), grads):
            o[...] = v

    return pl.pallas_call(
        body, name=name,
        out_shape=[jax.ShapeDtypeStruct((G, 1, P), F32)] * 2 + [jax.ShapeDtypeStruct((G, 1, 1), F32)]
        + [jax.ShapeDtypeStruct((G, C, P), F32)] * 2,
    )(a_re, a_im, log_dt, bt_re, bt_im, d_ar, d_ai, d_br, d_bi)


def _cmul(ar, ai, br, bi):
    return ar * br - ai * bi, ar * bi + ai * br


def _power_table(a_re, a_im):
    rows_re, rows_im = [a_re], [a_im]
    for _ in range(7):
        r, m = _cmul(rows_re[-1], rows_im[-1], a_re, a_im)
        rows_re.append(r)
        rows_im.append(m)
    row = lax.broadcasted_iota(jnp.int32, (8, a_re.shape[1]), 0)
    t_re = jnp.zeros((8, a_re.shape[1]), F32)
    t_im = jnp.zeros((8, a_re.shape[1]), F32)
    for k in range(8):
        t_re = jnp.where(row == k, rows_re[k], t_re)
        t_im = jnp.where(row == k, rows_im[k], t_im)
    return t_re, t_im, rows_re, rows_im


def _s5_fwd(name, u, b_re, b_im, ct_re, ct_im, abar_re, abar_im, d_skip, comm=None, comm_args=()):
    T, D = u.shape
    n_st = D // CH_TILE
    W = STATE_TILE
    tc = _row_tile(T, 512)
    n_tiles = tc // 8

    def body(u_ref, bre_ref, bim_ref, cre_ref, cim_ref, ar_ref, ai_ref, d_ref,
             y_ref, z_ref, xr_ref, xi_ref, carry_re, carry_im, pw_re, pw_im, sh_re, sh_im):
        c = pl.program_id(1)

        @pl.when(c == 0)
        def _():
            t_re, t_im, rows_re, rows_im = _power_table(ar_ref[...], ai_ref[...])
            pw_re[...] = t_re
            pw_im[...] = t_im
            for n, k in enumerate((0, 1, 3)):
                sh_re[n] = jnp.broadcast_to(rows_re[k], (8, W))
                sh_im[n] = jnp.broadcast_to(rows_im[k], (8, W))
            carry_re[...] = jnp.zeros_like(carry_re)
            carry_im[...] = jnp.zeros_like(carry_im)

        ub = u_ref[...].astype(BF16)
        xr_ref[...] = jnp.dot(ub, bre_ref[...], preferred_element_type=F32)
        xi_ref[...] = jnp.dot(ub, bim_ref[...], preferred_element_type=F32)
        row = lax.broadcasted_iota(jnp.int32, (8, W), 0)

        def tile(i, carry):
            c_re, c_im = carry
            rows = pl.ds(pl.multiple_of(i * 8, 8), 8)
            r = xr_ref[rows, :]
            m = xi_ref[rows, :]
            for n, d in enumerate((1, 2, 4)):
                pr, pm = _cmul(sh_re[n], sh_im[n], pltpu.roll(r, d, 0), pltpu.roll(m, d, 0))
                r = r + jnp.where(row >= d, pr, 0.0)
                m = m + jnp.where(row >= d, pm, 0.0)
            pr, pm = _cmul(pw_re[...], pw_im[...], c_re, c_im)
            r = r + pr
            m = m + pm
            xr_ref[rows, :] = r
            xi_ref[rows, :] = m
            return jnp.broadcast_to(r[7:8, :], (8, W)), jnp.broadcast_to(m[7:8, :], (8, W))

        c_re, c_im = lax.fori_loop(0, n_tiles, tile, (carry_re[...], carry_im[...]))
        carry_re[...] = c_re
        carry_im[...] = c_im
        y = (jnp.dot(xr_ref[...].astype(BF16), cre_ref[...], preferred_element_type=F32)
             - jnp.dot(xi_ref[...].astype(BF16), cim_ref[...], preferred_element_type=F32)
             + d_ref[...] * u_ref[...])
        y_ref[...] = y
        z_ref[...] = _gelu(y).astype(z_ref.dtype)

    ch = pl.BlockSpec((tc, CH_TILE), lambda s, c: (c, s))
    st = pl.BlockSpec((tc, W), lambda s, c: (c, s))
    return _grid_call(
        name, body, (n_st, T // tc),
        in_specs=[ch,
                  pl.BlockSpec((None, CH_TILE, W), lambda s, c: (s, 0, 0)),
                  pl.BlockSpec((None, CH_TILE, W), lambda s, c: (s, 0, 0)),
                  pl.BlockSpec((None, W, CH_TILE), lambda s, c: (s, 0, 0)),
                  pl.BlockSpec((None, W, CH_TILE), lambda s, c: (s, 0, 0)),
                  pl.BlockSpec((1, W), lambda s, c: (0, s)),
                  pl.BlockSpec((1, W), lambda s, c: (0, s)),
                  pl.BlockSpec((1, CH_TILE), lambda s, c: (0, s))],
        out_specs=[ch, ch, st, st],
        out_shape=[jax.ShapeDtypeStruct((T, D), F32), jax.ShapeDtypeStruct((T, D), BF16),
                   jax.ShapeDtypeStruct((T, 4 * D), F32), jax.ShapeDtypeStruct((T, 4 * D), F32)],
        scratch_shapes=[pltpu.VMEM((8, W), F32), pltpu.VMEM((8, W), F32),
                        pltpu.VMEM((8, W), F32), pltpu.VMEM((8, W), F32),
                        pltpu.VMEM((3, 8, W), F32), pltpu.VMEM((3, 8, W), F32)],
        args=(u, b_re, b_im, ct_re, ct_im, abar_re, abar_im, d_skip), comm=comm, comm_args=comm_args)


def _s5_bwd(name, dz, y, u, x_re, x_im, c_re, c_im, bt_re, bt_im, abar_re, abar_im, d_skip, comm=None, comm_args=()):
    T, D = u.shape
    n_st = D // CH_TILE
    W = STATE_TILE
    tc = _row_tile(T, 512)
    n_chunks = T // tc
    n_tiles = tc // 8
    tiles_per_chunk = tc // 8

    def body(dz_ref, y_ref, u_ref, xr_ref, xi_ref, xpr_ref, xpi_ref, cre_ref, cim_ref, btr_ref, bti_ref,
             ar_ref, ai_ref, d_ref,
             du_ref, dd_ref, dbr_ref, dbi_ref, dcr_ref, dci_ref, dar_ref, dai_ref,
             lam_re, lam_im, xe_re, xe_im, carry_re, carry_im, pw_re, pw_im, sh_re, sh_im, acc_ar, acc_ai):
        k = pl.program_id(1)
        first_chunk = k == n_chunks - 1

        @pl.when(k == 0)
        def _():
            t_re, t_im, rows_re, rows_im = _power_table(ar_ref[...], -ai_ref[...])
            row = lax.broadcasted_iota(jnp.int32, (8, W), 0)
            r_re = jnp.zeros((8, W), F32)
            r_im = jnp.zeros((8, W), F32)
            for j in range(8):
                r_re = jnp.where(row == j, rows_re[7 - j], r_re)
                r_im = jnp.where(row == j, rows_im[7 - j], r_im)
            pw_re[...] = r_re
            pw_im[...] = r_im
            for n, j in enumerate((0, 1, 3)):
                sh_re[n] = jnp.broadcast_to(rows_re[j], (8, W))
                sh_im[n] = jnp.broadcast_to(rows_im[j], (8, W))
            carry_re[...] = jnp.zeros_like(carry_re)
            carry_im[...] = jnp.zeros_like(carry_im)
            acc_ar[...] = jnp.zeros_like(acc_ar)
            acc_ai[...] = jnp.zeros_like(acc_ai)
            dd_ref[...] = jnp.zeros_like(dd_ref)
            dbr_ref[...] = jnp.zeros_like(dbr_ref)
            dbi_ref[...] = jnp.zeros_like(dbi_ref)
            dcr_ref[...] = jnp.zeros_like(dcr_ref)
            dci_ref[...] = jnp.zeros_like(dci_ref)

        uv = u_ref[...]
        dy = dz_ref[...] * _gelu_grad(y_ref[...])
        dyb = dy.astype(BF16)
        lam_re[...] = jnp.dot(dyb, cre_ref[...], preferred_element_type=F32)
        lam_im[...] = -jnp.dot(dyb, cim_ref[...], preferred_element_type=F32)
        keep = jnp.where(first_chunk, 0.0, 1.0)
        xe_re[pl.ds(0, 8), :] = xpr_ref[...] * keep
        xe_im[pl.ds(0, 8), :] = xpi_ref[...] * keep
        xe_re[pl.ds(8, tc), :] = xr_ref[...]
        xe_im[pl.ds(8, tc), :] = xi_ref[...]
        row = lax.broadcasted_iota(jnp.int32, (8, W), 0)

        def tile(n, carry):
            c_re, c_im, s_ar, s_ai = carry
            i = n_tiles - 1 - n
            rows = pl.ds(pl.multiple_of(i * 8, 8), 8)
            r = lam_re[rows, :]
            m = lam_im[rows, :]
            for q, d in enumerate((1, 2, 4)):
                pr, pm = _cmul(sh_re[q], sh_im[q], pltpu.roll(r, 8 - d, 0), pltpu.roll(m, 8 - d, 0))
                r = r + jnp.where(row < 8 - d, pr, 0.0)
                m = m + jnp.where(row < 8 - d, pm, 0.0)
            pr, pm = _cmul(pw_re[...], pw_im[...], c_re, c_im)
            r = r + pr
            m = m + pm
            lam_re[rows, :] = r
            lam_im[rows, :] = m
            cur_re = xe_re[pl.ds(pl.multiple_of(i * 8 + 8, 8), 8), :]
            cur_im = xe_im[pl.ds(pl.multiple_of(i * 8 + 8, 8), 8), :]
            bef_re = xe_re[rows, :]
            bef_im = xe_im[rows, :]
            xp_re = jnp.where(row == 0, jnp.broadcast_to(bef_re[7:8, :], (8, W)), pltpu.roll(cur_re, 1, 0))
            xp_im = jnp.where(row == 0, jnp.broadcast_to(bef_im[7:8, :], (8, W)), pltpu.roll(cur_im, 1, 0))
            s_ar = s_ar + r * xp_re + m * xp_im
            s_ai = s_ai + m * xp_re - r * xp_im
            return jnp.broadcast_to(r[0:1, :], (8, W)), jnp.broadcast_to(m[0:1, :], (8, W)), s_ar, s_ai

        c_re, c_im, s_ar, s_ai = lax.fori_loop(
            0, n_tiles, tile, (carry_re[...], carry_im[...], acc_ar[...], acc_ai[...]))
        carry_re[...] = c_re
        carry_im[...] = c_im
        acc_ar[...] = s_ar
        acc_ai[...] = s_ai
        lr = lam_re[...].astype(BF16)
        li = lam_im[...].astype(BF16)
        du_ref[...] = (dy * d_ref[...] + jnp.dot(lr, btr_ref[...], preferred_element_type=F32)
                       + jnp.dot(li, bti_ref[...], preferred_element_type=F32))
        dd_ref[...] += jnp.sum(dy * uv, axis=0, keepdims=True)
        tn_dims = (((0,), (0,)), ((), ()))
        ub = uv.astype(BF16)
        dbr_ref[...] += lax.dot_general(ub, lr, tn_dims, preferred_element_type=F32)
        dbi_ref[...] += lax.dot_general(ub, li, tn_dims, preferred_element_type=F32)
        dcr_ref[...] += lax.dot_general(dyb, xr_ref[...].astype(BF16), tn_dims, preferred_element_type=F32)
        dci_ref[...] -= lax.dot_general(dyb, xi_ref[...].astype(BF16), tn_dims, preferred_element_type=F32)

        @pl.when(first_chunk)
        def _():
            dar_ref[...] = jnp.sum(acc_ar[...], axis=0, keepdims=True)
            dai_ref[...] = jnp.sum(acc_ai[...], axis=0, keepdims=True)

    rev = lambda k: n_chunks - 1 - k
    ch = pl.BlockSpec((tc, CH_TILE), lambda s, k: (rev(k), s))
    st = pl.BlockSpec((tc, W), lambda s, k: (rev(k), s))
    prev = pl.BlockSpec((8, W), lambda s, k: (jnp.maximum(rev(k) * tiles_per_chunk - 1, 0), s))
    mat_cw = pl.BlockSpec((None, CH_TILE, W), lambda s, k: (s, 0, 0))
    mat_wc = pl.BlockSpec((None, W, CH_TILE), lambda s, k: (s, 0, 0))
    vec_w = pl.BlockSpec((1, W), lambda s, k: (0, s))
    vec_c = pl.BlockSpec((1, CH_TILE), lambda s, k: (0, s))
    dense = jax.ShapeDtypeStruct((n_st, CH_TILE, W), F32)
    return _grid_call(
        name, body, (n_st, n_chunks),
        in_specs=[ch, ch, ch, st, st, prev, prev, mat_cw, mat_cw, mat_wc, mat_wc, vec_w, vec_w, vec_c],
        out_specs=[ch, vec_c, mat_cw, mat_cw, mat_cw, mat_cw, vec_w, vec_w],
        out_shape=[jax.ShapeDtypeStruct((T, D), F32), jax.ShapeDtypeStruct((1, D), F32), dense, dense, dense, dense,
                   jax.ShapeDtypeStruct((1, 4 * D), F32), jax.ShapeDtypeStruct((1, 4 * D), F32)],
        scratch_shapes=[pltpu.VMEM((tc, W), F32), pltpu.VMEM((tc, W), F32),
                        pltpu.VMEM((tc + 8, W), F32), pltpu.VMEM((tc + 8, W), F32),
                        pltpu.VMEM((8, W), F32), pltpu.VMEM((8, W), F32),
                        pltpu.VMEM((8, W), F32), pltpu.VMEM((8, W), F32),
                        pltpu.VMEM((3, 8, W), F32), pltpu.VMEM((3, 8, W), F32),
                        pltpu.VMEM((8, W), F32), pltpu.VMEM((8, W), F32)],
        args=(dz, y, u, x_re, x_im, x_re, x_im, c_re, c_im, bt_re, bt_im, abar_re, abar_im, d_skip),
        comm=comm, comm_args=comm_args)


NT_DIMS = (((1,), (1,)), ((), ()))
TN_DIMS = (((0,), (0,)), ((), ()))


def _sums_matrix(strictly_later, copies):
    jj = lax.broadcasted_iota(jnp.int32, (copies * KEY_BLOCK, 2 * KEY_BLOCK), 0) & (KEY_BLOCK - 1)
    ss = lax.broadcasted_iota(jnp.int32, (copies * KEY_BLOCK, 2 * KEY_BLOCK), 1)
    tri = (jj > ss) if strictly_later else (jj < ss)
    return (tri | (ss >= KEY_BLOCK)).astype(BF16)


def _split_heads(blk):
    first = lax.broadcasted_iota(jnp.int32, blk.shape, 1) < HEAD_DIM
    zero = jnp.zeros_like(blk)
    return jnp.concatenate([jnp.where(first, blk, zero), jnp.where(first, zero, blk)], axis=0)


LOG2_E = 1.4426950408889634


def _sb_scores(z, mask, later):
    z2 = z * LOG2_E
    minus_abs = lax.bitcast_convert_type(lax.bitcast_convert_type(z2, jnp.uint32) | jnp.uint32(0x80000000), F32)
    lb = jnp.minimum(z2, 0.0) - jnp.log2(1.0 + jnp.exp2(minus_abs))
    lm = lb - z2
    if mask is not None:
        lm = jnp.where(mask, lm, 0.0)
    hi = lm.astype(BF16)
    lo = (lm - hi.astype(F32)).astype(BF16)
    return lb, jnp.dot(jnp.concatenate([hi, lo], axis=1), later, preferred_element_type=F32)


def _key_rows(kb):
    return pl.ds(pl.multiple_of(kb * KEY_BLOCK, KEY_BLOCK), KEY_BLOCK)


def _sb_mask(q_row0, k_row0, tq):
    tpos = q_row0 + lax.broadcasted_iota(jnp.int32, (tq, KEY_BLOCK), 0)
    spos = k_row0 + lax.broadcasted_iota(jnp.int32, (tq, KEY_BLOCK), 1)
    return (spos < tpos) & (spos >= META_START)


def _key_block_phases(iq, per_q, block, ascending):
    first_diag = iq * per_q

    def run(lo, n, masked):
        def step(i, carry):
            block(lo + i if ascending else lo + n - 1 - i, masked)
            return carry
        lax.fori_loop(0, n, step, 0)

    phases = [(0, jnp.minimum(iq, 1), True), (1, jnp.maximum(first_diag - 1, 0), False), (first_diag, per_q, True)]
    for lo, n, masked in (phases if ascending else phases[::-1]):
        run(lo, n, masked)


def _attn_fwd(name, q, kv):
    T, D = q.shape
    n_hp = D // 128
    tq = _row_tile(T, 512)
    per_q = tq // KEY_BLOCK
    scale = 1.0 / math.sqrt(HEAD_DIM)

    def body(q_ref, k_ref, v_ref, o_ref, l_ref, z_buf, w_buf, acc_ref, run_ref):
        iq = pl.program_id(1)
        n_kb = (iq + 1) * per_q
        qs = q_ref[...] * jnp.asarray(scale, BF16)
        later = _sums_matrix(True, 2)

        def scores(kb):
            return lax.dot_general(qs, _split_heads(k_ref[_key_rows(kb), :]), NT_DIMS, preferred_element_type=F32)

        def weighted_values(kb):
            return jnp.dot(w_buf[...], _split_heads(v_ref[_key_rows(kb), :]), preferred_element_type=F32)

        acc_ref[...] = jnp.zeros_like(acc_ref)
        run_ref[...] = jnp.zeros_like(run_ref)
        w_buf[...] = jnp.zeros_like(w_buf)
        z_buf[...] = scores(n_kb - 1)

        def block(kb, masked):
            acc_ref[...] += weighted_values(jnp.minimum(kb + 1, n_kb - 1))
            z_next = scores(jnp.maximum(kb - 1, 0))
            mask = _sb_mask(iq * tq, kb * KEY_BLOCK, tq) if masked else None
            heads = [_sb_scores(z_buf[:, h * KEY_BLOCK:(h + 1) * KEY_BLOCK], mask, later) for h in range(2)]
            for h, (lb, sums) in enumerate(heads):
                run = run_ref[h]
                w = jnp.exp2(lb + sums[:, :KEY_BLOCK] + run)
                if masked:
                    w = jnp.where(mask, w, 0.0)
                w_buf[:, h * KEY_BLOCK:(h + 1) * KEY_BLOCK] = w.astype(BF16)
                run_ref[h] = run + sums[:, KEY_BLOCK:]
            z_buf[...] = z_next

        _key_block_phases(iq, per_q, block, ascending=False)
        acc_ref[...] += weighted_values(0)
        lane = lax.broadcasted_iota(jnp.int32, (tq, 128), 1)
        o_ref[...] = acc_ref[...].astype(o_ref.dtype)
        l_ref[...] = jnp.where(lane < HEAD_DIM, run_ref[0], run_ref[1])

    blk = pl.BlockSpec((tq, 128), lambda h, i: (i, h))
    return pl.pallas_call(
        body, name=name, grid=(n_hp, T // tq),
        in_specs=[blk, pl.BlockSpec((T, 128), lambda h, i: (0, h)), pl.BlockSpec((T, 128), lambda h, i: (0, n_hp + h))],
        out_specs=[blk, blk],
        out_shape=[jax.ShapeDtypeStruct((T, D), BF16), jax.ShapeDtypeStruct((T, D), F32)],
        scratch_shapes=[pltpu.VMEM((tq, 2 * KEY_BLOCK), F32), pltpu.VMEM((tq, 2 * KEY_BLOCK), BF16),
                        pltpu.VMEM((tq, 128), F32), pltpu.VMEM((2, tq, 128), F32)],
        compiler_params=_cparams("parallel", "arbitrary"),
    )(q, kv, kv)


def _attn_bwd(name, q, kv, do, ltot):
    T, D = q.shape
    n_hp = D // 128
    tq = _row_tile(T, 512)
    n_q = T // tq
    per_q = tq // KEY_BLOCK
    scale = 1.0 / math.sqrt(HEAD_DIM)

    def body(q_ref, k_ref, v_ref, do_ref, l_ref, dq_ref, dk_ref, dv_ref,
             dk_acc, dv_acc, dq_acc, lpre_ref, cpre_ref, z_buf, dw_buf, dz_buf, w_buf):
        iq = pl.program_id(1)
        n_kb = (iq + 1) * per_q

        @pl.when(iq == 0)
        def _():
            dk_acc[...] = jnp.zeros_like(dk_acc)
            dv_acc[...] = jnp.zeros_like(dv_acc)

        first = lax.broadcasted_iota(jnp.int32, (tq, 128), 1) < HEAD_DIM
        qs = q_ref[...] * jnp.asarray(scale, BF16)
        dov = do_ref[...]
        ltv = l_ref[...]
        swapped = pltpu.roll(ltv, HEAD_DIM, 1)
        ltot = [jnp.where(first, ltv, swapped), jnp.where(first, swapped, ltv)]
        zero = jnp.zeros_like(qs)
        q_stack = jnp.concatenate([jnp.where(first, qs, zero), jnp.where(first, zero, qs)], axis=0)
        do_stack = jnp.concatenate([jnp.where(first, dov, zero), jnp.where(first, zero, dov)], axis=0)
        later = _sums_matrix(True, 2)
        earlier = _sums_matrix(False, 1)

        def scores(kb):
            rows = _key_rows(kb)
            return (lax.dot_general(qs, _split_heads(k_ref[rows, :]), NT_DIMS, preferred_element_type=F32),
                    lax.dot_general(dov, _split_heads(v_ref[rows, :]), NT_DIMS, preferred_element_type=F32))

        def flush(kb):
            rows = _key_rows(kb)
            k_heads = _split_heads(k_ref[rows, :])
            dq_acc[...] += (jnp.dot(dz_buf[:tq, :], k_heads[:KEY_BLOCK, :], preferred_element_type=F32)
                            + jnp.dot(dz_buf[tq:, :], k_heads[KEY_BLOCK:, :], preferred_element_type=F32))
            dk_acc[rows, :] += lax.dot_general(dz_buf[...], q_stack, TN_DIMS, preferred_element_type=F32)
            dv_acc[rows, :] += lax.dot_general(w_buf[...], do_stack, TN_DIMS, preferred_element_type=F32)

        dq_acc[...] = jnp.zeros_like(dq_acc)
        lpre_ref[...] = jnp.zeros_like(lpre_ref)
        cpre_ref[...] = jnp.zeros_like(cpre_ref)
        dz_buf[...] = jnp.zeros_like(dz_buf)
        w_buf[...] = jnp.zeros_like(w_buf)
        z_buf[...], dw_buf[...] = scores(0)

        def block(kb, masked):
            flush(jnp.maximum(kb - 1, 0))
            z_next, dw_next = scores(jnp.minimum(kb + 1, n_kb - 1))
            mask = _sb_mask(iq * tq, kb * KEY_BLOCK, tq) if masked else None
            heads = [_sb_scores(z_buf[:, h * KEY_BLOCK:(h + 1) * KEY_BLOCK], mask, later) for h in range(2)]
            grads = []
            for h, (lb, sums) in enumerate(heads):
                after = ltot[h] - lpre_ref[h] - sums[:, KEY_BLOCK:]
                w = jnp.exp2(lb + sums[:, :KEY_BLOCK] + after)
                if masked:
                    w = jnp.where(mask, w, 0.0)
                da = w * dw_buf[:, h * KEY_BLOCK:(h + 1) * KEY_BLOCK]
                w_buf[h * tq:(h + 1) * tq, :] = w.astype(BF16)
                lpre_ref[h] += sums[:, KEY_BLOCK:]
                grads.append((da, jnp.dot(da.astype(BF16), earlier, preferred_element_type=F32)))
            for h, (da, dsums) in enumerate(grads):
                sig = jnp.exp2(heads[h][0])
                through_later = sig * (dsums[:, :KEY_BLOCK] + cpre_ref[h])
                if masked:
                    through_later = jnp.where(mask, through_later, 0.0)
                dz_buf[h * tq:(h + 1) * tq, :] = (da * (1.0 - sig) - through_later).astype(BF16)
                cpre_ref[h] += dsums[:, KEY_BLOCK:]
            z_buf[...] = z_next
            dw_buf[...] = dw_next

        _key_block_phases(iq, per_q, block, ascending=True)
        flush(n_kb - 1)
        dq_ref[...] = (dq_acc[...] * scale).astype(dq_ref.dtype)

        @pl.when(iq == n_q - 1)
        def _():
            dk_ref[...] = dk_acc[...].astype(dk_ref.dtype)
            dv_ref[...] = dv_acc[...].astype(dv_ref.dtype)

    blk = pl.BlockSpec((tq, 128), lambda h, i: (i, h))
    full = pl.BlockSpec((T, 128), lambda h, i: (0, h))
    return pl.pallas_call(
        body, name=name, grid=(n_hp, n_q),
        in_specs=[blk, full, pl.BlockSpec((T, 128), lambda h, i: (0, n_hp + h)), blk, blk],
        out_specs=[blk, full, full],
        out_shape=[jax.ShapeDtypeStruct((T, D), BF16)] * 3,
        scratch_shapes=[pltpu.VMEM((T, 128), F32), pltpu.VMEM((T, 128), F32), pltpu.VMEM((tq, 128), F32),
                        pltpu.VMEM((2, tq, 128), F32), pltpu.VMEM((2, tq, 128), F32),
                        pltpu.VMEM((tq, 2 * KEY_BLOCK), F32), pltpu.VMEM((tq, 2 * KEY_BLOCK), F32),
                        pltpu.VMEM((2 * tq, 128), BF16), pltpu.VMEM((2 * tq, 128), BF16)],
        compiler_params=_cparams("parallel", "arbitrary"),
    )(q, kv, kv, do, ltot)


def _adamw(name, w, g, m, v):
    shape = w.shape
    size = w.size
    if w.ndim >= 2 and shape[-1] % 128 == 0:
        cols = shape[-1]
    else:
        cols = 1024 if size % 1024 == 0 else shape[-1]
    rows = size // cols
    tm = _row_tile(rows, max(8, (1024 * 1024) // (4 * cols) // 8 * 8)) if rows % 8 == 0 else rows
    c1 = 1.0 / (1.0 - ADAM_B1 ** ADAM_STEP)
    c2 = 1.0 / (1.0 - ADAM_B2 ** ADAM_STEP)

    def body(w_ref, g_ref, m_ref, v_ref, d_ref, nm_ref, nv_ref):
        gv = g_ref[...]
        nm = ADAM_B1 * m_ref[...] + (1.0 - ADAM_B1) * gv
        nv = ADAM_B2 * v_ref[...] + (1.0 - ADAM_B2) * (gv * gv)
        d_ref[...] = -ADAM_LR * ((nm * c1) / (jnp.sqrt(nv * c2) + ADAM_EPS) + ADAM_WD * w_ref[...])
        nm_ref[...] = nm
        nv_ref[...] = nv

    blk = pl.BlockSpec((tm, cols), lambda i: (i, 0))
    outs = pl.pallas_call(
        body, name=name, grid=(rows // tm,),
        in_specs=[blk] * 4, out_specs=[blk] * 3,
        out_shape=[jax.ShapeDtypeStruct((rows, cols), F32)] * 3,
        compiler_params=_cparams("parallel"),
    )(*[t.reshape(rows, cols) for t in (w, g, m, v)])
    return tuple(o.reshape(shape) for o in outs)


def _any_specs(n):
    return [pl.BlockSpec(memory_space=pl.ANY)] * n


def _chip_index():
    return 2 * lax.axis_index("x") + lax.axis_index("y")


def _place():
    x, y, c = lax.axis_index("x"), lax.axis_index("y"), lax.axis_index("c")
    chips = [(1 - x, y), (x, 1 - y), (1 - x, 1 - y)]
    return x, y, c, chips


def _all_gather_chips(name, shards):
    n = len(shards)

    def body(*refs):
        x_refs, o_refs = refs[:n], refs[n:2 * n]
        send_sems, recv_sems = refs[2 * n:]
        x, y, c, chips = _place()
        me = 2 * x + y
        sibling = (x, y, 1 - c)

        def half(ref, i, which):
            h = shards[i].shape[0] // 2
            return ref.at[pl.ds(which * h, h)]

        def remote(k, i, src, dst, to):
            return pltpu.make_async_remote_copy(src_ref=src, dst_ref=dst, send_sem=send_sems.at[k, i],
                                                recv_sem=recv_sems.at[k, i], device_id=to, device_id_type=MESH)

        sent = []
        for j, chip in enumerate(chips):
            for i in range(n):
                cp = remote(j, i, half(x_refs[i], i, c), half(o_refs[i].at[me], i, c), (*chip, c))
                cp.start()
                sent.append(cp)
        for j, chip in enumerate(chips):
            pj = 2 * chip[0] + chip[1]
            for i in range(n):
                landed = half(o_refs[i].at[pj], i, c)
                remote(j, i, landed, landed, (*chip, c)).wait_recv()
                cp = remote(3 + j, i, landed, landed, sibling)
                cp.start()
                sent.append(cp)
        for j, chip in enumerate(chips):
            pj = 2 * chip[0] + chip[1]
            for i in range(n):
                got = half(o_refs[i].at[pj], i, 1 - c)
                remote(3 + j, i, got, got, sibling).wait_recv()
        for cp in sent:
            cp.wait_send()

    outs = pl.pallas_call(
        body, name=name,
        in_specs=_any_specs(n), out_specs=_any_specs(n),
        out_shape=[jax.ShapeDtypeStruct((4,) + s.shape, s.dtype) for s in shards],
        scratch_shapes=[pltpu.SemaphoreType.DMA((6, n)), pltpu.SemaphoreType.DMA((6, n))],
    )(*shards)
    return [lax.dynamic_update_slice(o, s[None], (_chip_index(), 0, 0)) for o, s in zip(outs, shards)]


def _pair_split(name, grads):
    n = len(grads)

    def body(*refs):
        g_refs, got_refs = refs[:n], refs[n:2 * n]
        send_sems, recv_sems = refs[2 * n:]
        x, y, c, _ = _place()
        sibling = (x, y, 1 - c)
        sent = []
        for i in range(n):
            h = grads[i].shape[1] // 2
            rc = pltpu.make_async_remote_copy(
                src_ref=g_refs[i].at[:, pl.ds((1 - c) * h, h)], dst_ref=got_refs[i],
                send_sem=send_sems.at[i], recv_sem=recv_sems.at[i], device_id=sibling, device_id_type=MESH)
            rc.start()
            sent.append(rc)
        for rc in sent:
            rc.wait()

    return pl.pallas_call(
        body, name=name,
        in_specs=_any_specs(n), out_specs=_any_specs(n),
        out_shape=[jax.ShapeDtypeStruct((4, g.shape[1] // 2, g.shape[2]), g.dtype) for g in grads],
        scratch_shapes=[pltpu.SemaphoreType.DMA((n,)), pltpu.SemaphoreType.DMA((n,))],
    )(*grads)


def _chip_exchange(name, sums):
    return _exchange_finish(sums, _run_comm(name, _exchange_comm(sums), sums))


class _Comm:
    def __init__(self, out_shape, copies):
        self.n = len(out_shape)
        self.out_shape = out_shape
        self.copies = copies
        self.scratch = [pltpu.SemaphoreType.DMA((3, self.n)), pltpu.SemaphoreType.DMA((3, self.n))]

    def start(self, *refs):
        for cp in self.copies(*refs, False):
            cp.start()

    def finish(self, *refs):
        for cp in self.copies(*refs, True):
            cp.wait_recv()
        for cp in self.copies(*refs, False):
            cp.wait_send()


def _exchange_comm(sums):
    n = len(sums)

    def copies(s_refs, o_refs, send_sems, recv_sems, mirrors):
        x, y, c, chips = _place()
        me = 2 * x + y
        out = []
        for j, chip in enumerate(chips):
            pj = 2 * chip[0] + chip[1]
            for i in range(n):
                out.append(pltpu.make_async_remote_copy(
                    src_ref=s_refs[i].at[pj], dst_ref=o_refs[i].at[pj if mirrors else me], send_sem=send_sems.at[j, i],
                    recv_sem=recv_sems.at[j, i], device_id=(*chip, c), device_id_type=MESH))
        return out

    return _Comm([jax.ShapeDtypeStruct(s.shape, s.dtype) for s in sums], copies)


def _exchange_finish(sums, outs):
    me = _chip_index()
    return [lax.dynamic_update_slice(o, lax.dynamic_index_in_dim(s, me, 0, keepdims=True), (me, 0, 0))
            for o, s in zip(outs, sums)]


def _halves_gather_comm(shards):
    n = len(shards)

    def copies(x_refs, o_refs, send_sems, recv_sems, mirrors):
        x, y, c, chips = _place()
        me = 2 * x + y
        out = []
        for j, chip in enumerate(chips):
            pj = 2 * chip[0] + chip[1]
            for i in range(n):
                h = shards[i].shape[0] // 2
                rows = pl.ds(c * h, h)
                out.append(pltpu.make_async_remote_copy(
                    src_ref=x_refs[i].at[rows], dst_ref=o_refs[i].at[pj if mirrors else me, rows],
                    send_sem=send_sems.at[j, i], recv_sem=recv_sems.at[j, i], device_id=(*chip, c), device_id_type=MESH))
        return out

    return _Comm([jax.ShapeDtypeStruct((4,) + s.shape, s.dtype) for s in shards], copies)


def _run_comm(name, comm, arrays):
    n = comm.n

    def body(*refs):
        comm.start(refs[:n], refs[n:2 * n], *refs[2 * n:])
        comm.finish(refs[:n], refs[n:2 * n], *refs[2 * n:])

    return pl.pallas_call(
        body, name=name, in_specs=_any_specs(n), out_specs=_any_specs(n),
        out_shape=comm.out_shape, scratch_shapes=comm.scratch,
    )(*arrays)


def _embed_comm(comm, body, grid, n_in, n_out):
    n = comm.n

    def wrapped(*refs):
        ins, c_in = refs[:n_in], refs[n_in:n_in + n]
        outs, c_out = refs[n_in + n:n_in + n + n_out], refs[n_in + n + n_out:n_in + 2 * n + n_out]
        scratch, sems = refs[n_in + 2 * n + n_out:-2], refs[-2:]
        ids = [pl.program_id(a) for a in range(len(grid))]
        first = functools.reduce(jnp.logical_and, [i == 0 for i in ids])
        last = functools.reduce(jnp.logical_and, [i == g - 1 for i, g in zip(ids, grid)])

        @pl.when(first)
        def _():
            comm.start(c_in, c_out, *sems)

        body(*ins, *outs, *scratch)

        @pl.when(last)
        def _():
            comm.finish(c_in, c_out, *sems)

    return wrapped


def _pair_forward(name, landed, shards):
    n = len(landed)

    def body(*refs):
        o_refs = refs[n:2 * n]
        send_sems, recv_sems = refs[2 * n:]
        x, y, c, chips = _place()
        sibling = (x, y, 1 - c)
        sent, arriving = [], []
        for j, chip in enumerate(chips):
            pj = 2 * chip[0] + chip[1]
            for i in range(n):
                h = shards[i].shape[0] // 2
                mine = o_refs[i].at[pj, pl.ds(c * h, h)]
                theirs = o_refs[i].at[pj, pl.ds((1 - c) * h, h)]
                for ref, group in ((mine, sent), (theirs, arriving)):
                    group.append(pltpu.make_async_remote_copy(
                        src_ref=ref, dst_ref=ref, send_sem=send_sems.at[j, i], recv_sem=recv_sems.at[j, i],
                        device_id=sibling, device_id_type=MESH))
        for cp in sent:
            cp.start()
        for cp in arriving:
            cp.wait_recv()
        for cp in sent:
            cp.wait_send()

    outs = pl.pallas_call(
        body, name=name, in_specs=_any_specs(n), out_specs=_any_specs(n),
        out_shape=[jax.ShapeDtypeStruct(a.shape, a.dtype) for a in landed],
        input_output_aliases={i: i for i in range(n)},
        scratch_shapes=[pltpu.SemaphoreType.DMA((3, n)), pltpu.SemaphoreType.DMA((3, n))],
    )(*landed)
    return [lax.dynamic_update_slice(o, s[None], (_chip_index(), 0, 0)) for o, s in zip(outs, shards)]


def _pair_join(name, halves):
    n = len(halves)

    def body(*refs):
        h_refs, o_refs = refs[:n], refs[n:2 * n]
        send_sems, recv_sems = refs[2 * n:]
        x, y, c, _ = _place()
        sibling = (x, y, 1 - c)
        sent = []
        for i in range(n):
            h = halves[i].shape[0]
            mine = o_refs[i].at[pl.ds(c * h, h)]
            rc = pltpu.make_async_remote_copy(
                src_ref=h_refs[i], dst_ref=mine, send_sem=send_sems.at[i], recv_sem=recv_sems.at[i],
                device_id=sibling, device_id_type=MESH)
            rc.start()
            sent.append(rc)
        for i in range(n):
            h = halves[i].shape[0]
            theirs = o_refs[i].at[pl.ds((1 - c) * h, h)]
            pltpu.make_async_remote_copy(
                src_ref=h_refs[i], dst_ref=theirs, send_sem=send_sems.at[i], recv_sem=recv_sems.at[i],
                device_id=sibling, device_id_type=MESH).wait_recv()
        for rc in sent:
            rc.wait_send()

    outs = pl.pallas_call(
        body, name=name,
        in_specs=_any_specs(n), out_specs=_any_specs(n),
        out_shape=[jax.ShapeDtypeStruct((2 * s.shape[0], s.shape[1]), s.dtype) for s in halves],
        scratch_shapes=[pltpu.SemaphoreType.DMA((n,)), pltpu.SemaphoreType.DMA((n,))],
    )(*halves)
    c = lax.axis_index("c")
    return [lax.dynamic_update_slice(o, s, (c * s.shape[0], 0)) for o, s in zip(outs, halves)]


def _add_pair(name, a, b):
    _, H, C = a.shape
    th = _row_tile(H, max(8, (1024 * 1024) // (4 * C) // 8 * 8))

    def body(a_ref, b_ref, o_ref):
        o_ref[...] = (a_ref[...].astype(F32) + b_ref[...].astype(F32)).astype(o_ref.dtype)

    blk = pl.BlockSpec((None, th, C), lambda q, i: (q, i, 0))
    return pl.pallas_call(
        body, name=name, grid=(4, H // th), in_specs=[blk, blk], out_specs=blk,
        out_shape=jax.ShapeDtypeStruct(a.shape, a.dtype), compiler_params=_cparams("parallel", "parallel"),
    )(a, b)


def _add_chips(name, parts):
    _, H, C = parts.shape
    th = _row_tile(H, max(8, (1024 * 1024) // (4 * C) // 8 * 8))

    def body(p_ref, o_ref):
        acc = p_ref[0].astype(F32)
        for q in range(1, 4):
            acc = acc + p_ref[q].astype(F32)
        o_ref[...] = acc

    return pl.pallas_call(
        body, name=name, grid=(H // th,),
        in_specs=[pl.BlockSpec((4, th, C), lambda i: (0, i, 0))], out_specs=pl.BlockSpec((th, C), lambda i: (i, 0)),
        out_shape=jax.ShapeDtypeStruct((H, C), F32), compiler_params=_cparams("parallel"),
    )(parts)


def _pair_sums(tag, grads):
    c = lax.axis_index("c")
    got = _pair_split(f"rs_pair_split_{tag}", grads)
    kept = [lax.dynamic_slice_in_dim(g, c * (g.shape[1] // 2), g.shape[1] // 2, axis=1) for g in grads]
    return [_add_pair(f"rs_add_pair_{tag}_{i}", k, g) for i, (k, g) in enumerate(zip(kept, got))]


def _chip_sums(parts):
    return _pair_join("rs_pair_join", [_add_chips(f"rs_add_chips_{i}", p) for i, p in enumerate(parts)])


def _block_diag(t):
    G, A, B = t.shape
    eye = jnp.eye(8, dtype=t.dtype)
    return jnp.einsum("sgab,gh->sgahb", t.reshape(G // 8, 8, A, B), eye).reshape(G // 8, 8 * A, 8 * B)


def _block_diag_extract(m, A, B):
    n = m.shape[0]
    eye = jnp.eye(8, dtype=m.dtype)
    return jnp.einsum("sgahb,gh->sgab", m.reshape(n, 8, A, 8, B), eye).reshape(8 * n, A, B)


def kernel(x, meta_tokens, norm_mix, norm_ffn, s5_a_re, s5_a_im, s5_log_dt, s5_b_re, s5_b_im, s5_c_re, s5_c_im, s5_d, s5_w_glu, norm_kv, w_kv, w_q, w_o, w_ffn_in, w_ffn_out, norm_final, loss_target, m_meta_tokens, m_norm_mix, m_norm_ffn, m_s5_a_re, m_s5_a_im, m_s5_log_dt, m_s5_b_re, m_s5_b_im, m_s5_c_re, m_s5_c_im, m_s5_d, m_s5_w_glu, m_norm_kv, m_w_kv, m_w_q, m_w_o, m_w_ffn_in, m_w_ffn_out, m_norm_final, v_meta_tokens, v_norm_mix, v_norm_ffn, v_s5_a_re, v_s5_a_im, v_s5_log_dt, v_s5_b_re, v_s5_b_im, v_s5_c_re, v_s5_c_im, v_s5_d, v_s5_w_glu, v_norm_kv, v_w_kv, v_w_q, v_w_o, v_w_ffn_in, v_w_ffn_out, v_norm_final):
    seq, D = x.shape[1], x.shape[2]
    T = X_START + seq
    G, P, C = s5_a_re.shape[1], S5_STATE, S5_GROUP
    d_ff = w_ffn_out.shape[1] * 4
    dq4 = D // 4
    chip = 2 * lax.axis_index("x") + lax.axis_index("y")

    small_in = jnp.concatenate([meta_tokens, jnp.pad(s5_d, ((0, 15), (0, 0)))], axis=0)
    (small_all,) = _all_gather_chips("ag_small", [small_in])
    meta_full = small_all[:, :N_META, :].transpose(1, 0, 2).reshape(N_META, D)
    d_skip = small_all[:, N_META, :].reshape(1, D)
    shards = [s.astype(BF16) for s in (s5_w_glu[0], w_kv, w_q[0], w_o[0],
                                       w_ffn_in.reshape(2 * D, -1), w_ffn_out.reshape(-1, D))]
    rows_out = d_ff // 4

    row = lambda v: v.reshape(1, -1)
    g_mix0, g_mix1 = row(norm_mix[0]), row(norm_mix[1])
    g_ffn = [row(norm_ffn[0]), row(norm_ffn[1])]
    g_kv, g_final = row(norm_kv), row(norm_final)

    a_re3 = s5_a_re[0].reshape(G, 1, P)
    a_im3 = s5_a_im[0].reshape(G, 1, P)
    log_dt3 = s5_log_dt[0].reshape(G, 1, 1)
    bt_re = s5_b_re[0].transpose(0, 2, 1)
    bt_im = s5_b_im[0].transpose(0, 2, 1)
    ab_re, ab_im, bb_re, bb_im = _s5_prep_fwd("s5_prep", a_re3, a_im3, log_dt3, bt_re, bt_im)
    abar_re, abar_im = ab_re.reshape(1, G * P), ab_im.reshape(1, G * P)
    bd_b_re = _block_diag(bb_re).astype(BF16)
    bd_b_im = _block_diag(bb_im).astype(BF16)
    bd_bt_re = bd_b_re.transpose(0, 2, 1)
    bd_bt_im = bd_b_im.transpose(0, 2, 1)
    bd_c_re = _block_diag(s5_c_re[0]).astype(BF16)
    bd_c_im = _block_diag(s5_c_im[0]).astype(BF16)
    bd_ct_re = bd_c_re.transpose(0, 2, 1)
    bd_ct_im = bd_c_im.transpose(0, 2, 1)

    h0 = jnp.concatenate([jnp.zeros((META_START, D), F32), meta_full, x[0]], axis=0)
    (u,) = _rmsnorm_fwd("norm_mix0", h0, [g_mix0], [F32])
    (y, z, x_re, x_im), landed = _s5_fwd(
        "s5_scan", u, bd_b_re, bd_b_im, bd_ct_re, bd_ct_im, abar_re, abar_im, d_skip,
        comm=_halves_gather_comm(shards), comm_args=shards)
    wg_glu, wg_kv, wg_q, wg_o, wg_in, wg_out = _pair_forward("ag_forward", landed, shards)
    wg_q = wg_q.reshape(1, D, D)
    wg_o = wg_o.reshape(1, D, D)
    wg_out = [wg_out[:, l * rows_out:(l + 1) * rows_out, :].reshape(1, d_ff, D) for l in range(2)]
    vg = _mm_nn("glu_proj", z, wg_glu)
    h1 = _glu_fwd("glu", vg, h0)

    def ffn_fwd(l, h):
        (n,) = _rmsnorm_fwd(f"norm_ffn{l}", h, [g_ffn[l]], [BF16])
        gu = _mm_nn(f"ffn_in{l}", n, wg_in, k_blk=l)
        mid = _swiglu_fwd(f"swiglu{l}", gu)
        return n, gu, mid, _mm_nn(f"ffn_out{l}", mid, wg_out[l], res=h)

    n1, gu0, mid0, h2 = ffn_fwd(0, h1)
    nk, nq = _rmsnorm_fwd("norm_kv_q", h2, [g_kv, g_mix1], [BF16, BF16])
    kv = _mm_nn("kv_proj", nk, wg_kv, out_dtype=BF16)
    q = _mm_nn("q_proj", nq, wg_q, out_dtype=BF16)
    o, ltot = _attn_fwd("attn_fwd", q, kv)
    h3 = _mm_nn("o_proj", o, wg_o, res=h2)
    n3, gu1, mid1, h4 = ffn_fwd(1, h3)
    loss_part, dh4, dg_final = _final_loss("final_loss", h4, g_final, loss_target[0])

    def ffn_bwd(l, dh, h, n, gu, mid):
        dmid = _mm_nt_k(f"ffn_out{l}_dx", dh, wg_out[l], d_ff)
        dgu = _swiglu_bwd(f"swiglu{l}_bwd", gu, dmid)
        dw_out = _mm_tn(f"ffn_out{l}_dw", mid, dh, 1)
        dw_in = _mm_tn(f"ffn_in{l}_dw", n, dgu, 4)
        dn = _mm_nt_k(f"ffn_in{l}_dx", dgu, wg_in, D, k_blk=l)
        dh_prev, (dg,) = _rmsnorm_bwd(f"norm_ffn{l}_bwd", h, [(g_ffn[l], dn)], dh)
        return dh_prev, dg, dw_in, dw_out

    dh3, dg_ffn1, dw_in1, dw_out1 = ffn_bwd(1, dh4, h3, n3, gu1, mid1)
    d_o = _mm_nt_k("o_proj_dx", dh3, wg_o, D, out_dtype=BF16)
    dw_o = _mm_tn("o_proj_dw", o, dh3, 1)
    dq, dk, dv = _attn_bwd("attn_bwd", q, kv, d_o, ltot)
    dkv = jnp.concatenate([dk, dv], axis=1)
    dw_q = _mm_tn("q_proj_dw", nq, dq, 1)
    dnq = _mm_nt_k("q_proj_dx", dq, wg_q, D)
    dw_kv = _mm_tn("kv_proj_dw", nk, dkv, 4)
    dnk = _mm_nt_k("kv_proj_dx", dkv, wg_kv, D)
    dh2, (dg_mix1, dg_kv) = _rmsnorm_bwd("norm_kv_q_bwd", h2, [(g_mix1, dnq), (g_kv, dnk)], dh3)
    late = [dw_kv, dw_q.reshape(4, D // 4, D), dw_o.reshape(4, D // 4, D), dw_in1, dw_out1.reshape(4, rows_out, D)]
    late_pairs = _pair_sums("late", late)
    dh1, dg_ffn0, dw_in0, dw_out0 = ffn_bwd(0, dh2, h1, n1, gu0, mid0)
    dvg = _glu_bwd("glu_bwd", vg, dh1)
    dw_glu = _mm_tn("glu_proj_dw", z, dvg, 4)
    dz = _mm_nt_k("glu_proj_dx", dvg, wg_glu, D)
    (du, dd, dbd_b_re, dbd_b_im, dbd_c_re, dbd_c_im, dab_re, dab_im), late_parts = _s5_bwd(
        "s5_scan_bwd", dz, y, u, x_re, x_im, bd_c_re, bd_c_im, bd_bt_re, bd_bt_im, abar_re, abar_im, d_skip,
        comm=_exchange_comm(late_pairs), comm_args=late_pairs)
    late_parts = _exchange_finish(late_pairs, late_parts)
    dh0, (dg_mix0,) = _rmsnorm_bwd("norm_mix0_bwd", h0, [(g_mix0, du)], dh1)
    da_re, da_im, dlog_dt, dbt_re, dbt_im = _s5_prep_bwd(
        "s5_prep_bwd", a_re3, a_im3, log_dt3, bt_re, bt_im,
        dab_re.reshape(G, 1, P), dab_im.reshape(G, 1, P),
        _block_diag_extract(dbd_b_re, C, P), _block_diag_extract(dbd_b_im, C, P))
    grad_x = dh0[X_START:][None]

    small_parts = [
        dg_mix0, dg_mix1, dg_ffn0, dg_ffn1, da_re, da_im,
        dbt_re.transpose(0, 2, 1), dbt_im.transpose(0, 2, 1),
        _block_diag_extract(dbd_c_re, C, P), _block_diag_extract(dbd_c_im, C, P),
        dg_kv, dg_final, dh0[META_START:X_START], dd, loss_part, dlog_dt]
    small_sizes = [p.size for p in small_parts]
    unit = 4 * 2 * 8 * 128
    padded = -(-sum(small_sizes) // unit) * unit
    tail = jnp.concatenate([loss_part.reshape(-1), dlog_dt.reshape(-1)])
    small_flat = jnp.concatenate(
        [p.reshape(-1) for p in small_parts[:-2]] + [jnp.pad(tail, (0, padded - sum(small_sizes)))])
    small_blocks = small_flat.reshape(4, padded // (4 * 128), 128)
    early = [dw_glu, dw_in0, dw_out0.reshape(4, rows_out, D), small_blocks]
    early_parts = _chip_exchange("rs_chip_exchange", _pair_sums("early", early))
    gw_kv, gw_q, gw_o, gw_in1, gw_out1, gw_glu, gw_in0, gw_out0, small_mine = _chip_sums(late_parts + early_parts)
    gw_in = jnp.concatenate([gw_in0, gw_in1], axis=0)
    gw_out = jnp.concatenate([gw_out0, gw_out1], axis=0)
    (small_red,) = _all_gather_chips("ag_small_grads", [small_mine])
    small_red = small_red.reshape(-1)
    pieces, at = [], 0
    for p, size in zip(small_parts, small_sizes):
        pieces.append(small_red[at:at + size].reshape(p.shape))
        at += size
    (gn_mix0, gn_mix1, gn_ffn0, gn_ffn1, ga_re, ga_im, gb_re, gb_im, gc_re, gc_im, gn_kv, gn_final,
     gmeta_full, gd_full, loss_all, glog_dt) = pieces
    loss = loss_all[0, 0]
    gn_mix = small_red[:2 * D].reshape(2, D)
    gn_ffn = small_red[2 * D:4 * D].reshape(2, D)
    gmeta = lax.dynamic_slice_in_dim(gmeta_full, chip * dq4, dq4, axis=1)
    gd = lax.dynamic_slice_in_dim(gd_full, chip * dq4, dq4, axis=1)

    grads = {
        "meta_tokens": gmeta, "norm_mix": gn_mix, "norm_ffn": gn_ffn,
        "s5_a_re": ga_re.reshape(s5_a_re.shape), "s5_a_im": ga_im.reshape(s5_a_im.shape),
        "s5_log_dt": glog_dt.reshape(s5_log_dt.shape),
        "s5_b_re": gb_re.reshape(s5_b_re.shape), "s5_b_im": gb_im.reshape(s5_b_im.shape),
        "s5_c_re": gc_re.reshape(s5_c_re.shape), "s5_c_im": gc_im.reshape(s5_c_im.shape),
        "s5_d": gd, "s5_w_glu": gw_glu.reshape(s5_w_glu.shape), "norm_kv": gn_kv.reshape(norm_kv.shape),
        "w_kv": gw_kv, "w_q": gw_q.reshape(w_q.shape), "w_o": gw_o.reshape(w_o.shape),
        "w_ffn_in": gw_in.reshape(w_ffn_in.shape), "w_ffn_out": gw_out.reshape(w_ffn_out.shape),
        "norm_final": gn_final.reshape(norm_final.shape),
    }
    weights = {
        "meta_tokens": (meta_tokens, m_meta_tokens, v_meta_tokens), "norm_mix": (norm_mix, m_norm_mix, v_norm_mix),
        "norm_ffn": (norm_ffn, m_norm_ffn, v_norm_ffn), "s5_a_re": (s5_a_re, m_s5_a_re, v_s5_a_re),
        "s5_a_im": (s5_a_im, m_s5_a_im, v_s5_a_im), "s5_log_dt": (s5_log_dt, m_s5_log_dt, v_s5_log_dt),
        "s5_b_re": (s5_b_re, m_s5_b_re, v_s5_b_re), "s5_b_im": (s5_b_im, m_s5_b_im, v_s5_b_im),
        "s5_c_re": (s5_c_re, m_s5_c_re, v_s5_c_re), "s5_c_im": (s5_c_im, m_s5_c_im, v_s5_c_im),
        "s5_d": (s5_d, m_s5_d, v_s5_d), "s5_w_glu": (s5_w_glu, m_s5_w_glu, v_s5_w_glu),
        "norm_kv": (norm_kv, m_norm_kv, v_norm_kv), "w_kv": (w_kv, m_w_kv, v_w_kv), "w_q": (w_q, m_w_q, v_w_q),
        "w_o": (w_o, m_w_o, v_w_o), "w_ffn_in": (w_ffn_in, m_w_ffn_in, v_w_ffn_in),
        "w_ffn_out": (w_ffn_out, m_w_ffn_out, v_w_ffn_out), "norm_final": (norm_final, m_norm_final, v_norm_final),
    }
    names = list(weights)
    deltas, new_m, new_v = [], [], []
    for name in names:
        w, m, v = weights[name]
        d, nm, nv = _adamw(f"adamw_{name}", w, grads[name], m, v)
        deltas.append(d)
        new_m.append(nm)
        new_v.append(nv)
    return (loss, grad_x, *[grads[n] for n in names], *deltas, *new_m, *new_v)
```

```python
import functools
import math

import jax
import jax.numpy as jnp
from jax import lax
from jax.experimental import pallas as pl
from jax.experimental.pallas import tpu as pltpu

F32 = jnp.float32
BF16 = jnp.bfloat16

N_META = 16
X_START = 128
META_START = X_START - N_META
S5_GROUP = 16
S5_STATE = 64
HEAD_DIM = 64
KEY_BLOCK = 128
STATE_TILE = 512
CH_TILE = 128
RMS_EPS = 1e-6
ADAM_LR, ADAM_B1, ADAM_B2, ADAM_EPS, ADAM_WD, ADAM_STEP = 0.001, 0.9, 0.999, 1e-08, 0.01, 10
VMEM_LIMIT_BYTES = 48 * 1024 * 1024
MESH = pl.DeviceIdType.MESH
NT_DIMS = (((1,), (1,)), ((), ()))
TN_DIMS = (((0,), (0,)), ((), ()))


def _cparams(*sem):
    return pltpu.CompilerParams(dimension_semantics=sem, vmem_limit_bytes=VMEM_LIMIT_BYTES)


def _row_tile(rows, cap):
    for unit in (128, 8):
        best = 0
        for t in range(unit, min(rows, cap) + 1, unit):
            if rows % t == 0:
                best = t
        if best:
            return best
    return rows


def _col_tile(cols, cap):
    best = 0
    for t in range(128, min(cols, cap) + 1, 128):
        if cols % t == 0:
            best = t
    return best if best else cols


def _gelu(x):
    k = math.sqrt(2.0 / math.pi)
    return 0.5 * x * (1.0 + jnp.tanh(k * (x + 0.044715 * x * x * x)))


def _gelu_grad(x):
    k = math.sqrt(2.0 / math.pi)
    t = jnp.tanh(k * (x + 0.044715 * x * x * x))
    return 0.5 * (1.0 + t) + 0.5 * x * (1.0 - t * t) * k * (1.0 + 3.0 * 0.044715 * x * x)


def _sigmoid(x):
    return 1.0 / (1.0 + jnp.exp(-x))


def _rmsnorm_fwd(name, x, gains, out_dtypes):
    T, D = x.shape
    tm = _row_tile(T, 512)
    n = len(gains)

    def body(x_ref, *refs):
        xv = x_ref[...]
        xh = xv * lax.rsqrt(jnp.mean(xv * xv, axis=-1, keepdims=True) + RMS_EPS)
        for g_ref, o_ref in zip(refs[:n], refs[n:]):
            o_ref[...] = (xh * g_ref[...]).astype(o_ref.dtype)

    row = pl.BlockSpec((tm, D), lambda i: (i, 0))
    vec = pl.BlockSpec((1, D), lambda i: (0, 0))
    return pl.pallas_call(
        body, name=name, grid=(T // tm,),
        in_specs=[row] + [vec] * n, out_specs=[row] * n,
        out_shape=[jax.ShapeDtypeStruct((T, D), dt) for dt in out_dtypes],
        compiler_params=_cparams("parallel"),
    )(x, *gains)


def _rmsnorm_bwd(name, x, pairs, dres):
    T, D = x.shape
    tm = _row_tile(T, 256)
    n = len(pairs)

    def body(x_ref, dres_ref, *refs):
        g_refs, dy_refs = refs[:n], refs[n:2 * n]
        dx_ref, dg_refs = refs[2 * n], refs[2 * n + 1:]
        i = pl.program_id(0)
        xv = x_ref[...]
        r = lax.rsqrt(jnp.mean(xv * xv, axis=-1, keepdims=True) + RMS_EPS)
        xh = xv * r
        dxh = jnp.zeros_like(xv)
        for g_ref, dy_ref, dg_ref in zip(g_refs, dy_refs, dg_refs):
            dy = dy_ref[...].astype(F32)
            part = jnp.sum(dy * xh, axis=0, keepdims=True)

            @pl.when(i == 0)
            def _():
                dg_ref[...] = part

            @pl.when(i > 0)
            def _():
                dg_ref[...] += part

            dxh = dxh + dy * g_ref[...]
        dx = r * (dxh - xh * jnp.mean(dxh * xh, axis=-1, keepdims=True))
        dx_ref[...] = dres_ref[...] + dx

    row = pl.BlockSpec((tm, D), lambda i: (i, 0))
    vec = pl.BlockSpec((1, D), lambda i: (0, 0))
    outs = pl.pallas_call(
        body, name=name, grid=(T // tm,),
        in_specs=[row, row] + [vec] * n + [row] * n,
        out_specs=[row] + [vec] * n,
        out_shape=[jax.ShapeDtypeStruct((T, D), F32)] + [jax.ShapeDtypeStruct((1, D), F32)] * n,
        compiler_params=_cparams("arbitrary"),
    )(x, dres, *[g for g, _ in pairs], *[dy for _, dy in pairs])
    return outs[0], outs[1:]


def _mm_nn(name, a, w, k_blk=0, res=None, out_dtype=F32):
    M, K = a.shape
    S, _, Ns = w.shape
    tm = _row_tile(M, 512)
    tn = _col_tile(Ns, 1408)
    nt = Ns // tn

    def body(a_ref, w_ref, *refs):
        o_ref = refs[-1]
        acc = jnp.dot(a_ref[...].astype(BF16), w_ref[...], preferred_element_type=F32)
        if res is not None:
            acc = acc + refs[0][...]
        o_ref[...] = acc.astype(o_ref.dtype)

    in_specs = [pl.BlockSpec((tm, K), lambda j, i: (i, 0)),
                pl.BlockSpec((None, K, tn), lambda j, i: (j // nt, k_blk, j % nt))]
    args = [a, w]
    if res is not None:
        in_specs.append(pl.BlockSpec((tm, tn), lambda j, i: (i, j)))
        args.append(res)
    return pl.pallas_call(
        body, name=name, grid=(S * nt, M // tm),
        in_specs=in_specs, out_specs=pl.BlockSpec((tm, tn), lambda j, i: (i, j)),
        out_shape=jax.ShapeDtypeStruct((M, S * Ns), out_dtype),
        compiler_params=_cparams("parallel", "parallel"),
    )(*args)


def _mm_nt_k(name, dy, w, K, k_blk=0, out_dtype=F32):
    M = dy.shape[0]
    S, _, Ns = w.shape
    tm = _row_tile(M, 512)
    tn = _col_tile(Ns, 1408)
    nt = Ns // tn
    steps = S * nt

    def body(dy_ref, w_ref, o_ref, acc_ref):
        j = pl.program_id(1)
        part = lax.dot_general(dy_ref[...].astype(BF16), w_ref[...], (((1,), (1,)), ((), ())),
                               preferred_element_type=F32)

        @pl.when(j == 0)
        def _():
            acc_ref[...] = part

        @pl.when(j > 0)
        def _():
            acc_ref[...] += part

        @pl.when(j == steps - 1)
        def _():
            o_ref[...] = acc_ref[...].astype(o_ref.dtype)

    return pl.pallas_call(
        body, name=name, grid=(M // tm, steps),
        in_specs=[pl.BlockSpec((tm, tn), lambda i, j: (i, j)),
                  pl.BlockSpec((None, K, tn), lambda i, j: (j // nt, k_blk, j % nt))],
        out_specs=pl.BlockSpec((tm, K), lambda i, j: (i, 0)),
        out_shape=jax.ShapeDtypeStruct((M, K), out_dtype),
        scratch_shapes=[pltpu.VMEM((tm, K), F32)],
        compiler_params=_cparams("parallel", "arbitrary"),
    )(dy, w)


def _mm_tn(name, a, dy, S, out_dtype=BF16):
    T, K = a.shape
    Ns = dy.shape[1] // S
    tn = _col_tile(Ns, max(128, (6 * 1024 * 1024) // (4 * K) // 128 * 128))
    nt = Ns // tn
    tt = _row_tile(T, 512)
    steps = T // tt

    def body(a_ref, dy_ref, o_ref, acc_ref):
        t = pl.program_id(1)
        part = lax.dot_general(a_ref[...].astype(BF16), dy_ref[...].astype(BF16), (((0,), (0,)), ((), ())),
                               preferred_element_type=F32)

        @pl.when(t == 0)
        def _():
            acc_ref[...] = part

        @pl.when(t > 0)
        def _():
            acc_ref[...] += part

        @pl.when(t == steps - 1)
        def _():
            o_ref[...] = acc_ref[...].astype(o_ref.dtype)

    return pl.pallas_call(
        body, name=name, grid=(S * nt, steps),
        in_specs=[pl.BlockSpec((tt, K), lambda j, t: (t, 0)),
                  pl.BlockSpec((tt, tn), lambda j, t: (t, j))],
        out_specs=pl.BlockSpec((None, K, tn), lambda j, t: (j // nt, 0, j % nt)),
        out_shape=jax.ShapeDtypeStruct((S, K, Ns), out_dtype),
        scratch_shapes=[pltpu.VMEM((K, tn), F32)],
        compiler_params=_cparams("parallel", "arbitrary"),
    )(a, dy)


def _gated_tile(T, width):
    return _row_tile(T, max(8, (2 * 1024 * 1024) // (4 * width) // 8 * 8))


def _glu_fwd(name, vg, h):
    T, D = h.shape
    tm = _gated_tile(T, 2 * D)

    def body(vg_ref, h_ref, o_ref):
        o_ref[...] = h_ref[...] + vg_ref[:, :D] * _sigmoid(vg_ref[:, D:])

    return pl.pallas_call(
        body, name=name, grid=(T // tm,),
        in_specs=[pl.BlockSpec((tm, 2 * D), lambda i: (i, 0)), pl.BlockSpec((tm, D), lambda i: (i, 0))],
        out_specs=pl.BlockSpec((tm, D), lambda i: (i, 0)),
        out_shape=jax.ShapeDtypeStruct((T, D), F32),
        compiler_params=_cparams("parallel"),
    )(vg, h)


def _glu_bwd(name, vg, dout):
    T, D = dout.shape
    tm = _gated_tile(T, 2 * D)

    def body(vg_ref, d_ref, o_ref):
        s = _sigmoid(vg_ref[:, D:])
        d = d_ref[...]
        o_ref[:, :D] = (d * s).astype(o_ref.dtype)
        o_ref[:, D:] = (d * vg_ref[:, :D] * s * (1.0 - s)).astype(o_ref.dtype)

    return pl.pallas_call(
        body, name=name, grid=(T // tm,),
        in_specs=[pl.BlockSpec((tm, 2 * D), lambda i: (i, 0)), pl.BlockSpec((tm, D), lambda i: (i, 0))],
        out_specs=pl.BlockSpec((tm, 2 * D), lambda i: (i, 0)),
        out_shape=jax.ShapeDtypeStruct((T, 2 * D), BF16),
        compiler_params=_cparams("parallel"),
    )(vg, dout)


def _mm_swiglu(name, a, w, k_blk):
    M, K = a.shape
    Ns = w.shape[2]
    tm = _row_tile(M, 512)

    def body(a_ref, wg_ref, wu_ref, g_ref, u_ref, mid_ref):
        ab = a_ref[...].astype(BF16)
        g = jnp.dot(ab, wg_ref[...], preferred_element_type=F32)
        u = jnp.dot(ab, wu_ref[...], preferred_element_type=F32)
        g_ref[...] = g.astype(g_ref.dtype)
        u_ref[...] = u.astype(u_ref.dtype)
        mid_ref[...] = (g * _sigmoid(g) * u).astype(mid_ref.dtype)

    out = pl.BlockSpec((tm, Ns), lambda j, i: (i, j))
    return pl.pallas_call(
        body, name=name, grid=(2, M // tm),
        in_specs=[pl.BlockSpec((tm, K), lambda j, i: (i, 0)),
                  pl.BlockSpec((None, K, Ns), lambda j, i: (j, k_blk, 0)),
                  pl.BlockSpec((None, K, Ns), lambda j, i: (j + 2, k_blk, 0))],
        out_specs=[out, out, out],
        out_shape=[jax.ShapeDtypeStruct((M, 2 * Ns), BF16)] * 3,
        compiler_params=_cparams("parallel", "parallel"),
    )(a, w, w)


def _mm_nt_swiglu_bwd(name, dh, w, g, u):
    M, D = dh.shape
    F = w.shape[1]
    tm = _row_tile(M, 512)

    def body(dh_ref, w_ref, g_ref, u_ref, o_ref):
        d = lax.dot_general(dh_ref[...].astype(BF16), w_ref[...], NT_DIMS, preferred_element_type=F32)
        gv = g_ref[...].astype(F32)
        s = _sigmoid(gv)
        o_ref[:, :F] = (d * u_ref[...].astype(F32) * s * (1.0 + gv * (1.0 - s))).astype(o_ref.dtype)
        o_ref[:, F:] = (d * gv * s).astype(o_ref.dtype)

    half = pl.BlockSpec((tm, F), lambda i: (i, 0))
    return pl.pallas_call(
        body, name=name, grid=(M // tm,),
        in_specs=[pl.BlockSpec((tm, D), lambda i: (i, 0)), pl.BlockSpec((None, F, D), lambda i: (0, 0, 0)), half, half],
        out_specs=pl.BlockSpec((tm, 2 * F), lambda i: (i, 0)),
        out_shape=jax.ShapeDtypeStruct((M, 2 * F), BF16),
        compiler_params=_cparams("parallel"),
    )(dh, w, g, u)


def _final_loss(name, h, gain, target):
    T, D = h.shape
    tm = X_START
    lead = X_START // tm

    def body(h_ref, g_ref, t_ref, loss_ref, dh_ref, dg_ref):
        i = pl.program_id(0)

        @pl.when(i == 0)
        def _():
            loss_ref[...] = jnp.zeros_like(loss_ref)
            dg_ref[...] = jnp.zeros_like(dg_ref)
            dh_ref[...] = jnp.zeros_like(dh_ref)

        @pl.when(i >= lead)
        def _():
            xv = h_ref[...]
            r = lax.rsqrt(jnp.mean(xv * xv, axis=-1, keepdims=True) + RMS_EPS)
            xh = xv * r
            g = g_ref[...]
            diff = xh * g - t_ref[...]
            loss_ref[...] += 0.5 * jnp.sum(jnp.mean(diff * diff, axis=-1, keepdims=True), axis=0, keepdims=True)
            dout = diff * (1.0 / D)
            dg_ref[...] += jnp.sum(dout * xh, axis=0, keepdims=True)
            dxh = dout * g
            dh_ref[...] = r * (dxh - xh * jnp.mean(dxh * xh, axis=-1, keepdims=True))

    return pl.pallas_call(
        body, name=name, grid=(T // tm,),
        in_specs=[pl.BlockSpec((tm, D), lambda i: (i, 0)), pl.BlockSpec((1, D), lambda i: (0, 0)),
                  pl.BlockSpec((tm, D), lambda i: (jnp.maximum(i - lead, 0), 0))],
        out_specs=[pl.BlockSpec((1, 128), lambda i: (0, 0)), pl.BlockSpec((tm, D), lambda i: (i, 0)),
                   pl.BlockSpec((1, D), lambda i: (0, 0))],
        out_shape=[jax.ShapeDtypeStruct((1, 128), F32), jax.ShapeDtypeStruct((T, D), F32),
                   jax.ShapeDtypeStruct((1, D), F32)],
        compiler_params=_cparams("arbitrary"),
    )(h, gain, target)


def _grid_call(name, body, grid, in_specs, out_specs, out_shape, scratch_shapes, args, comm=None, comm_args=()):
    params = _cparams(*(("arbitrary",) * len(grid)))
    if comm is None:
        return pl.pallas_call(body, name=name, grid=grid, in_specs=in_specs, out_specs=out_specs,
                              out_shape=out_shape, scratch_shapes=scratch_shapes, compiler_params=params)(*args), []
    outs = pl.pallas_call(
        _embed_comm(comm, body, grid, len(in_specs), len(out_specs)), name=name, grid=grid,
        in_specs=list(in_specs) + _any_specs(comm.n), out_specs=list(out_specs) + _any_specs(comm.n),
        out_shape=list(out_shape) + comm.out_shape, scratch_shapes=list(scratch_shapes) + comm.scratch,
        compiler_params=params)(*args, *comm_args)
    return outs[:len(out_specs)], outs[len(out_specs):]


def _s5_discretise(a_re, a_im, log_dt, bt_re, bt_im):
    dt = jnp.exp(log_dt)
    mag = jnp.exp(dt * a_re)
    ang = dt * a_im
    abar_re = mag * jnp.cos(ang)
    abar_im = mag * jnp.sin(ang)
    den = a_re * a_re + a_im * a_im
    coef_re = ((abar_re - 1.0) * a_re + abar_im * a_im) / den
    coef_im = (abar_im * a_re - (abar_re - 1.0) * a_im) / den
    bbar_re = coef_re * bt_re - coef_im * bt_im
    bbar_im = coef_re * bt_im + coef_im * bt_re
    return abar_re, abar_im, bbar_re, bbar_im


def _s5_prep_fwd(name, a_re, a_im, log_dt, bt_re, bt_im):
    G, _, P = a_re.shape
    C = bt_re.shape[1]

    def body(ar, ai, ld, br, bi, o_ar, o_ai, o_br, o_bi):
        outs = _s5_discretise(ar[...], ai[...], ld[...], br[...], bi[...])
        for o, v in zip((o_ar, o_ai, o_br, o_bi), outs):
            o[...] = v

    return pl.pallas_call(
        body, name=name,
        out_shape=[jax.ShapeDtypeStruct((G, 1, P), F32)] * 2 + [jax.ShapeDtypeStruct((G, C, P), F32)] * 2,
    )(a_re, a_im, log_dt, bt_re, bt_im)


def _s5_prep_bwd(name, a_re, a_im, log_dt, bt_re, bt_im, d_ar, d_ai, d_br, d_bi):
    G, _, P = a_re.shape
    C = bt_re.shape[1]

    def body(ar, ai, ld, br, bi, gar, gai, gbr, gbi, o_ar, o_ai, o_ld, o_br, o_bi):
        _, vjp = jax.vjp(_s5_discretise, ar[...], ai[...], ld[...], br[...], bi[...])
        grads = vjp((gar[...], gai[...], gbr[...], gbi[...]))
        for o, v in zip((o_ar, o_ai, o_ld, o_br, o_bi), grads):
            o[...] = v

    return pl.pallas_call(
        body, name=name,
        out_shape=[jax.ShapeDtypeStruct((G, 1, P), F32)] * 2 + [jax.ShapeDtypeStruct((G, 1, 1), F32)]
        + [jax.ShapeDtypeStruct((G, C, P), F32)] * 2,
    )(a_re, a_im, log_dt, bt_re, bt_im, d_ar, d_ai, d_br, d_bi)


def _cmul(ar, ai, br, bi):
    return ar * br - ai * bi, ar * bi + ai * br


def _power_table(a_re, a_im):
    rows_re, rows_im = [a_re], [a_im]
    for _ in range(7):
        r, m = _cmul(rows_re[-1], rows_im[-1], a_re, a_im)
        rows_re.append(r)
        rows_im.append(m)
    row = lax.broadcasted_iota(jnp.int32, (8, a_re.shape[1]), 0)
    t_re = jnp.zeros((8, a_re.shape[1]), F32)
    t_im = jnp.zeros((8, a_re.shape[1]), F32)
    for k in range(8):
        t_re = jnp.where(row == k, rows_re[k], t_re)
        t_im = jnp.where(row == k, rows_im[k], t_im)
    return t_re, t_im, rows_re, rows_im


def _s5_fwd(name, u, b_re, b_im, ct_re, ct_im, abar_re, abar_im, d_skip, comm=None, comm_args=()):
    T, D = u.shape
    n_st = D // CH_TILE
    W = STATE_TILE
    tc = _row_tile(T, 512)
    n_tiles = tc // 8

    def body(u_ref, bre_ref, bim_ref, cre_ref, cim_ref, ar_ref, ai_ref, d_ref,
             y_ref, z_ref, xr_ref, xi_ref, carry_re, carry_im, pw_re, pw_im, sh_re, sh_im):
        c = pl.program_id(1)

        @pl.when(c == 0)
        def _():
            t_re, t_im, rows_re, rows_im = _power_table(ar_ref[...], ai_ref[...])
            pw_re[...] = t_re
            pw_im[...] = t_im
            first_rows = lax.broadcasted_iota(jnp.int32, (8, W), 0)
            for n, d in enumerate((1, 2, 4)):
                sh_re[n] = jnp.where(first_rows >= d, jnp.broadcast_to(rows_re[d - 1], (8, W)), 0.0)
                sh_im[n] = jnp.where(first_rows >= d, jnp.broadcast_to(rows_im[d - 1], (8, W)), 0.0)
            carry_re[...] = jnp.zeros_like(carry_re)
            carry_im[...] = jnp.zeros_like(carry_im)

        ub = u_ref[...].astype(BF16)
        xr_ref[...] = jnp.dot(ub, bre_ref[...], preferred_element_type=F32)
        xi_ref[...] = jnp.dot(ub, bim_ref[...], preferred_element_type=F32)
        row = lax.broadcasted_iota(jnp.int32, (8, W), 0)

        def tile(i, carry):
            c_re, c_im = carry
            rows = pl.ds(pl.multiple_of(i * 8, 8), 8)
            r = xr_ref[rows, :]
            m = xi_ref[rows, :]
            for n, d in enumerate((1, 2, 4)):
                pr, pm = _cmul(sh_re[n], sh_im[n], pltpu.roll(r, d, 0), pltpu.roll(m, d, 0))
                r = r + pr
                m = m + pm
            pr, pm = _cmul(pw_re[...], pw_im[...], c_re, c_im)
            r = r + pr
            m = m + pm
            xr_ref[rows, :] = r
            xi_ref[rows, :] = m
            return jnp.broadcast_to(r[7:8, :], (8, W)), jnp.broadcast_to(m[7:8, :], (8, W))

        c_re, c_im = lax.fori_loop(0, n_tiles, tile, (carry_re[...], carry_im[...]))
        carry_re[...] = c_re
        carry_im[...] = c_im
        y = (jnp.dot(xr_ref[...].astype(BF16), cre_ref[...], preferred_element_type=F32)
             - jnp.dot(xi_ref[...].astype(BF16), cim_ref[...], preferred_element_type=F32)
             + d_ref[...] * u_ref[...])
        y_ref[...] = y
        z_ref[...] = _gelu(y).astype(z_ref.dtype)

    ch = pl.BlockSpec((tc, CH_TILE), lambda s, c: (c, s))
    st = pl.BlockSpec((tc, W), lambda s, c: (c, s))
    return _grid_call(
        name, body, (n_st, T // tc),
        in_specs=[ch,
                  pl.BlockSpec((None, CH_TILE, W), lambda s, c: (s, 0, 0)),
                  pl.BlockSpec((None, CH_TILE, W), lambda s, c: (s, 0, 0)),
                  pl.BlockSpec((None, W, CH_TILE), lambda s, c: (s, 0, 0)),
                  pl.BlockSpec((None, W, CH_TILE), lambda s, c: (s, 0, 0)),
                  pl.BlockSpec((1, W), lambda s, c: (0, s)),
                  pl.BlockSpec((1, W), lambda s, c: (0, s)),
                  pl.BlockSpec((1, CH_TILE), lambda s, c: (0, s))],
        out_specs=[ch, ch, st, st],
        out_shape=[jax.ShapeDtypeStruct((T, D), F32), jax.ShapeDtypeStruct((T, D), BF16),
                   jax.ShapeDtypeStruct((T, 4 * D), F32), jax.ShapeDtypeStruct((T, 4 * D), F32)],
        scratch_shapes=[pltpu.VMEM((8, W), F32), pltpu.VMEM((8, W), F32),
                        pltpu.VMEM((8, W), F32), pltpu.VMEM((8, W), F32),
                        pltpu.VMEM((3, 8, W), F32), pltpu.VMEM((3, 8, W), F32)],
        args=(u, b_re, b_im, ct_re, ct_im, abar_re, abar_im, d_skip), comm=comm, comm_args=comm_args)


def _s5_bwd(name, dz, y, u, x_re, x_im, c_re, c_im, bt_re, bt_im, abar_re, abar_im, d_skip, comm=None, comm_args=()):
    T, D = u.shape
    n_st = D // CH_TILE
    W = STATE_TILE
    tc = _row_tile(T, 512)
    n_chunks = T // tc
    n_tiles = tc // 8
    tiles_per_chunk = tc // 8

    def body(dz_ref, y_ref, u_ref, xr_ref, xi_ref, xpr_ref, xpi_ref, cre_ref, cim_ref, btr_ref, bti_ref,
             ar_ref, ai_ref, d_ref,
             du_ref, dd_ref, dbr_ref, dbi_ref, dcr_ref, dci_ref, dar_ref, dai_ref,
             lam_re, lam_im, xe_re, xe_im, carry_re, carry_im, pw_re, pw_im, sh_re, sh_im, acc_ar, acc_ai):
        k = pl.program_id(1)
        first_chunk = k == n_chunks - 1

        @pl.when(k == 0)
        def _():
            t_re, t_im, rows_re, rows_im = _power_table(ar_ref[...], -ai_ref[...])
            row = lax.broadcasted_iota(jnp.int32, (8, W), 0)
            r_re = jnp.zeros((8, W), F32)
            r_im = jnp.zeros((8, W), F32)
            for j in range(8):
                r_re = jnp.where(row == j, rows_re[7 - j], r_re)
                r_im = jnp.where(row == j, rows_im[7 - j], r_im)
            pw_re[...] = r_re
            pw_im[...] = r_im
            for n, d in enumerate((1, 2, 4)):
                sh_re[n] = jnp.where(row < 8 - d, jnp.broadcast_to(rows_re[d - 1], (8, W)), 0.0)
                sh_im[n] = jnp.where(row < 8 - d, jnp.broadcast_to(rows_im[d - 1], (8, W)), 0.0)
            carry_re[...] = jnp.zeros_like(carry_re)
            carry_im[...] = jnp.zeros_like(carry_im)
            acc_ar[...] = jnp.zeros_like(acc_ar)
            acc_ai[...] = jnp.zeros_like(acc_ai)
            dd_ref[...] = jnp.zeros_like(dd_ref)
            dbr_ref[...] = jnp.zeros_like(dbr_ref)
            dbi_ref[...] = jnp.zeros_like(dbi_ref)
            dcr_ref[...] = jnp.zeros_like(dcr_ref)
            dci_ref[...] = jnp.zeros_like(dci_ref)

        uv = u_ref[...]
        dy = dz_ref[...] * _gelu_grad(y_ref[...])
        dyb = dy.astype(BF16)
        lam_re[...] = jnp.dot(dyb, cre_ref[...], preferred_element_type=F32)
        lam_im[...] = -jnp.dot(dyb, cim_ref[...], preferred_element_type=F32)
        keep = jnp.where(first_chunk, 0.0, 1.0)
        xe_re[pl.ds(0, 8), :] = xpr_ref[...] * keep
        xe_im[pl.ds(0, 8), :] = xpi_ref[...] * keep
        xe_re[pl.ds(8, tc), :] = xr_ref[...]
        xe_im[pl.ds(8, tc), :] = xi_ref[...]
        row = lax.broadcasted_iota(jnp.int32, (8, W), 0)

        def tile(n, carry):
            c_re, c_im, s_ar, s_ai = carry
            i = n_tiles - 1 - n
            rows = pl.ds(pl.multiple_of(i * 8, 8), 8)
            r = lam_re[rows, :]
            m = lam_im[rows, :]
            for q, d in enumerate((1, 2, 4)):
                pr, pm = _cmul(sh_re[q], sh_im[q], pltpu.roll(r, 8 - d, 0), pltpu.roll(m, 8 - d, 0))
                r = r + pr
                m = m + pm
            pr, pm = _cmul(pw_re[...], pw_im[...], c_re, c_im)
            r = r + pr
            m = m + pm
            lam_re[rows, :] = r
            lam_im[rows, :] = m
            cur_re = xe_re[pl.ds(pl.multiple_of(i * 8 + 8, 8), 8), :]
            cur_im = xe_im[pl.ds(pl.multiple_of(i * 8 + 8, 8), 8), :]
            bef_re = xe_re[rows, :]
            bef_im = xe_im[rows, :]
            xp_re = jnp.where(row == 0, jnp.broadcast_to(bef_re[7:8, :], (8, W)), pltpu.roll(cur_re, 1, 0))
            xp_im = jnp.where(row == 0, jnp.broadcast_to(bef_im[7:8, :], (8, W)), pltpu.roll(cur_im, 1, 0))
            s_ar = s_ar + r * xp_re + m * xp_im
            s_ai = s_ai + m * xp_re - r * xp_im
            return jnp.broadcast_to(r[0:1, :], (8, W)), jnp.broadcast_to(m[0:1, :], (8, W)), s_ar, s_ai

        c_re, c_im, s_ar, s_ai = lax.fori_loop(
            0, n_tiles, tile, (carry_re[...], carry_im[...], acc_ar[...], acc_ai[...]))
        carry_re[...] = c_re
        carry_im[...] = c_im
        acc_ar[...] = s_ar
        acc_ai[...] = s_ai
        lr = lam_re[...].astype(BF16)
        li = lam_im[...].astype(BF16)
        du_ref[...] = (dy * d_ref[...] + jnp.dot(lr, btr_ref[...], preferred_element_type=F32)
                       + jnp.dot(li, bti_ref[...], preferred_element_type=F32))
        dd_ref[...] += jnp.sum(dy * uv, axis=0, keepdims=True)
        tn_dims = (((0,), (0,)), ((), ()))
        ub = uv.astype(BF16)
        dbr_ref[...] += lax.dot_general(ub, lr, tn_dims, preferred_element_type=F32)
        dbi_ref[...] += lax.dot_general(ub, li, tn_dims, preferred_element_type=F32)
        dcr_ref[...] += lax.dot_general(dyb, xr_ref[...].astype(BF16), tn_dims, preferred_element_type=F32)
        dci_ref[...] -= lax.dot_general(dyb, xi_ref[...].astype(BF16), tn_dims, preferred_element_type=F32)

        @pl.when(first_chunk)
        def _():
            dar_ref[...] = jnp.sum(acc_ar[...], axis=0, keepdims=True)
            dai_ref[...] = jnp.sum(acc_ai[...], axis=0, keepdims=True)

    rev = lambda k: n_chunks - 1 - k
    ch = pl.BlockSpec((tc, CH_TILE), lambda s, k: (rev(k), s))
    st = pl.BlockSpec((tc, W), lambda s, k: (rev(k), s))
    prev = pl.BlockSpec((8, W), lambda s, k: (jnp.maximum(rev(k) * tiles_per_chunk - 1, 0), s))
    mat_cw = pl.BlockSpec((None, CH_TILE, W), lambda s, k: (s, 0, 0))
    mat_wc = pl.BlockSpec((None, W, CH_TILE), lambda s, k: (s, 0, 0))
    vec_w = pl.BlockSpec((1, W), lambda s, k: (0, s))
    vec_c = pl.BlockSpec((1, CH_TILE), lambda s, k: (0, s))
    dense = jax.ShapeDtypeStruct((n_st, CH_TILE, W), F32)
    return _grid_call(
        name, body, (n_st, n_chunks),
        in_specs=[ch, ch, ch, st, st, prev, prev, mat_cw, mat_cw, mat_wc, mat_wc, vec_w, vec_w, vec_c],
        out_specs=[ch, vec_c, mat_cw, mat_cw, mat_cw, mat_cw, vec_w, vec_w],
        out_shape=[jax.ShapeDtypeStruct((T, D), F32), jax.ShapeDtypeStruct((1, D), F32), dense, dense, dense, dense,
                   jax.ShapeDtypeStruct((1, 4 * D), F32), jax.ShapeDtypeStruct((1, 4 * D), F32)],
        scratch_shapes=[pltpu.VMEM((tc, W), F32), pltpu.VMEM((tc, W), F32),
                        pltpu.VMEM((tc + 8, W), F32), pltpu.VMEM((tc + 8, W), F32),
                        pltpu.VMEM((8, W), F32), pltpu.VMEM((8, W), F32),
                        pltpu.VMEM((8, W), F32), pltpu.VMEM((8, W), F32),
                        pltpu.VMEM((3, 8, W), F32), pltpu.VMEM((3, 8, W), F32),
                        pltpu.VMEM((8, W), F32), pltpu.VMEM((8, W), F32)],
        args=(dz, y, u, x_re, x_im, x_re, x_im, c_re, c_im, bt_re, bt_im, abar_re, abar_im, d_skip),
        comm=comm, comm_args=comm_args)


def _sums_matrix(strictly_later, copies):
    jj = lax.broadcasted_iota(jnp.int32, (copies * KEY_BLOCK, 2 * KEY_BLOCK), 0) & (KEY_BLOCK - 1)
    ss = lax.broadcasted_iota(jnp.int32, (copies * KEY_BLOCK, 2 * KEY_BLOCK), 1)
    tri = (jj > ss) if strictly_later else (jj < ss)
    return (tri | (ss >= KEY_BLOCK)).astype(BF16)


def _split_heads(blk):
    first = lax.broadcasted_iota(jnp.int32, blk.shape, 1) < HEAD_DIM
    zero = jnp.zeros_like(blk)
    return jnp.concatenate([jnp.where(first, blk, zero), jnp.where(first, zero, blk)], axis=0)


LOG2_E = 1.4426950408889634


def _sb_scores(z, mask, later):
    z2 = z * LOG2_E
    minus_abs = lax.bitcast_convert_type(lax.bitcast_convert_type(z2, jnp.uint32) | jnp.uint32(0x80000000), F32)
    lb = jnp.minimum(z2, 0.0) - jnp.log2(1.0 + jnp.exp2(minus_abs))
    lm = lb - z2
    if mask is not None:
        lm = jnp.where(mask, lm, 0.0)
    hi = lm.astype(BF16)
    lo = (lm - hi.astype(F32)).astype(BF16)
    return lb, jnp.dot(jnp.concatenate([hi, lo], axis=1), later, preferred_element_type=F32)


def _key_rows(kb):
    return pl.ds(pl.multiple_of(kb * KEY_BLOCK, KEY_BLOCK), KEY_BLOCK)


def _sb_mask(q_row0, k_row0, tq):
    tpos = q_row0 + lax.broadcasted_iota(jnp.int32, (tq, KEY_BLOCK), 0)
    spos = k_row0 + lax.broadcasted_iota(jnp.int32, (tq, KEY_BLOCK), 1)
    return (spos < tpos) & (spos >= META_START)


def _key_block_phases(iq, per_q, block, ascending):
    first_diag = iq * per_q

    def run(lo, n, masked):
        def step(i, carry):
            block(lo + i if ascending else lo + n - 1 - i, masked)
            return carry
        lax.fori_loop(0, n, step, 0)

    phases = [(0, jnp.minimum(iq, 1), True), (1, jnp.maximum(first_diag - 1, 0), False), (first_diag, per_q, True)]
    for lo, n, masked in (phases if ascending else phases[::-1]):
        run(lo, n, masked)


def _attn_fwd(name, q, kv):
    T, D = q.shape
    n_hp = D // 128
    tq = _row_tile(T, 512)
    per_q = tq // KEY_BLOCK
    scale = 1.0 / math.sqrt(HEAD_DIM)

    def body(q_ref, k_ref, v_ref, o_ref, l_ref, z_buf, w_buf, acc_ref, run_ref):
        iq = pl.program_id(1)
        n_kb = (iq + 1) * per_q
        qs = q_ref[...] * jnp.asarray(scale, BF16)
        later = _sums_matrix(True, 2)

        def scores(kb):
            return lax.dot_general(qs, _split_heads(k_ref[_key_rows(kb), :]), NT_DIMS, preferred_element_type=F32)

        def weighted_values(kb):
            return jnp.dot(w_buf[...], _split_heads(v_ref[_key_rows(kb), :]), preferred_element_type=F32)

        acc_ref[...] = jnp.zeros_like(acc_ref)
        run_ref[...] = jnp.zeros_like(run_ref)
        w_buf[...] = jnp.zeros_like(w_buf)
        z_buf[...] = scores(n_kb - 1)

        def block(kb, masked):
            acc_ref[...] += weighted_values(jnp.minimum(kb + 1, n_kb - 1))
            z_next = scores(jnp.maximum(kb - 1, 0))
            mask = _sb_mask(iq * tq, kb * KEY_BLOCK, tq) if masked else None
            heads = [_sb_scores(z_buf[:, h * KEY_BLOCK:(h + 1) * KEY_BLOCK], mask, later) for h in range(2)]
            for h, (lb, sums) in enumerate(heads):
                run = run_ref[h]
                w = jnp.exp2(lb + sums[:, :KEY_BLOCK] + run)
                if masked:
                    w = jnp.where(mask, w, 0.0)
                w_buf[:, h * KEY_BLOCK:(h + 1) * KEY_BLOCK] = w.astype(BF16)
                run_ref[h] = run + sums[:, KEY_BLOCK:]
            z_buf[...] = z_next

        _key_block_phases(iq, per_q, block, ascending=False)
        acc_ref[...] += weighted_values(0)
        lane = lax.broadcasted_iota(jnp.int32, (tq, 128), 1)
        o_ref[...] = acc_ref[...].astype(o_ref.dtype)
        l_ref[...] = jnp.where(lane < HEAD_DIM, run_ref[0], run_ref[1])

    blk = pl.BlockSpec((tq, 128), lambda h, i: (i, h))
    return pl.pallas_call(
        body, name=name, grid=(n_hp, T // tq),
        in_specs=[blk, pl.BlockSpec((T, 128), lambda h, i: (0, h)), pl.BlockSpec((T, 128), lambda h, i: (0, n_hp + h))],
        out_specs=[blk, blk],
        out_shape=[jax.ShapeDtypeStruct((T, D), BF16), jax.ShapeDtypeStruct((T, D), F32)],
        scratch_shapes=[pltpu.VMEM((tq, 2 * KEY_BLOCK), F32), pltpu.VMEM((tq, 2 * KEY_BLOCK), BF16),
                        pltpu.VMEM((tq, 128), F32), pltpu.VMEM((2, tq, 128), F32)],
        compiler_params=_cparams("parallel", "arbitrary"),
    )(q, kv, kv)


def _attn_bwd(name, q, kv, do, ltot):
    T, D = q.shape
    n_hp = D // 128
    tq = _row_tile(T, 512)
    n_q = T // tq
    per_q = tq // KEY_BLOCK
    scale = 1.0 / math.sqrt(HEAD_DIM)

    def body(q_ref, k_ref, v_ref, do_ref, l_ref, dq_ref, dk_ref, dv_ref,
             dk_acc, dv_acc, dq_acc, lpre_ref, cpre_ref, z_buf, dw_buf, dz_buf, w_buf):
        iq = pl.program_id(1)
        n_kb = (iq + 1) * per_q

        @pl.when(iq == 0)
        def _():
            dk_acc[...] = jnp.zeros_like(dk_acc)
            dv_acc[...] = jnp.zeros_like(dv_acc)

        first = lax.broadcasted_iota(jnp.int32, (tq, 128), 1) < HEAD_DIM
        qs = q_ref[...] * jnp.asarray(scale, BF16)
        dov = do_ref[...]
        ltv = l_ref[...]
        swapped = pltpu.roll(ltv, HEAD_DIM, 1)
        ltot = [jnp.where(first, ltv, swapped), jnp.where(first, swapped, ltv)]
        zero = jnp.zeros_like(qs)
        q_stack = jnp.concatenate([jnp.where(first, qs, zero), jnp.where(first, zero, qs)], axis=0)
        do_stack = jnp.concatenate([jnp.where(first, dov, zero), jnp.where(first, zero, dov)], axis=0)
        later = _sums_matrix(True, 2)
        earlier = _sums_matrix(False, 1)

        def scores(kb):
            rows = _key_rows(kb)
            return (lax.dot_general(qs, _split_heads(k_ref[rows, :]), NT_DIMS, preferred_element_type=F32),
                    lax.dot_general(dov, _split_heads(v_ref[rows, :]), NT_DIMS, preferred_element_type=F32))

        def flush(kb):
            rows = _key_rows(kb)
            k_heads = _split_heads(k_ref[rows, :])
            dq_acc[...] += (jnp.dot(dz_buf[:tq, :], k_heads[:KEY_BLOCK, :], preferred_element_type=F32)
                            + jnp.dot(dz_buf[tq:, :], k_heads[KEY_BLOCK:, :], preferred_element_type=F32))
            dk_acc[rows, :] += lax.dot_general(dz_buf[...], q_stack, TN_DIMS, preferred_element_type=F32)
            dv_acc[rows, :] += lax.dot_general(w_buf[...], do_stack, TN_DIMS, preferred_element_type=F32)

        dq_acc[...] = jnp.zeros_like(dq_acc)
        lpre_ref[...] = jnp.zeros_like(lpre_ref)
        cpre_ref[...] = jnp.zeros_like(cpre_ref)
        dz_buf[...] = jnp.zeros_like(dz_buf)
        w_buf[...] = jnp.zeros_like(w_buf)
        z_buf[...], dw_buf[...] = scores(0)

        def block(kb, masked):
            flush(jnp.maximum(kb - 1, 0))
            z_next, dw_next = scores(jnp.minimum(kb + 1, n_kb - 1))
            mask = _sb_mask(iq * tq, kb * KEY_BLOCK, tq) if masked else None
            heads = [_sb_scores(z_buf[:, h * KEY_BLOCK:(h + 1) * KEY_BLOCK], mask, later) for h in range(2)]
            grads = []
            for h, (lb, sums) in enumerate(heads):
                after = ltot[h] - lpre_ref[h] - sums[:, KEY_BLOCK:]
                w = jnp.exp2(lb + sums[:, :KEY_BLOCK] + after)
                if masked:
                    w = jnp.where(mask, w, 0.0)
                da = w * dw_buf[:, h * KEY_BLOCK:(h + 1) * KEY_BLOCK]
                w_buf[h * tq:(h + 1) * tq, :] = w.astype(BF16)
                lpre_ref[h] += sums[:, KEY_BLOCK:]
                grads.append((da, jnp.dot(da.astype(BF16), earlier, preferred_element_type=F32)))
            for h, (da, dsums) in enumerate(grads):
                sig = jnp.exp2(heads[h][0])
                through_later = sig * (dsums[:, :KEY_BLOCK] + cpre_ref[h])
                if masked:
                    through_later = jnp.where(mask, through_later, 0.0)
                dz_buf[h * tq:(h + 1) * tq, :] = (da * (1.0 - sig) - through_later).astype(BF16)
                cpre_ref[h] += dsums[:, KEY_BLOCK:]
            z_buf[...] = z_next
            dw_buf[...] = dw_next

        _key_block_phases(iq, per_q, block, ascending=True)
        flush(n_kb - 1)
        dq_ref[...] = (dq_acc[...] * scale).astype(dq_ref.dtype)

        @pl.when(iq == n_q - 1)
        def _():
            dk_ref[...] = dk_acc[...].astype(dk_ref.dtype)
            dv_ref[...] = dv_acc[...].astype(dv_ref.dtype)

    blk = pl.BlockSpec((tq, 128), lambda h, i: (i, h))
    full = pl.BlockSpec((T, 128), lambda h, i: (0, h))
    return pl.pallas_call(
        body, name=name, grid=(n_hp, n_q),
        in_specs=[blk, full, pl.BlockSpec((T, 128), lambda h, i: (0, n_hp + h)), blk, blk],
        out_specs=[blk, full, full],
        out_shape=[jax.ShapeDtypeStruct((T, D), BF16)] * 3,
        scratch_shapes=[pltpu.VMEM((T, 128), F32), pltpu.VMEM((T, 128), F32), pltpu.VMEM((tq, 128), F32),
                        pltpu.VMEM((2, tq, 128), F32), pltpu.VMEM((2, tq, 128), F32),
                        pltpu.VMEM((tq, 2 * KEY_BLOCK), F32), pltpu.VMEM((tq, 2 * KEY_BLOCK), F32),
                        pltpu.VMEM((2 * tq, 128), BF16), pltpu.VMEM((2 * tq, 128), BF16)],
        compiler_params=_cparams("parallel", "arbitrary"),
    )(q, kv, kv, do, ltot)


def _adamw(name, w, g, m, v):
    shape = w.shape
    size = w.size
    if w.ndim >= 2 and shape[-1] % 128 == 0:
        cols = shape[-1]
    else:
        cols = 1024 if size % 1024 == 0 else shape[-1]
    rows = size // cols
    tm = _row_tile(rows, max(8, (1024 * 1024) // (4 * cols) // 8 * 8)) if rows % 8 == 0 else rows
    c1 = 1.0 / (1.0 - ADAM_B1 ** ADAM_STEP)
    c2 = 1.0 / (1.0 - ADAM_B2 ** ADAM_STEP)

    def body(w_ref, g_ref, m_ref, v_ref, d_ref, nm_ref, nv_ref):
        gv = g_ref[...]
        nm = ADAM_B1 * m_ref[...] + (1.0 - ADAM_B1) * gv
        nv = ADAM_B2 * v_ref[...] + (1.0 - ADAM_B2) * (gv * gv)
        d_ref[...] = -ADAM_LR * ((nm * c1) / (jnp.sqrt(nv * c2) + ADAM_EPS) + ADAM_WD * w_ref[...])
        nm_ref[...] = nm
        nv_ref[...] = nv

    blk = pl.BlockSpec((tm, cols), lambda i: (i, 0))
    outs = pl.pallas_call(
        body, name=name, grid=(rows // tm,),
        in_specs=[blk] * 4, out_specs=[blk] * 3,
        out_shape=[jax.ShapeDtypeStruct((rows, cols), F32)] * 3,
        compiler_params=_cparams("parallel"),
    )(*[t.reshape(rows, cols) for t in (w, g, m, v)])
    return tuple(o.reshape(shape) for o in outs)


def _any_specs(n):
    return [pl.BlockSpec(memory_space=pl.ANY)] * n


def _chip_index():
    return 2 * lax.axis_index("x") + lax.axis_index("y")


def _place():
    x, y, c = lax.axis_index("x"), lax.axis_index("y"), lax.axis_index("c")
    chips = [(1 - x, y), (x, 1 - y), (1 - x, 1 - y)]
    return x, y, c, chips


def _all_gather_chips(name, shards):
    n = len(shards)

    def body(*refs):
        x_refs, o_refs = refs[:n], refs[n:2 * n]
        send_sems, recv_sems = refs[2 * n:]
        x, y, c, chips = _place()
        me = 2 * x + y
        sibling = (x, y, 1 - c)

        def half(ref, i, which):
            h = shards[i].shape[0] // 2
            return ref.at[pl.ds(which * h, h)]

        def remote(k, i, src, dst, to):
            return pltpu.make_async_remote_copy(src_ref=src, dst_ref=dst, send_sem=send_sems.at[k, i],
                                                recv_sem=recv_sems.at[k, i], device_id=to, device_id_type=MESH)

        sent = []
        for j, chip in enumerate(chips):
            for i in range(n):
                cp = remote(j, i, half(x_refs[i], i, c), half(o_refs[i].at[me], i, c), (*chip, c))
                cp.start()
                sent.append(cp)
        for j, chip in enumerate(chips):
            pj = 2 * chip[0] + chip[1]
            for i in range(n):
                landed = half(o_refs[i].at[pj], i, c)
                remote(j, i, landed, landed, (*chip, c)).wait_recv()
                cp = remote(3 + j, i, landed, landed, sibling)
                cp.start()
                sent.append(cp)
        for j, chip in enumerate(chips):
            pj = 2 * chip[0] + chip[1]
            for i in range(n):
                got = half(o_refs[i].at[pj], i, 1 - c)
                remote(3 + j, i, got, got, sibling).wait_recv()
        for cp in sent:
            cp.wait_send()

    outs = pl.pallas_call(
        body, name=name,
        in_specs=_any_specs(n), out_specs=_any_specs(n),
        out_shape=[jax.ShapeDtypeStruct((4,) + s.shape, s.dtype) for s in shards],
        scratch_shapes=[pltpu.SemaphoreType.DMA((6, n)), pltpu.SemaphoreType.DMA((6, n))],
    )(*shards)
    return [lax.dynamic_update_slice(o, s[None], (_chip_index(), 0, 0)) for o, s in zip(outs, shards)]


def _pair_split(name, grads):
    n = len(grads)

    def body(*refs):
        g_refs, got_refs = refs[:n], refs[n:2 * n]
        send_sems, recv_sems = refs[2 * n:]
        x, y, c, _ = _place()
        sibling = (x, y, 1 - c)
        sent = []
        for i in range(n):
            h = grads[i].shape[1] // 2
            rc = pltpu.make_async_remote_copy(
                src_ref=g_refs[i].at[:, pl.ds((1 - c) * h, h)], dst_ref=got_refs[i],
                send_sem=send_sems.at[i], recv_sem=recv_sems.at[i], device_id=sibling, device_id_type=MESH)
            rc.start()
            sent.append(rc)
        for rc in sent:
            rc.wait()

    return pl.pallas_call(
        body, name=name,
        in_specs=_any_specs(n), out_specs=_any_specs(n),
        out_shape=[jax.ShapeDtypeStruct((4, g.shape[1] // 2, g.shape[2]), g.dtype) for g in grads],
        scratch_shapes=[pltpu.SemaphoreType.DMA((n,)), pltpu.SemaphoreType.DMA((n,))],
    )(*grads)


def _chip_exchange(name, sums):
    return _exchange_finish(sums, _run_comm(name, _exchange_comm(sums), sums))


class _Comm:
    def __init__(self, out_shape, copies):
        self.n = len(out_shape)
        self.out_shape = out_shape
        self.copies = copies
        self.scratch = [pltpu.SemaphoreType.DMA((3, self.n)), pltpu.SemaphoreType.DMA((3, self.n))]

    def start(self, *refs):
        for cp in self.copies(*refs, False):
            cp.start()

    def finish(self, *refs):
        for cp in self.copies(*refs, True):
            cp.wait_recv()
        for cp in self.copies(*refs, False):
            cp.wait_send()


def _exchange_comm(sums):
    n = len(sums)

    def copies(s_refs, o_refs, send_sems, recv_sems, mirrors):
        x, y, c, chips = _place()
        me = 2 * x + y
        out = []
        for j, chip in enumerate(chips):
            pj = 2 * chip[0] + chip[1]
            for i in range(n):
                out.append(pltpu.make_async_remote_copy(
                    src_ref=s_refs[i].at[pj], dst_ref=o_refs[i].at[pj if mirrors else me], send_sem=send_sems.at[j, i],
                    recv_sem=recv_sems.at[j, i], device_id=(*chip, c), device_id_type=MESH))
        return out

    return _Comm([jax.ShapeDtypeStruct(s.shape, s.dtype) for s in sums], copies)


def _exchange_finish(sums, outs):
    me = _chip_index()
    return [lax.dynamic_update_slice(o, lax.dynamic_index_in_dim(s, me, 0, keepdims=True), (me, 0, 0))
            for o, s in zip(outs, sums)]


def _halves_gather_comm(shards):
    n = len(shards)

    def copies(x_refs, o_refs, send_sems, recv_sems, mirrors):
        x, y, c, chips = _place()
        me = 2 * x + y
        out = []
        for j, chip in enumerate(chips):
            pj = 2 * chip[0] + chip[1]
            for i in range(n):
                h = shards[i].shape[0] // 2
                rows = pl.ds(c * h, h)
                out.append(pltpu.make_async_remote_copy(
                    src_ref=x_refs[i].at[rows], dst_ref=o_refs[i].at[pj if mirrors else me, rows],
                    send_sem=send_sems.at[j, i], recv_sem=recv_sems.at[j, i], device_id=(*chip, c), device_id_type=MESH))
        return out

    return _Comm([jax.ShapeDtypeStruct((4,) + s.shape, s.dtype) for s in shards], copies)


def _run_comm(name, comm, arrays):
    n = comm.n

    def body(*refs):
        comm.start(refs[:n], refs[n:2 * n], *refs[2 * n:])
        comm.finish(refs[:n], refs[n:2 * n], *refs[2 * n:])

    return pl.pallas_call(
        body, name=name, in_specs=_any_specs(n), out_specs=_any_specs(n),
        out_shape=comm.out_shape, scratch_shapes=comm.scratch,
    )(*arrays)


def _embed_comm(comm, body, grid, n_in, n_out):
    n = comm.n

    def wrapped(*refs):
        ins, c_in = refs[:n_in], refs[n_in:n_in + n]
        outs, c_out = refs[n_in + n:n_in + n + n_out], refs[n_in + n + n_out:n_in + 2 * n + n_out]
        scratch, sems = refs[n_in + 2 * n + n_out:-2], refs[-2:]
        ids = [pl.program_id(a) for a in range(len(grid))]
        first = functools.reduce(jnp.logical_and, [i == 0 for i in ids])
        last = functools.reduce(jnp.logical_and, [i == g - 1 for i, g in zip(ids, grid)])

        @pl.when(first)
        def _():
            comm.start(c_in, c_out, *sems)

        body(*ins, *outs, *scratch)

        @pl.when(last)
        def _():
            comm.finish(c_in, c_out, *sems)

    return wrapped


def _pair_forward(name, landed, shards):
    n = len(landed)

    def body(*refs):
        o_refs = refs[n:2 * n]
        send_sems, recv_sems = refs[2 * n:]
        x, y, c, chips = _place()
        sibling = (x, y, 1 - c)
        sent, arriving = [], []
        for j, chip in enumerate(chips):
            pj = 2 * chip[0] + chip[1]
            for i in range(n):
                h = shards[i].shape[0] // 2
                mine = o_refs[i].at[pj, pl.ds(c * h, h)]
                theirs = o_refs[i].at[pj, pl.ds((1 - c) * h, h)]
                for ref, group in ((mine, sent), (theirs, arriving)):
                    group.append(pltpu.make_async_remote_copy(
                        src_ref=ref, dst_ref=ref, send_sem=send_sems.at[j, i], recv_sem=recv_sems.at[j, i],
                        device_id=sibling, device_id_type=MESH))
        for cp in sent:
            cp.start()
        for cp in arriving:
            cp.wait_recv()
        for cp in sent:
            cp.wait_send()

    outs = pl.pallas_call(
        body, name=name, in_specs=_any_specs(n), out_specs=_any_specs(n),
        out_shape=[jax.ShapeDtypeStruct(a.shape, a.dtype) for a in landed],
        input_output_aliases={i: i for i in range(n)},
        scratch_shapes=[pltpu.SemaphoreType.DMA((3, n)), pltpu.SemaphoreType.DMA((3, n))],
    )(*landed)
    return [lax.dynamic_update_slice(o, s[None], (_chip_index(), 0, 0)) for o, s in zip(outs, shards)]


def _pair_join(name, halves):
    n = len(halves)

    def body(*refs):
        h_refs, o_refs = refs[:n], refs[n:2 * n]
        send_sems, recv_sems = refs[2 * n:]
        x, y, c, _ = _place()
        sibling = (x, y, 1 - c)
        sent = []
        for i in range(n):
            h = halves[i].shape[0]
            mine = o_refs[i].at[pl.ds(c * h, h)]
            rc = pltpu.make_async_remote_copy(
                src_ref=h_refs[i], dst_ref=mine, send_sem=send_sems.at[i], recv_sem=recv_sems.at[i],
                device_id=sibling, device_id_type=MESH)
            rc.start()
            sent.append(rc)
        for i in range(n):
            h = halves[i].shape[0]
            theirs = o_refs[i].at[pl.ds((1 - c) * h, h)]
            pltpu.make_async_remote_copy(
                src_ref=h_refs[i], dst_ref=theirs, send_sem=send_sems.at[i], recv_sem=recv_sems.at[i],
                device_id=sibling, device_id_type=MESH).wait_recv()
        for rc in sent:
            rc.wait_send()

    outs = pl.pallas_call(
        body, name=name,
        in_specs=_any_specs(n), out_specs=_any_specs(n),
        out_shape=[jax.ShapeDtypeStruct((2 * s.shape[0], s.shape[1]), s.dtype) for s in halves],
        scratch_shapes=[pltpu.SemaphoreType.DMA((n,)), pltpu.SemaphoreType.DMA((n,))],
    )(*halves)
    c = lax.axis_index("c")
    return [lax.dynamic_update_slice(o, s, (c * s.shape[0], 0)) for o, s in zip(outs, halves)]


def _add_pair(name, a, b):
    _, H, C = a.shape
    th = _row_tile(H, max(8, (1024 * 1024) // (4 * C) // 8 * 8))

    def body(a_ref, b_ref, o_ref):
        o_ref[...] = (a_ref[...].astype(F32) + b_ref[...].astype(F32)).astype(o_ref.dtype)

    blk = pl.BlockSpec((None, th, C), lambda q, i: (q, i, 0))
    return pl.pallas_call(
        body, name=name, grid=(4, H // th), in_specs=[blk, blk], out_specs=blk,
        out_shape=jax.ShapeDtypeStruct(a.shape, a.dtype), compiler_params=_cparams("parallel", "parallel"),
    )(a, b)


def _add_chips(name, parts):
    _, H, C = parts.shape
    th = _row_tile(H, max(8, (1024 * 1024) // (4 * C) // 8 * 8))

    def body(p_ref, o_ref):
        acc = p_ref[0].astype(F32)
        for q in range(1, 4):
            acc = acc + p_ref[q].astype(F32)
        o_ref[...] = acc

    return pl.pallas_call(
        body, name=name, grid=(H // th,),
        in_specs=[pl.BlockSpec((4, th, C), lambda i: (0, i, 0))], out_specs=pl.BlockSpec((th, C), lambda i: (i, 0)),
        out_shape=jax.ShapeDtypeStruct((H, C), F32), compiler_params=_cparams("parallel"),
    )(parts)


def _pair_sums(tag, grads):
    c = lax.axis_index("c")
    got = _pair_split(f"rs_pair_split_{tag}", grads)
    kept = [lax.dynamic_slice_in_dim(g, c * (g.shape[1] // 2), g.shape[1] // 2, axis=1) for g in grads]
    return [_add_pair(f"rs_add_pair_{tag}_{i}", k, g) for i, (k, g) in enumerate(zip(kept, got))]


def _chip_sums(parts):
    return _pair_join("rs_pair_join", [_add_chips(f"rs_add_chips_{i}", p) for i, p in enumerate(parts)])


def _block_diag(t):
    G, A, B = t.shape
    eye = jnp.eye(8, dtype=t.dtype)
    return jnp.einsum("sgab,gh->sgahb", t.reshape(G // 8, 8, A, B), eye).reshape(G // 8, 8 * A, 8 * B)


def _block_diag_extract(m, A, B):
    n = m.shape[0]
    eye = jnp.eye(8, dtype=m.dtype)
    return jnp.einsum("sgahb,gh->sgab", m.reshape(n, 8, A, 8, B), eye).reshape(8 * n, A, B)


def kernel(x, meta_tokens, norm_mix, norm_ffn, s5_a_re, s5_a_im, s5_log_dt, s5_b_re, s5_b_im, s5_c_re, s5_c_im, s5_d, s5_w_glu, norm_kv, w_kv, w_q, w_o, w_ffn_in, w_ffn_out, norm_final, loss_target, m_meta_tokens, m_norm_mix, m_norm_ffn, m_s5_a_re, m_s5_a_im, m_s5_log_dt, m_s5_b_re, m_s5_b_im, m_s5_c_re, m_s5_c_im, m_s5_d, m_s5_w_glu, m_norm_kv, m_w_kv, m_w_q, m_w_o, m_w_ffn_in, m_w_ffn_out, m_norm_final, v_meta_tokens, v_norm_mix, v_norm_ffn, v_s5_a_re, v_s5_a_im, v_s5_log_dt, v_s5_b_re, v_s5_b_im, v_s5_c_re, v_s5_c_im, v_s5_d, v_s5_w_glu, v_norm_kv, v_w_kv, v_w_q, v_w_o, v_w_ffn_in, v_w_ffn_out, v_norm_final):
    seq, D = x.shape[1], x.shape[2]
    T = X_START + seq
    G, P, C = s5_a_re.shape[1], S5_STATE, S5_GROUP
    d_ff = w_ffn_out.shape[1] * 4
    dq4 = D // 4
    chip = 2 * lax.axis_index("x") + lax.axis_index("y")

    small_in = jnp.concatenate([meta_tokens, jnp.pad(s5_d, ((0, 15), (0, 0)))], axis=0)
    (small_all,) = _all_gather_chips("ag_small", [small_in])
    meta_full = small_all[:, :N_META, :].transpose(1, 0, 2).reshape(N_META, D)
    d_skip = small_all[:, N_META, :].reshape(1, D)
    shards = [s.astype(BF16) for s in (s5_w_glu[0], w_kv, w_q[0], w_o[0],
                                       w_ffn_in.reshape(2 * D, -1), w_ffn_out.reshape(-1, D))]
    rows_out = d_ff // 4

    row = lambda v: v.reshape(1, -1)
    g_mix0, g_mix1 = row(norm_mix[0]), row(norm_mix[1])
    g_ffn = [row(norm_ffn[0]), row(norm_ffn[1])]
    g_kv, g_final = row(norm_kv), row(norm_final)

    a_re3 = s5_a_re[0].reshape(G, 1, P)
    a_im3 = s5_a_im[0].reshape(G, 1, P)
    log_dt3 = s5_log_dt[0].reshape(G, 1, 1)
    bt_re = s5_b_re[0].transpose(0, 2, 1)
    bt_im = s5_b_im[0].transpose(0, 2, 1)
    ab_re, ab_im, bb_re, bb_im = _s5_prep_fwd("s5_prep", a_re3, a_im3, log_dt3, bt_re, bt_im)
    abar_re, abar_im = ab_re.reshape(1, G * P), ab_im.reshape(1, G * P)
    bd_b_re = _block_diag(bb_re).astype(BF16)
    bd_b_im = _block_diag(bb_im).astype(BF16)
    bd_bt_re = bd_b_re.transpose(0, 2, 1)
    bd_bt_im = bd_b_im.transpose(0, 2, 1)
    bd_c_re = _block_diag(s5_c_re[0]).astype(BF16)
    bd_c_im = _block_diag(s5_c_im[0]).astype(BF16)
    bd_ct_re = bd_c_re.transpose(0, 2, 1)
    bd_ct_im = bd_c_im.transpose(0, 2, 1)

    h0 = jnp.concatenate([jnp.zeros((META_START, D), F32), meta_full, x[0]], axis=0)
    (u,) = _rmsnorm_fwd("norm_mix0", h0, [g_mix0], [F32])
    (y, z, x_re, x_im), landed = _s5_fwd(
        "s5_scan", u, bd_b_re, bd_b_im, bd_ct_re, bd_ct_im, abar_re, abar_im, d_skip,
        comm=_halves_gather_comm(shards), comm_args=shards)
    wg_glu, wg_kv, wg_q, wg_o, wg_in, wg_out = _pair_forward("ag_forward", landed, shards)
    wg_q = wg_q.reshape(1, D, D)
    wg_o = wg_o.reshape(1, D, D)
    wg_out = [wg_out[:, l * rows_out:(l + 1) * rows_out, :].reshape(1, d_ff, D) for l in range(2)]
    vg = _mm_nn("glu_proj", z, wg_glu)
    h1 = _glu_fwd("glu", vg, h0)

    def ffn_fwd(l, h):
        (n,) = _rmsnorm_fwd(f"norm_ffn{l}", h, [g_ffn[l]], [BF16])
        g, u, mid = _mm_swiglu(f"ffn_in{l}", n, wg_in, l)
        return n, (g, u), mid, _mm_nn(f"ffn_out{l}", mid, wg_out[l], res=h)

    n1, gu0, mid0, h2 = ffn_fwd(0, h1)
    nk, nq = _rmsnorm_fwd("norm_kv_q", h2, [g_kv, g_mix1], [BF16, BF16])
    kv = _mm_nn("kv_proj", nk, wg_kv, out_dtype=BF16)
    q = _mm_nn("q_proj", nq, wg_q, out_dtype=BF16)
    o, ltot = _attn_fwd("attn_fwd", q, kv)
    h3 = _mm_nn("o_proj", o, wg_o, res=h2)
    n3, gu1, mid1, h4 = ffn_fwd(1, h3)
    loss_part, dh4, dg_final = _final_loss("final_loss", h4, g_final, loss_target[0])

    def ffn_bwd(l, dh, h, n, gu, mid):
        dgu = _mm_nt_swiglu_bwd(f"ffn_out{l}_dx", dh, wg_out[l], *gu)
        dw_out = _mm_tn(f"ffn_out{l}_dw", mid, dh, 1)
        dw_in = _mm_tn(f"ffn_in{l}_dw", n, dgu, 4)
        dn = _mm_nt_k(f"ffn_in{l}_dx", dgu, wg_in, D, k_blk=l)
        dh_prev, (dg,) = _rmsnorm_bwd(f"norm_ffn{l}_bwd", h, [(g_ffn[l], dn)], dh)
        return dh_prev, dg, dw_in, dw_out

    dh3, dg_ffn1, dw_in1, dw_out1 = ffn_bwd(1, dh4, h3, n3, gu1, mid1)
    d_o = _mm_nt_k("o_proj_dx", dh3, wg_o, D, out_dtype=BF16)
    dw_o = _mm_tn("o_proj_dw", o, dh3, 1)
    dq, dk, dv = _attn_bwd("attn_bwd", q, kv, d_o, ltot)
    dkv = jnp.concatenate([dk, dv], axis=1)
    dw_q = _mm_tn("q_proj_dw", nq, dq, 1)
    dnq = _mm_nt_k("q_proj_dx", dq, wg_q, D)
    dw_kv = _mm_tn("kv_proj_dw", nk, dkv, 4)
    dnk = _mm_nt_k("kv_proj_dx", dkv, wg_kv, D)
    dh2, (dg_mix1, dg_kv) = _rmsnorm_bwd("norm_kv_q_bwd", h2, [(g_mix1, dnq), (g_kv, dnk)], dh3)
    dh1, dg_ffn0, dw_in0, dw_out0 = ffn_bwd(0, dh2, h1, n1, gu0, mid0)
    dvg = _glu_bwd("glu_bwd", vg, dh1)
    dw_glu = _mm_tn("glu_proj_dw", z, dvg, 4)
    dz = _mm_nt_k("glu_proj_dx", dvg, wg_glu, D)
    big = [dw_kv, dw_q.reshape(4, D // 4, D), dw_o.reshape(4, D // 4, D), dw_in1, dw_out1.reshape(4, rows_out, D),
           dw_glu, dw_in0, dw_out0.reshape(4, rows_out, D)]
    big_pairs = _pair_sums("big", big)
    (du, dd, dbd_b_re, dbd_b_im, dbd_c_re, dbd_c_im, dab_re, dab_im), big_parts = _s5_bwd(
        "s5_scan_bwd", dz, y, u, x_re, x_im, bd_c_re, bd_c_im, bd_bt_re, bd_bt_im, abar_re, abar_im, d_skip,
        comm=_exchange_comm(big_pairs), comm_args=big_pairs)
    big_parts = _exchange_finish(big_pairs, big_parts)
    dh0, (dg_mix0,) = _rmsnorm_bwd("norm_mix0_bwd", h0, [(g_mix0, du)], dh1)
    da_re, da_im, dlog_dt, dbt_re, dbt_im = _s5_prep_bwd(
        "s5_prep_bwd", a_re3, a_im3, log_dt3, bt_re, bt_im,
        dab_re.reshape(G, 1, P), dab_im.reshape(G, 1, P),
        _block_diag_extract(dbd_b_re, C, P), _block_diag_extract(dbd_b_im, C, P))
    grad_x = dh0[X_START:][None]

    small_parts = [
        dg_mix0, dg_mix1, dg_ffn0, dg_ffn1, da_re, da_im,
        dbt_re.transpose(0, 2, 1), dbt_im.transpose(0, 2, 1),
        _block_diag_extract(dbd_c_re, C, P), _block_diag_extract(dbd_c_im, C, P),
        dg_kv, dg_final, dh0[META_START:X_START], dd, loss_part, dlog_dt]
    small_sizes = [p.size for p in small_parts]
    unit = 4 * 2 * 8 * 128
    padded = -(-sum(small_sizes) // unit) * unit
    tail = jnp.concatenate([loss_part.reshape(-1), dlog_dt.reshape(-1)])
    small_flat = jnp.concatenate(
        [p.reshape(-1) for p in small_parts[:-2]] + [jnp.pad(tail, (0, padded - sum(small_sizes)))])
    small_blocks = small_flat.reshape(4, padded // (4 * 128), 128)
    small_exchanged = _chip_exchange("rs_chip_exchange", _pair_sums("small", [small_blocks]))
    gw_kv, gw_q, gw_o, gw_in1, gw_out1, gw_glu, gw_in0, gw_out0, small_mine = _chip_sums(big_parts + small_exchanged)
    gw_in = jnp.concatenate([gw_in0, gw_in1], axis=0)
    gw_out = jnp.concatenate([gw_out0, gw_out1], axis=0)
    (small_red,) = _all_gather_chips("ag_small_grads", [small_mine])
    small_red = small_red.reshape(-1)
    pieces, at = [], 0
    for p, size in zip(small_parts, small_sizes):
        pieces.append(small_red[at:at + size].reshape(p.shape))
        at += size
    (gn_mix0, gn_mix1, gn_ffn0, gn_ffn1, ga_re, ga_im, gb_re, gb_im, gc_re, gc_im, gn_kv, gn_final,
     gmeta_full, gd_full, loss_all, glog_dt) = pieces
    loss = loss_all[0, 0]
    gn_mix = small_red[:2 * D].reshape(2, D)
    gn_ffn = small_red[2 * D:4 * D].reshape(2, D)
    gmeta = lax.dynamic_slice_in_dim(gmeta_full, chip * dq4, dq4, axis=1)
    gd = lax.dynamic_slice_in_dim(gd_full, chip * dq4, dq4, axis=1)

    grads = {
        "meta_tokens": gmeta, "norm_mix": gn_mix, "norm_ffn": gn_ffn,
        "s5_a_re": ga_re.reshape(s5_a_re.shape), "s5_a_im": ga_im.reshape(s5_a_im.shape),
        "s5_log_dt": glog_dt.reshape(s5_log_dt.shape),
        "s5_b_re": gb_re.reshape(s5_b_re.shape), "s5_b_im": gb_im.reshape(s5_b_im.shape),
        "s5_c_re": gc_re.reshape(s5_c_re.shape), "s5_c_im": gc_im.reshape(s5_c_im.shape),
        "s5_d": gd, "s5_w_glu": gw_glu.reshape(s5_w_glu.shape), "norm_kv": gn_kv.reshape(norm_kv.shape),
        "w_kv": gw_kv, "w_q": gw_q.reshape(w_q.shape), "w_o": gw_o.reshape(w_o.shape),
        "w_ffn_in": gw_in.reshape(w_ffn_in.shape), "w_ffn_out": gw_out.reshape(w_ffn_out.shape),
        "norm_final": gn_final.reshape(norm_final.shape),
    }
    weights = {
        "meta_tokens": (meta_tokens, m_meta_tokens, v_meta_tokens), "norm_mix": (norm_mix, m_norm_mix, v_norm_mix),
        "norm_ffn": (norm_ffn, m_norm_ffn, v_norm_ffn), "s5_a_re": (s5_a_re, m_s5_a_re, v_s5_a_re),
        "s5_a_im": (s5_a_im, m_s5_a_im, v_s5_a_im), "s5_log_dt": (s5_log_dt, m_s5_log_dt, v_s5_log_dt),
        "s5_b_re": (s5_b_re, m_s5_b_re, v_s5_b_re), "s5_b_im": (s5_b_im, m_s5_b_im, v_s5_b_im),
        "s5_c_re": (s5_c_re, m_s5_c_re, v_s5_c_re), "s5_c_im": (s5_c_im, m_s5_c_im, v_s5_c_im),
        "s5_d": (s5_d, m_s5_d, v_s5_d), "s5_w_glu": (s5_w_glu, m_s5_w_glu, v_s5_w_glu),
        "norm_kv": (norm_kv, m_norm_kv, v_norm_kv), "w_kv": (w_kv, m_w_kv, v_w_kv), "w_q": (w_q, m_w_q, v_w_q),
        "w_o": (w_o, m_w_o, v_w_o), "w_ffn_in": (w_ffn_in, m_w_ffn_in, v_w_ffn_in),
        "w_ffn_out": (w_ffn_out, m_w_ffn_out, v_w_ffn_out), "norm_final": (norm_final, m_norm_final, v_norm_final),
    }
    names = list(weights)
    deltas, new_m, new_v = [], [], []
    for name in names:
        w, m, v = weights[name]
        d, nm, nv = _adamw(f"adamw_{name}", w, grads[name], m, v)
        deltas.append(d)
        new_m.append(nm)
        new_v.append(nv)
    return (loss, grad_x, *[grads[n] for n in names], *deltas, *new_m, *new_v)
```

```python
import functools
import math

import jax
import jax.numpy as jnp
from jax import lax
from jax.experimental import pallas as pl
from jax.experimental.pallas import tpu as pltpu

F32 = jnp.float32
BF16 = jnp.bfloat16

N_META = 16
X_START = 128
META_START = X_START - N_META
S5_GROUP = 16
S5_STATE = 64
HEAD_DIM = 64
KEY_BLOCK = 128
STATE_TILE = 512
CH_TILE = 128
RMS_EPS = 1e-6
ADAM_LR, ADAM_B1, ADAM_B2, ADAM_EPS, ADAM_WD, ADAM_STEP = 0.001, 0.9, 0.999, 1e-08, 0.01, 10
VMEM_LIMIT_BYTES = 48 * 1024 * 1024
MESH = pl.DeviceIdType.MESH
NT_DIMS = (((1,), (1,)), ((), ()))
TN_DIMS = (((0,), (0,)), ((), ()))


def _cparams(*sem):
    return pltpu.CompilerParams(dimension_semantics=sem, vmem_limit_bytes=VMEM_LIMIT_BYTES)


def _row_tile(rows, cap):
    for unit in (128, 8):
        best = 0
        for t in range(unit, min(rows, cap) + 1, unit):
            if rows % t == 0:
                best = t
        if best:
            return best
    return rows


def _col_tile(cols, cap):
    best = 0
    for t in range(128, min(cols, cap) + 1, 128):
        if cols % t == 0:
            best = t
    return best if best else cols


def _gelu(x):
    k = math.sqrt(2.0 / math.pi)
    return 0.5 * x * (1.0 + jnp.tanh(k * (x + 0.044715 * x * x * x)))


def _gelu_grad(x):
    k = math.sqrt(2.0 / math.pi)
    t = jnp.tanh(k * (x + 0.044715 * x * x * x))
    return 0.5 * (1.0 + t) + 0.5 * x * (1.0 - t * t) * k * (1.0 + 3.0 * 0.044715 * x * x)


def _sigmoid(x):
    return 1.0 / (1.0 + jnp.exp(-x))


def _rmsnorm_fwd(name, x, gains, out_dtypes):
    T, D = x.shape
    tm = _row_tile(T, 512)
    n = len(gains)

    def body(x_ref, *refs):
        xv = x_ref[...]
        xh = xv * lax.rsqrt(jnp.mean(xv * xv, axis=-1, keepdims=True) + RMS_EPS)
        for g_ref, o_ref in zip(refs[:n], refs[n:]):
            o_ref[...] = (xh * g_ref[...]).astype(o_ref.dtype)

    row = pl.BlockSpec((tm, D), lambda i: (i, 0))
    vec = pl.BlockSpec((1, D), lambda i: (0, 0))
    return pl.pallas_call(
        body, name=name, grid=(T // tm,),
        in_specs=[row] + [vec] * n, out_specs=[row] * n,
        out_shape=[jax.ShapeDtypeStruct((T, D), dt) for dt in out_dtypes],
        compiler_params=_cparams("parallel"),
    )(x, *gains)


def _rmsnorm_bwd(name, x, pairs, dres):
    T, D = x.shape
    tm = _row_tile(T, 512)
    n = len(pairs)

    def body(x_ref, dres_ref, *refs):
        g_refs, dy_refs = refs[:n], refs[n:2 * n]
        dx_ref, dg_refs = refs[2 * n], refs[2 * n + 1:]
        i = pl.program_id(0)
        xv = x_ref[...]
        r = lax.rsqrt(jnp.mean(xv * xv, axis=-1, keepdims=True) + RMS_EPS)
        xh = xv * r
        dxh = jnp.zeros_like(xv)
        for g_ref, dy_ref, dg_ref in zip(g_refs, dy_refs, dg_refs):
            dy = dy_ref[...].astype(F32)
            part = jnp.sum(dy * xh, axis=0, keepdims=True)

            @pl.when(i == 0)
            def _():
                dg_ref[...] = part

            @pl.when(i > 0)
            def _():
                dg_ref[...] += part

            dxh = dxh + dy * g_ref[...]
        dx = r * (dxh - xh * jnp.mean(dxh * xh, axis=-1, keepdims=True))
        dx_ref[...] = dres_ref[...] + dx

    row = pl.BlockSpec((tm, D), lambda i: (i, 0))
    vec = pl.BlockSpec((1, D), lambda i: (0, 0))
    outs = pl.pallas_call(
        body, name=name, grid=(T // tm,),
        in_specs=[row, row] + [vec] * n + [row] * n,
        out_specs=[row] + [vec] * n,
        out_shape=[jax.ShapeDtypeStruct((T, D), F32)] + [jax.ShapeDtypeStruct((1, D), F32)] * n,
        compiler_params=_cparams("arbitrary"),
    )(x, dres, *[g for g, _ in pairs], *[dy for _, dy in pairs])
    return outs[0], outs[1:]


def _mm_nn(name, a, w, k_blk=0, res=None, out_dtype=F32):
    M, K = a.shape
    S, _, Ns = w.shape
    tm = _row_tile(M, 512)
    tn = _col_tile(Ns, 1408)
    nt = Ns // tn

    def body(a_ref, w_ref, *refs):
        o_ref = refs[-1]
        acc = jnp.dot(a_ref[...].astype(BF16), w_ref[...], preferred_element_type=F32)
        if res is not None:
            acc = acc + refs[0][...]
        o_ref[...] = acc.astype(o_ref.dtype)

    in_specs = [pl.BlockSpec((tm, K), lambda j, i: (i, 0)),
                pl.BlockSpec((None, K, tn), lambda j, i: (j // nt, k_blk, j % nt))]
    args = [a, w]
    if res is not None:
        in_specs.append(pl.BlockSpec((tm, tn), lambda j, i: (i, j)))
        args.append(res)
    return pl.pallas_call(
        body, name=name, grid=(S * nt, M // tm),
        in_specs=in_specs, out_specs=pl.BlockSpec((tm, tn), lambda j, i: (i, j)),
        out_shape=jax.ShapeDtypeStruct((M, S * Ns), out_dtype),
        compiler_params=_cparams("parallel", "parallel"),
    )(*args)


def _mm_nt_k(name, dy, w, K, k_blk=0, out_dtype=F32):
    M = dy.shape[0]
    S, _, Ns = w.shape
    tm = _row_tile(M, 1408)
    tn = _col_tile(Ns, 1408)
    nt = Ns // tn
    steps = S * nt

    def body(dy_ref, w_ref, o_ref, acc_ref):
        j = pl.program_id(1)
        part = lax.dot_general(dy_ref[...].astype(BF16), w_ref[...], (((1,), (1,)), ((), ())),
                               preferred_element_type=F32)

        @pl.when(j == 0)
        def _():
            acc_ref[...] = part

        @pl.when(j > 0)
        def _():
            acc_ref[...] += part

        @pl.when(j == steps - 1)
        def _():
            o_ref[...] = acc_ref[...].astype(o_ref.dtype)

    return pl.pallas_call(
        body, name=name, grid=(M // tm, steps),
        in_specs=[pl.BlockSpec((tm, tn), lambda i, j: (i, j)),
                  pl.BlockSpec((None, K, tn), lambda i, j: (j // nt, k_blk, j % nt))],
        out_specs=pl.BlockSpec((tm, K), lambda i, j: (i, 0)),
        out_shape=jax.ShapeDtypeStruct((M, K), out_dtype),
        scratch_shapes=[pltpu.VMEM((tm, K), F32)],
        compiler_params=_cparams("parallel", "arbitrary"),
    )(dy, w)


def _mm_tn(name, a, dy, S, out_dtype=BF16):
    T, K = a.shape
    Ns = dy.shape[1] // S
    tn = _col_tile(Ns, max(128, (6 * 1024 * 1024) // (4 * K) // 128 * 128))
    nt = Ns // tn
    tt = _row_tile(T, 1408)
    steps = T // tt

    def body(a_ref, dy_ref, o_ref, acc_ref):
        t = pl.program_id(1)
        part = lax.dot_general(a_ref[...].astype(BF16), dy_ref[...].astype(BF16), (((0,), (0,)), ((), ())),
                               preferred_element_type=F32)

        @pl.when(t == 0)
        def _():
            acc_ref[...] = part

        @pl.when(t > 0)
        def _():
            acc_ref[...] += part

        @pl.when(t == steps - 1)
        def _():
            o_ref[...] = acc_ref[...].astype(o_ref.dtype)

    return pl.pallas_call(
        body, name=name, grid=(S * nt, steps),
        in_specs=[pl.BlockSpec((tt, K), lambda j, t: (t, 0)),
                  pl.BlockSpec((tt, tn), lambda j, t: (t, j))],
        out_specs=pl.BlockSpec((None, K, tn), lambda j, t: (j // nt, 0, j % nt)),
        out_shape=jax.ShapeDtypeStruct((S, K, Ns), out_dtype),
        scratch_shapes=[pltpu.VMEM((K, tn), F32)],
        compiler_params=_cparams("parallel", "arbitrary"),
    )(a, dy)


def _gated_tile(T, width):
    return _row_tile(T, max(8, (2 * 1024 * 1024) // (4 * width) // 8 * 8))


def _mm_glu(name, a, w, res):
    M, K = a.shape
    Ns = w.shape[2]
    tm = _row_tile(M, 512)

    def body(a_ref, wv_ref, wg_ref, r_ref, v_ref, g_ref, o_ref):
        ab = a_ref[...].astype(BF16)
        val = jnp.dot(ab, wv_ref[...], preferred_element_type=F32)
        gate = jnp.dot(ab, wg_ref[...], preferred_element_type=F32)
        v_ref[...] = val
        g_ref[...] = gate
        o_ref[...] = r_ref[...] + val * _sigmoid(gate)

    out = pl.BlockSpec((tm, Ns), lambda j, i: (i, j))
    return pl.pallas_call(
        body, name=name, grid=(2, M // tm),
        in_specs=[pl.BlockSpec((tm, K), lambda j, i: (i, 0)),
                  pl.BlockSpec((None, K, Ns), lambda j, i: (j, 0, 0)),
                  pl.BlockSpec((None, K, Ns), lambda j, i: (j + 2, 0, 0)), out],
        out_specs=[out, out, out],
        out_shape=[jax.ShapeDtypeStruct((M, 2 * Ns), F32)] * 3,
        compiler_params=_cparams("parallel", "parallel"),
    )(a, w, w, res)


def _glu_bwd(name, val, gate, dout):
    T, D = dout.shape
    tm = _gated_tile(T, 2 * D)

    def body(v_ref, g_ref, d_ref, o_ref):
        s = _sigmoid(g_ref[...])
        d = d_ref[...]
        o_ref[:, :D] = (d * s).astype(o_ref.dtype)
        o_ref[:, D:] = (d * v_ref[...] * s * (1.0 - s)).astype(o_ref.dtype)

    blk = pl.BlockSpec((tm, D), lambda i: (i, 0))
    return pl.pallas_call(
        body, name=name, grid=(T // tm,),
        in_specs=[blk, blk, blk],
        out_specs=pl.BlockSpec((tm, 2 * D), lambda i: (i, 0)),
        out_shape=jax.ShapeDtypeStruct((T, 2 * D), BF16),
        compiler_params=_cparams("parallel"),
    )(val, gate, dout)


def _mm_swiglu(name, a, w, k_blk):
    M, K = a.shape
    Ns = w.shape[2]
    tm = _row_tile(M, 512)

    def body(a_ref, wg_ref, wu_ref, g_ref, u_ref, mid_ref):
        ab = a_ref[...].astype(BF16)
        g = jnp.dot(ab, wg_ref[...], preferred_element_type=F32)
        u = jnp.dot(ab, wu_ref[...], preferred_element_type=F32)
        g_ref[...] = g.astype(g_ref.dtype)
        u_ref[...] = u.astype(u_ref.dtype)
        mid_ref[...] = (g * _sigmoid(g) * u).astype(mid_ref.dtype)

    out = pl.BlockSpec((tm, Ns), lambda j, i: (i, j))
    return pl.pallas_call(
        body, name=name, grid=(2, M // tm),
        in_specs=[pl.BlockSpec((tm, K), lambda j, i: (i, 0)),
                  pl.BlockSpec((None, K, Ns), lambda j, i: (j, k_blk, 0)),
                  pl.BlockSpec((None, K, Ns), lambda j, i: (j + 2, k_blk, 0))],
        out_specs=[out, out, out],
        out_shape=[jax.ShapeDtypeStruct((M, 2 * Ns), BF16)] * 3,
        compiler_params=_cparams("parallel", "parallel"),
    )(a, w, w)


def _mm_nt_swiglu_bwd(name, dh, w, g, u):
    M, D = dh.shape
    F = w.shape[1]
    tm = _row_tile(M, 512)

    def body(dh_ref, w_ref, g_ref, u_ref, o_ref):
        d = lax.dot_general(dh_ref[...].astype(BF16), w_ref[...], NT_DIMS, preferred_element_type=F32)
        gv = g_ref[...].astype(F32)
        s = _sigmoid(gv)
        o_ref[:, :F] = (d * u_ref[...].astype(F32) * s * (1.0 + gv * (1.0 - s))).astype(o_ref.dtype)
        o_ref[:, F:] = (d * gv * s).astype(o_ref.dtype)

    half = pl.BlockSpec((tm, F), lambda i: (i, 0))
    return pl.pallas_call(
        body, name=name, grid=(M // tm,),
        in_specs=[pl.BlockSpec((tm, D), lambda i: (i, 0)), pl.BlockSpec((None, F, D), lambda i: (0, 0, 0)), half, half],
        out_specs=pl.BlockSpec((tm, 2 * F), lambda i: (i, 0)),
        out_shape=jax.ShapeDtypeStruct((M, 2 * F), BF16),
        compiler_params=_cparams("parallel"),
    )(dh, w, g, u)


def _final_loss(name, h, gain, target):
    T, D = h.shape
    tm = X_START
    lead = X_START // tm

    def body(h_ref, g_ref, t_ref, loss_ref, dh_ref, dg_ref):
        i = pl.program_id(0)

        @pl.when(i == 0)
        def _():
            loss_ref[...] = jnp.zeros_like(loss_ref)
            dg_ref[...] = jnp.zeros_like(dg_ref)
            dh_ref[...] = jnp.zeros_like(dh_ref)

        @pl.when(i >= lead)
        def _():
            xv = h_ref[...]
            r = lax.rsqrt(jnp.mean(xv * xv, axis=-1, keepdims=True) + RMS_EPS)
            xh = xv * r
            g = g_ref[...]
            diff = xh * g - t_ref[...]
            loss_ref[...] += 0.5 * jnp.sum(jnp.mean(diff * diff, axis=-1, keepdims=True), axis=0, keepdims=True)
            dout = diff * (1.0 / D)
            dg_ref[...] += jnp.sum(dout * xh, axis=0, keepdims=True)
            dxh = dout * g
            dh_ref[...] = r * (dxh - xh * jnp.mean(dxh * xh, axis=-1, keepdims=True))

    return pl.pallas_call(
        body, name=name, grid=(T // tm,),
        in_specs=[pl.BlockSpec((tm, D), lambda i: (i, 0)), pl.BlockSpec((1, D), lambda i: (0, 0)),
                  pl.BlockSpec((tm, D), lambda i: (jnp.maximum(i - lead, 0), 0))],
        out_specs=[pl.BlockSpec((1, 128), lambda i: (0, 0)), pl.BlockSpec((tm, D), lambda i: (i, 0)),
                   pl.BlockSpec((1, D), lambda i: (0, 0))],
        out_shape=[jax.ShapeDtypeStruct((1, 128), F32), jax.ShapeDtypeStruct((T, D), F32),
                   jax.ShapeDtypeStruct((1, D), F32)],
        compiler_params=_cparams("arbitrary"),
    )(h, gain, target)


def _grid_call(name, body, grid, in_specs, out_specs, out_shape, scratch_shapes, args, comm=None, comm_args=()):
    params = _cparams(*(("arbitrary",) * len(grid)))
    if comm is None:
        return pl.pallas_call(body, name=name, grid=grid, in_specs=in_specs, out_specs=out_specs,
                              out_shape=out_shape, scratch_shapes=scratch_shapes, compiler_params=params)(*args), []
    outs = pl.pallas_call(
        _embed_comm(comm, body, grid, len(in_specs), len(out_specs)), name=name, grid=grid,
        in_specs=list(in_specs) + _any_specs(comm.n), out_specs=list(out_specs) + _any_specs(comm.n),
        out_shape=list(out_shape) + comm.out_shape, scratch_shapes=list(scratch_shapes) + comm.scratch,
        compiler_params=params)(*args, *comm_args)
    return outs[:len(out_specs)], outs[len(out_specs):]


def _s5_discretise(a_re, a_im, log_dt, bt_re, bt_im):
    dt = jnp.exp(log_dt)
    mag = jnp.exp(dt * a_re)
    ang = dt * a_im
    abar_re = mag * jnp.cos(ang)
    abar_im = mag * jnp.sin(ang)
    den = a_re * a_re + a_im * a_im
    coef_re = ((abar_re - 1.0) * a_re + abar_im * a_im) / den
    coef_im = (abar_im * a_re - (abar_re - 1.0) * a_im) / den
    bbar_re = coef_re * bt_re - coef_im * bt_im
    bbar_im = coef_re * bt_im + coef_im * bt_re
    return abar_re, abar_im, bbar_re, bbar_im


def _s5_prep_fwd(name, a_re, a_im, log_dt, bt_re, bt_im):
    G, _, P = a_re.shape
    C = bt_re.shape[1]

    def body(ar, ai, ld, br, bi, o_ar, o_ai, o_br, o_bi):
        outs = _s5_discretise(ar[...], ai[...], ld[...], br[...], bi[...])
        for o, v in zip((o_ar, o_ai, o_br, o_bi), outs):
            o[...] = v

    return pl.pallas_call(
        body, name=name,
        out_shape=[jax.ShapeDtypeStruct((G, 1, P), F32)] * 2 + [jax.ShapeDtypeStruct((G, C, P), F32)] * 2,
    )(a_re, a_im, log_dt, bt_re, bt_im)


def _s5_prep_bwd(name, a_re, a_im, log_dt, bt_re, bt_im, d_ar, d_ai, d_br, d_bi):
    G, _, P = a_re.shape
    C = bt_re.shape[1]

    def body(ar, ai, ld, br, bi, gar, gai, gbr, gbi, o_ar, o_ai, o_ld, o_br, o_bi):
        _, vjp = jax.vjp(_s5_discretise, ar[...], ai[...], ld[...], br[...], bi[...])
        grads = vjp((gar[...], gai[...], gbr[...], gbi[...]))
        for o, v in zip((o_ar, o_ai, o_ld, o_br, o_bi), grads):
            o[...] = v

    return pl.pallas_call(
        body, name=name,
        out_shape=[jax.ShapeDtypeStruct((G, 1, P), F32)] * 2 + [jax.ShapeDtypeStruct((G, 1, 1), F32)]
        + [jax.ShapeDtypeStruct((G, C, P), F32)] * 2,
    )(a_re, a_im, log_dt, bt_re, bt_im, d_ar, d_ai, d_br, d_bi)


def _cmul(ar, ai, br, bi):
    return ar * br - ai * bi, ar * bi + ai * br


def _power_table(a_re, a_im):
    rows_re, rows_im = [a_re], [a_im]
    for _ in range(7):
        r, m = _cmul(rows_re[-1], rows_im[-1], a_re, a_im)
        rows_re.append(r)
        rows_im.append(m)
    row = lax.broadcasted_iota(jnp.int32, (8, a_re.shape[1]), 0)
    t_re = jnp.zeros((8, a_re.shape[1]), F32)
    t_im = jnp.zeros((8, a_re.shape[1]), F32)
    for k in range(8):
        t_re = jnp.where(row == k, rows_re[k], t_re)
        t_im = jnp.where(row == k, rows_im[k], t_im)
    return t_re, t_im, rows_re, rows_im


def _s5_fwd(name, u, b_re, b_im, ct_re, ct_im, abar_re, abar_im, d_skip, comm=None, comm_args=()):
    T, D = u.shape
    n_st = D // CH_TILE
    W = STATE_TILE
    tc = _row_tile(T, 512)
    n_tiles = tc // 8

    def body(u_ref, bre_ref, bim_ref, cre_ref, cim_ref, ar_ref, ai_ref, d_ref,
             y_ref, z_ref, xr_ref, xi_ref, carry_re, carry_im, pw_re, pw_im, sh_re, sh_im):
        c = pl.program_id(1)

        @pl.when(c == 0)
        def _():
            t_re, t_im, rows_re, rows_im = _power_table(ar_ref[...], ai_ref[...])
            pw_re[...] = t_re
            pw_im[...] = t_im
            first_rows = lax.broadcasted_iota(jnp.int32, (8, W), 0)
            for n, d in enumerate((1, 2, 4)):
                sh_re[n] = jnp.where(first_rows >= d, jnp.broadcast_to(rows_re[d - 1], (8, W)), 0.0)
                sh_im[n] = jnp.where(first_rows >= d, jnp.broadcast_to(rows_im[d - 1], (8, W)), 0.0)
            carry_re[...] = jnp.zeros_like(carry_re)
            carry_im[...] = jnp.zeros_like(carry_im)

        ub = u_ref[...].astype(BF16)
        xr_ref[...] = jnp.dot(ub, bre_ref[...], preferred_element_type=F32)
        xi_ref[...] = jnp.dot(ub, bim_ref[...], preferred_element_type=F32)
        row = lax.broadcasted_iota(jnp.int32, (8, W), 0)

        def tile(i, carry):
            c_re, c_im = carry
            rows = pl.ds(pl.multiple_of(i * 8, 8), 8)
            r = xr_ref[rows, :]
            m = xi_ref[rows, :]
            for n, d in enumerate((1, 2, 4)):
                pr, pm = _cmul(sh_re[n], sh_im[n], pltpu.roll(r, d, 0), pltpu.roll(m, d, 0))
                r = r + pr
                m = m + pm
            pr, pm = _cmul(pw_re[...], pw_im[...], c_re, c_im)
            r = r + pr
            m = m + pm
            xr_ref[rows, :] = r
            xi_ref[rows, :] = m
            return jnp.broadcast_to(r[7:8, :], (8, W)), jnp.broadcast_to(m[7:8, :], (8, W))

        c_re, c_im = lax.fori_loop(0, n_tiles, tile, (carry_re[...], carry_im[...]))
        carry_re[...] = c_re
        carry_im[...] = c_im
        y = (jnp.dot(xr_ref[...].astype(BF16), cre_ref[...], preferred_element_type=F32)
             - jnp.dot(xi_ref[...].astype(BF16), cim_ref[...], preferred_element_type=F32)
             + d_ref[...] * u_ref[...])
        y_ref[...] = y
        z_ref[...] = _gelu(y).astype(z_ref.dtype)

    ch = pl.BlockSpec((tc, CH_TILE), lambda s, c: (c, s))
    st = pl.BlockSpec((tc, W), lambda s, c: (c, s))
    return _grid_call(
        name, body, (n_st, T // tc),
        in_specs=[ch,
                  pl.BlockSpec((None, CH_TILE, W), lambda s, c: (s, 0, 0)),
                  pl.BlockSpec((None, CH_TILE, W), lambda s, c: (s, 0, 0)),
                  pl.BlockSpec((None, W, CH_TILE), lambda s, c: (s, 0, 0)),
                  pl.BlockSpec((None, W, CH_TILE), lambda s, c: (s, 0, 0)),
                  pl.BlockSpec((1, W), lambda s, c: (0, s)),
                  pl.BlockSpec((1, W), lambda s, c: (0, s)),
                  pl.BlockSpec((1, CH_TILE), lambda s, c: (0, s))],
        out_specs=[ch, ch, st, st],
        out_shape=[jax.ShapeDtypeStruct((T, D), F32), jax.ShapeDtypeStruct((T, D), BF16),
                   jax.ShapeDtypeStruct((T, 4 * D), F32), jax.ShapeDtypeStruct((T, 4 * D), F32)],
        scratch_shapes=[pltpu.VMEM((8, W), F32), pltpu.VMEM((8, W), F32),
                        pltpu.VMEM((8, W), F32), pltpu.VMEM((8, W), F32),
                        pltpu.VMEM((3, 8, W), F32), pltpu.VMEM((3, 8, W), F32)],
        args=(u, b_re, b_im, ct_re, ct_im, abar_re, abar_im, d_skip), comm=comm, comm_args=comm_args)


def _s5_bwd(name, dz, y, u, x_re, x_im, c_re, c_im, bt_re, bt_im, abar_re, abar_im, d_skip, comm=None, comm_args=()):
    T, D = u.shape
    n_st = D // CH_TILE
    W = STATE_TILE
    tc = _row_tile(T, 512)
    n_chunks = T // tc
    n_tiles = tc // 8
    tiles_per_chunk = tc // 8

    def body(dz_ref, y_ref, u_ref, xr_ref, xi_ref, xpr_ref, xpi_ref, cre_ref, cim_ref, btr_ref, bti_ref,
             ar_ref, ai_ref, d_ref,
             du_ref, dd_ref, dbr_ref, dbi_ref, dcr_ref, dci_ref, dar_ref, dai_ref,
             lam_re, lam_im, xe_re, xe_im, carry_re, carry_im, pw_re, pw_im, sh_re, sh_im, acc_ar, acc_ai):
        k = pl.program_id(1)
        first_chunk = k == n_chunks - 1

        @pl.when(k == 0)
        def _():
            t_re, t_im, rows_re, rows_im = _power_table(ar_ref[...], -ai_ref[...])
            row = lax.broadcasted_iota(jnp.int32, (8, W), 0)
            r_re = jnp.zeros((8, W), F32)
            r_im = jnp.zeros((8, W), F32)
            for j in range(8):
                r_re = jnp.where(row == j, rows_re[7 - j], r_re)
                r_im = jnp.where(row == j, rows_im[7 - j], r_im)
            pw_re[...] = r_re
            pw_im[...] = r_im
            for n, d in enumerate((1, 2, 4)):
                sh_re[n] = jnp.where(row < 8 - d, jnp.broadcast_to(rows_re[d - 1], (8, W)), 0.0)
                sh_im[n] = jnp.where(row < 8 - d, jnp.broadcast_to(rows_im[d - 1], (8, W)), 0.0)
            carry_re[...] = jnp.zeros_like(carry_re)
            carry_im[...] = jnp.zeros_like(carry_im)
            acc_ar[...] = jnp.zeros_like(acc_ar)
            acc_ai[...] = jnp.zeros_like(acc_ai)
            dd_ref[...] = jnp.zeros_like(dd_ref)
            dbr_ref[...] = jnp.zeros_like(dbr_ref)
            dbi_ref[...] = jnp.zeros_like(dbi_ref)
            dcr_ref[...] = jnp.zeros_like(dcr_ref)
            dci_ref[...] = jnp.zeros_like(dci_ref)

        uv = u_ref[...]
        dy = dz_ref[...] * _gelu_grad(y_ref[...])
        dyb = dy.astype(BF16)
        lam_re[...] = jnp.dot(dyb, cre_ref[...], preferred_element_type=F32)
        lam_im[...] = -jnp.dot(dyb, cim_ref[...], preferred_element_type=F32)
        keep = jnp.where(first_chunk, 0.0, 1.0)
        xe_re[pl.ds(0, 8), :] = xpr_ref[...] * keep
        xe_im[pl.ds(0, 8), :] = xpi_ref[...] * keep
        xe_re[pl.ds(8, tc), :] = xr_ref[...]
        xe_im[pl.ds(8, tc), :] = xi_ref[...]
        row = lax.broadcasted_iota(jnp.int32, (8, W), 0)

        def tile(n, carry):
            c_re, c_im, s_ar, s_ai = carry
            i = n_tiles - 1 - n
            rows = pl.ds(pl.multiple_of(i * 8, 8), 8)
            r = lam_re[rows, :]
            m = lam_im[rows, :]
            for q, d in enumerate((1, 2, 4)):
                pr, pm = _cmul(sh_re[q], sh_im[q], pltpu.roll(r, 8 - d, 0), pltpu.roll(m, 8 - d, 0))
                r = r + pr
                m = m + pm
            pr, pm = _cmul(pw_re[...], pw_im[...], c_re, c_im)
            r = r + pr
            m = m + pm
            lam_re[rows, :] = r
            lam_im[rows, :] = m
            cur_re = xe_re[pl.ds(pl.multiple_of(i * 8 + 8, 8), 8), :]
            cur_im = xe_im[pl.ds(pl.multiple_of(i * 8 + 8, 8), 8), :]
            bef_re = xe_re[rows, :]
            bef_im = xe_im[rows, :]
            xp_re = jnp.where(row == 0, jnp.broadcast_to(bef_re[7:8, :], (8, W)), pltpu.roll(cur_re, 1, 0))
            xp_im = jnp.where(row == 0, jnp.broadcast_to(bef_im[7:8, :], (8, W)), pltpu.roll(cur_im, 1, 0))
            s_ar = s_ar + r * xp_re + m * xp_im
            s_ai = s_ai + m * xp_re - r * xp_im
            return jnp.broadcast_to(r[0:1, :], (8, W)), jnp.broadcast_to(m[0:1, :], (8, W)), s_ar, s_ai

        c_re, c_im, s_ar, s_ai = lax.fori_loop(
            0, n_tiles, tile, (carry_re[...], carry_im[...], acc_ar[...], acc_ai[...]))
        carry_re[...] = c_re
        carry_im[...] = c_im
        acc_ar[...] = s_ar
        acc_ai[...] = s_ai
        lr = lam_re[...].astype(BF16)
        li = lam_im[...].astype(BF16)
        du_ref[...] = (dy * d_ref[...] + jnp.dot(lr, btr_ref[...], preferred_element_type=F32)
                       + jnp.dot(li, bti_ref[...], preferred_element_type=F32))
        dd_ref[...] += jnp.sum(dy * uv, axis=0, keepdims=True)
        tn_dims = (((0,), (0,)), ((), ()))
        ub = uv.astype(BF16)
        dbr_ref[...] += lax.dot_general(ub, lr, tn_dims, preferred_element_type=F32)
        dbi_ref[...] += lax.dot_general(ub, li, tn_dims, preferred_element_type=F32)
        dcr_ref[...] += lax.dot_general(dyb, xr_ref[...].astype(BF16), tn_dims, preferred_element_type=F32)
        dci_ref[...] -= lax.dot_general(dyb, xi_ref[...].astype(BF16), tn_dims, preferred_element_type=F32)

        @pl.when(first_chunk)
        def _():
            dar_ref[...] = jnp.sum(acc_ar[...], axis=0, keepdims=True)
            dai_ref[...] = jnp.sum(acc_ai[...], axis=0, keepdims=True)

    rev = lambda k: n_chunks - 1 - k
    ch = pl.BlockSpec((tc, CH_TILE), lambda s, k: (rev(k), s))
    st = pl.BlockSpec((tc, W), lambda s, k: (rev(k), s))
    prev = pl.BlockSpec((8, W), lambda s, k: (jnp.maximum(rev(k) * tiles_per_chunk - 1, 0), s))
    mat_cw = pl.BlockSpec((None, CH_TILE, W), lambda s, k: (s, 0, 0))
    mat_wc = pl.BlockSpec((None, W, CH_TILE), lambda s, k: (s, 0, 0))
    vec_w = pl.BlockSpec((1, W), lambda s, k: (0, s))
    vec_c = pl.BlockSpec((1, CH_TILE), lambda s, k: (0, s))
    dense = jax.ShapeDtypeStruct((n_st, CH_TILE, W), F32)
    return _grid_call(
        name, body, (n_st, n_chunks),
        in_specs=[ch, ch, ch, st, st, prev, prev, mat_cw, mat_cw, mat_wc, mat_wc, vec_w, vec_w, vec_c],
        out_specs=[ch, vec_c, mat_cw, mat_cw, mat_cw, mat_cw, vec_w, vec_w],
        out_shape=[jax.ShapeDtypeStruct((T, D), F32), jax.ShapeDtypeStruct((1, D), F32), dense, dense, dense, dense,
                   jax.ShapeDtypeStruct((1, 4 * D), F32), jax.ShapeDtypeStruct((1, 4 * D), F32)],
        scratch_shapes=[pltpu.VMEM((tc, W), F32), pltpu.VMEM((tc, W), F32),
                        pltpu.VMEM((tc + 8, W), F32), pltpu.VMEM((tc + 8, W), F32),
                        pltpu.VMEM((8, W), F32), pltpu.VMEM((8, W), F32),
                        pltpu.VMEM((8, W), F32), pltpu.VMEM((8, W), F32),
                        pltpu.VMEM((3, 8, W), F32), pltpu.VMEM((3, 8, W), F32),
                        pltpu.VMEM((8, W), F32), pltpu.VMEM((8, W), F32)],
        args=(dz, y, u, x_re, x_im, x_re, x_im, c_re, c_im, bt_re, bt_im, abar_re, abar_im, d_skip),
        comm=comm, comm_args=comm_args)


def _sums_matrix(strictly_later, copies):
    jj = lax.broadcasted_iota(jnp.int32, (copies * KEY_BLOCK, 2 * KEY_BLOCK), 0) & (KEY_BLOCK - 1)
    ss = lax.broadcasted_iota(jnp.int32, (copies * KEY_BLOCK, 2 * KEY_BLOCK), 1)
    tri = (jj > ss) if strictly_later else (jj < ss)
    return (tri | (ss >= KEY_BLOCK)).astype(BF16)


def _split_heads(blk):
    first = lax.broadcasted_iota(jnp.int32, blk.shape, 1) < HEAD_DIM
    zero = jnp.zeros_like(blk)
    return jnp.concatenate([jnp.where(first, blk, zero), jnp.where(first, zero, blk)], axis=0)


LOG2_E = 1.4426950408889634


def _sb_scores(z, mask, later):
    z2 = z * LOG2_E
    minus_abs = lax.bitcast_convert_type(lax.bitcast_convert_type(z2, jnp.uint32) | jnp.uint32(0x80000000), F32)
    lb = jnp.minimum(z2, 0.0) - jnp.log2(1.0 + jnp.exp2(minus_abs))
    lm = lb - z2
    if mask is not None:
        lm = jnp.where(mask, lm, 0.0)
    hi = lm.astype(BF16)
    lo = (lm - hi.astype(F32)).astype(BF16)
    return lb, jnp.dot(jnp.concatenate([hi, lo], axis=1), later, preferred_element_type=F32)


def _key_rows(kb):
    return pl.ds(pl.multiple_of(kb * KEY_BLOCK, KEY_BLOCK), KEY_BLOCK)


def _sb_mask(q_row0, k_row0, tq):
    tpos = q_row0 + lax.broadcasted_iota(jnp.int32, (tq, KEY_BLOCK), 0)
    spos = k_row0 + lax.broadcasted_iota(jnp.int32, (tq, KEY_BLOCK), 1)
    return (spos < tpos) & (spos >= META_START)


def _key_block_phases(iq, per_q, block, ascending):
    first_diag = iq * per_q

    def run(lo, n, masked):
        def step(i, carry):
            block(lo + i if ascending else lo + n - 1 - i, masked)
            return carry
        lax.fori_loop(0, n, step, 0)

    phases = [(0, jnp.minimum(iq, 1), True), (1, jnp.maximum(first_diag - 1, 0), False), (first_diag, per_q, True)]
    for lo, n, masked in (phases if ascending else phases[::-1]):
        run(lo, n, masked)


def _attn_fwd(name, q, kv):
    T, D = q.shape
    n_hp = D // 128
    tq = _row_tile(T, 512)
    per_q = tq // KEY_BLOCK
    scale = 1.0 / math.sqrt(HEAD_DIM)

    def body(q_ref, k_ref, v_ref, o_ref, l_ref, z_buf, w_buf, acc_ref, run_ref):
        iq = pl.program_id(1)
        n_kb = (iq + 1) * per_q
        qs = q_ref[...] * jnp.asarray(scale, BF16)
        later = _sums_matrix(True, 2)

        def scores(kb):
            return lax.dot_general(qs, _split_heads(k_ref[_key_rows(kb), :]), NT_DIMS, preferred_element_type=F32)

        def weighted_values(kb):
            return jnp.dot(w_buf[...], _split_heads(v_ref[_key_rows(kb), :]), preferred_element_type=F32)

        acc_ref[...] = jnp.zeros_like(acc_ref)
        run_ref[...] = jnp.zeros_like(run_ref)
        w_buf[...] = jnp.zeros_like(w_buf)
        z_buf[...] = scores(n_kb - 1)

        def block(kb, masked):
            acc_ref[...] += weighted_values(jnp.minimum(kb + 1, n_kb - 1))
            z_next = scores(jnp.maximum(kb - 1, 0))
            mask = _sb_mask(iq * tq, kb * KEY_BLOCK, tq) if masked else None
            heads = [_sb_scores(z_buf[:, h * KEY_BLOCK:(h + 1) * KEY_BLOCK], mask, later) for h in range(2)]
            for h, (lb, sums) in enumerate(heads):
                run = run_ref[h]
                w = jnp.exp2(lb + sums[:, :KEY_BLOCK] + run)
                if masked:
                    w = jnp.where(mask, w, 0.0)
                w_buf[:, h * KEY_BLOCK:(h + 1) * KEY_BLOCK] = w.astype(BF16)
                run_ref[h] = run + sums[:, KEY_BLOCK:]
            z_buf[...] = z_next

        _key_block_phases(iq, per_q, block, ascending=False)
        acc_ref[...] += weighted_values(0)
        lane = lax.broadcasted_iota(jnp.int32, (tq, 128), 1)
        o_ref[...] = acc_ref[...].astype(o_ref.dtype)
        l_ref[...] = jnp.where(lane < HEAD_DIM, run_ref[0], run_ref[1])

    blk = pl.BlockSpec((tq, 128), lambda h, i: (i, h))
    return pl.pallas_call(
        body, name=name, grid=(n_hp, T // tq),
        in_specs=[blk, pl.BlockSpec((T, 128), lambda h, i: (0, h)), pl.BlockSpec((T, 128), lambda h, i: (0, n_hp + h))],
        out_specs=[blk, blk],
        out_shape=[jax.ShapeDtypeStruct((T, D), BF16), jax.ShapeDtypeStruct((T, D), F32)],
        scratch_shapes=[pltpu.VMEM((tq, 2 * KEY_BLOCK), F32), pltpu.VMEM((tq, 2 * KEY_BLOCK), BF16),
                        pltpu.VMEM((tq, 128), F32), pltpu.VMEM((2, tq, 128), F32)],
        compiler_params=_cparams("parallel", "arbitrary"),
    )(q, kv, kv)


def _attn_bwd(name, q, kv, do, ltot):
    T, D = q.shape
    n_hp = D // 128
    tq = _row_tile(T, 512)
    n_q = T // tq
    per_q = tq // KEY_BLOCK
    scale = 1.0 / math.sqrt(HEAD_DIM)

    def body(q_ref, k_ref, v_ref, do_ref, l_ref, dq_ref, dk_ref, dv_ref,
             dk_acc, dv_acc, dq_acc, lpre_ref, cpre_ref, z_buf, dw_buf, dz_buf, dz_wide, w_buf):
        iq = pl.program_id(1)
        n_kb = (iq + 1) * per_q

        @pl.when(iq == 0)
        def _():
            dk_acc[...] = jnp.zeros_like(dk_acc)
            dv_acc[...] = jnp.zeros_like(dv_acc)

        first = lax.broadcasted_iota(jnp.int32, (tq, 128), 1) < HEAD_DIM
        qs = q_ref[...] * jnp.asarray(scale, BF16)
        dov = do_ref[...]
        ltv = l_ref[...]
        swapped = pltpu.roll(ltv, HEAD_DIM, 1)
        ltot = [jnp.where(first, ltv, swapped), jnp.where(first, swapped, ltv)]
        zero = jnp.zeros_like(qs)
        q_stack = jnp.concatenate([jnp.where(first, qs, zero), jnp.where(first, zero, qs)], axis=0)
        do_stack = jnp.concatenate([jnp.where(first, dov, zero), jnp.where(first, zero, dov)], axis=0)
        later = _sums_matrix(True, 2)
        earlier = _sums_matrix(False, 1)

        def scores(kb):
            rows = _key_rows(kb)
            return (lax.dot_general(qs, _split_heads(k_ref[rows, :]), NT_DIMS, preferred_element_type=F32),
                    lax.dot_general(dov, _split_heads(v_ref[rows, :]), NT_DIMS, preferred_element_type=F32))

        def flush(kb):
            rows = _key_rows(kb)
            k_heads = _split_heads(k_ref[rows, :])
            dq_acc[...] += jnp.dot(dz_wide[...], k_heads, preferred_element_type=F32)
            dk_acc[rows, :] += lax.dot_general(dz_buf[...], q_stack, TN_DIMS, preferred_element_type=F32)
            dv_acc[rows, :] += lax.dot_general(w_buf[...], do_stack, TN_DIMS, preferred_element_type=F32)

        dq_acc[...] = jnp.zeros_like(dq_acc)
        lpre_ref[...] = jnp.zeros_like(lpre_ref)
        cpre_ref[...] = jnp.zeros_like(cpre_ref)
        dz_buf[...] = jnp.zeros_like(dz_buf)
        dz_wide[...] = jnp.zeros_like(dz_wide)
        w_buf[...] = jnp.zeros_like(w_buf)
        z_buf[...], dw_buf[...] = scores(0)

        def block(kb, masked):
            flush(jnp.maximum(kb - 1, 0))
            z_next, dw_next = scores(jnp.minimum(kb + 1, n_kb - 1))
            mask = _sb_mask(iq * tq, kb * KEY_BLOCK, tq) if masked else None
            heads = [_sb_scores(z_buf[:, h * KEY_BLOCK:(h + 1) * KEY_BLOCK], mask, later) for h in range(2)]
            grads = []
            for h, (lb, sums) in enumerate(heads):
                after = ltot[h] - lpre_ref[h] - sums[:, KEY_BLOCK:]
                w = jnp.exp2(lb + sums[:, :KEY_BLOCK] + after)
                if masked:
                    w = jnp.where(mask, w, 0.0)
                da = w * dw_buf[:, h * KEY_BLOCK:(h + 1) * KEY_BLOCK]
                w_buf[h * tq:(h + 1) * tq, :] = w.astype(BF16)
                lpre_ref[h] += sums[:, KEY_BLOCK:]
                grads.append((da, jnp.dot(da.astype(BF16), earlier, preferred_element_type=F32)))
            for h, (da, dsums) in enumerate(grads):
                sig = jnp.exp2(heads[h][0])
                through_later = sig * (dsums[:, :KEY_BLOCK] + cpre_ref[h])
                if masked:
                    through_later = jnp.where(mask, through_later, 0.0)
                dz = (da * (1.0 - sig) - through_later).astype(BF16)
                dz_buf[h * tq:(h + 1) * tq, :] = dz
                dz_wide[:, h * KEY_BLOCK:(h + 1) * KEY_BLOCK] = dz
                cpre_ref[h] += dsums[:, KEY_BLOCK:]
            z_buf[...] = z_next
            dw_buf[...] = dw_next

        _key_block_phases(iq, per_q, block, ascending=True)
        flush(n_kb - 1)
        dq_ref[...] = (dq_acc[...] * scale).astype(dq_ref.dtype)

        @pl.when(iq == n_q - 1)
        def _():
            dk_ref[...] = dk_acc[...].astype(dk_ref.dtype)
            dv_ref[...] = dv_acc[...].astype(dv_ref.dtype)

    blk = pl.BlockSpec((tq, 128), lambda h, i: (i, h))
    full = pl.BlockSpec((T, 128), lambda h, i: (0, h))
    return pl.pallas_call(
        body, name=name, grid=(n_hp, n_q),
        in_specs=[blk, full, pl.BlockSpec((T, 128), lambda h, i: (0, n_hp + h)), blk, blk],
        out_specs=[blk, full, full],
        out_shape=[jax.ShapeDtypeStruct((T, D), BF16)] * 3,
        scratch_shapes=[pltpu.VMEM((T, 128), F32), pltpu.VMEM((T, 128), F32), pltpu.VMEM((tq, 128), F32),
                        pltpu.VMEM((2, tq, 128), F32), pltpu.VMEM((2, tq, 128), F32),
                        pltpu.VMEM((tq, 2 * KEY_BLOCK), F32), pltpu.VMEM((tq, 2 * KEY_BLOCK), F32),
                        pltpu.VMEM((2 * tq, 128), BF16), pltpu.VMEM((tq, 2 * KEY_BLOCK), BF16),
                        pltpu.VMEM((2 * tq, 128), BF16)],
        compiler_params=_cparams("parallel", "arbitrary"),
    )(q, kv, kv, do, ltot)


def _adamw(name, w, g, m, v):
    shape = w.shape
    size = w.size
    if w.ndim >= 2 and shape[-1] % 128 == 0:
        cols = shape[-1]
    else:
        cols = 1024 if size % 1024 == 0 else shape[-1]
    rows = size // cols
    tm = _row_tile(rows, max(8, (1024 * 1024) // (4 * cols) // 8 * 8)) if rows % 8 == 0 else rows
    c1 = 1.0 / (1.0 - ADAM_B1 ** ADAM_STEP)
    c2 = 1.0 / (1.0 - ADAM_B2 ** ADAM_STEP)

    def body(w_ref, g_ref, m_ref, v_ref, d_ref, nm_ref, nv_ref):
        gv = g_ref[...]
        nm = ADAM_B1 * m_ref[...] + (1.0 - ADAM_B1) * gv
        nv = ADAM_B2 * v_ref[...] + (1.0 - ADAM_B2) * (gv * gv)
        d_ref[...] = -ADAM_LR * ((nm * c1) / (jnp.sqrt(nv * c2) + ADAM_EPS) + ADAM_WD * w_ref[...])
        nm_ref[...] = nm
        nv_ref[...] = nv

    blk = pl.BlockSpec((tm, cols), lambda i: (i, 0))
    outs = pl.pallas_call(
        body, name=name, grid=(rows // tm,),
        in_specs=[blk] * 4, out_specs=[blk] * 3,
        out_shape=[jax.ShapeDtypeStruct((rows, cols), F32)] * 3,
        compiler_params=_cparams("parallel"),
    )(*[t.reshape(rows, cols) for t in (w, g, m, v)])
    return tuple(o.reshape(shape) for o in outs)


def _any_specs(n):
    return [pl.BlockSpec(memory_space=pl.ANY)] * n


def _chip_index():
    return 2 * lax.axis_index("x") + lax.axis_index("y")


def _place():
    x, y, c = lax.axis_index("x"), lax.axis_index("y"), lax.axis_index("c")
    chips = [(1 - x, y), (x, 1 - y), (1 - x, 1 - y)]
    return x, y, c, chips


def _all_gather_chips(name, shards):
    n = len(shards)

    def body(*refs):
        x_refs, o_refs = refs[:n], refs[n:2 * n]
        send_sems, recv_sems = refs[2 * n:]
        x, y, c, chips = _place()
        me = 2 * x + y
        sibling = (x, y, 1 - c)

        def half(ref, i, which):
            h = shards[i].shape[0] // 2
            return ref.at[pl.ds(which * h, h)]

        def remote(k, i, src, dst, to):
            return pltpu.make_async_remote_copy(src_ref=src, dst_ref=dst, send_sem=send_sems.at[k, i],
                                                recv_sem=recv_sems.at[k, i], device_id=to, device_id_type=MESH)

        sent = []
        for j, chip in enumerate(chips):
            for i in range(n):
                cp = remote(j, i, half(x_refs[i], i, c), half(o_refs[i].at[me], i, c), (*chip, c))
                cp.start()
                sent.append(cp)
        for j, chip in enumerate(chips):
            pj = 2 * chip[0] + chip[1]
            for i in range(n):
                landed = half(o_refs[i].at[pj], i, c)
                remote(j, i, landed, landed, (*chip, c)).wait_recv()
                cp = remote(3 + j, i, landed, landed, sibling)
                cp.start()
                sent.append(cp)
        for j, chip in enumerate(chips):
            pj = 2 * chip[0] + chip[1]
            for i in range(n):
                got = half(o_refs[i].at[pj], i, 1 - c)
                remote(3 + j, i, got, got, sibling).wait_recv()
        for cp in sent:
            cp.wait_send()

    outs = pl.pallas_call(
        body, name=name,
        in_specs=_any_specs(n), out_specs=_any_specs(n),
        out_shape=[jax.ShapeDtypeStruct((4,) + s.shape, s.dtype) for s in shards],
        scratch_shapes=[pltpu.SemaphoreType.DMA((6, n)), pltpu.SemaphoreType.DMA((6, n))],
    )(*shards)
    return [lax.dynamic_update_slice(o, s[None], (_chip_index(), 0, 0)) for o, s in zip(outs, shards)]


def _pair_split(name, grads):
    n = len(grads)

    def body(*refs):
        g_refs, got_refs = refs[:n], refs[n:2 * n]
        send_sems, recv_sems = refs[2 * n:]
        x, y, c, _ = _place()
        sibling = (x, y, 1 - c)
        sent = []
        for i in range(n):
            h = grads[i].shape[1] // 2
            rc = pltpu.make_async_remote_copy(
                src_ref=g_refs[i].at[:, pl.ds((1 - c) * h, h)], dst_ref=got_refs[i],
                send_sem=send_sems.at[i], recv_sem=recv_sems.at[i], device_id=sibling, device_id_type=MESH)
            rc.start()
            sent.append(rc)
        for rc in sent:
            rc.wait()

    return pl.pallas_call(
        body, name=name,
        in_specs=_any_specs(n), out_specs=_any_specs(n),
        out_shape=[jax.ShapeDtypeStruct((4, g.shape[1] // 2, g.shape[2]), g.dtype) for g in grads],
        scratch_shapes=[pltpu.SemaphoreType.DMA((n,)), pltpu.SemaphoreType.DMA((n,))],
    )(*grads)


def _chip_exchange(name, sums):
    return _exchange_finish(sums, _run_comm(name, _exchange_comm(sums), sums))


class _Comm:
    def __init__(self, out_shape, copies):
        self.n = len(out_shape)
        self.out_shape = out_shape
        self.copies = copies
        self.scratch = [pltpu.SemaphoreType.DMA((3, self.n)), pltpu.SemaphoreType.DMA((3, self.n))]

    def start(self, *refs):
        for cp in self.copies(*refs, False):
            cp.start()

    def finish(self, *refs):
        for cp in self.copies(*refs, True):
            cp.wait_recv()
        for cp in self.copies(*refs, False):
            cp.wait_send()


def _exchange_comm(sums):
    n = len(sums)

    def copies(s_refs, o_refs, send_sems, recv_sems, mirrors):
        x, y, c, chips = _place()
        me = 2 * x + y
        out = []
        for j, chip in enumerate(chips):
            pj = 2 * chip[0] + chip[1]
            for i in range(n):
                out.append(pltpu.make_async_remote_copy(
                    src_ref=s_refs[i].at[pj], dst_ref=o_refs[i].at[pj if mirrors else me], send_sem=send_sems.at[j, i],
                    recv_sem=recv_sems.at[j, i], device_id=(*chip, c), device_id_type=MESH))
        return out

    return _Comm([jax.ShapeDtypeStruct(s.shape, s.dtype) for s in sums], copies)


def _exchange_finish(sums, outs):
    me = _chip_index()
    return [lax.dynamic_update_slice(o, lax.dynamic_index_in_dim(s, me, 0, keepdims=True), (me, 0, 0))
            for o, s in zip(outs, sums)]


def _halves_gather_comm(shards):
    n = len(shards)

    def copies(x_refs, o_refs, send_sems, recv_sems, mirrors):
        x, y, c, chips = _place()
        me = 2 * x + y
        out = []
        for j, chip in enumerate(chips):
            pj = 2 * chip[0] + chip[1]
            for i in range(n):
                h = shards[i].shape[0] // 2
                rows = pl.ds(c * h, h)
                out.append(pltpu.make_async_remote_copy(
                    src_ref=x_refs[i].at[rows], dst_ref=o_refs[i].at[pj if mirrors else me, rows],
                    send_sem=send_sems.at[j, i], recv_sem=recv_sems.at[j, i], device_id=(*chip, c), device_id_type=MESH))
        return out

    return _Comm([jax.ShapeDtypeStruct((4,) + s.shape, s.dtype) for s in shards], copies)


def _run_comm(name, comm, arrays):
    n = comm.n

    def body(*refs):
        comm.start(refs[:n], refs[n:2 * n], *refs[2 * n:])
        comm.finish(refs[:n], refs[n:2 * n], *refs[2 * n:])

    return pl.pallas_call(
        body, name=name, in_specs=_any_specs(n), out_specs=_any_specs(n),
        out_shape=comm.out_shape, scratch_shapes=comm.scratch,
    )(*arrays)


def _embed_comm(comm, body, grid, n_in, n_out):
    n = comm.n

    def wrapped(*refs):
        ins, c_in = refs[:n_in], refs[n_in:n_in + n]
        outs, c_out = refs[n_in + n:n_in + n + n_out], refs[n_in + n + n_out:n_in + 2 * n + n_out]
        scratch, sems = refs[n_in + 2 * n + n_out:-2], refs[-2:]
        ids = [pl.program_id(a) for a in range(len(grid))]
        first = functools.reduce(jnp.logical_and, [i == 0 for i in ids])
        last = functools.reduce(jnp.logical_and, [i == g - 1 for i, g in zip(ids, grid)])

        @pl.when(first)
        def _():
            comm.start(c_in, c_out, *sems)

        body(*ins, *outs, *scratch)

        @pl.when(last)
        def _():
            comm.finish(c_in, c_out, *sems)

    return wrapped


def _pair_forward(name, landed, shards):
    n = len(landed)

    def body(*refs):
        o_refs = refs[n:2 * n]
        send_sems, recv_sems = refs[2 * n:]
        x, y, c, chips = _place()
        sibling = (x, y, 1 - c)
        sent, arriving = [], []
        for j, chip in enumerate(chips):
            pj = 2 * chip[0] + chip[1]
            for i in range(n):
                h = shards[i].shape[0] // 2
                mine = o_refs[i].at[pj, pl.ds(c * h, h)]
                theirs = o_refs[i].at[pj, pl.ds((1 - c) * h, h)]
                for ref, group in ((mine, sent), (theirs, arriving)):
                    group.append(pltpu.make_async_remote_copy(
                        src_ref=ref, dst_ref=ref, send_sem=send_sems.at[j, i], recv_sem=recv_sems.at[j, i],
                        device_id=sibling, device_id_type=MESH))
        for cp in sent:
            cp.start()
        for cp in arriving:
            cp.wait_recv()
        for cp in sent:
            cp.wait_send()

    outs = pl.pallas_call(
        body, name=name, in_specs=_any_specs(n), out_specs=_any_specs(n),
        out_shape=[jax.ShapeDtypeStruct(a.shape, a.dtype) for a in landed],
        input_output_aliases={i: i for i in range(n)},
        scratch_shapes=[pltpu.SemaphoreType.DMA((3, n)), pltpu.SemaphoreType.DMA((3, n))],
    )(*landed)
    return [lax.dynamic_update_slice(o, s[None], (_chip_index(), 0, 0)) for o, s in zip(outs, shards)]


def _pair_join(name, halves):
    n = len(halves)

    def body(*refs):
        h_refs, o_refs = refs[:n], refs[n:2 * n]
        send_sems, recv_sems = refs[2 * n:]
        x, y, c, _ = _place()
        sibling = (x, y, 1 - c)
        sent = []
        for i in range(n):
            h = halves[i].shape[0]
            mine = o_refs[i].at[pl.ds(c * h, h)]
            rc = pltpu.make_async_remote_copy(
                src_ref=h_refs[i], dst_ref=mine, send_sem=send_sems.at[i], recv_sem=recv_sems.at[i],
                device_id=sibling, device_id_type=MESH)
            rc.start()
            sent.append(rc)
        for i in range(n):
            h = halves[i].shape[0]
            theirs = o_refs[i].at[pl.ds((1 - c) * h, h)]
            pltpu.make_async_remote_copy(
                src_ref=h_refs[i], dst_ref=theirs, send_sem=send_sems.at[i], recv_sem=recv_sems.at[i],
                device_id=sibling, device_id_type=MESH).wait_recv()
        for rc in sent:
            rc.wait_send()

    outs = pl.pallas_call(
        body, name=name,
        in_specs=_any_specs(n), out_specs=_any_specs(n),
        out_shape=[jax.ShapeDtypeStruct((2 * s.shape[0], s.shape[1]), s.dtype) for s in halves],
        scratch_shapes=[pltpu.SemaphoreType.DMA((n,)), pltpu.SemaphoreType.DMA((n,))],
    )(*halves)
    c = lax.axis_index("c")
    return [lax.dynamic_update_slice(o, s, (c * s.shape[0], 0)) for o, s in zip(outs, halves)]


def _add_pair(name, a, b):
    _, H, C = a.shape
    th = _row_tile(H, max(8, (1024 * 1024) // (4 * C) // 8 * 8))

    def body(a_ref, b_ref, o_ref):
        o_ref[...] = (a_ref[...].astype(F32) + b_ref[...].astype(F32)).astype(o_ref.dtype)

    blk = pl.BlockSpec((None, th, C), lambda q, i: (q, i, 0))
    return pl.pallas_call(
        body, name=name, grid=(4, H // th), in_specs=[blk, blk], out_specs=blk,
        out_shape=jax.ShapeDtypeStruct(a.shape, a.dtype), compiler_params=_cparams("parallel", "parallel"),
    )(a, b)


def _add_chips(name, parts):
    _, H, C = parts.shape
    th = _row_tile(H, max(8, (1024 * 1024) // (4 * C) // 8 * 8))

    def body(p_ref, o_ref):
        acc = p_ref[0].astype(F32)
        for q in range(1, 4):
            acc = acc + p_ref[q].astype(F32)
        o_ref[...] = acc

    return pl.pallas_call(
        body, name=name, grid=(H // th,),
        in_specs=[pl.BlockSpec((4, th, C), lambda i: (0, i, 0))], out_specs=pl.BlockSpec((th, C), lambda i: (i, 0)),
        out_shape=jax.ShapeDtypeStruct((H, C), F32), compiler_params=_cparams("parallel"),
    )(parts)


def _pair_sums(tag, grads):
    c = lax.axis_index("c")
    got = _pair_split(f"rs_pair_split_{tag}", grads)
    kept = [lax.dynamic_slice_in_dim(g, c * (g.shape[1] // 2), g.shape[1] // 2, axis=1) for g in grads]
    return [_add_pair(f"rs_add_pair_{tag}_{i}", k, g) for i, (k, g) in enumerate(zip(kept, got))]


def _chip_sums(parts):
    return _pair_join("rs_pair_join", [_add_chips(f"rs_add_chips_{i}", p) for i, p in enumerate(parts)])


def _block_diag(t):
    G, A, B = t.shape
    eye = jnp.eye(8, dtype=t.dtype)
    return jnp.einsum("sgab,gh->sgahb", t.reshape(G // 8, 8, A, B), eye).reshape(G // 8, 8 * A, 8 * B)


def _block_diag_extract(m, A, B):
    n = m.shape[0]
    eye = jnp.eye(8, dtype=m.dtype)
    return jnp.einsum("sgahb,gh->sgab", m.reshape(n, 8, A, 8, B), eye).reshape(8 * n, A, B)


def kernel(x, meta_tokens, norm_mix, norm_ffn, s5_a_re, s5_a_im, s5_log_dt, s5_b_re, s5_b_im, s5_c_re, s5_c_im, s5_d, s5_w_glu, norm_kv, w_kv, w_q, w_o, w_ffn_in, w_ffn_out, norm_final, loss_target, m_meta_tokens, m_norm_mix, m_norm_ffn, m_s5_a_re, m_s5_a_im, m_s5_log_dt, m_s5_b_re, m_s5_b_im, m_s5_c_re, m_s5_c_im, m_s5_d, m_s5_w_glu, m_norm_kv, m_w_kv, m_w_q, m_w_o, m_w_ffn_in, m_w_ffn_out, m_norm_final, v_meta_tokens, v_norm_mix, v_norm_ffn, v_s5_a_re, v_s5_a_im, v_s5_log_dt, v_s5_b_re, v_s5_b_im, v_s5_c_re, v_s5_c_im, v_s5_d, v_s5_w_glu, v_norm_kv, v_w_kv, v_w_q, v_w_o, v_w_ffn_in, v_w_ffn_out, v_norm_final):
    seq, D = x.shape[1], x.shape[2]
    T = X_START + seq
    G, P, C = s5_a_re.shape[1], S5_STATE, S5_GROUP
    d_ff = w_ffn_out.shape[1] * 4
    dq4 = D // 4
    chip = 2 * lax.axis_index("x") + lax.axis_index("y")

    small_in = jnp.concatenate([meta_tokens, jnp.pad(s5_d, ((0, 15), (0, 0)))], axis=0)
    (small_all,) = _all_gather_chips("ag_small", [small_in])
    meta_full = small_all[:, :N_META, :].transpose(1, 0, 2).reshape(N_META, D)
    d_skip = small_all[:, N_META, :].reshape(1, D)
    shards = [s.astype(BF16) for s in (s5_w_glu[0], w_kv, w_q[0], w_o[0],
                                       w_ffn_in.reshape(2 * D, -1), w_ffn_out.reshape(-1, D))]
    rows_out = d_ff // 4

    row = lambda v: v.reshape(1, -1)
    g_mix0, g_mix1 = row(norm_mix[0]), row(norm_mix[1])
    g_ffn = [row(norm_ffn[0]), row(norm_ffn[1])]
    g_kv, g_final = row(norm_kv), row(norm_final)

    a_re3 = s5_a_re[0].reshape(G, 1, P)
    a_im3 = s5_a_im[0].reshape(G, 1, P)
    log_dt3 = s5_log_dt[0].reshape(G, 1, 1)
    bt_re = s5_b_re[0].transpose(0, 2, 1)
    bt_im = s5_b_im[0].transpose(0, 2, 1)
    ab_re, ab_im, bb_re, bb_im = _s5_prep_fwd("s5_prep", a_re3, a_im3, log_dt3, bt_re, bt_im)
    abar_re, abar_im = ab_re.reshape(1, G * P), ab_im.reshape(1, G * P)
    bd_b_re = _block_diag(bb_re).astype(BF16)
    bd_b_im = _block_diag(bb_im).astype(BF16)
    bd_bt_re = bd_b_re.transpose(0, 2, 1)
    bd_bt_im = bd_b_im.transpose(0, 2, 1)
    bd_c_re = _block_diag(s5_c_re[0]).astype(BF16)
    bd_c_im = _block_diag(s5_c_im[0]).astype(BF16)
    bd_ct_re = bd_c_re.transpose(0, 2, 1)
    bd_ct_im = bd_c_im.transpose(0, 2, 1)

    h0 = jnp.concatenate([jnp.zeros((META_START, D), F32), meta_full, x[0]], axis=0)
    (u,) = _rmsnorm_fwd("norm_mix0", h0, [g_mix0], [F32])
    (y, z, x_re, x_im), landed = _s5_fwd(
        "s5_scan", u, bd_b_re, bd_b_im, bd_ct_re, bd_ct_im, abar_re, abar_im, d_skip,
        comm=_halves_gather_comm(shards), comm_args=shards)
    wg_glu, wg_kv, wg_q, wg_o, wg_in, wg_out = _pair_forward("ag_forward", landed, shards)
    wg_q = wg_q.reshape(1, D, D)
    wg_o = wg_o.reshape(1, D, D)
    wg_out = [wg_out[:, l * rows_out:(l + 1) * rows_out, :].reshape(1, d_ff, D) for l in range(2)]
    val, gate, h1 = _mm_glu("glu_proj", z, wg_glu, h0)

    def ffn_fwd(l, h):
        (n,) = _rmsnorm_fwd(f"norm_ffn{l}", h, [g_ffn[l]], [BF16])
        g, u, mid = _mm_swiglu(f"ffn_in{l}", n, wg_in, l)
        return n, (g, u), mid, _mm_nn(f"ffn_out{l}", mid, wg_out[l], res=h)

    n1, gu0, mid0, h2 = ffn_fwd(0, h1)
    nk, nq = _rmsnorm_fwd("norm_kv_q", h2, [g_kv, g_mix1], [BF16, BF16])
    kv = _mm_nn("kv_proj", nk, wg_kv, out_dtype=BF16)
    q = _mm_nn("q_proj", nq, wg_q, out_dtype=BF16)
    o, ltot = _attn_fwd("attn_fwd", q, kv)
    h3 = _mm_nn("o_proj", o, wg_o, res=h2)
    n3, gu1, mid1, h4 = ffn_fwd(1, h3)
    loss_part, dh4, dg_final = _final_loss("final_loss", h4, g_final, loss_target[0])

    def ffn_bwd(l, dh, h, n, gu, mid):
        dgu = _mm_nt_swiglu_bwd(f"ffn_out{l}_dx", dh, wg_out[l], *gu)
        dw_out = _mm_tn(f"ffn_out{l}_dw", mid, dh, 1)
        dw_in = _mm_tn(f"ffn_in{l}_dw", n, dgu, 4)
        dn = _mm_nt_k(f"ffn_in{l}_dx", dgu, wg_in, D, k_blk=l)
        dh_prev, (dg,) = _rmsnorm_bwd(f"norm_ffn{l}_bwd", h, [(g_ffn[l], dn)], dh)
        return dh_prev, dg, dw_in, dw_out

    dh3, dg_ffn1, dw_in1, dw_out1 = ffn_bwd(1, dh4, h3, n3, gu1, mid1)
    d_o = _mm_nt_k("o_proj_dx", dh3, wg_o, D, out_dtype=BF16)
    dw_o = _mm_tn("o_proj_dw", o, dh3, 1)
    dq, dk, dv = _attn_bwd("attn_bwd", q, kv, d_o, ltot)
    dkv = jnp.concatenate([dk, dv], axis=1)
    dw_q = _mm_tn("q_proj_dw", nq, dq, 1)
    dnq = _mm_nt_k("q_proj_dx", dq, wg_q, D)
    dw_kv = _mm_tn("kv_proj_dw", nk, dkv, 4)
    dnk = _mm_nt_k("kv_proj_dx", dkv, wg_kv, D)
    dh2, (dg_mix1, dg_kv) = _rmsnorm_bwd("norm_kv_q_bwd", h2, [(g_mix1, dnq), (g_kv, dnk)], dh3)
    dh1, dg_ffn0, dw_in0, dw_out0 = ffn_bwd(0, dh2, h1, n1, gu0, mid0)
    dvg = _glu_bwd("glu_bwd", val, gate, dh1)
    dw_glu = _mm_tn("glu_proj_dw", z, dvg, 4)
    dz = _mm_nt_k("glu_proj_dx", dvg, wg_glu, D)
    big = [dw_kv, dw_q.reshape(4, D // 4, D), dw_o.reshape(4, D // 4, D), dw_in1, dw_out1.reshape(4, rows_out, D),
           dw_glu, dw_in0, dw_out0.reshape(4, rows_out, D)]
    big_pairs = _pair_sums("big", big)
    (du, dd, dbd_b_re, dbd_b_im, dbd_c_re, dbd_c_im, dab_re, dab_im), big_parts = _s5_bwd(
        "s5_scan_bwd", dz, y, u, x_re, x_im, bd_c_re, bd_c_im, bd_bt_re, bd_bt_im, abar_re, abar_im, d_skip,
        comm=_exchange_comm(big_pairs), comm_args=big_pairs)
    big_parts = _exchange_finish(big_pairs, big_parts)
    dh0, (dg_mix0,) = _rmsnorm_bwd("norm_mix0_bwd", h0, [(g_mix0, du)], dh1)
    da_re, da_im, dlog_dt, dbt_re, dbt_im = _s5_prep_bwd(
        "s5_prep_bwd", a_re3, a_im3, log_dt3, bt_re, bt_im,
        dab_re.reshape(G, 1, P), dab_im.reshape(G, 1, P),
        _block_diag_extract(dbd_b_re, C, P), _block_diag_extract(dbd_b_im, C, P))
    grad_x = dh0[X_START:][None]

    small_parts = [
        dg_mix0, dg_mix1, dg_ffn0, dg_ffn1, da_re, da_im,
        dbt_re.transpose(0, 2, 1), dbt_im.transpose(0, 2, 1),
        _block_diag_extract(dbd_c_re, C, P), _block_diag_extract(dbd_c_im, C, P),
        dg_kv, dg_final, dh0[META_START:X_START], dd, loss_part, dlog_dt]
    small_sizes = [p.size for p in small_parts]
    unit = 4 * 2 * 8 * 128
    padded = -(-sum(small_sizes) // unit) * unit
    tail = jnp.concatenate([loss_part.reshape(-1), dlog_dt.reshape(-1)])
    small_flat = jnp.concatenate(
        [p.reshape(-1) for p in small_parts[:-2]] + [jnp.pad(tail, (0, padded - sum(small_sizes)))])
    small_blocks = small_flat.reshape(4, padded // (4 * 128), 128)
    small_exchanged = _chip_exchange("rs_chip_exchange", _pair_sums("small", [small_blocks]))
    gw_kv, gw_q, gw_o, gw_in1, gw_out1, gw_glu, gw_in0, gw_out0, small_mine = _chip_sums(big_parts + small_exchanged)
    gw_in = jnp.concatenate([gw_in0, gw_in1], axis=0)
    gw_out = jnp.concatenate([gw_out0, gw_out1], axis=0)
    (small_red,) = _all_gather_chips("ag_small_grads", [small_mine])
    small_red = small_red.reshape(-1)
    pieces, at = [], 0
    for p, size in zip(small_parts, small_sizes):
        pieces.append(small_red[at:at + size].reshape(p.shape))
        at += size
    (gn_mix0, gn_mix1, gn_ffn0, gn_ffn1, ga_re, ga_im, gb_re, gb_im, gc_re, gc_im, gn_kv, gn_final,
     gmeta_full, gd_full, loss_all, glog_dt) = pieces
    loss = loss_all[0, 0]
    gn_mix = small_red[:2 * D].reshape(2, D)
    gn_ffn = small_red[2 * D:4 * D].reshape(2, D)
    gmeta = lax.dynamic_slice_in_dim(gmeta_full, chip * dq4, dq4, axis=1)
    gd = lax.dynamic_slice_in_dim(gd_full, chip * dq4, dq4, axis=1)

    grads = {
        "meta_tokens": gmeta, "norm_mix": gn_mix, "norm_ffn": gn_ffn,
        "s5_a_re": ga_re.reshape(s5_a_re.shape), "s5_a_im": ga_im.reshape(s5_a_im.shape),
        "s5_log_dt": glog_dt.reshape(s5_log_dt.shape),
        "s5_b_re": gb_re.reshape(s5_b_re.shape), "s5_b_im": gb_im.reshape(s5_b_im.shape),
        "s5_c_re": gc_re.reshape(s5_c_re.shape), "s5_c_im": gc_im.reshape(s5_c_im.shape),
        "s5_d": gd, "s5_w_glu": gw_glu.reshape(s5_w_glu.shape), "norm_kv": gn_kv.reshape(norm_kv.shape),
        "w_kv": gw_kv, "w_q": gw_q.reshape(w_q.shape), "w_o": gw_o.reshape(w_o.shape),
        "w_ffn_in": gw_in.reshape(w_ffn_in.shape), "w_ffn_out": gw_out.reshape(w_ffn_out.shape),
        "norm_final": gn_final.reshape(norm_final.shape),
    }
    weights = {
        "meta_tokens": (meta_tokens, m_meta_tokens, v_meta_tokens), "norm_mix": (norm_mix, m_norm_mix, v_norm_mix),
        "norm_ffn": (norm_ffn, m_norm_ffn, v_norm_ffn), "s5_a_re": (s5_a_re, m_s5_a_re, v_s5_a_re),
        "s5_a_im": (s5_a_im, m_s5_a_im, v_s5_a_im), "s5_log_dt": (s5_log_dt, m_s5_log_dt, v_s5_log_dt),
        "s5_b_re": (s5_b_re, m_s5_b_re, v_s5_b_re), "s5_b_im": (s5_b_im, m_s5_b_im, v_s5_b_im),
        "s5_c_re": (s5_c_re, m_s5_c_re, v_s5_c_re), "s5_c_im": (s5_c_im, m_s5_c_im, v_s5_c_im),
        "s5_d": (s5_d, m_s5_d, v_s5_d), "s5_w_glu": (s5_w_glu, m_s5_w_glu, v_s5_w_glu),
        "norm_kv": (norm_kv, m_norm_kv, v_norm_kv), "w_kv": (w_kv, m_w_kv, v_w_kv), "w_q": (w_q, m_w_q, v_w_q),
        "w_o": (w_o, m_w_o, v_w_o), "w_ffn_in": (w_ffn_in, m_w_ffn_in, v_w_ffn_in),
        "w_ffn_out": (w_ffn_out, m_w_ffn_out, v_w_ffn_out), "norm_final": (norm_final, m_norm_final, v_norm_final),
    }
    names = list(weights)
    deltas, new_m, new_v = [], [], []
    for name in names:
        w, m, v = weights[name]
        d, nm, nv = _adamw(f"adamw_{name}", w, grads[name], m, v)
        deltas.append(d)
        new_m.append(nm)
        new_v.append(nv)
    return (loss, grad_x, *[grads[n] for n in names], *deltas, *new_m, *new_v)
```

```python
import functools
import math

import jax
import jax.numpy as jnp
from jax import lax
from jax.experimental import pallas as pl
from jax.experimental.pallas import tpu as pltpu

F32 = jnp.float32
BF16 = jnp.bfloat16

N_META = 16
X_START = 128
META_START = X_START - N_META
S5_GROUP = 16
S5_STATE = 64
HEAD_DIM = 64
KEY_BLOCK = 128
STATE_TILE = 512
CH_TILE = 128
RMS_EPS = 1e-6
ADAM_LR, ADAM_B1, ADAM_B2, ADAM_EPS, ADAM_WD, ADAM_STEP = 0.001, 0.9, 0.999, 1e-08, 0.01, 10
VMEM_LIMIT_BYTES = 48 * 1024 * 1024
MESH = pl.DeviceIdType.MESH
NT_DIMS = (((1,), (1,)), ((), ()))
TN_DIMS = (((0,), (0,)), ((), ()))


def _cparams(*sem):
    return pltpu.CompilerParams(dimension_semantics=sem, vmem_limit_bytes=VMEM_LIMIT_BYTES)


def _row_tile(rows, cap):
    for unit in (128, 8):
        best = 0
        for t in range(unit, min(rows, cap) + 1, unit):
            if rows % t == 0:
                best = t
        if best:
            return best
    return rows


def _col_tile(cols, cap):
    best = 0
    for t in range(128, min(cols, cap) + 1, 128):
        if cols % t == 0:
            best = t
    return best if best else cols


def _gelu(x):
    k = math.sqrt(2.0 / math.pi)
    return 0.5 * x * (1.0 + jnp.tanh(k * (x + 0.044715 * x * x * x)))


def _gelu_grad(x):
    k = math.sqrt(2.0 / math.pi)
    t = jnp.tanh(k * (x + 0.044715 * x * x * x))
    return 0.5 * (1.0 + t) + 0.5 * x * (1.0 - t * t) * k * (1.0 + 3.0 * 0.044715 * x * x)


def _sigmoid(x):
    return 1.0 / (1.0 + jnp.exp(-x))


def _rmsnorm_fwd(name, x, gains, out_dtypes):
    T, D = x.shape
    tm = _row_tile(T, 512)
    n = len(gains)

    def body(x_ref, *refs):
        xv = x_ref[...]
        xh = xv * lax.rsqrt(jnp.mean(xv * xv, axis=-1, keepdims=True) + RMS_EPS)
        for g_ref, o_ref in zip(refs[:n], refs[n:]):
            o_ref[...] = (xh * g_ref[...]).astype(o_ref.dtype)

    row = pl.BlockSpec((tm, D), lambda i: (i, 0))
    vec = pl.BlockSpec((1, D), lambda i: (0, 0))
    return pl.pallas_call(
        body, name=name, grid=(T // tm,),
        in_specs=[row] + [vec] * n, out_specs=[row] * n,
        out_shape=[jax.ShapeDtypeStruct((T, D), dt) for dt in out_dtypes],
        compiler_params=_cparams("parallel"),
    )(x, *gains)


def _rmsnorm_bwd(name, x, pairs, dres):
    T, D = x.shape
    tm = _row_tile(T, 512)
    n = len(pairs)

    def body(x_ref, dres_ref, *refs):
        g_refs, dy_refs = refs[:n], refs[n:2 * n]
        dx_ref, dg_refs = refs[2 * n], refs[2 * n + 1:]
        i = pl.program_id(0)
        xv = x_ref[...]
        r = lax.rsqrt(jnp.mean(xv * xv, axis=-1, keepdims=True) + RMS_EPS)
        xh = xv * r
        dxh = jnp.zeros_like(xv)
        for g_ref, dy_ref, dg_ref in zip(g_refs, dy_refs, dg_refs):
            dy = dy_ref[...].astype(F32)
            part = jnp.sum(dy * xh, axis=0, keepdims=True)

            @pl.when(i == 0)
            def _():
                dg_ref[...] = part

            @pl.when(i > 0)
            def _():
                dg_ref[...] += part

            dxh = dxh + dy * g_ref[...]
        dx = r * (dxh - xh * jnp.mean(dxh * xh, axis=-1, keepdims=True))
        dx_ref[...] = dres_ref[...] + dx

    row = pl.BlockSpec((tm, D), lambda i: (i, 0))
    vec = pl.BlockSpec((1, D), lambda i: (0, 0))
    outs = pl.pallas_call(
        body, name=name, grid=(T // tm,),
        in_specs=[row, row] + [vec] * n + [row] * n,
        out_specs=[row] + [vec] * n,
        out_shape=[jax.ShapeDtypeStruct((T, D), F32)] + [jax.ShapeDtypeStruct((1, D), F32)] * n,
        compiler_params=_cparams("arbitrary"),
    )(x, dres, *[g for g, _ in pairs], *[dy for _, dy in pairs])
    return outs[0], outs[1:]


def _mm_nn(name, a, w, k_blk=0, res=None, out_dtype=F32):
    M, K = a.shape
    S, _, Ns = w.shape
    tm = _row_tile(M, 512)
    tn = _col_tile(Ns, 1408)
    nt = Ns // tn

    def body(a_ref, w_ref, *refs):
        o_ref = refs[-1]
        acc = jnp.dot(a_ref[...].astype(BF16), w_ref[...], preferred_element_type=F32)
        if res is not None:
            acc = acc + refs[0][...]
        o_ref[...] = acc.astype(o_ref.dtype)

    in_specs = [pl.BlockSpec((tm, K), lambda j, i: (i, 0)),
                pl.BlockSpec((None, K, tn), lambda j, i: (j // nt, k_blk, j % nt))]
    args = [a, w]
    if res is not None:
        in_specs.append(pl.BlockSpec((tm, tn), lambda j, i: (i, j)))
        args.append(res)
    return pl.pallas_call(
        body, name=name, grid=(S * nt, M // tm),
        in_specs=in_specs, out_specs=pl.BlockSpec((tm, tn), lambda j, i: (i, j)),
        out_shape=jax.ShapeDtypeStruct((M, S * Ns), out_dtype),
        compiler_params=_cparams("parallel", "parallel"),
    )(*args)


def _mm_nt_k(name, dy, w, K, k_blk=0, out_dtype=F32):
    M = dy.shape[0]
    S, _, Ns = w.shape
    tm = _row_tile(M, 1408)
    tn = _col_tile(Ns, 1408)
    nt = Ns // tn
    steps = S * nt

    def body(dy_ref, w_ref, o_ref, acc_ref):
        j = pl.program_id(1)
        part = lax.dot_general(dy_ref[...].astype(BF16), w_ref[...], (((1,), (1,)), ((), ())),
                               preferred_element_type=F32)

        @pl.when(j == 0)
        def _():
            acc_ref[...] = part

        @pl.when(j > 0)
        def _():
            acc_ref[...] += part

        @pl.when(j == steps - 1)
        def _():
            o_ref[...] = acc_ref[...].astype(o_ref.dtype)

    return pl.pallas_call(
        body, name=name, grid=(M // tm, steps),
        in_specs=[pl.BlockSpec((tm, tn), lambda i, j: (i, j)),
                  pl.BlockSpec((None, K, tn), lambda i, j: (j // nt, k_blk, j % nt))],
        out_specs=pl.BlockSpec((tm, K), lambda i, j: (i, 0)),
        out_shape=jax.ShapeDtypeStruct((M, K), out_dtype),
        scratch_shapes=[pltpu.VMEM((tm, K), F32)],
        compiler_params=_cparams("parallel", "arbitrary"),
    )(dy, w)


def _mm_tn(name, a, dy, S, out_dtype=BF16):
    T, K = a.shape
    Ns = dy.shape[1] // S
    tn = _col_tile(Ns, max(128, (6 * 1024 * 1024) // (4 * K) // 128 * 128))
    nt = Ns // tn
    tt = _row_tile(T, 1408)
    steps = T // tt

    def body(a_ref, dy_ref, o_ref, acc_ref):
        t = pl.program_id(1)
        part = lax.dot_general(a_ref[...].astype(BF16), dy_ref[...].astype(BF16), (((0,), (0,)), ((), ())),
                               preferred_element_type=F32)

        @pl.when(t == 0)
        def _():
            acc_ref[...] = part

        @pl.when(t > 0)
        def _():
            acc_ref[...] += part

        @pl.when(t == steps - 1)
        def _():
            o_ref[...] = acc_ref[...].astype(o_ref.dtype)

    return pl.pallas_call(
        body, name=name, grid=(S * nt, steps),
        in_specs=[pl.BlockSpec((tt, K), lambda j, t: (t, 0)),
                  pl.BlockSpec((tt, tn), lambda j, t: (t, j))],
        out_specs=pl.BlockSpec((None, K, tn), lambda j, t: (j // nt, 0, j % nt)),
        out_shape=jax.ShapeDtypeStruct((S, K, Ns), out_dtype),
        scratch_shapes=[pltpu.VMEM((K, tn), F32)],
        compiler_params=_cparams("parallel", "arbitrary"),
    )(a, dy)


def _gated_tile(T, width):
    return _row_tile(T, max(8, (2 * 1024 * 1024) // (4 * width) // 8 * 8))


def _mm_glu(name, a, w, res):
    M, K = a.shape
    Ns = w.shape[2]
    tm = _row_tile(M, 512)

    def body(a_ref, wv_ref, wg_ref, r_ref, v_ref, g_ref, o_ref):
        ab = a_ref[...].astype(BF16)
        val = jnp.dot(ab, wv_ref[...], preferred_element_type=F32)
        gate = jnp.dot(ab, wg_ref[...], preferred_element_type=F32)
        v_ref[...] = val
        g_ref[...] = gate
        o_ref[...] = r_ref[...] + val * _sigmoid(gate)

    out = pl.BlockSpec((tm, Ns), lambda j, i: (i, j))
    return pl.pallas_call(
        body, name=name, grid=(2, M // tm),
        in_specs=[pl.BlockSpec((tm, K), lambda j, i: (i, 0)),
                  pl.BlockSpec((None, K, Ns), lambda j, i: (j, 0, 0)),
                  pl.BlockSpec((None, K, Ns), lambda j, i: (j + 2, 0, 0)), out],
        out_specs=[out, out, out],
        out_shape=[jax.ShapeDtypeStruct((M, 2 * Ns), F32)] * 3,
        compiler_params=_cparams("parallel", "parallel"),
    )(a, w, w, res)


def _glu_bwd(name, val, gate, dout):
    T, D = dout.shape
    tm = _gated_tile(T, 2 * D)

    def body(v_ref, g_ref, d_ref, o_ref):
        s = _sigmoid(g_ref[...])
        d = d_ref[...]
        o_ref[:, :D] = (d * s).astype(o_ref.dtype)
        o_ref[:, D:] = (d * v_ref[...] * s * (1.0 - s)).astype(o_ref.dtype)

    blk = pl.BlockSpec((tm, D), lambda i: (i, 0))
    return pl.pallas_call(
        body, name=name, grid=(T // tm,),
        in_specs=[blk, blk, blk],
        out_specs=pl.BlockSpec((tm, 2 * D), lambda i: (i, 0)),
        out_shape=jax.ShapeDtypeStruct((T, 2 * D), BF16),
        compiler_params=_cparams("parallel"),
    )(val, gate, dout)


def _mm_swiglu(name, a, w, k_blk):
    M, K = a.shape
    Ns = w.shape[2]
    tm = _row_tile(M, 512)

    def body(a_ref, wg_ref, wu_ref, g_ref, u_ref, mid_ref):
        ab = a_ref[...].astype(BF16)
        g = jnp.dot(ab, wg_ref[...], preferred_element_type=F32)
        u = jnp.dot(ab, wu_ref[...], preferred_element_type=F32)
        g_ref[...] = g.astype(g_ref.dtype)
        u_ref[...] = u.astype(u_ref.dtype)
        mid_ref[...] = (g * _sigmoid(g) * u).astype(mid_ref.dtype)

    out = pl.BlockSpec((tm, Ns), lambda j, i: (i, j))
    return pl.pallas_call(
        body, name=name, grid=(2, M // tm),
        in_specs=[pl.BlockSpec((tm, K), lambda j, i: (i, 0)),
                  pl.BlockSpec((None, K, Ns), lambda j, i: (j, k_blk, 0)),
                  pl.BlockSpec((None, K, Ns), lambda j, i: (j + 2, k_blk, 0))],
        out_specs=[out, out, out],
        out_shape=[jax.ShapeDtypeStruct((M, 2 * Ns), BF16)] * 3,
        compiler_params=_cparams("parallel", "parallel"),
    )(a, w, w)


def _mm_nt_swiglu_bwd(name, dh, w, g, u):
    M, D = dh.shape
    F = w.shape[1]
    tm = _row_tile(M, 512)

    def body(dh_ref, w_ref, g_ref, u_ref, o_ref):
        d = lax.dot_general(dh_ref[...].astype(BF16), w_ref[...], NT_DIMS, preferred_element_type=F32)
        gv = g_ref[...].astype(F32)
        s = _sigmoid(gv)
        o_ref[:, :F] = (d * u_ref[...].astype(F32) * s * (1.0 + gv * (1.0 - s))).astype(o_ref.dtype)
        o_ref[:, F:] = (d * gv * s).astype(o_ref.dtype)

    half = pl.BlockSpec((tm, F), lambda i: (i, 0))
    return pl.pallas_call(
        body, name=name, grid=(M // tm,),
        in_specs=[pl.BlockSpec((tm, D), lambda i: (i, 0)), pl.BlockSpec((None, F, D), lambda i: (0, 0, 0)), half, half],
        out_specs=pl.BlockSpec((tm, 2 * F), lambda i: (i, 0)),
        out_shape=jax.ShapeDtypeStruct((M, 2 * F), BF16),
        compiler_params=_cparams("parallel"),
    )(dh, w, g, u)


def _final_loss(name, h, gain, target):
    T, D = h.shape
    tm = X_START
    lead = X_START // tm

    def body(h_ref, g_ref, t_ref, loss_ref, dh_ref, dg_ref):
        i = pl.program_id(0)

        @pl.when(i == 0)
        def _():
            loss_ref[...] = jnp.zeros_like(loss_ref)
            dg_ref[...] = jnp.zeros_like(dg_ref)
            dh_ref[...] = jnp.zeros_like(dh_ref)

        @pl.when(i >= lead)
        def _():
            xv = h_ref[...]
            r = lax.rsqrt(jnp.mean(xv * xv, axis=-1, keepdims=True) + RMS_EPS)
            xh = xv * r
            g = g_ref[...]
            diff = xh * g - t_ref[...]
            loss_ref[...] += 0.5 * jnp.sum(jnp.mean(diff * diff, axis=-1, keepdims=True), axis=0, keepdims=True)
            dout = diff * (1.0 / D)
            dg_ref[...] += jnp.sum(dout * xh, axis=0, keepdims=True)
            dxh = dout * g
            dh_ref[...] = r * (dxh - xh * jnp.mean(dxh * xh, axis=-1, keepdims=True))

    return pl.pallas_call(
        body, name=name, grid=(T // tm,),
        in_specs=[pl.BlockSpec((tm, D), lambda i: (i, 0)), pl.BlockSpec((1, D), lambda i: (0, 0)),
                  pl.BlockSpec((tm, D), lambda i: (jnp.maximum(i - lead, 0), 0))],
        out_specs=[pl.BlockSpec((1, 128), lambda i: (0, 0)), pl.BlockSpec((tm, D), lambda i: (i, 0)),
                   pl.BlockSpec((1, D), lambda i: (0, 0))],
        out_shape=[jax.ShapeDtypeStruct((1, 128), F32), jax.ShapeDtypeStruct((T, D), F32),
                   jax.ShapeDtypeStruct((1, D), F32)],
        compiler_params=_cparams("arbitrary"),
    )(h, gain, target)


def _grid_call(name, body, grid, in_specs, out_specs, out_shape, scratch_shapes, args, comm=None, comm_args=()):
    params = _cparams(*(("arbitrary",) * len(grid)))
    if comm is None:
        return pl.pallas_call(body, name=name, grid=grid, in_specs=in_specs, out_specs=out_specs,
                              out_shape=out_shape, scratch_shapes=scratch_shapes, compiler_params=params)(*args), []
    outs = pl.pallas_call(
        _embed_comm(comm, body, grid, len(in_specs), len(out_specs)), name=name, grid=grid,
        in_specs=list(in_specs) + _any_specs(comm.n), out_specs=list(out_specs) + _any_specs(comm.n),
        out_shape=list(out_shape) + comm.out_shape, scratch_shapes=list(scratch_shapes) + comm.scratch,
        compiler_params=params)(*args, *comm_args)
    return outs[:len(out_specs)], outs[len(out_specs):]


def _s5_discretise(a_re, a_im, log_dt, bt_re, bt_im):
    dt = jnp.exp(log_dt)
    mag = jnp.exp(dt * a_re)
    ang = dt * a_im
    abar_re = mag * jnp.cos(ang)
    abar_im = mag * jnp.sin(ang)
    den = a_re * a_re + a_im * a_im
    coef_re = ((abar_re - 1.0) * a_re + abar_im * a_im) / den
    coef_im = (abar_im * a_re - (abar_re - 1.0) * a_im) / den
    bbar_re = coef_re * bt_re - coef_im * bt_im
    bbar_im = coef_re * bt_im + coef_im * bt_re
    return abar_re, abar_im, bbar_re, bbar_im


def _s5_prep_fwd(name, a_re, a_im, log_dt, bt_re, bt_im):
    G, _, P = a_re.shape
    C = bt_re.shape[1]

    def body(ar, ai, ld, br, bi, o_ar, o_ai, o_br, o_bi):
        outs = _s5_discretise(ar[...], ai[...], ld[...], br[...], bi[...])
        for o, v in zip((o_ar, o_ai, o_br, o_bi), outs):
            o[...] = v

    return pl.pallas_call(
        body, name=name,
        out_shape=[jax.ShapeDtypeStruct((G, 1, P), F32)] * 2 + [jax.ShapeDtypeStruct((G, C, P), F32)] * 2,
    )(a_re, a_im, log_dt, bt_re, bt_im)


def _s5_prep_bwd(name, a_re, a_im, log_dt, bt_re, bt_im, d_ar, d_ai, d_br, d_bi):
    G, _, P = a_re.shape
    C = bt_re.shape[1]

    def body(ar, ai, ld, br, bi, gar, gai, gbr, gbi, o_ar, o_ai, o_ld, o_br, o_bi):
        _, vjp = jax.vjp(_s5_discretise, ar[...], ai[...], ld[...], br[...], bi[...])
        grads = vjp((gar[...], gai[...], gbr[...], gbi[...]))
        for o, v in zip((o_ar, o_ai, o_ld, o_br, o_bi), grads):
            o[...] = v

    return pl.pallas_call(
        body, name=name,
        out_shape=[jax.ShapeDtypeStruct((G, 1, P), F32)] * 2 + [jax.ShapeDtypeStruct((G, 1, 1), F32)]
        + [jax.ShapeDtypeStruct((G, C, P), F32)] * 2,
    )(a_re, a_im, log_dt, bt_re, bt_im, d_ar, d_ai, d_br, d_bi)


def _cmul(ar, ai, br, bi):
    return ar * br - ai * bi, ar * bi + ai * br


def _power_table(a_re, a_im):
    rows_re, rows_im = [a_re], [a_im]
    for _ in range(7):
        r, m = _cmul(rows_re[-1], rows_im[-1], a_re, a_im)
        rows_re.append(r)
        rows_im.append(m)
    row = lax.broadcasted_iota(jnp.int32, (8, a_re.shape[1]), 0)
    t_re = jnp.zeros((8, a_re.shape[1]), F32)
    t_im = jnp.zeros((8, a_re.shape[1]), F32)
    for k in range(8):
        t_re = jnp.where(row == k, rows_re[k], t_re)
        t_im = jnp.where(row == k, rows_im[k], t_im)
    return t_re, t_im, rows_re, rows_im


def _s5_fwd(name, u, b_re, b_im, ct_re, ct_im, abar_re, abar_im, d_skip, comm=None, comm_args=()):
    T, D = u.shape
    n_st = D // CH_TILE
    W = STATE_TILE
    tc = _row_tile(T, 512)
    n_tiles = tc // 8

    def body(u_ref, bre_ref, bim_ref, cre_ref, cim_ref, ar_ref, ai_ref, d_ref,
             y_ref, z_ref, xr_ref, xi_ref, carry_re, carry_im, pw_re, pw_im, sh_re, sh_im):
        c = pl.program_id(1)

        @pl.when(c == 0)
        def _():
            t_re, t_im, rows_re, rows_im = _power_table(ar_ref[...], ai_ref[...])
            pw_re[...] = t_re
            pw_im[...] = t_im
            first_rows = lax.broadcasted_iota(jnp.int32, (8, W), 0)
            for n, d in enumerate((1, 2, 4)):
                sh_re[n] = jnp.where(first_rows >= d, jnp.broadcast_to(rows_re[d - 1], (8, W)), 0.0)
                sh_im[n] = jnp.where(first_rows >= d, jnp.broadcast_to(rows_im[d - 1], (8, W)), 0.0)
            carry_re[...] = jnp.zeros_like(carry_re)
            carry_im[...] = jnp.zeros_like(carry_im)

        ub = u_ref[...].astype(BF16)
        xr_ref[...] = jnp.dot(ub, bre_ref[...], preferred_element_type=F32)
        xi_ref[...] = jnp.dot(ub, bim_ref[...], preferred_element_type=F32)
        row = lax.broadcasted_iota(jnp.int32, (8, W), 0)

        def tile(i, carry):
            c_re, c_im = carry
            rows = pl.ds(pl.multiple_of(i * 8, 8), 8)
            r = xr_ref[rows, :]
            m = xi_ref[rows, :]
            for n, d in enumerate((1, 2, 4)):
                pr, pm = _cmul(sh_re[n], sh_im[n], pltpu.roll(r, d, 0), pltpu.roll(m, d, 0))
                r = r + pr
                m = m + pm
            pr, pm = _cmul(pw_re[...], pw_im[...], c_re, c_im)
            r = r + pr
            m = m + pm
            xr_ref[rows, :] = r
            xi_ref[rows, :] = m
            return jnp.broadcast_to(r[7:8, :], (8, W)), jnp.broadcast_to(m[7:8, :], (8, W))

        c_re, c_im = lax.fori_loop(0, n_tiles, tile, (carry_re[...], carry_im[...]))
        carry_re[...] = c_re
        carry_im[...] = c_im
        y = (jnp.dot(xr_ref[...].astype(BF16), cre_ref[...], preferred_element_type=F32)
             - jnp.dot(xi_ref[...].astype(BF16), cim_ref[...], preferred_element_type=F32)
             + d_ref[...] * u_ref[...])
        y_ref[...] = y
        z_ref[...] = _gelu(y).astype(z_ref.dtype)

    ch = pl.BlockSpec((tc, CH_TILE), lambda s, c: (c, s))
    st = pl.BlockSpec((tc, W), lambda s, c: (c, s))
    return _grid_call(
        name, body, (n_st, T // tc),
        in_specs=[ch,
                  pl.BlockSpec((None, CH_TILE, W), lambda s, c: (s, 0, 0)),
                  pl.BlockSpec((None, CH_TILE, W), lambda s, c: (s, 0, 0)),
                  pl.BlockSpec((None, W, CH_TILE), lambda s, c: (s, 0, 0)),
                  pl.BlockSpec((None, W, CH_TILE), lambda s, c: (s, 0, 0)),
                  pl.BlockSpec((1, W), lambda s, c: (0, s)),
                  pl.BlockSpec((1, W), lambda s, c: (0, s)),
                  pl.BlockSpec((1, CH_TILE), lambda s, c: (0, s))],
        out_specs=[ch, ch, st, st],
        out_shape=[jax.ShapeDtypeStruct((T, D), F32), jax.ShapeDtypeStruct((T, D), BF16),
                   jax.ShapeDtypeStruct((T, 4 * D), F32), jax.ShapeDtypeStruct((T, 4 * D), F32)],
        scratch_shapes=[pltpu.VMEM((8, W), F32), pltpu.VMEM((8, W), F32),
                        pltpu.VMEM((8, W), F32), pltpu.VMEM((8, W), F32),
                        pltpu.VMEM((3, 8, W), F32), pltpu.VMEM((3, 8, W), F32)],
        args=(u, b_re, b_im, ct_re, ct_im, abar_re, abar_im, d_skip), comm=comm, comm_args=comm_args)


def _s5_bwd(name, dz, y, u, x_re, x_im, c_re, c_im, bt_re, bt_im, abar_re, abar_im, d_skip, comm=None, comm_args=()):
    T, D = u.shape
    n_st = D // CH_TILE
    W = STATE_TILE
    tc = _row_tile(T, 512)
    n_chunks = T // tc
    n_tiles = tc // 8
    tiles_per_chunk = tc // 8

    def body(dz_ref, y_ref, u_ref, xr_ref, xi_ref, xpr_ref, xpi_ref, cre_ref, cim_ref, btr_ref, bti_ref,
             ar_ref, ai_ref, d_ref,
             du_ref, dd_ref, dbr_ref, dbi_ref, dcr_ref, dci_ref, dar_ref, dai_ref,
             lam_re, lam_im, xe_re, xe_im, carry_re, carry_im, pw_re, pw_im, sh_re, sh_im, acc_ar, acc_ai):
        k = pl.program_id(1)
        first_chunk = k == n_chunks - 1

        @pl.when(k == 0)
        def _():
            t_re, t_im, rows_re, rows_im = _power_table(ar_ref[...], -ai_ref[...])
            row = lax.broadcasted_iota(jnp.int32, (8, W), 0)
            r_re = jnp.zeros((8, W), F32)
            r_im = jnp.zeros((8, W), F32)
            for j in range(8):
                r_re = jnp.where(row == j, rows_re[7 - j], r_re)
                r_im = jnp.where(row == j, rows_im[7 - j], r_im)
            pw_re[...] = r_re
            pw_im[...] = r_im
            for n, d in enumerate((1, 2, 4)):
                sh_re[n] = jnp.where(row < 8 - d, jnp.broadcast_to(rows_re[d - 1], (8, W)), 0.0)
                sh_im[n] = jnp.where(row < 8 - d, jnp.broadcast_to(rows_im[d - 1], (8, W)), 0.0)
            carry_re[...] = jnp.zeros_like(carry_re)
            carry_im[...] = jnp.zeros_like(carry_im)
            acc_ar[...] = jnp.zeros_like(acc_ar)
            acc_ai[...] = jnp.zeros_like(acc_ai)
            dd_ref[...] = jnp.zeros_like(dd_ref)
            dbr_ref[...] = jnp.zeros_like(dbr_ref)
            dbi_ref[...] = jnp.zeros_like(dbi_ref)
            dcr_ref[...] = jnp.zeros_like(dcr_ref)
            dci_ref[...] = jnp.zeros_like(dci_ref)

        uv = u_ref[...]
        dy = dz_ref[...] * _gelu_grad(y_ref[...])
        dyb = dy.astype(BF16)
        lam_re[...] = jnp.dot(dyb, cre_ref[...], preferred_element_type=F32)
        lam_im[...] = -jnp.dot(dyb, cim_ref[...], preferred_element_type=F32)
        keep = jnp.where(first_chunk, 0.0, 1.0)
        xe_re[pl.ds(0, 8), :] = xpr_ref[...] * keep
        xe_im[pl.ds(0, 8), :] = xpi_ref[...] * keep
        xe_re[pl.ds(8, tc), :] = xr_ref[...]
        xe_im[pl.ds(8, tc), :] = xi_ref[...]
        row = lax.broadcasted_iota(jnp.int32, (8, W), 0)

        def tile(n, carry):
            c_re, c_im, s_ar, s_ai = carry
            i = n_tiles - 1 - n
            rows = pl.ds(pl.multiple_of(i * 8, 8), 8)
            r = lam_re[rows, :]
            m = lam_im[rows, :]
            for q, d in enumerate((1, 2, 4)):
                pr, pm = _cmul(sh_re[q], sh_im[q], pltpu.roll(r, 8 - d, 0), pltpu.roll(m, 8 - d, 0))
                r = r + pr
                m = m + pm
            pr, pm = _cmul(pw_re[...], pw_im[...], c_re, c_im)
            r = r + pr
            m = m + pm
            lam_re[rows, :] = r
            lam_im[rows, :] = m
            cur_re = xe_re[pl.ds(pl.multiple_of(i * 8 + 8, 8), 8), :]
            cur_im = xe_im[pl.ds(pl.multiple_of(i * 8 + 8, 8), 8), :]
            bef_re = xe_re[rows, :]
            bef_im = xe_im[rows, :]
            xp_re = jnp.where(row == 0, jnp.broadcast_to(bef_re[7:8, :], (8, W)), pltpu.roll(cur_re, 1, 0))
            xp_im = jnp.where(row == 0, jnp.broadcast_to(bef_im[7:8, :], (8, W)), pltpu.roll(cur_im, 1, 0))
            s_ar = s_ar + r * xp_re + m * xp_im
            s_ai = s_ai + m * xp_re - r * xp_im
            return jnp.broadcast_to(r[0:1, :], (8, W)), jnp.broadcast_to(m[0:1, :], (8, W)), s_ar, s_ai

        c_re, c_im, s_ar, s_ai = lax.fori_loop(
            0, n_tiles, tile, (carry_re[...], carry_im[...], acc_ar[...], acc_ai[...]))
        carry_re[...] = c_re
        carry_im[...] = c_im
        acc_ar[...] = s_ar
        acc_ai[...] = s_ai
        lr = lam_re[...].astype(BF16)
        li = lam_im[...].astype(BF16)
        du_ref[...] = (dy * d_ref[...] + jnp.dot(lr, btr_ref[...], preferred_element_type=F32)
                       + jnp.dot(li, bti_ref[...], preferred_element_type=F32))
        dd_ref[...] += jnp.sum(dy * uv, axis=0, keepdims=True)
        tn_dims = (((0,), (0,)), ((), ()))
        ub = uv.astype(BF16)
        dbr_ref[...] += lax.dot_general(ub, lr, tn_dims, preferred_element_type=F32)
        dbi_ref[...] += lax.dot_general(ub, li, tn_dims, preferred_element_type=F32)
        dcr_ref[...] += lax.dot_general(dyb, xr_ref[...].astype(BF16), tn_dims, preferred_element_type=F32)
        dci_ref[...] -= lax.dot_general(dyb, xi_ref[...].astype(BF16), tn_dims, preferred_element_type=F32)

        @pl.when(first_chunk)
        def _():
            dar_ref[...] = jnp.sum(acc_ar[...], axis=0, keepdims=True)
            dai_ref[...] = jnp.sum(acc_ai[...], axis=0, keepdims=True)

    rev = lambda k: n_chunks - 1 - k
    ch = pl.BlockSpec((tc, CH_TILE), lambda s, k: (rev(k), s))
    st = pl.BlockSpec((tc, W), lambda s, k: (rev(k), s))
    prev = pl.BlockSpec((8, W), lambda s, k: (jnp.maximum(rev(k) * tiles_per_chunk - 1, 0), s))
    mat_cw = pl.BlockSpec((None, CH_TILE, W), lambda s, k: (s, 0, 0))
    mat_wc = pl.BlockSpec((None, W, CH_TILE), lambda s, k: (s, 0, 0))
    vec_w = pl.BlockSpec((1, W), lambda s, k: (0, s))
    vec_c = pl.BlockSpec((1, CH_TILE), lambda s, k: (0, s))
    dense = jax.ShapeDtypeStruct((n_st, CH_TILE, W), F32)
    return _grid_call(
        name, body, (n_st, n_chunks),
        in_specs=[ch, ch, ch, st, st, prev, prev, mat_cw, mat_cw, mat_wc, mat_wc, vec_w, vec_w, vec_c],
        out_specs=[ch, vec_c, mat_cw, mat_cw, mat_cw, mat_cw, vec_w, vec_w],
        out_shape=[jax.ShapeDtypeStruct((T, D), F32), jax.ShapeDtypeStruct((1, D), F32), dense, dense, dense, dense,
                   jax.ShapeDtypeStruct((1, 4 * D), F32), jax.ShapeDtypeStruct((1, 4 * D), F32)],
        scratch_shapes=[pltpu.VMEM((tc, W), F32), pltpu.VMEM((tc, W), F32),
                        pltpu.VMEM((tc + 8, W), F32), pltpu.VMEM((tc + 8, W), F32),
                        pltpu.VMEM((8, W), F32), pltpu.VMEM((8, W), F32),
                        pltpu.VMEM((8, W), F32), pltpu.VMEM((8, W), F32),
                        pltpu.VMEM((3, 8, W), F32), pltpu.VMEM((3, 8, W), F32),
                        pltpu.VMEM((8, W), F32), pltpu.VMEM((8, W), F32)],
        args=(dz, y, u, x_re, x_im, x_re, x_im, c_re, c_im, bt_re, bt_im, abar_re, abar_im, d_skip),
        comm=comm, comm_args=comm_args)


def _sums_matrix(strictly_later, copies):
    jj = lax.broadcasted_iota(jnp.int32, (copies * KEY_BLOCK, 2 * KEY_BLOCK), 0) & (KEY_BLOCK - 1)
    ss = lax.broadcasted_iota(jnp.int32, (copies * KEY_BLOCK, 2 * KEY_BLOCK), 1)
    tri = (jj > ss) if strictly_later else (jj < ss)
    return (tri | (ss >= KEY_BLOCK)).astype(BF16)


def _split_heads(blk):
    first = lax.broadcasted_iota(jnp.int32, blk.shape, 1) < HEAD_DIM
    zero = jnp.zeros_like(blk)
    return jnp.concatenate([jnp.where(first, blk, zero), jnp.where(first, zero, blk)], axis=0)


LOG2_E = 1.4426950408889634


def _sb_scores(z, mask, later):
    z2 = z * LOG2_E
    minus_abs = lax.bitcast_convert_type(lax.bitcast_convert_type(z2, jnp.uint32) | jnp.uint32(0x80000000), F32)
    lb = jnp.minimum(z2, 0.0) - jnp.log2(1.0 + jnp.exp2(minus_abs))
    lm = lb - z2
    if mask is not None:
        lm = jnp.where(mask, lm, 0.0)
    hi = lm.astype(BF16)
    lo = (lm - hi.astype(F32)).astype(BF16)
    return lb, jnp.dot(jnp.concatenate([hi, lo], axis=1), later, preferred_element_type=F32)


def _key_rows(kb):
    return pl.ds(pl.multiple_of(kb * KEY_BLOCK, KEY_BLOCK), KEY_BLOCK)


def _sb_mask(q_row0, k_row0, tq):
    tpos = q_row0 + lax.broadcasted_iota(jnp.int32, (tq, KEY_BLOCK), 0)
    spos = k_row0 + lax.broadcasted_iota(jnp.int32, (tq, KEY_BLOCK), 1)
    return (spos < tpos) & (spos >= META_START)


def _key_block_phases(iq, per_q, block, ascending):
    first_diag = iq * per_q

    def nth(lo, n, i):
        return lo + i if ascending else lo + n - 1 - i

    def run(lo, n, masked):
        if isinstance(n, int):
            for i in range(n):
                block(nth(lo, n, i), masked)
            return

        def pair(i, carry):
            block(nth(lo, n, 2 * i), masked)
            block(nth(lo, n, 2 * i + 1), masked)
            return carry

        def single(i, carry):
            block(nth(lo, n, n - 1), masked)
            return carry

        lax.fori_loop(0, jnp.right_shift(n, 1), pair, 0)
        lax.fori_loop(0, jnp.bitwise_and(n, 1), single, 0)

    phases = [(0, jnp.minimum(iq, 1), True), (1, jnp.maximum(first_diag - 1, 0), False), (first_diag, per_q, True)]
    for lo, n, masked in (phases if ascending else phases[::-1]):
        run(lo, n, masked)


def _attn_fwd(name, q, kv):
    T, D = q.shape
    n_hp = D // 128
    tq = _row_tile(T, 512)
    per_q = tq // KEY_BLOCK
    scale = 1.0 / math.sqrt(HEAD_DIM)

    half = tq // 2
    chains = [(h, r) for h in range(2) for r in range(2)]
    rows = lambda r: slice(r * half, (r + 1) * half)
    cols = lambda h: slice(h * KEY_BLOCK, (h + 1) * KEY_BLOCK)

    def body(q_ref, k_ref, v_ref, o_ref, l_ref, z_buf, w_buf, acc_ref, run_ref):
        iq = pl.program_id(1)
        n_kb = (iq + 1) * per_q
        qs = q_ref[...] * jnp.asarray(scale, BF16)
        later = _sums_matrix(True, 2)

        def scores(kb):
            return lax.dot_general(qs, _split_heads(k_ref[_key_rows(kb), :]), NT_DIMS, preferred_element_type=F32)

        def weighted_values(kb):
            return jnp.dot(w_buf[...], _split_heads(v_ref[_key_rows(kb), :]), preferred_element_type=F32)

        acc_ref[...] = jnp.zeros_like(acc_ref)
        run_ref[...] = jnp.zeros_like(run_ref)
        w_buf[...] = jnp.zeros_like(w_buf)
        z_buf[...] = scores(n_kb - 1)

        def block(kb, masked):
            acc_ref[...] += weighted_values(jnp.minimum(kb + 1, n_kb - 1))
            z_next = scores(jnp.maximum(kb - 1, 0))
            masks = [_sb_mask(iq * tq + r * half, kb * KEY_BLOCK, half) if masked else None for r in range(2)]
            first = [_sb_scores(z_buf[rows(r), cols(h)], masks[r], later) for h, r in chains]
            for (h, r), (lb, sums) in zip(chains, first):
                run = run_ref[h, rows(r), :]
                w = jnp.exp2(lb + sums[:, :KEY_BLOCK] + run)
                if masked:
                    w = jnp.where(masks[r], w, 0.0)
                w_buf[rows(r), cols(h)] = w.astype(BF16)
                run_ref[h, rows(r), :] = run + sums[:, KEY_BLOCK:]
            z_buf[...] = z_next

        _key_block_phases(iq, per_q, block, ascending=False)
        acc_ref[...] += weighted_values(0)
        lane = lax.broadcasted_iota(jnp.int32, (tq, 128), 1)
        o_ref[...] = acc_ref[...].astype(o_ref.dtype)
        l_ref[...] = jnp.where(lane < HEAD_DIM, run_ref[0], run_ref[1])

    blk = pl.BlockSpec((tq, 128), lambda h, i: (i, h))
    return pl.pallas_call(
        body, name=name, grid=(n_hp, T // tq),
        in_specs=[blk, pl.BlockSpec((T, 128), lambda h, i: (0, h)), pl.BlockSpec((T, 128), lambda h, i: (0, n_hp + h))],
        out_specs=[blk, blk],
        out_shape=[jax.ShapeDtypeStruct((T, D), BF16), jax.ShapeDtypeStruct((T, D), F32)],
        scratch_shapes=[pltpu.VMEM((tq, 2 * KEY_BLOCK), F32), pltpu.VMEM((tq, 2 * KEY_BLOCK), BF16),
                        pltpu.VMEM((tq, 128), F32), pltpu.VMEM((2, tq, 128), F32)],
        compiler_params=_cparams("parallel", "arbitrary"),
    )(q, kv, kv)


def _attn_bwd(name, q, kv, do, ltot):
    T, D = q.shape
    n_hp = D // 128
    tq = _row_tile(T, 512)
    n_q = T // tq
    per_q = tq // KEY_BLOCK
    scale = 1.0 / math.sqrt(HEAD_DIM)

    half = tq // 2
    chains = [(h, r) for h in range(2) for r in range(2)]
    row_half = lambda r: slice(r * half, (r + 1) * half)
    head_cols = lambda h: slice(h * KEY_BLOCK, (h + 1) * KEY_BLOCK)
    stacked = lambda h, r: slice(h * tq + r * half, h * tq + (r + 1) * half)

    def body(q_ref, k_ref, v_ref, do_ref, l_ref, dq_ref, dk_ref, dv_ref,
             dk_acc, dv_acc, dq_acc, lpre_ref, cpre_ref, z_buf, dw_buf, dz_buf, w_buf):
        iq = pl.program_id(1)
        n_kb = (iq + 1) * per_q

        @pl.when(iq == 0)
        def _():
            dk_acc[...] = jnp.zeros_like(dk_acc)
            dv_acc[...] = jnp.zeros_like(dv_acc)

        first = lax.broadcasted_iota(jnp.int32, (tq, 128), 1) < HEAD_DIM
        qs = q_ref[...] * jnp.asarray(scale, BF16)
        dov = do_ref[...]
        ltv = l_ref[...]
        swapped = pltpu.roll(ltv, HEAD_DIM, 1)
        ltot = [jnp.where(first, ltv, swapped), jnp.where(first, swapped, ltv)]
        zero = jnp.zeros_like(qs)
        q_stack = jnp.concatenate([jnp.where(first, qs, zero), jnp.where(first, zero, qs)], axis=0)
        do_stack = jnp.concatenate([jnp.where(first, dov, zero), jnp.where(first, zero, dov)], axis=0)
        later = _sums_matrix(True, 2)
        earlier = _sums_matrix(False, 1)

        def scores(kb):
            rows = _key_rows(kb)
            return (lax.dot_general(qs, _split_heads(k_ref[rows, :]), NT_DIMS, preferred_element_type=F32),
                    lax.dot_general(dov, _split_heads(v_ref[rows, :]), NT_DIMS, preferred_element_type=F32))

        def flush(kb):
            rows = _key_rows(kb)
            k_heads = _split_heads(k_ref[rows, :])
            dq_acc[...] += (jnp.dot(dz_buf[:tq, :], k_heads[:KEY_BLOCK, :], preferred_element_type=F32)
                            + jnp.dot(dz_buf[tq:, :], k_heads[KEY_BLOCK:, :], preferred_element_type=F32))
            dk_acc[rows, :] += lax.dot_general(dz_buf[...], q_stack, TN_DIMS, preferred_element_type=F32)
            dv_acc[rows, :] += lax.dot_general(w_buf[...], do_stack, TN_DIMS, preferred_element_type=F32)

        dq_acc[...] = jnp.zeros_like(dq_acc)
        lpre_ref[...] = jnp.zeros_like(lpre_ref)
        cpre_ref[...] = jnp.zeros_like(cpre_ref)
        dz_buf[...] = jnp.zeros_like(dz_buf)
        w_buf[...] = jnp.zeros_like(w_buf)
        z_buf[...], dw_buf[...] = scores(0)

        def block(kb, masked):
            flush(jnp.maximum(kb - 1, 0))
            z_next, dw_next = scores(jnp.minimum(kb + 1, n_kb - 1))
            masks = [_sb_mask(iq * tq + r * half, kb * KEY_BLOCK, half) if masked else None for r in range(2)]
            first_stage = [_sb_scores(z_buf[row_half(r), head_cols(h)], masks[r], later) for h, r in chains]
            second_stage = []
            for (h, r), (lb, sums) in zip(chains, first_stage):
                after = ltot[h][row_half(r), :] - lpre_ref[h, row_half(r), :] - sums[:, KEY_BLOCK:]
                w = jnp.exp2(lb + sums[:, :KEY_BLOCK] + after)
                if masked:
                    w = jnp.where(masks[r], w, 0.0)
                da = w * dw_buf[row_half(r), head_cols(h)]
                w_buf[stacked(h, r), :] = w.astype(BF16)
                lpre_ref[h, row_half(r), :] += sums[:, KEY_BLOCK:]
                second_stage.append((da, jnp.dot(da.astype(BF16), earlier, preferred_element_type=F32)))
            for (h, r), (lb, _), (da, dsums) in zip(chains, first_stage, second_stage):
                sig = jnp.exp2(lb)
                through_later = sig * (dsums[:, :KEY_BLOCK] + cpre_ref[h, row_half(r), :])
                if masked:
                    through_later = jnp.where(masks[r], through_later, 0.0)
                dz_buf[stacked(h, r), :] = (da * (1.0 - sig) - through_later).astype(BF16)
                cpre_ref[h, row_half(r), :] += dsums[:, KEY_BLOCK:]
            z_buf[...] = z_next
            dw_buf[...] = dw_next

        _key_block_phases(iq, per_q, block, ascending=True)
        flush(n_kb - 1)
        dq_ref[...] = (dq_acc[...] * scale).astype(dq_ref.dtype)

        @pl.when(iq == n_q - 1)
        def _():
            dk_ref[...] = dk_acc[...].astype(dk_ref.dtype)
            dv_ref[...] = dv_acc[...].astype(dv_ref.dtype)

    blk = pl.BlockSpec((tq, 128), lambda h, i: (i, h))
    full = pl.BlockSpec((T, 128), lambda h, i: (0, h))
    return pl.pallas_call(
        body, name=name, grid=(n_hp, n_q),
        in_specs=[blk, full, pl.BlockSpec((T, 128), lambda h, i: (0, n_hp + h)), blk, blk],
        out_specs=[blk, full, full],
        out_shape=[jax.ShapeDtypeStruct((T, D), BF16)] * 3,
        scratch_shapes=[pltpu.VMEM((T, 128), F32), pltpu.VMEM((T, 128), F32), pltpu.VMEM((tq, 128), F32),
                        pltpu.VMEM((2, tq, 128), F32), pltpu.VMEM((2, tq, 128), F32),
                        pltpu.VMEM((tq, 2 * KEY_BLOCK), F32), pltpu.VMEM((tq, 2 * KEY_BLOCK), F32),
                        pltpu.VMEM((2 * tq, 128), BF16), pltpu.VMEM((2 * tq, 128), BF16)],
        compiler_params=_cparams("parallel", "arbitrary"),
    )(q, kv, kv, do, ltot)


def _adamw(name, w, g, m, v):
    shape = w.shape
    size = w.size
    if w.ndim >= 2 and shape[-1] % 128 == 0:
        cols = shape[-1]
    else:
        cols = 1024 if size % 1024 == 0 else shape[-1]
    rows = size // cols
    tm = _row_tile(rows, max(8, (1024 * 1024) // (4 * cols) // 8 * 8)) if rows % 8 == 0 else rows
    c1 = 1.0 / (1.0 - ADAM_B1 ** ADAM_STEP)
    c2 = 1.0 / (1.0 - ADAM_B2 ** ADAM_STEP)

    def body(w_ref, g_ref, m_ref, v_ref, d_ref, nm_ref, nv_ref):
        gv = g_ref[...]
        nm = ADAM_B1 * m_ref[...] + (1.0 - ADAM_B1) * gv
        nv = ADAM_B2 * v_ref[...] + (1.0 - ADAM_B2) * (gv * gv)
        d_ref[...] = -ADAM_LR * ((nm * c1) / (jnp.sqrt(nv * c2) + ADAM_EPS) + ADAM_WD * w_ref[...])
        nm_ref[...] = nm
        nv_ref[...] = nv

    blk = pl.BlockSpec((tm, cols), lambda i: (i, 0))
    outs = pl.pallas_call(
        body, name=name, grid=(rows // tm,),
        in_specs=[blk] * 4, out_specs=[blk] * 3,
        out_shape=[jax.ShapeDtypeStruct((rows, cols), F32)] * 3,
        compiler_params=_cparams("parallel"),
    )(*[t.reshape(rows, cols) for t in (w, g, m, v)])
    return tuple(o.reshape(shape) for o in outs)


def _any_specs(n):
    return [pl.BlockSpec(memory_space=pl.ANY)] * n


def _chip_index():
    return 2 * lax.axis_index("x") + lax.axis_index("y")


def _place():
    x, y, c = lax.axis_index("x"), lax.axis_index("y"), lax.axis_index("c")
    chips = [(1 - x, y), (x, 1 - y), (1 - x, 1 - y)]
    return x, y, c, chips


def _all_gather_chips(name, shards):
    n = len(shards)

    def body(*refs):
        x_refs, o_refs = refs[:n], refs[n:2 * n]
        send_sems, recv_sems = refs[2 * n:]
        x, y, c, chips = _place()
        me = 2 * x + y
        sibling = (x, y, 1 - c)

        def half(ref, i, which):
            h = shards[i].shape[0] // 2
            return ref.at[pl.ds(which * h, h)]

        def remote(k, i, src, dst, to):
            return pltpu.make_async_remote_copy(src_ref=src, dst_ref=dst, send_sem=send_sems.at[k, i],
                                                recv_sem=recv_sems.at[k, i], device_id=to, device_id_type=MESH)

        sent = []
        for j, chip in enumerate(chips):
            for i in range(n):
                cp = remote(j, i, half(x_refs[i], i, c), half(o_refs[i].at[me], i, c), (*chip, c))
                cp.start()
                sent.append(cp)
        for j, chip in enumerate(chips):
            pj = 2 * chip[0] + chip[1]
            for i in range(n):
                landed = half(o_refs[i].at[pj], i, c)
                remote(j, i, landed, landed, (*chip, c)).wait_recv()
                cp = remote(3 + j, i, landed, landed, sibling)
                cp.start()
                sent.append(cp)
        for j, chip in enumerate(chips):
            pj = 2 * chip[0] + chip[1]
            for i in range(n):
                got = half(o_refs[i].at[pj], i, 1 - c)
                remote(3 + j, i, got, got, sibling).wait_recv()
        for cp in sent:
            cp.wait_send()

    outs = pl.pallas_call(
        body, name=name,
        in_specs=_any_specs(n), out_specs=_any_specs(n),
        out_shape=[jax.ShapeDtypeStruct((4,) + s.shape, s.dtype) for s in shards],
        scratch_shapes=[pltpu.SemaphoreType.DMA((6, n)), pltpu.SemaphoreType.DMA((6, n))],
    )(*shards)
    return [lax.dynamic_update_slice(o, s[None], (_chip_index(), 0, 0)) for o, s in zip(outs, shards)]


def _pair_split(name, grads):
    n = len(grads)

    def body(*refs):
        g_refs, got_refs = refs[:n], refs[n:2 * n]
        send_sems, recv_sems = refs[2 * n:]
        x, y, c, _ = _place()
        sibling = (x, y, 1 - c)
        sent = []
        for i in range(n):
            h = grads[i].shape[1] // 2
            rc = pltpu.make_async_remote_copy(
                src_ref=g_refs[i].at[:, pl.ds((1 - c) * h, h)], dst_ref=got_refs[i],
                send_sem=send_sems.at[i], recv_sem=recv_sems.at[i], device_id=sibling, device_id_type=MESH)
            rc.start()
            sent.append(rc)
        for rc in sent:
            rc.wait()

    return pl.pallas_call(
        body, name=name,
        in_specs=_any_specs(n), out_specs=_any_specs(n),
        out_shape=[jax.ShapeDtypeStruct((4, g.shape[1] // 2, g.shape[2]), g.dtype) for g in grads],
        scratch_shapes=[pltpu.SemaphoreType.DMA((n,)), pltpu.SemaphoreType.DMA((n,))],
    )(*grads)


def _chip_exchange(name, sums):
    return _exchange_finish(sums, _run_comm(name, _exchange_comm(sums), sums))


class _Comm:
    def __init__(self, out_shape, copies):
        self.n = len(out_shape)
        self.out_shape = out_shape
        self.copies = copies
        self.scratch = [pltpu.SemaphoreType.DMA((3, self.n)), pltpu.SemaphoreType.DMA((3, self.n))]

    def start(self, *refs):
        for cp in self.copies(*refs, False):
            cp.start()

    def finish(self, *refs):
        for cp in self.copies(*refs, True):
            cp.wait_recv()
        for cp in self.copies(*refs, False):
            cp.wait_send()


def _exchange_comm(sums):
    n = len(sums)

    def copies(s_refs, o_refs, send_sems, recv_sems, mirrors):
        x, y, c, chips = _place()
        me = 2 * x + y
        out = []
        for j, chip in enumerate(chips):
            pj = 2 * chip[0] + chip[1]
            for i in range(n):
                out.append(pltpu.make_async_remote_copy(
                    src_ref=s_refs[i].at[pj], dst_ref=o_refs[i].at[pj if mirrors else me], send_sem=send_sems.at[j, i],
                    recv_sem=recv_sems.at[j, i], device_id=(*chip, c), device_id_type=MESH))
        return out

    return _Comm([jax.ShapeDtypeStruct(s.shape, s.dtype) for s in sums], copies)


def _exchange_finish(sums, outs):
    me = _chip_index()
    return [lax.dynamic_update_slice(o, lax.dynamic_index_in_dim(s, me, 0, keepdims=True), (me, 0, 0))
            for o, s in zip(outs, sums)]


def _halves_gather_comm(shards):
    n = len(shards)

    def copies(x_refs, o_refs, send_sems, recv_sems, mirrors):
        x, y, c, chips = _place()
        me = 2 * x + y
        out = []
        for j, chip in enumerate(chips):
            pj = 2 * chip[0] + chip[1]
            for i in range(n):
                h = shards[i].shape[0] // 2
                rows = pl.ds(c * h, h)
                out.append(pltpu.make_async_remote_copy(
                    src_ref=x_refs[i].at[rows], dst_ref=o_refs[i].at[pj if mirrors else me, rows],
                    send_sem=send_sems.at[j, i], recv_sem=recv_sems.at[j, i], device_id=(*chip, c), device_id_type=MESH))
        return out

    return _Comm([jax.ShapeDtypeStruct((4,) + s.shape, s.dtype) for s in shards], copies)


def _run_comm(name, comm, arrays):
    n = comm.n

    def body(*refs):
        comm.start(refs[:n], refs[n:2 * n], *refs[2 * n:])
        comm.finish(refs[:n], refs[n:2 * n], *refs[2 * n:])

    return pl.pallas_call(
        body, name=name, in_specs=_any_specs(n), out_specs=_any_specs(n),
        out_shape=comm.out_shape, scratch_shapes=comm.scratch,
    )(*arrays)


def _embed_comm(comm, body, grid, n_in, n_out):
    n = comm.n

    def wrapped(*refs):
        ins, c_in = refs[:n_in], refs[n_in:n_in + n]
        outs, c_out = refs[n_in + n:n_in + n + n_out], refs[n_in + n + n_out:n_in + 2 * n + n_out]
        scratch, sems = refs[n_in + 2 * n + n_out:-2], refs[-2:]
        ids = [pl.program_id(a) for a in range(len(grid))]
        first = functools.reduce(jnp.logical_and, [i == 0 for i in ids])
        last = functools.reduce(jnp.logical_and, [i == g - 1 for i, g in zip(ids, grid)])

        @pl.when(first)
        def _():
            comm.start(c_in, c_out, *sems)

        body(*ins, *outs, *scratch)

        @pl.when(last)
        def _():
            comm.finish(c_in, c_out, *sems)

    return wrapped


def _pair_forward(name, landed, shards):
    n = len(landed)

    def body(*refs):
        o_refs = refs[n:2 * n]
        send_sems, recv_sems = refs[2 * n:]
        x, y, c, chips = _place()
        sibling = (x, y, 1 - c)
        sent, arriving = [], []
        for j, chip in enumerate(chips):
            pj = 2 * chip[0] + chip[1]
            for i in range(n):
                h = shards[i].shape[0] // 2
                mine = o_refs[i].at[pj, pl.ds(c * h, h)]
                theirs = o_refs[i].at[pj, pl.ds((1 - c) * h, h)]
                for ref, group in ((mine, sent), (theirs, arriving)):
                    group.append(pltpu.make_async_remote_copy(
                        src_ref=ref, dst_ref=ref, send_sem=send_sems.at[j, i], recv_sem=recv_sems.at[j, i],
                        device_id=sibling, device_id_type=MESH))
        for cp in sent:
            cp.start()
        for cp in arriving:
            cp.wait_recv()
        for cp in sent:
            cp.wait_send()

    outs = pl.pallas_call(
        body, name=name, in_specs=_any_specs(n), out_specs=_any_specs(n),
        out_shape=[jax.ShapeDtypeStruct(a.shape, a.dtype) for a in landed],
        input_output_aliases={i: i for i in range(n)},
        scratch_shapes=[pltpu.SemaphoreType.DMA((3, n)), pltpu.SemaphoreType.DMA((3, n))],
    )(*landed)
    return [lax.dynamic_update_slice(o, s[None], (_chip_index(), 0, 0)) for o, s in zip(outs, shards)]


def _pair_join(name, halves):
    n = len(halves)

    def body(*refs):
        h_refs, o_refs = refs[:n], refs[n:2 * n]
        send_sems, recv_sems = refs[2 * n:]
        x, y, c, _ = _place()
        sibling = (x, y, 1 - c)
        sent = []
        for i in range(n):
            h = halves[i].shape[0]
            mine = o_refs[i].at[pl.ds(c * h, h)]
            rc = pltpu.make_async_remote_copy(
                src_ref=h_refs[i], dst_ref=mine, send_sem=send_sems.at[i], recv_sem=recv_sems.at[i],
                device_id=sibling, device_id_type=MESH)
            rc.start()
            sent.append(rc)
        for i in range(n):
            h = halves[i].shape[0]
            theirs = o_refs[i].at[pl.ds((1 - c) * h, h)]
            pltpu.make_async_remote_copy(
                src_ref=h_refs[i], dst_ref=theirs, send_sem=send_sems.at[i], recv_sem=recv_sems.at[i],
                device_id=sibling, device_id_type=MESH).wait_recv()
        for rc in sent:
            rc.wait_send()

    outs = pl.pallas_call(
        body, name=name,
        in_specs=_any_specs(n), out_specs=_any_specs(n),
        out_shape=[jax.ShapeDtypeStruct((2 * s.shape[0], s.shape[1]), s.dtype) for s in halves],
        scratch_shapes=[pltpu.SemaphoreType.DMA((n,)), pltpu.SemaphoreType.DMA((n,))],
    )(*halves)
    c = lax.axis_index("c")
    return [lax.dynamic_update_slice(o, s, (c * s.shape[0], 0)) for o, s in zip(outs, halves)]


def _add_pair(name, a, b):
    _, H, C = a.shape
    th = _row_tile(H, max(8, (1024 * 1024) // (4 * C) // 8 * 8))

    def body(a_ref, b_ref, o_ref):
        o_ref[...] = (a_ref[...].astype(F32) + b_ref[...].astype(F32)).astype(o_ref.dtype)

    blk = pl.BlockSpec((None, th, C), lambda q, i: (q, i, 0))
    return pl.pallas_call(
        body, name=name, grid=(4, H // th), in_specs=[blk, blk], out_specs=blk,
        out_shape=jax.ShapeDtypeStruct(a.shape, a.dtype), compiler_params=_cparams("parallel", "parallel"),
    )(a, b)


def _add_chips(name, parts):
    _, H, C = parts.shape
    th = _row_tile(H, max(8, (1024 * 1024) // (4 * C) // 8 * 8))

    def body(p_ref, o_ref):
        acc = p_ref[0].astype(F32)
        for q in range(1, 4):
            acc = acc + p_ref[q].astype(F32)
        o_ref[...] = acc

    return pl.pallas_call(
        body, name=name, grid=(H // th,),
        in_specs=[pl.BlockSpec((4, th, C), lambda i: (0, i, 0))], out_specs=pl.BlockSpec((th, C), lambda i: (i, 0)),
        out_shape=jax.ShapeDtypeStruct((H, C), F32), compiler_params=_cparams("parallel"),
    )(parts)


def _pair_sums(tag, grads):
    c = lax.axis_index("c")
    got = _pair_split(f"rs_pair_split_{tag}", grads)
    kept = [lax.dynamic_slice_in_dim(g, c * (g.shape[1] // 2), g.shape[1] // 2, axis=1) for g in grads]
    return [_add_pair(f"rs_add_pair_{tag}_{i}", k, g) for i, (k, g) in enumerate(zip(kept, got))]


def _chip_sums(parts):
    return _pair_join("rs_pair_join", [_add_chips(f"rs_add_chips_{i}", p) for i, p in enumerate(parts)])


def _block_diag(t):
    G, A, B = t.shape
    eye = jnp.eye(8, dtype=t.dtype)
    return jnp.einsum("sgab,gh->sgahb", t.reshape(G // 8, 8, A, B), eye).reshape(G // 8, 8 * A, 8 * B)


def _block_diag_extract(m, A, B):
    n = m.shape[0]
    eye = jnp.eye(8, dtype=m.dtype)
    return jnp.einsum("sgahb,gh->sgab", m.reshape(n, 8, A, 8, B), eye).reshape(8 * n, A, B)


def kernel(x, meta_tokens, norm_mix, norm_ffn, s5_a_re, s5_a_im, s5_log_dt, s5_b_re, s5_b_im, s5_c_re, s5_c_im, s5_d, s5_w_glu, norm_kv, w_kv, w_q, w_o, w_ffn_in, w_ffn_out, norm_final, loss_target, m_meta_tokens, m_norm_mix, m_norm_ffn, m_s5_a_re, m_s5_a_im, m_s5_log_dt, m_s5_b_re, m_s5_b_im, m_s5_c_re, m_s5_c_im, m_s5_d, m_s5_w_glu, m_norm_kv, m_w_kv, m_w_q, m_w_o, m_w_ffn_in, m_w_ffn_out, m_norm_final, v_meta_tokens, v_norm_mix, v_norm_ffn, v_s5_a_re, v_s5_a_im, v_s5_log_dt, v_s5_b_re, v_s5_b_im, v_s5_c_re, v_s5_c_im, v_s5_d, v_s5_w_glu, v_norm_kv, v_w_kv, v_w_q, v_w_o, v_w_ffn_in, v_w_ffn_out, v_norm_final):
    seq, D = x.shape[1], x.shape[2]
    T = X_START + seq
    G, P, C = s5_a_re.shape[1], S5_STATE, S5_GROUP
    d_ff = w_ffn_out.shape[1] * 4
    dq4 = D // 4
    chip = 2 * lax.axis_index("x") + lax.axis_index("y")

    small_in = jnp.concatenate([meta_tokens, jnp.pad(s5_d, ((0, 15), (0, 0)))], axis=0)
    (small_all,) = _all_gather_chips("ag_small", [small_in])
    meta_full = small_all[:, :N_META, :].transpose(1, 0, 2).reshape(N_META, D)
    d_skip = small_all[:, N_META, :].reshape(1, D)
    shards = [s.astype(BF16) for s in (s5_w_glu[0], w_kv, w_q[0], w_o[0],
                                       w_ffn_in.reshape(2 * D, -1), w_ffn_out.reshape(-1, D))]
    rows_out = d_ff // 4

    row = lambda v: v.reshape(1, -1)
    g_mix0, g_mix1 = row(norm_mix[0]), row(norm_mix[1])
    g_ffn = [row(norm_ffn[0]), row(norm_ffn[1])]
    g_kv, g_final = row(norm_kv), row(norm_final)

    a_re3 = s5_a_re[0].reshape(G, 1, P)
    a_im3 = s5_a_im[0].reshape(G, 1, P)
    log_dt3 = s5_log_dt[0].reshape(G, 1, 1)
    bt_re = s5_b_re[0].transpose(0, 2, 1)
    bt_im = s5_b_im[0].transpose(0, 2, 1)
    ab_re, ab_im, bb_re, bb_im = _s5_prep_fwd("s5_prep", a_re3, a_im3, log_dt3, bt_re, bt_im)
    abar_re, abar_im = ab_re.reshape(1, G * P), ab_im.reshape(1, G * P)
    bd_b_re = _block_diag(bb_re).astype(BF16)
    bd_b_im = _block_diag(bb_im).astype(BF16)
    bd_bt_re = bd_b_re.transpose(0, 2, 1)
    bd_bt_im = bd_b_im.transpose(0, 2, 1)
    bd_c_re = _block_diag(s5_c_re[0]).astype(BF16)
    bd_c_im = _block_diag(s5_c_im[0]).astype(BF16)
    bd_ct_re = bd_c_re.transpose(0, 2, 1)
    bd_ct_im = bd_c_im.transpose(0, 2, 1)

    h0 = jnp.concatenate([jnp.zeros((META_START, D), F32), meta_full, x[0]], axis=0)
    (u,) = _rmsnorm_fwd("norm_mix0", h0, [g_mix0], [F32])
    (y, z, x_re, x_im), landed = _s5_fwd(
        "s5_scan", u, bd_b_re, bd_b_im, bd_ct_re, bd_ct_im, abar_re, abar_im, d_skip,
        comm=_halves_gather_comm(shards), comm_args=shards)
    wg_glu, wg_kv, wg_q, wg_o, wg_in, wg_out = _pair_forward("ag_forward", landed, shards)
    wg_q = wg_q.reshape(1, D, D)
    wg_o = wg_o.reshape(1, D, D)
    wg_out = [wg_out[:, l * rows_out:(l + 1) * rows_out, :].reshape(1, d_ff, D) for l in range(2)]
    val, gate, h1 = _mm_glu("glu_proj", z, wg_glu, h0)

    def ffn_fwd(l, h):
        (n,) = _rmsnorm_fwd(f"norm_ffn{l}", h, [g_ffn[l]], [BF16])
        g, u, mid = _mm_swiglu(f"ffn_in{l}", n, wg_in, l)
        return n, (g, u), mid, _mm_nn(f"ffn_out{l}", mid, wg_out[l], res=h)

    n1, gu0, mid0, h2 = ffn_fwd(0, h1)
    nk, nq = _rmsnorm_fwd("norm_kv_q", h2, [g_kv, g_mix1], [BF16, BF16])
    kv = _mm_nn("kv_proj", nk, wg_kv, out_dtype=BF16)
    q = _mm_nn("q_proj", nq, wg_q, out_dtype=BF16)
    o, ltot = _attn_fwd("attn_fwd", q, kv)
    h3 = _mm_nn("o_proj", o, wg_o, res=h2)
    n3, gu1, mid1, h4 = ffn_fwd(1, h3)
    loss_part, dh4, dg_final = _final_loss("final_loss", h4, g_final, loss_target[0])

    def ffn_bwd(l, dh, h, n, gu, mid):
        dgu = _mm_nt_swiglu_bwd(f"ffn_out{l}_dx", dh, wg_out[l], *gu)
        dw_out = _mm_tn(f"ffn_out{l}_dw", mid, dh, 1)
        dw_in = _mm_tn(f"ffn_in{l}_dw", n, dgu, 4)
        dn = _mm_nt_k(f"ffn_in{l}_dx", dgu, wg_in, D, k_blk=l)
        dh_prev, (dg,) = _rmsnorm_bwd(f"norm_ffn{l}_bwd", h, [(g_ffn[l], dn)], dh)
        return dh_prev, dg, dw_in, dw_out

    dh3, dg_ffn1, dw_in1, dw_out1 = ffn_bwd(1, dh4, h3, n3, gu1, mid1)
    d_o = _mm_nt_k("o_proj_dx", dh3, wg_o, D, out_dtype=BF16)
    dw_o = _mm_tn("o_proj_dw", o, dh3, 1)
    dq, dk, dv = _attn_bwd("attn_bwd", q, kv, d_o, ltot)
    dkv = jnp.concatenate([dk, dv], axis=1)
    dw_q = _mm_tn("q_proj_dw", nq, dq, 1)
    dnq = _mm_nt_k("q_proj_dx", dq, wg_q, D)
    dw_kv = _mm_tn("kv_proj_dw", nk, dkv, 4)
    dnk = _mm_nt_k("kv_proj_dx", dkv, wg_kv, D)
    dh2, (dg_mix1, dg_kv) = _rmsnorm_bwd("norm_kv_q_bwd", h2, [(g_mix1, dnq), (g_kv, dnk)], dh3)
    dh1, dg_ffn0, dw_in0, dw_out0 = ffn_bwd(0, dh2, h1, n1, gu0, mid0)
    dvg = _glu_bwd("glu_bwd", val, gate, dh1)
    dw_glu = _mm_tn("glu_proj_dw", z, dvg, 4)
    dz = _mm_nt_k("glu_proj_dx", dvg, wg_glu, D)
    big = [dw_kv, dw_q.reshape(4, D // 4, D), dw_o.reshape(4, D // 4, D), dw_in1, dw_out1.reshape(4, rows_out, D),
           dw_glu, dw_in0, dw_out0.reshape(4, rows_out, D)]
    big_pairs = _pair_sums("big", big)
    (du, dd, dbd_b_re, dbd_b_im, dbd_c_re, dbd_c_im, dab_re, dab_im), big_parts = _s5_bwd(
        "s5_scan_bwd", dz, y, u, x_re, x_im, bd_c_re, bd_c_im, bd_bt_re, bd_bt_im, abar_re, abar_im, d_skip,
        comm=_exchange_comm(big_pairs), comm_args=big_pairs)
    big_parts = _exchange_finish(big_pairs, big_parts)
    dh0, (dg_mix0,) = _rmsnorm_bwd("norm_mix0_bwd", h0, [(g_mix0, du)], dh1)
    da_re, da_im, dlog_dt, dbt_re, dbt_im = _s5_prep_bwd(
        "s5_prep_bwd", a_re3, a_im3, log_dt3, bt_re, bt_im,
        dab_re.reshape(G, 1, P), dab_im.reshape(G, 1, P),
        _block_diag_extract(dbd_b_re, C, P), _block_diag_extract(dbd_b_im, C, P))
    grad_x = dh0[X_START:][None]

    small_parts = [
        dg_mix0, dg_mix1, dg_ffn0, dg_ffn1, da_re, da_im,
        dbt_re.transpose(0, 2, 1), dbt_im.transpose(0, 2, 1),
        _block_diag_extract(dbd_c_re, C, P), _block_diag_extract(dbd_c_im, C, P),
        dg_kv, dg_final, dh0[META_START:X_START], dd, loss_part, dlog_dt]
    small_sizes = [p.size for p in small_parts]
    unit = 4 * 2 * 8 * 128
    padded = -(-sum(small_sizes) // unit) * unit
    tail = jnp.concatenate([loss_part.reshape(-1), dlog_dt.reshape(-1)])
    small_flat = jnp.concatenate(
        [p.reshape(-1) for p in small_parts[:-2]] + [jnp.pad(tail, (0, padded - sum(small_sizes)))])
    small_blocks = small_flat.reshape(4, padded // (4 * 128), 128)
    small_exchanged = _chip_exchange("rs_chip_exchange", _pair_sums("small", [small_blocks]))
    gw_kv, gw_q, gw_o, gw_in1, gw_out1, gw_glu, gw_in0, gw_out0, small_mine = _chip_sums(big_parts + small_exchanged)
    gw_in = jnp.concatenate([gw_in0, gw_in1], axis=0)
    gw_out = jnp.concatenate([gw_out0, gw_out1], axis=0)
    (small_red,) = _all_gather_chips("ag_small_grads", [small_mine])
    small_red = small_red.reshape(-1)
    pieces, at = [], 0
    for p, size in zip(small_parts, small_sizes):
        pieces.append(small_red[at:at + size].reshape(p.shape))
        at += size
    (gn_mix0, gn_mix1, gn_ffn0, gn_ffn1, ga_re, ga_im, gb_re, gb_im, gc_re, gc_im, gn_kv, gn_final,
     gmeta_full, gd_full, loss_all, glog_dt) = pieces
    loss = loss_all[0, 0]
    gn_mix = small_red[:2 * D].reshape(2, D)
    gn_ffn = small_red[2 * D:4 * D].reshape(2, D)
    gmeta = lax.dynamic_slice_in_dim(gmeta_full, chip * dq4, dq4, axis=1)
    gd = lax.dynamic_slice_in_dim(gd_full, chip * dq4, dq4, axis=1)

    grads = {
        "meta_tokens": gmeta, "norm_mix": gn_mix, "norm_ffn": gn_ffn,
        "s5_a_re": ga_re.reshape(s5_a_re.shape), "s5_a_im": ga_im.reshape(s5_a_im.shape),
        "s5_log_dt": glog_dt.reshape(s5_log_dt.shape),
        "s5_b_re": gb_re.reshape(s5_b_re.shape), "s5_b_im": gb_im.reshape(s5_b_im.shape),
        "s5_c_re": gc_re.reshape(s5_c_re.shape), "s5_c_im": gc_im.reshape(s5_c_im.shape),
        "s5_d": gd, "s5_w_glu": gw_glu.reshape(s5_w_glu.shape), "norm_kv": gn_kv.reshape(norm_kv.shape),
        "w_kv": gw_kv, "w_q": gw_q.reshape(w_q.shape), "w_o": gw_o.reshape(w_o.shape),
        "w_ffn_in": gw_in.reshape(w_ffn_in.shape), "w_ffn_out": gw_out.reshape(w_ffn_out.shape),
        "norm_final": gn_final.reshape(norm_final.shape),
    }
    weights = {
        "meta_tokens": (meta_tokens, m_meta_tokens, v_meta_tokens), "norm_mix": (norm_mix, m_norm_mix, v_norm_mix),
        "norm_ffn": (norm_ffn, m_norm_ffn, v_norm_ffn), "s5_a_re": (s5_a_re, m_s5_a_re, v_s5_a_re),
        "s5_a_im": (s5_a_im, m_s5_a_im, v_s5_a_im), "s5_log_dt": (s5_log_dt, m_s5_log_dt, v_s5_log_dt),
        "s5_b_re": (s5_b_re, m_s5_b_re, v_s5_b_re), "s5_b_im": (s5_b_im, m_s5_b_im, v_s5_b_im),
        "s5_c_re": (s5_c_re, m_s5_c_re, v_s5_c_re), "s5_c_im": (s5_c_im, m_s5_c_im, v_s5_c_im),
        "s5_d": (s5_d, m_s5_d, v_s5_d), "s5_w_glu": (s5_w_glu, m_s5_w_glu, v_s5_w_glu),
        "norm_kv": (norm_kv, m_norm_kv, v_norm_kv), "w_kv": (w_kv, m_w_kv, v_w_kv), "w_q": (w_q, m_w_q, v_w_q),
        "w_o": (w_o, m_w_o, v_w_o), "w_ffn_in": (w_ffn_in, m_w_ffn_in, v_w_ffn_in),
        "w_ffn_out": (w_ffn_out, m_w_ffn_out, v_w_ffn_out), "norm_final": (norm_final, m_norm_final, v_norm_final),
    }
    names = list(weights)
    deltas, new_m, new_v = [], [], []
    for name in names:
        w, m, v = weights[name]
        d, nm, nv = _adamw(f"adamw_{name}", w, grads[name], m, v)
        deltas.append(d)
        new_m.append(nm)
        new_v.append(nv)
    return (loss, grad_x, *[grads[n] for n in names], *deltas, *new_m, *new_v)
```

```python
import functools
import math

import jax
import jax.numpy as jnp
from jax import lax
from jax.experimental import pallas as pl
from jax.experimental.pallas import tpu as pltpu

F32 = jnp.float32
BF16 = jnp.bfloat16

N_META = 16
X_START = 128
META_START = X_START - N_META
S5_GROUP = 16
S5_STATE = 64
HEAD_DIM = 64
KEY_BLOCK = 128
STATE_TILE = 512
CH_TILE = 128
RMS_EPS = 1e-6
ADAM_LR, ADAM_B1, ADAM_B2, ADAM_EPS, ADAM_WD, ADAM_STEP = 0.001, 0.9, 0.999, 1e-08, 0.01, 10
VMEM_LIMIT_BYTES = 48 * 1024 * 1024
MESH = pl.DeviceIdType.MESH
NT_DIMS = (((1,), (1,)), ((), ()))
TN_DIMS = (((0,), (0,)), ((), ()))


def _cparams(*sem):
    return pltpu.CompilerParams(dimension_semantics=sem, vmem_limit_bytes=VMEM_LIMIT_BYTES)


def _row_tile(rows, cap):
    for unit in (128, 8):
        best = 0
        for t in range(unit, min(rows, cap) + 1, unit):
            if rows % t == 0:
                best = t
        if best:
            return best
    return rows


def _col_tile(cols, cap):
    best = 0
    for t in range(128, min(cols, cap) + 1, 128):
        if cols % t == 0:
            best = t
    return best if best else cols


def _gelu(x):
    k = math.sqrt(2.0 / math.pi)
    return 0.5 * x * (1.0 + jnp.tanh(k * (x + 0.044715 * x * x * x)))


def _gelu_grad(x):
    k = math.sqrt(2.0 / math.pi)
    t = jnp.tanh(k * (x + 0.044715 * x * x * x))
    return 0.5 * (1.0 + t) + 0.5 * x * (1.0 - t * t) * k * (1.0 + 3.0 * 0.044715 * x * x)


def _sigmoid(x):
    return 1.0 / (1.0 + jnp.exp(-x))


def _rmsnorm_fwd(name, x, gains, out_dtypes):
    T, D = x.shape
    tm = _row_tile(T, 512)
    n = len(gains)

    def body(x_ref, *refs):
        xv = x_ref[...]
        xh = xv * lax.rsqrt(jnp.mean(xv * xv, axis=-1, keepdims=True) + RMS_EPS)
        for g_ref, o_ref in zip(refs[:n], refs[n:]):
            o_ref[...] = (xh * g_ref[...]).astype(o_ref.dtype)

    row = pl.BlockSpec((tm, D), lambda i: (i, 0))
    vec = pl.BlockSpec((1, D), lambda i: (0, 0))
    return pl.pallas_call(
        body, name=name, grid=(T // tm,),
        in_specs=[row] + [vec] * n, out_specs=[row] * n,
        out_shape=[jax.ShapeDtypeStruct((T, D), dt) for dt in out_dtypes],
        compiler_params=_cparams("parallel"),
    )(x, *gains)


def _rmsnorm_bwd(name, x, pairs, dres):
    T, D = x.shape
    tm = _row_tile(T, 512)
    n = len(pairs)

    def body(x_ref, dres_ref, *refs):
        g_refs, dy_refs = refs[:n], refs[n:2 * n]
        dx_ref, dg_refs = refs[2 * n], refs[2 * n + 1:]
        i = pl.program_id(0)
        xv = x_ref[...]
        r = lax.rsqrt(jnp.mean(xv * xv, axis=-1, keepdims=True) + RMS_EPS)
        xh = xv * r
        dxh = jnp.zeros_like(xv)
        for g_ref, dy_ref, dg_ref in zip(g_refs, dy_refs, dg_refs):
            dy = dy_ref[...].astype(F32)
            part = jnp.sum(dy * xh, axis=0, keepdims=True)

            @pl.when(i == 0)
            def _():
                dg_ref[...] = part

            @pl.when(i > 0)
            def _():
                dg_ref[...] += part

            dxh = dxh + dy * g_ref[...]
        dx = r * (dxh - xh * jnp.mean(dxh * xh, axis=-1, keepdims=True))
        dx_ref[...] = dres_ref[...] + dx

    row = pl.BlockSpec((tm, D), lambda i: (i, 0))
    vec = pl.BlockSpec((1, D), lambda i: (0, 0))
    outs = pl.pallas_call(
        body, name=name, grid=(T // tm,),
        in_specs=[row, row] + [vec] * n + [row] * n,
        out_specs=[row] + [vec] * n,
        out_shape=[jax.ShapeDtypeStruct((T, D), F32)] + [jax.ShapeDtypeStruct((1, D), F32)] * n,
        compiler_params=_cparams("arbitrary"),
    )(x, dres, *[g for g, _ in pairs], *[dy for _, dy in pairs])
    return outs[0], outs[1:]


def _mm_nn(name, a, w, k_blk=0, res=None, out_dtype=F32):
    M, K = a.shape
    S, _, Ns = w.shape
    tm = _row_tile(M, 512)
    tn = _col_tile(Ns, 1408)
    nt = Ns // tn

    def body(a_ref, w_ref, *refs):
        o_ref = refs[-1]
        acc = jnp.dot(a_ref[...].astype(BF16), w_ref[...], preferred_element_type=F32)
        if res is not None:
            acc = acc + refs[0][...]
        o_ref[...] = acc.astype(o_ref.dtype)

    in_specs = [pl.BlockSpec((tm, K), lambda j, i: (i, 0)),
                pl.BlockSpec((None, K, tn), lambda j, i: (j // nt, k_blk, j % nt))]
    args = [a, w]
    if res is not None:
        in_specs.append(pl.BlockSpec((tm, tn), lambda j, i: (i, j)))
        args.append(res)
    return pl.pallas_call(
        body, name=name, grid=(S * nt, M // tm),
        in_specs=in_specs, out_specs=pl.BlockSpec((tm, tn), lambda j, i: (i, j)),
        out_shape=jax.ShapeDtypeStruct((M, S * Ns), out_dtype),
        compiler_params=_cparams("parallel", "parallel"),
    )(*args)


def _mm_nt_k(name, dy, w, K, k_blk=0, out_dtype=F32):
    M = dy.shape[0]
    S, _, Ns = w.shape
    tm = _row_tile(M, 1408)
    tn = _col_tile(Ns, 1408)
    nt = Ns // tn
    steps = S * nt

    def body(dy_ref, w_ref, o_ref, acc_ref):
        j = pl.program_id(1)
        part = lax.dot_general(dy_ref[...].astype(BF16), w_ref[...], (((1,), (1,)), ((), ())),
                               preferred_element_type=F32)

        @pl.when(j == 0)
        def _():
            acc_ref[...] = part

        @pl.when(j > 0)
        def _():
            acc_ref[...] += part

        @pl.when(j == steps - 1)
        def _():
            o_ref[...] = acc_ref[...].astype(o_ref.dtype)

    return pl.pallas_call(
        body, name=name, grid=(M // tm, steps),
        in_specs=[pl.BlockSpec((tm, tn), lambda i, j: (i, j)),
                  pl.BlockSpec((None, K, tn), lambda i, j: (j // nt, k_blk, j % nt))],
        out_specs=pl.BlockSpec((tm, K), lambda i, j: (i, 0)),
        out_shape=jax.ShapeDtypeStruct((M, K), out_dtype),
        scratch_shapes=[pltpu.VMEM((tm, K), F32)],
        compiler_params=_cparams("parallel", "arbitrary"),
    )(dy, w)


def _mm_tn(name, a, dy, S, out_dtype=BF16):
    T, K = a.shape
    Ns = dy.shape[1] // S
    tn = _col_tile(Ns, max(128, (6 * 1024 * 1024) // (4 * K) // 128 * 128))
    nt = Ns // tn
    tt = _row_tile(T, 1408)
    steps = T // tt

    def body(a_ref, dy_ref, o_ref, acc_ref):
        t = pl.program_id(1)
        part = lax.dot_general(a_ref[...].astype(BF16), dy_ref[...].astype(BF16), (((0,), (0,)), ((), ())),
                               preferred_element_type=F32)

        @pl.when(t == 0)
        def _():
            acc_ref[...] = part

        @pl.when(t > 0)
        def _():
            acc_ref[...] += part

        @pl.when(t == steps - 1)
        def _():
            o_ref[...] = acc_ref[...].astype(o_ref.dtype)

    return pl.pallas_call(
        body, name=name, grid=(S * nt, steps),
        in_specs=[pl.BlockSpec((tt, K), lambda j, t: (t, 0)),
                  pl.BlockSpec((tt, tn), lambda j, t: (t, j))],
        out_specs=pl.BlockSpec((None, K, tn), lambda j, t: (j // nt, 0, j % nt)),
        out_shape=jax.ShapeDtypeStruct((S, K, Ns), out_dtype),
        scratch_shapes=[pltpu.VMEM((K, tn), F32)],
        compiler_params=_cparams("parallel", "arbitrary"),
    )(a, dy)


def _gated_tile(T, width):
    return _row_tile(T, max(8, (2 * 1024 * 1024) // (4 * width) // 8 * 8))


def _mm_glu(name, a, w, res):
    M, K = a.shape
    Ns = w.shape[2]
    tm = _row_tile(M, 512)

    def body(a_ref, wv_ref, wg_ref, r_ref, v_ref, g_ref, o_ref):
        ab = a_ref[...].astype(BF16)
        val = jnp.dot(ab, wv_ref[...], preferred_element_type=F32)
        gate = jnp.dot(ab, wg_ref[...], preferred_element_type=F32)
        v_ref[...] = val
        g_ref[...] = gate
        o_ref[...] = r_ref[...] + val * _sigmoid(gate)

    out = pl.BlockSpec((tm, Ns), lambda j, i: (i, j))
    return pl.pallas_call(
        body, name=name, grid=(2, M // tm),
        in_specs=[pl.BlockSpec((tm, K), lambda j, i: (i, 0)),
                  pl.BlockSpec((None, K, Ns), lambda j, i: (j, 0, 0)),
                  pl.BlockSpec((None, K, Ns), lambda j, i: (j + 2, 0, 0)), out],
        out_specs=[out, out, out],
        out_shape=[jax.ShapeDtypeStruct((M, 2 * Ns), F32)] * 3,
        compiler_params=_cparams("parallel", "parallel"),
    )(a, w, w, res)


def _glu_bwd(name, val, gate, dout):
    T, D = dout.shape
    tm = _gated_tile(T, 2 * D)

    def body(v_ref, g_ref, d_ref, o_ref):
        s = _sigmoid(g_ref[...])
        d = d_ref[...]
        o_ref[:, :D] = (d * s).astype(o_ref.dtype)
        o_ref[:, D:] = (d * v_ref[...] * s * (1.0 - s)).astype(o_ref.dtype)

    blk = pl.BlockSpec((tm, D), lambda i: (i, 0))
    return pl.pallas_call(
        body, name=name, grid=(T // tm,),
        in_specs=[blk, blk, blk],
        out_specs=pl.BlockSpec((tm, 2 * D), lambda i: (i, 0)),
        out_shape=jax.ShapeDtypeStruct((T, 2 * D), BF16),
        compiler_params=_cparams("parallel"),
    )(val, gate, dout)


def _mm_swiglu(name, a, w, k_blk):
    M, K = a.shape
    Ns = w.shape[2]
    tm = _row_tile(M, 512)

    def body(a_ref, wg_ref, wu_ref, g_ref, u_ref, mid_ref):
        ab = a_ref[...].astype(BF16)
        g = jnp.dot(ab, wg_ref[...], preferred_element_type=F32)
        u = jnp.dot(ab, wu_ref[...], preferred_element_type=F32)
        g_ref[...] = g.astype(g_ref.dtype)
        u_ref[...] = u.astype(u_ref.dtype)
        mid_ref[...] = (g * _sigmoid(g) * u).astype(mid_ref.dtype)

    out = pl.BlockSpec((tm, Ns), lambda j, i: (i, j))
    return pl.pallas_call(
        body, name=name, grid=(2, M // tm),
        in_specs=[pl.BlockSpec((tm, K), lambda j, i: (i, 0)),
                  pl.BlockSpec((None, K, Ns), lambda j, i: (j, k_blk, 0)),
                  pl.BlockSpec((None, K, Ns), lambda j, i: (j + 2, k_blk, 0))],
        out_specs=[out, out, out],
        out_shape=[jax.ShapeDtypeStruct((M, 2 * Ns), BF16)] * 3,
        compiler_params=_cparams("parallel", "parallel"),
    )(a, w, w)


def _mm_nt_swiglu_bwd(name, dh, w, g, u):
    M, D = dh.shape
    F = w.shape[1]
    tm = _row_tile(M, 512)

    def body(dh_ref, w_ref, g_ref, u_ref, o_ref):
        d = lax.dot_general(dh_ref[...].astype(BF16), w_ref[...], NT_DIMS, preferred_element_type=F32)
        gv = g_ref[...].astype(F32)
        s = _sigmoid(gv)
        o_ref[:, :F] = (d * u_ref[...].astype(F32) * s * (1.0 + gv * (1.0 - s))).astype(o_ref.dtype)
        o_ref[:, F:] = (d * gv * s).astype(o_ref.dtype)

    half = pl.BlockSpec((tm, F), lambda i: (i, 0))
    return pl.pallas_call(
        body, name=name, grid=(M // tm,),
        in_specs=[pl.BlockSpec((tm, D), lambda i: (i, 0)), pl.BlockSpec((None, F, D), lambda i: (0, 0, 0)), half, half],
        out_specs=pl.BlockSpec((tm, 2 * F), lambda i: (i, 0)),
        out_shape=jax.ShapeDtypeStruct((M, 2 * F), BF16),
        compiler_params=_cparams("parallel"),
    )(dh, w, g, u)


def _final_loss(name, h, gain, target):
    T, D = h.shape
    tm = X_START
    lead = X_START // tm

    def body(h_ref, g_ref, t_ref, loss_ref, dh_ref, dg_ref):
        i = pl.program_id(0)

        @pl.when(i == 0)
        def _():
            loss_ref[...] = jnp.zeros_like(loss_ref)
            dg_ref[...] = jnp.zeros_like(dg_ref)
            dh_ref[...] = jnp.zeros_like(dh_ref)

        @pl.when(i >= lead)
        def _():
            xv = h_ref[...]
            r = lax.rsqrt(jnp.mean(xv * xv, axis=-1, keepdims=True) + RMS_EPS)
            xh = xv * r
            g = g_ref[...]
            diff = xh * g - t_ref[...]
            loss_ref[...] += 0.5 * jnp.sum(jnp.mean(diff * diff, axis=-1, keepdims=True), axis=0, keepdims=True)
            dout = diff * (1.0 / D)
            dg_ref[...] += jnp.sum(dout * xh, axis=0, keepdims=True)
            dxh = dout * g
            dh_ref[...] = r * (dxh - xh * jnp.mean(dxh * xh, axis=-1, keepdims=True))

    return pl.pallas_call(
        body, name=name, grid=(T // tm,),
        in_specs=[pl.BlockSpec((tm, D), lambda i: (i, 0)), pl.BlockSpec((1, D), lambda i: (0, 0)),
                  pl.BlockSpec((tm, D), lambda i: (jnp.maximum(i - lead, 0), 0))],
        out_specs=[pl.BlockSpec((1, 128), lambda i: (0, 0)), pl.BlockSpec((tm, D), lambda i: (i, 0)),
                   pl.BlockSpec((1, D), lambda i: (0, 0))],
        out_shape=[jax.ShapeDtypeStruct((1, 128), F32), jax.ShapeDtypeStruct((T, D), F32),
                   jax.ShapeDtypeStruct((1, D), F32)],
        compiler_params=_cparams("arbitrary"),
    )(h, gain, target)


def _grid_call(name, body, grid, in_specs, out_specs, out_shape, scratch_shapes, args, comm=None, comm_args=()):
    params = _cparams(*(("arbitrary",) * len(grid)))
    if comm is None:
        return pl.pallas_call(body, name=name, grid=grid, in_specs=in_specs, out_specs=out_specs,
                              out_shape=out_shape, scratch_shapes=scratch_shapes, compiler_params=params)(*args), []
    outs = pl.pallas_call(
        _embed_comm(comm, body, grid, len(in_specs), len(out_specs)), name=name, grid=grid,
        in_specs=list(in_specs) + _any_specs(comm.n), out_specs=list(out_specs) + _any_specs(comm.n),
        out_shape=list(out_shape) + comm.out_shape, scratch_shapes=list(scratch_shapes) + comm.scratch,
        compiler_params=params)(*args, *comm_args)
    return outs[:len(out_specs)], outs[len(out_specs):]


def _s5_discretise(a_re, a_im, log_dt, bt_re, bt_im):
    dt = jnp.exp(log_dt)
    mag = jnp.exp(dt * a_re)
    ang = dt * a_im
    abar_re = mag * jnp.cos(ang)
    abar_im = mag * jnp.sin(ang)
    den = a_re * a_re + a_im * a_im
    coef_re = ((abar_re - 1.0) * a_re + abar_im * a_im) / den
    coef_im = (abar_im * a_re - (abar_re - 1.0) * a_im) / den
    bbar_re = coef_re * bt_re - coef_im * bt_im
    bbar_im = coef_re * bt_im + coef_im * bt_re
    return abar_re, abar_im, bbar_re, bbar_im


def _s5_prep_fwd(name, a_re, a_im, log_dt, bt_re, bt_im):
    G, _, P = a_re.shape
    C = bt_re.shape[1]

    def body(ar, ai, ld, br, bi, o_ar, o_ai, o_br, o_bi):
        outs = _s5_discretise(ar[...], ai[...], ld[...], br[...], bi[...])
        for o, v in zip((o_ar, o_ai, o_br, o_bi), outs):
            o[...] = v

    return pl.pallas_call(
        body, name=name,
        out_shape=[jax.ShapeDtypeStruct((G, 1, P), F32)] * 2 + [jax.ShapeDtypeStruct((G, C, P), F32)] * 2,
    )(a_re, a_im, log_dt, bt_re, bt_im)


def _s5_prep_bwd(name, a_re, a_im, log_dt, bt_re, bt_im, d_ar, d_ai, d_br, d_bi):
    G, _, P = a_re.shape
    C = bt_re.shape[1]

    def body(ar, ai, ld, br, bi, gar, gai, gbr, gbi, o_ar, o_ai, o_ld, o_br, o_bi):
        _, vjp = jax.vjp(_s5_discretise, ar[...], ai[...], ld[...], br[...], bi[...])
        grads = vjp((gar[...], gai[...], gbr[...], gbi[...]))
        for o, v in zip((o_ar, o_ai, o_ld, o_br, o_bi), grads):
            o[...] = v

    return pl.pallas_call(
        body, name=name,
        out_shape=[jax.ShapeDtypeStruct((G, 1, P), F32)] * 2 + [jax.ShapeDtypeStruct((G, 1, 1), F32)]
        + [jax.ShapeDtypeStruct((G, C, P), F32)] * 2,
    )(a_re, a_im, log_dt, bt_re, bt_im, d_ar, d_ai, d_br, d_bi)


def _cmul(ar, ai, br, bi):
    return ar * br - ai * bi, ar * bi + ai * br


def _power_table(a_re, a_im):
    rows_re, rows_im = [a_re], [a_im]
    for _ in range(7):
        r, m = _cmul(rows_re[-1], rows_im[-1], a_re, a_im)
        rows_re.append(r)
        rows_im.append(m)
    row = lax.broadcasted_iota(jnp.int32, (8, a_re.shape[1]), 0)
    t_re = jnp.zeros((8, a_re.shape[1]), F32)
    t_im = jnp.zeros((8, a_re.shape[1]), F32)
    for k in range(8):
        t_re = jnp.where(row == k, rows_re[k], t_re)
        t_im = jnp.where(row == k, rows_im[k], t_im)
    return t_re, t_im, rows_re, rows_im


def _s5_fwd(name, u, b_re, b_im, ct_re, ct_im, abar_re, abar_im, d_skip, comm=None, comm_args=()):
    T, D = u.shape
    n_st = D // CH_TILE
    W = STATE_TILE
    tc = _row_tile(T, 512)
    n_tiles = tc // 8

    def body(u_ref, bre_ref, bim_ref, cre_ref, cim_ref, ar_ref, ai_ref, d_ref,
             y_ref, z_ref, xr_ref, xi_ref, carry_re, carry_im, pw_re, pw_im, sh_re, sh_im):
        c = pl.program_id(1)

        @pl.when(c == 0)
        def _():
            t_re, t_im, rows_re, rows_im = _power_table(ar_ref[...], ai_ref[...])
            pw_re[...] = t_re
            pw_im[...] = t_im
            first_rows = lax.broadcasted_iota(jnp.int32, (8, W), 0)
            for n, d in enumerate((1, 2, 4)):
                sh_re[n] = jnp.where(first_rows >= d, jnp.broadcast_to(rows_re[d - 1], (8, W)), 0.0)
                sh_im[n] = jnp.where(first_rows >= d, jnp.broadcast_to(rows_im[d - 1], (8, W)), 0.0)
            carry_re[...] = jnp.zeros_like(carry_re)
            carry_im[...] = jnp.zeros_like(carry_im)

        ub = u_ref[...].astype(BF16)
        xr_ref[...] = jnp.dot(ub, bre_ref[...], preferred_element_type=F32)
        xi_ref[...] = jnp.dot(ub, bim_ref[...], preferred_element_type=F32)
        row = lax.broadcasted_iota(jnp.int32, (8, W), 0)

        def tile(i, carry):
            c_re, c_im = carry
            rows = pl.ds(pl.multiple_of(i * 8, 8), 8)
            r = xr_ref[rows, :]
            m = xi_ref[rows, :]
            for n, d in enumerate((1, 2, 4)):
                pr, pm = _cmul(sh_re[n], sh_im[n], pltpu.roll(r, d, 0), pltpu.roll(m, d, 0))
                r = r + pr
                m = m + pm
            pr, pm = _cmul(pw_re[...], pw_im[...], c_re, c_im)
            r = r + pr
            m = m + pm
            xr_ref[rows, :] = r
            xi_ref[rows, :] = m
            return jnp.broadcast_to(r[7:8, :], (8, W)), jnp.broadcast_to(m[7:8, :], (8, W))

        c_re, c_im = lax.fori_loop(0, n_tiles, tile, (carry_re[...], carry_im[...]))
        carry_re[...] = c_re
        carry_im[...] = c_im
        y = (jnp.dot(xr_ref[...].astype(BF16), cre_ref[...], preferred_element_type=F32)
             - jnp.dot(xi_ref[...].astype(BF16), cim_ref[...], preferred_element_type=F32)
             + d_ref[...] * u_ref[...])
        y_ref[...] = y
        z_ref[...] = _gelu(y).astype(z_ref.dtype)

    ch = pl.BlockSpec((tc, CH_TILE), lambda s, c: (c, s))
    st = pl.BlockSpec((tc, W), lambda s, c: (c, s))
    return _grid_call(
        name, body, (n_st, T // tc),
        in_specs=[ch,
                  pl.BlockSpec((None, CH_TILE, W), lambda s, c: (s, 0, 0)),
                  pl.BlockSpec((None, CH_TILE, W), lambda s, c: (s, 0, 0)),
                  pl.BlockSpec((None, W, CH_TILE), lambda s, c: (s, 0, 0)),
                  pl.BlockSpec((None, W, CH_TILE), lambda s, c: (s, 0, 0)),
                  pl.BlockSpec((1, W), lambda s, c: (0, s)),
                  pl.BlockSpec((1, W), lambda s, c: (0, s)),
                  pl.BlockSpec((1, CH_TILE), lambda s, c: (0, s))],
        out_specs=[ch, ch, st, st],
        out_shape=[jax.ShapeDtypeStruct((T, D), F32), jax.ShapeDtypeStruct((T, D), BF16),
                   jax.ShapeDtypeStruct((T, 4 * D), F32), jax.ShapeDtypeStruct((T, 4 * D), F32)],
        scratch_shapes=[pltpu.VMEM((8, W), F32), pltpu.VMEM((8, W), F32),
                        pltpu.VMEM((8, W), F32), pltpu.VMEM((8, W), F32),
                        pltpu.VMEM((3, 8, W), F32), pltpu.VMEM((3, 8, W), F32)],
        args=(u, b_re, b_im, ct_re, ct_im, abar_re, abar_im, d_skip), comm=comm, comm_args=comm_args)


def _s5_bwd(name, dz, y, u, x_re, x_im, c_re, c_im, bt_re, bt_im, abar_re, abar_im, d_skip, comm=None, comm_args=()):
    T, D = u.shape
    n_st = D // CH_TILE
    W = STATE_TILE
    tc = _row_tile(T, 512)
    n_chunks = T // tc
    n_tiles = tc // 8
    tiles_per_chunk = tc // 8

    def body(dz_ref, y_ref, u_ref, xr_ref, xi_ref, xpr_ref, xpi_ref, cre_ref, cim_ref, btr_ref, bti_ref,
             ar_ref, ai_ref, d_ref,
             du_ref, dd_ref, dbr_ref, dbi_ref, dcr_ref, dci_ref, dar_ref, dai_ref,
             lam_re, lam_im, xe_re, xe_im, carry_re, carry_im, pw_re, pw_im, sh_re, sh_im, acc_ar, acc_ai):
        k = pl.program_id(1)
        first_chunk = k == n_chunks - 1

        @pl.when(k == 0)
        def _():
            t_re, t_im, rows_re, rows_im = _power_table(ar_ref[...], -ai_ref[...])
            row = lax.broadcasted_iota(jnp.int32, (8, W), 0)
            r_re = jnp.zeros((8, W), F32)
            r_im = jnp.zeros((8, W), F32)
            for j in range(8):
                r_re = jnp.where(row == j, rows_re[7 - j], r_re)
                r_im = jnp.where(row == j, rows_im[7 - j], r_im)
            pw_re[...] = r_re
            pw_im[...] = r_im
            for n, d in enumerate((1, 2, 4)):
                sh_re[n] = jnp.where(row < 8 - d, jnp.broadcast_to(rows_re[d - 1], (8, W)), 0.0)
                sh_im[n] = jnp.where(row < 8 - d, jnp.broadcast_to(rows_im[d - 1], (8, W)), 0.0)
            carry_re[...] = jnp.zeros_like(carry_re)
            carry_im[...] = jnp.zeros_like(carry_im)
            acc_ar[...] = jnp.zeros_like(acc_ar)
            acc_ai[...] = jnp.zeros_like(acc_ai)
            dd_ref[...] = jnp.zeros_like(dd_ref)
            dbr_ref[...] = jnp.zeros_like(dbr_ref)
            dbi_ref[...] = jnp.zeros_like(dbi_ref)
            dcr_ref[...] = jnp.zeros_like(dcr_ref)
            dci_ref[...] = jnp.zeros_like(dci_ref)

        uv = u_ref[...]
        dy = dz_ref[...] * _gelu_grad(y_ref[...])
        dyb = dy.astype(BF16)
        lam_re[...] = jnp.dot(dyb, cre_ref[...], preferred_element_type=F32)
        lam_im[...] = -jnp.dot(dyb, cim_ref[...], preferred_element_type=F32)
        keep = jnp.where(first_chunk, 0.0, 1.0)
        xe_re[pl.ds(0, 8), :] = xpr_ref[...] * keep
        xe_im[pl.ds(0, 8), :] = xpi_ref[...] * keep
        xe_re[pl.ds(8, tc), :] = xr_ref[...]
        xe_im[pl.ds(8, tc), :] = xi_ref[...]
        row = lax.broadcasted_iota(jnp.int32, (8, W), 0)

        def tile(n, carry):
            c_re, c_im, s_ar, s_ai = carry
            i = n_tiles - 1 - n
            rows = pl.ds(pl.multiple_of(i * 8, 8), 8)
            r = lam_re[rows, :]
            m = lam_im[rows, :]
            for q, d in enumerate((1, 2, 4)):
                pr, pm = _cmul(sh_re[q], sh_im[q], pltpu.roll(r, 8 - d, 0), pltpu.roll(m, 8 - d, 0))
                r = r + pr
                m = m + pm
            pr, pm = _cmul(pw_re[...], pw_im[...], c_re, c_im)
            r = r + pr
            m = m + pm
            lam_re[rows, :] = r
            lam_im[rows, :] = m
            cur_re = xe_re[pl.ds(pl.multiple_of(i * 8 + 8, 8), 8), :]
            cur_im = xe_im[pl.ds(pl.multiple_of(i * 8 + 8, 8), 8), :]
            bef_re = xe_re[rows, :]
            bef_im = xe_im[rows, :]
            xp_re = jnp.where(row == 0, jnp.broadcast_to(bef_re[7:8, :], (8, W)), pltpu.roll(cur_re, 1, 0))
            xp_im = jnp.where(row == 0, jnp.broadcast_to(bef_im[7:8, :], (8, W)), pltpu.roll(cur_im, 1, 0))
            s_ar = s_ar + r * xp_re + m * xp_im
            s_ai = s_ai + m * xp_re - r * xp_im
            return jnp.broadcast_to(r[0:1, :], (8, W)), jnp.broadcast_to(m[0:1, :], (8, W)), s_ar, s_ai

        c_re, c_im, s_ar, s_ai = lax.fori_loop(
            0, n_tiles, tile, (carry_re[...], carry_im[...], acc_ar[...], acc_ai[...]))
        carry_re[...] = c_re
        carry_im[...] = c_im
        acc_ar[...] = s_ar
        acc_ai[...] = s_ai
        lr = lam_re[...].astype(BF16)
        li = lam_im[...].astype(BF16)
        du_ref[...] = (dy * d_ref[...] + jnp.dot(lr, btr_ref[...], preferred_element_type=F32)
                       + jnp.dot(li, bti_ref[...], preferred_element_type=F32))
        dd_ref[...] += jnp.sum(dy * uv, axis=0, keepdims=True)
        tn_dims = (((0,), (0,)), ((), ()))
        ub = uv.astype(BF16)
        dbr_ref[...] += lax.dot_general(ub, lr, tn_dims, preferred_element_type=F32)
        dbi_ref[...] += lax.dot_general(ub, li, tn_dims, preferred_element_type=F32)
        dcr_ref[...] += lax.dot_general(dyb, xr_ref[...].astype(BF16), tn_dims, preferred_element_type=F32)
        dci_ref[...] -= lax.dot_general(dyb, xi_ref[...].astype(BF16), tn_dims, preferred_element_type=F32)

        @pl.when(first_chunk)
        def _():
            dar_ref[...] = jnp.sum(acc_ar[...], axis=0, keepdims=True)
            dai_ref[...] = jnp.sum(acc_ai[...], axis=0, keepdims=True)

    rev = lambda k: n_chunks - 1 - k
    ch = pl.BlockSpec((tc, CH_TILE), lambda s, k: (rev(k), s))
    st = pl.BlockSpec((tc, W), lambda s, k: (rev(k), s))
    prev = pl.BlockSpec((8, W), lambda s, k: (jnp.maximum(rev(k) * tiles_per_chunk - 1, 0), s))
    mat_cw = pl.BlockSpec((None, CH_TILE, W), lambda s, k: (s, 0, 0))
    mat_wc = pl.BlockSpec((None, W, CH_TILE), lambda s, k: (s, 0, 0))
    vec_w = pl.BlockSpec((1, W), lambda s, k: (0, s))
    vec_c = pl.BlockSpec((1, CH_TILE), lambda s, k: (0, s))
    dense = jax.ShapeDtypeStruct((n_st, CH_TILE, W), F32)
    return _grid_call(
        name, body, (n_st, n_chunks),
        in_specs=[ch, ch, ch, st, st, prev, prev, mat_cw, mat_cw, mat_wc, mat_wc, vec_w, vec_w, vec_c],
        out_specs=[ch, vec_c, mat_cw, mat_cw, mat_cw, mat_cw, vec_w, vec_w],
        out_shape=[jax.ShapeDtypeStruct((T, D), F32), jax.ShapeDtypeStruct((1, D), F32), dense, dense, dense, dense,
                   jax.ShapeDtypeStruct((1, 4 * D), F32), jax.ShapeDtypeStruct((1, 4 * D), F32)],
        scratch_shapes=[pltpu.VMEM((tc, W), F32), pltpu.VMEM((tc, W), F32),
                        pltpu.VMEM((tc + 8, W), F32), pltpu.VMEM((tc + 8, W), F32),
                        pltpu.VMEM((8, W), F32), pltpu.VMEM((8, W), F32),
                        pltpu.VMEM((8, W), F32), pltpu.VMEM((8, W), F32),
                        pltpu.VMEM((3, 8, W), F32), pltpu.VMEM((3, 8, W), F32),
                        pltpu.VMEM((8, W), F32), pltpu.VMEM((8, W), F32)],
        args=(dz, y, u, x_re, x_im, x_re, x_im, c_re, c_im, bt_re, bt_im, abar_re, abar_im, d_skip),
        comm=comm, comm_args=comm_args)


def _sums_matrix(strictly_later, copies):
    jj = lax.broadcasted_iota(jnp.int32, (copies * KEY_BLOCK, 2 * KEY_BLOCK), 0) & (KEY_BLOCK - 1)
    ss = lax.broadcasted_iota(jnp.int32, (copies * KEY_BLOCK, 2 * KEY_BLOCK), 1)
    tri = (jj > ss) if strictly_later else (jj < ss)
    return (tri | (ss >= KEY_BLOCK)).astype(BF16)


def _split_heads(blk):
    first = lax.broadcasted_iota(jnp.int32, blk.shape, 1) < HEAD_DIM
    zero = jnp.zeros_like(blk)
    return jnp.concatenate([jnp.where(first, blk, zero), jnp.where(first, zero, blk)], axis=0)


LOG2_E = 1.4426950408889634


def _sb_scores(z, mask, later):
    z2 = z * LOG2_E
    minus_abs = lax.bitcast_convert_type(lax.bitcast_convert_type(z2, jnp.uint32) | jnp.uint32(0x80000000), F32)
    lb = jnp.minimum(z2, 0.0) - jnp.log2(1.0 + jnp.exp2(minus_abs))
    lm = lb - z2
    if mask is not None:
        lm = jnp.where(mask, lm, 0.0)
    hi = lm.astype(BF16)
    lo = (lm - hi.astype(F32)).astype(BF16)
    return lb, jnp.dot(jnp.concatenate([hi, lo], axis=1), later, preferred_element_type=F32)


def _key_rows(kb):
    return pl.ds(pl.multiple_of(kb * KEY_BLOCK, KEY_BLOCK), KEY_BLOCK)


def _sb_mask(q_row0, k_row0, tq):
    tpos = q_row0 + lax.broadcasted_iota(jnp.int32, (tq, KEY_BLOCK), 0)
    spos = k_row0 + lax.broadcasted_iota(jnp.int32, (tq, KEY_BLOCK), 1)
    return (spos < tpos) & (spos >= META_START)


def _key_block_phases(iq, per_q, block, ascending, per_step):
    first_diag = iq * per_q

    def nth(lo, n, i):
        return lo + i if ascending else lo + n - 1 - i

    def run(lo, n, masked):
        if isinstance(n, int):
            for i in range(n):
                block(nth(lo, n, i), masked)
            return

        def group(i, carry):
            for j in range(per_step):
                block(nth(lo, n, per_step * i + j), masked)
            return carry

        def single(i, carry):
            block(nth(lo, n, n - rest + i), masked)
            return carry

        rest = jnp.bitwise_and(n, per_step - 1)
        lax.fori_loop(0, jnp.right_shift(n, per_step.bit_length() - 1), group, 0)
        lax.fori_loop(0, rest, single, 0)

    phases = [(0, jnp.minimum(iq, 1), True), (1, jnp.maximum(first_diag - 1, 0), False), (first_diag, per_q, True)]
    for lo, n, masked in (phases if ascending else phases[::-1]):
        run(lo, n, masked)


def _attn_fwd(name, q, kv):
    T, D = q.shape
    n_hp = D // 128
    tq = _row_tile(T, 512)
    per_q = tq // KEY_BLOCK
    scale = 1.0 / math.sqrt(HEAD_DIM)

    half = tq // 2
    chains = [(h, r) for h in range(2) for r in range(2)]
    rows = lambda r: slice(r * half, (r + 1) * half)
    cols = lambda h: slice(h * KEY_BLOCK, (h + 1) * KEY_BLOCK)

    def body(q_ref, k_ref, v_ref, o_ref, l_ref, z_buf, w_buf, acc_ref, run_ref):
        iq = pl.program_id(1)
        n_kb = (iq + 1) * per_q
        qs = q_ref[...] * jnp.asarray(scale, BF16)
        later = _sums_matrix(True, 2)

        def scores(kb):
            return lax.dot_general(qs, _split_heads(k_ref[_key_rows(kb), :]), NT_DIMS, preferred_element_type=F32)

        def weighted_values(kb):
            return jnp.dot(w_buf[...], _split_heads(v_ref[_key_rows(kb), :]), preferred_element_type=F32)

        acc_ref[...] = jnp.zeros_like(acc_ref)
        run_ref[...] = jnp.zeros_like(run_ref)
        w_buf[...] = jnp.zeros_like(w_buf)
        z_buf[...] = scores(n_kb - 1)

        def block(kb, masked):
            acc_ref[...] += weighted_values(jnp.minimum(kb + 1, n_kb - 1))
            z_next = scores(jnp.maximum(kb - 1, 0))
            masks = [_sb_mask(iq * tq + r * half, kb * KEY_BLOCK, half) if masked else None for r in range(2)]
            first = [_sb_scores(z_buf[rows(r), cols(h)], masks[r], later) for h, r in chains]
            for (h, r), (lb, sums) in zip(chains, first):
                run = run_ref[h, rows(r), :]
                w = jnp.exp2(lb + sums[:, :KEY_BLOCK] + run)
                if masked:
                    w = jnp.where(masks[r], w, 0.0)
                w_buf[rows(r), cols(h)] = w.astype(BF16)
                run_ref[h, rows(r), :] = run + sums[:, KEY_BLOCK:]
            z_buf[...] = z_next

        _key_block_phases(iq, per_q, block, ascending=False, per_step=4)
        acc_ref[...] += weighted_values(0)
        lane = lax.broadcasted_iota(jnp.int32, (tq, 128), 1)
        o_ref[...] = acc_ref[...].astype(o_ref.dtype)
        l_ref[...] = jnp.where(lane < HEAD_DIM, run_ref[0], run_ref[1])

    blk = pl.BlockSpec((tq, 128), lambda h, i: (i, h))
    return pl.pallas_call(
        body, name=name, grid=(n_hp, T // tq),
        in_specs=[blk, pl.BlockSpec((T, 128), lambda h, i: (0, h)), pl.BlockSpec((T, 128), lambda h, i: (0, n_hp + h))],
        out_specs=[blk, blk],
        out_shape=[jax.ShapeDtypeStruct((T, D), BF16), jax.ShapeDtypeStruct((T, D), F32)],
        scratch_shapes=[pltpu.VMEM((tq, 2 * KEY_BLOCK), F32), pltpu.VMEM((tq, 2 * KEY_BLOCK), BF16),
                        pltpu.VMEM((tq, 128), F32), pltpu.VMEM((2, tq, 128), F32)],
        compiler_params=_cparams("parallel", "arbitrary"),
    )(q, kv, kv)


def _attn_bwd(name, q, kv, do, ltot):
    T, D = q.shape
    n_hp = D // 128
    tq = _row_tile(T, 512)
    n_q = T // tq
    per_q = tq // KEY_BLOCK
    scale = 1.0 / math.sqrt(HEAD_DIM)

    half = tq // 2
    chains = [(h, r) for h in range(2) for r in range(2)]
    row_half = lambda r: slice(r * half, (r + 1) * half)
    head_cols = lambda h: slice(h * KEY_BLOCK, (h + 1) * KEY_BLOCK)
    stacked = lambda h, r: slice(h * tq + r * half, h * tq + (r + 1) * half)

    def body(q_ref, k_ref, v_ref, do_ref, l_ref, dq_ref, dk_ref, dv_ref,
             dk_acc, dv_acc, dq_acc, lpre_ref, cpre_ref, z_buf, dw_buf, dz_buf, w_buf):
        iq = pl.program_id(1)
        n_kb = (iq + 1) * per_q

        @pl.when(iq == 0)
        def _():
            dk_acc[...] = jnp.zeros_like(dk_acc)
            dv_acc[...] = jnp.zeros_like(dv_acc)

        first = lax.broadcasted_iota(jnp.int32, (tq, 128), 1) < HEAD_DIM
        qs = q_ref[...] * jnp.asarray(scale, BF16)
        dov = do_ref[...]
        ltv = l_ref[...]
        swapped = pltpu.roll(ltv, HEAD_DIM, 1)
        ltot = [jnp.where(first, ltv, swapped), jnp.where(first, swapped, ltv)]
        zero = jnp.zeros_like(qs)
        q_stack = jnp.concatenate([jnp.where(first, qs, zero), jnp.where(first, zero, qs)], axis=0)
        do_stack = jnp.concatenate([jnp.where(first, dov, zero), jnp.where(first, zero, dov)], axis=0)
        later = _sums_matrix(True, 2)
        earlier = _sums_matrix(False, 1)

        def scores(kb):
            rows = _key_rows(kb)
            return (lax.dot_general(qs, _split_heads(k_ref[rows, :]), NT_DIMS, preferred_element_type=F32),
                    lax.dot_general(dov, _split_heads(v_ref[rows, :]), NT_DIMS, preferred_element_type=F32))

        def flush(kb):
            rows = _key_rows(kb)
            k_heads = _split_heads(k_ref[rows, :])
            dq_acc[...] += (jnp.dot(dz_buf[:tq, :], k_heads[:KEY_BLOCK, :], preferred_element_type=F32)
                            + jnp.dot(dz_buf[tq:, :], k_heads[KEY_BLOCK:, :], preferred_element_type=F32))
            dk_acc[rows, :] += lax.dot_general(dz_buf[...], q_stack, TN_DIMS, preferred_element_type=F32)
            dv_acc[rows, :] += lax.dot_general(w_buf[...], do_stack, TN_DIMS, preferred_element_type=F32)

        dq_acc[...] = jnp.zeros_like(dq_acc)
        lpre_ref[...] = jnp.zeros_like(lpre_ref)
        cpre_ref[...] = jnp.zeros_like(cpre_ref)
        dz_buf[...] = jnp.zeros_like(dz_buf)
        w_buf[...] = jnp.zeros_like(w_buf)
        z_buf[...], dw_buf[...] = scores(0)

        def block(kb, masked):
            flush(jnp.maximum(kb - 1, 0))
            z_next, dw_next = scores(jnp.minimum(kb + 1, n_kb - 1))
            masks = [_sb_mask(iq * tq + r * half, kb * KEY_BLOCK, half) if masked else None for r in range(2)]
            first_stage = [_sb_scores(z_buf[row_half(r), head_cols(h)], masks[r], later) for h, r in chains]
            second_stage = []
            for (h, r), (lb, sums) in zip(chains, first_stage):
                after = ltot[h][row_half(r), :] - lpre_ref[h, row_half(r), :] - sums[:, KEY_BLOCK:]
                w = jnp.exp2(lb + sums[:, :KEY_BLOCK] + after)
                if masked:
                    w = jnp.where(masks[r], w, 0.0)
                da = w * dw_buf[row_half(r), head_cols(h)]
                w_buf[stacked(h, r), :] = w.astype(BF16)
                lpre_ref[h, row_half(r), :] += sums[:, KEY_BLOCK:]
                second_stage.append((da, jnp.dot(da.astype(BF16), earlier, preferred_element_type=F32)))
            for (h, r), (lb, _), (da, dsums) in zip(chains, first_stage, second_stage):
                sig = jnp.exp2(lb)
                through_later = sig * (dsums[:, :KEY_BLOCK] + cpre_ref[h, row_half(r), :])
                if masked:
                    through_later = jnp.where(masks[r], through_later, 0.0)
                dz_buf[stacked(h, r), :] = (da * (1.0 - sig) - through_later).astype(BF16)
                cpre_ref[h, row_half(r), :] += dsums[:, KEY_BLOCK:]
            z_buf[...] = z_next
            dw_buf[...] = dw_next

        _key_block_phases(iq, per_q, block, ascending=True, per_step=2)
        flush(n_kb - 1)
        dq_ref[...] = (dq_acc[...] * scale).astype(dq_ref.dtype)

        @pl.when(iq == n_q - 1)
        def _():
            dk_ref[...] = dk_acc[...].astype(dk_ref.dtype)
            dv_ref[...] = dv_acc[...].astype(dv_ref.dtype)

    blk = pl.BlockSpec((tq, 128), lambda h, i: (i, h))
    full = pl.BlockSpec((T, 128), lambda h, i: (0, h))
    return pl.pallas_call(
        body, name=name, grid=(n_hp, n_q),
        in_specs=[blk, full, pl.BlockSpec((T, 128), lambda h, i: (0, n_hp + h)), blk, blk],
        out_specs=[blk, full, full],
        out_shape=[jax.ShapeDtypeStruct((T, D), BF16)] * 3,
        scratch_shapes=[pltpu.VMEM((T, 128), F32), pltpu.VMEM((T, 128), F32), pltpu.VMEM((tq, 128), F32),
                        pltpu.VMEM((2, tq, 128), F32), pltpu.VMEM((2, tq, 128), F32),
                        pltpu.VMEM((tq, 2 * KEY_BLOCK), F32), pltpu.VMEM((tq, 2 * KEY_BLOCK), F32),
                        pltpu.VMEM((2 * tq, 128), BF16), pltpu.VMEM((2 * tq, 128), BF16)],
        compiler_params=_cparams("parallel", "arbitrary"),
    )(q, kv, kv, do, ltot)


def _adamw(name, w, g, m, v):
    shape = w.shape
    size = w.size
    if w.ndim >= 2 and shape[-1] % 128 == 0:
        cols = shape[-1]
    else:
        cols = 1024 if size % 1024 == 0 else shape[-1]
    rows = size // cols
    tm = _row_tile(rows, max(8, (1024 * 1024) // (4 * cols) // 8 * 8)) if rows % 8 == 0 else rows
    c1 = 1.0 / (1.0 - ADAM_B1 ** ADAM_STEP)
    c2 = 1.0 / (1.0 - ADAM_B2 ** ADAM_STEP)

    def body(w_ref, g_ref, m_ref, v_ref, d_ref, nm_ref, nv_ref):
        gv = g_ref[...]
        nm = ADAM_B1 * m_ref[...] + (1.0 - ADAM_B1) * gv
        nv = ADAM_B2 * v_ref[...] + (1.0 - ADAM_B2) * (gv * gv)
        d_ref[...] = -ADAM_LR * ((nm * c1) / (jnp.sqrt(nv * c2) + ADAM_EPS) + ADAM_WD * w_ref[...])
        nm_ref[...] = nm
        nv_ref[...] = nv

    blk = pl.BlockSpec((tm, cols), lambda i: (i, 0))
    outs = pl.pallas_call(
        body, name=name, grid=(rows // tm,),
        in_specs=[blk] * 4, out_specs=[blk] * 3,
        out_shape=[jax.ShapeDtypeStruct((rows, cols), F32)] * 3,
        compiler_params=_cparams("parallel"),
    )(*[t.reshape(rows, cols) for t in (w, g, m, v)])
    return tuple(o.reshape(shape) for o in outs)


def _any_specs(n):
    return [pl.BlockSpec(memory_space=pl.ANY)] * n


def _chip_index():
    return 2 * lax.axis_index("x") + lax.axis_index("y")


def _place():
    x, y, c = lax.axis_index("x"), lax.axis_index("y"), lax.axis_index("c")
    chips = [(1 - x, y), (x, 1 - y), (1 - x, 1 - y)]
    return x, y, c, chips


def _all_gather_chips(name, shards):
    n = len(shards)

    def body(*refs):
        x_refs, o_refs = refs[:n], refs[n:2 * n]
        send_sems, recv_sems = refs[2 * n:]
        x, y, c, chips = _place()
        me = 2 * x + y
        sibling = (x, y, 1 - c)

        def half(ref, i, which):
            h = shards[i].shape[0] // 2
            return ref.at[pl.ds(which * h, h)]

        def remote(k, i, src, dst, to):
            return pltpu.make_async_remote_copy(src_ref=src, dst_ref=dst, send_sem=send_sems.at[k, i],
                                                recv_sem=recv_sems.at[k, i], device_id=to, device_id_type=MESH)

        sent = []
        for j, chip in enumerate(chips):
            for i in range(n):
                cp = remote(j, i, half(x_refs[i], i, c), half(o_refs[i].at[me], i, c), (*chip, c))
                cp.start()
                sent.append(cp)
        for j, chip in enumerate(chips):
            pj = 2 * chip[0] + chip[1]
            for i in range(n):
                landed = half(o_refs[i].at[pj], i, c)
                remote(j, i, landed, landed, (*chip, c)).wait_recv()
                cp = remote(3 + j, i, landed, landed, sibling)
                cp.start()
                sent.append(cp)
        for j, chip in enumerate(chips):
            pj = 2 * chip[0] + chip[1]
            for i in range(n):
                got = half(o_refs[i].at[pj], i, 1 - c)
                remote(3 + j, i, got, got, sibling).wait_recv()
        for cp in sent:
            cp.wait_send()

    outs = pl.pallas_call(
        body, name=name,
        in_specs=_any_specs(n), out_specs=_any_specs(n),
        out_shape=[jax.ShapeDtypeStruct((4,) + s.shape, s.dtype) for s in shards],
        scratch_shapes=[pltpu.SemaphoreType.DMA((6, n)), pltpu.SemaphoreType.DMA((6, n))],
    )(*shards)
    return [lax.dynamic_update_slice(o, s[None], (_chip_index(), 0, 0)) for o, s in zip(outs, shards)]


def _pair_split(name, grads):
    n = len(grads)

    def body(*refs):
        g_refs, got_refs = refs[:n], refs[n:2 * n]
        send_sems, recv_sems = refs[2 * n:]
        x, y, c, _ = _place()
        sibling = (x, y, 1 - c)
        sent = []
        for i in range(n):
            h = grads[i].shape[1] // 2
            rc = pltpu.make_async_remote_copy(
                src_ref=g_refs[i].at[:, pl.ds((1 - c) * h, h)], dst_ref=got_refs[i],
                send_sem=send_sems.at[i], recv_sem=recv_sems.at[i], device_id=sibling, device_id_type=MESH)
            rc.start()
            sent.append(rc)
        for rc in sent:
            rc.wait()

    return pl.pallas_call(
        body, name=name,
        in_specs=_any_specs(n), out_specs=_any_specs(n),
        out_shape=[jax.ShapeDtypeStruct((4, g.shape[1] // 2, g.shape[2]), g.dtype) for g in grads],
        scratch_shapes=[pltpu.SemaphoreType.DMA((n,)), pltpu.SemaphoreType.DMA((n,))],
    )(*grads)


def _chip_exchange(name, sums):
    return _exchange_finish(sums, _run_comm(name, _exchange_comm(sums), sums))


class _Comm:
    def __init__(self, out_shape, copies):
        self.n = len(out_shape)
        self.out_shape = out_shape
        self.copies = copies
        self.scratch = [pltpu.SemaphoreType.DMA((3, self.n)), pltpu.SemaphoreType.DMA((3, self.n))]

    def start(self, *refs):
        for cp in self.copies(*refs, False):
            cp.start()

    def finish(self, *refs):
        for cp in self.copies(*refs, True):
            cp.wait_recv()
        for cp in self.copies(*refs, False):
            cp.wait_send()


def _exchange_comm(sums):
    n = len(sums)

    def copies(s_refs, o_refs, send_sems, recv_sems, mirrors):
        x, y, c, chips = _place()
        me = 2 * x + y
        out = []
        for j, chip in enumerate(chips):
            pj = 2 * chip[0] + chip[1]
            for i in range(n):
                out.append(pltpu.make_async_remote_copy(
                    src_ref=s_refs[i].at[pj], dst_ref=o_refs[i].at[pj if mirrors else me], send_sem=send_sems.at[j, i],
                    recv_sem=recv_sems.at[j, i], device_id=(*chip, c), device_id_type=MESH))
        return out

    return _Comm([jax.ShapeDtypeStruct(s.shape, s.dtype) for s in sums], copies)


def _exchange_finish(sums, outs):
    me = _chip_index()
    return [lax.dynamic_update_slice(o, lax.dynamic_index_in_dim(s, me, 0, keepdims=True), (me, 0, 0))
            for o, s in zip(outs, sums)]


def _halves_gather_comm(shards):
    n = len(shards)

    def copies(x_refs, o_refs, send_sems, recv_sems, mirrors):
        x, y, c, chips = _place()
        me = 2 * x + y
        out = []
        for j, chip in enumerate(chips):
            pj = 2 * chip[0] + chip[1]
            for i in range(n):
                h = shards[i].shape[0] // 2
                rows = pl.ds(c * h, h)
                out.append(pltpu.make_async_remote_copy(
                    src_ref=x_refs[i].at[rows], dst_ref=o_refs[i].at[pj if mirrors else me, rows],
                    send_sem=send_sems.at[j, i], recv_sem=recv_sems.at[j, i], device_id=(*chip, c), device_id_type=MESH))
        return out

    return _Comm([jax.ShapeDtypeStruct((4,) + s.shape, s.dtype) for s in shards], copies)


def _run_comm(name, comm, arrays):
    n = comm.n

    def body(*refs):
        comm.start(refs[:n], refs[n:2 * n], *refs[2 * n:])
        comm.finish(refs[:n], refs[n:2 * n], *refs[2 * n:])

    return pl.pallas_call(
        body, name=name, in_specs=_any_specs(n), out_specs=_any_specs(n),
        out_shape=comm.out_shape, scratch_shapes=comm.scratch,
    )(*arrays)


def _embed_comm(comm, body, grid, n_in, n_out):
    n = comm.n

    def wrapped(*refs):
        ins, c_in = refs[:n_in], refs[n_in:n_in + n]
        outs, c_out = refs[n_in + n:n_in + n + n_out], refs[n_in + n + n_out:n_in + 2 * n + n_out]
        scratch, sems = refs[n_in + 2 * n + n_out:-2], refs[-2:]
        ids = [pl.program_id(a) for a in range(len(grid))]
        first = functools.reduce(jnp.logical_and, [i == 0 for i in ids])
        last = functools.reduce(jnp.logical_and, [i == g - 1 for i, g in zip(ids, grid)])

        @pl.when(first)
        def _():
            comm.start(c_in, c_out, *sems)

        body(*ins, *outs, *scratch)

        @pl.when(last)
        def _():
            comm.finish(c_in, c_out, *sems)

    return wrapped


def _pair_forward(name, landed, shards):
    n = len(landed)

    def body(*refs):
        o_refs = refs[n:2 * n]
        send_sems, recv_sems = refs[2 * n:]
        x, y, c, chips = _place()
        sibling = (x, y, 1 - c)
        sent, arriving = [], []
        for j, chip in enumerate(chips):
            pj = 2 * chip[0] + chip[1]
            for i in range(n):
                h = shards[i].shape[0] // 2
                mine = o_refs[i].at[pj, pl.ds(c * h, h)]
                theirs = o_refs[i].at[pj, pl.ds((1 - c) * h, h)]
                for ref, group in ((mine, sent), (theirs, arriving)):
                    group.append(pltpu.make_async_remote_copy(
                        src_ref=ref, dst_ref=ref, send_sem=send_sems.at[j, i], recv_sem=recv_sems.at[j, i],
                        device_id=sibling, device_id_type=MESH))
        for cp in sent:
            cp.start()
        for cp in arriving:
            cp.wait_recv()
        for cp in sent:
            cp.wait_send()

    outs = pl.pallas_call(
        body, name=name, in_specs=_any_specs(n), out_specs=_any_specs(n),
        out_shape=[jax.ShapeDtypeStruct(a.shape, a.dtype) for a in landed],
        input_output_aliases={i: i for i in range(n)},
        scratch_shapes=[pltpu.SemaphoreType.DMA((3, n)), pltpu.SemaphoreType.DMA((3, n))],
    )(*landed)
    return [lax.dynamic_update_slice(o, s[None], (_chip_index(), 0, 0)) for o, s in zip(outs, shards)]


def _pair_join(name, halves):
    n = len(halves)

    def body(*refs):
        h_refs, o_refs = refs[:n], refs[n:2 * n]
        send_sems, recv_sems = refs[2 * n:]
        x, y, c, _ = _place()
        sibling = (x, y, 1 - c)
        sent = []
        for i in range(n):
            h = halves[i].shape[0]
            mine = o_refs[i].at[pl.ds(c * h, h)]
            rc = pltpu.make_async_remote_copy(
                src_ref=h_refs[i], dst_ref=mine, send_sem=send_sems.at[i], recv_sem=recv_sems.at[i],
                device_id=sibling, device_id_type=MESH)
            rc.start()
            sent.append(rc)
        for i in range(n):
            h = halves[i].shape[0]
            theirs = o_refs[i].at[pl.ds((1 - c) * h, h)]
            pltpu.make_async_remote_copy(
                src_ref=h_refs[i], dst_ref=theirs, send_sem=send_sems.at[i], recv_sem=recv_sems.at[i],
                device_id=sibling, device_id_type=MESH).wait_recv()
        for rc in sent:
            rc.wait_send()

    outs = pl.pallas_call(
        body, name=name,
        in_specs=_any_specs(n), out_specs=_any_specs(n),
        out_shape=[jax.ShapeDtypeStruct((2 * s.shape[0], s.shape[1]), s.dtype) for s in halves],
        scratch_shapes=[pltpu.SemaphoreType.DMA((n,)), pltpu.SemaphoreType.DMA((n,))],
    )(*halves)
    c = lax.axis_index("c")
    return [lax.dynamic_update_slice(o, s, (c * s.shape[0], 0)) for o, s in zip(outs, halves)]


def _add_pair(name, a, b):
    _, H, C = a.shape
    th = _row_tile(H, max(8, (1024 * 1024) // (4 * C) // 8 * 8))

    def body(a_ref, b_ref, o_ref):
        o_ref[...] = (a_ref[...].astype(F32) + b_ref[...].astype(F32)).astype(o_ref.dtype)

    blk = pl.BlockSpec((None, th, C), lambda q, i: (q, i, 0))
    return pl.pallas_call(
        body, name=name, grid=(4, H // th), in_specs=[blk, blk], out_specs=blk,
        out_shape=jax.ShapeDtypeStruct(a.shape, a.dtype), compiler_params=_cparams("parallel", "parallel"),
    )(a, b)


def _add_chips(name, parts):
    _, H, C = parts.shape
    th = _row_tile(H, max(8, (1024 * 1024) // (4 * C) // 8 * 8))

    def body(p_ref, o_ref):
        acc = p_ref[0].astype(F32)
        for q in range(1, 4):
            acc = acc + p_ref[q].astype(F32)
        o_ref[...] = acc

    return pl.pallas_call(
        body, name=name, grid=(H // th,),
        in_specs=[pl.BlockSpec((4, th, C), lambda i: (0, i, 0))], out_specs=pl.BlockSpec((th, C), lambda i: (i, 0)),
        out_shape=jax.ShapeDtypeStruct((H, C), F32), compiler_params=_cparams("parallel"),
    )(parts)


def _pair_sums(tag, grads):
    c = lax.axis_index("c")
    got = _pair_split(f"rs_pair_split_{tag}", grads)
    kept = [lax.dynamic_slice_in_dim(g, c * (g.shape[1] // 2), g.shape[1] // 2, axis=1) for g in grads]
    return [_add_pair(f"rs_add_pair_{tag}_{i}", k, g) for i, (k, g) in enumerate(zip(kept, got))]


def _chip_sums(parts):
    return _pair_join("rs_pair_join", [_add_chips(f"rs_add_chips_{i}", p) for i, p in enumerate(parts)])


def _block_diag(t):
    G, A, B = t.shape
    eye = jnp.eye(8, dtype=t.dtype)
    return jnp.einsum("sgab,gh->sgahb", t.reshape(G // 8, 8, A, B), eye).reshape(G // 8, 8 * A, 8 * B)


def _block_diag_extract(m, A, B):
    n = m.shape[0]
    eye = jnp.eye(8, dtype=m.dtype)
    return jnp.einsum("sgahb,gh->sgab", m.reshape(n, 8, A, 8, B), eye).reshape(8 * n, A, B)


def kernel(x, meta_tokens, norm_mix, norm_ffn, s5_a_re, s5_a_im, s5_log_dt, s5_b_re, s5_b_im, s5_c_re, s5_c_im, s5_d, s5_w_glu, norm_kv, w_kv, w_q, w_o, w_ffn_in, w_ffn_out, norm_final, loss_target, m_meta_tokens, m_norm_mix, m_norm_ffn, m_s5_a_re, m_s5_a_im, m_s5_log_dt, m_s5_b_re, m_s5_b_im, m_s5_c_re, m_s5_c_im, m_s5_d, m_s5_w_glu, m_norm_kv, m_w_kv, m_w_q, m_w_o, m_w_ffn_in, m_w_ffn_out, m_norm_final, v_meta_tokens, v_norm_mix, v_norm_ffn, v_s5_a_re, v_s5_a_im, v_s5_log_dt, v_s5_b_re, v_s5_b_im, v_s5_c_re, v_s5_c_im, v_s5_d, v_s5_w_glu, v_norm_kv, v_w_kv, v_w_q, v_w_o, v_w_ffn_in, v_w_ffn_out, v_norm_final):
    seq, D = x.shape[1], x.shape[2]
    T = X_START + seq
    G, P, C = s5_a_re.shape[1], S5_STATE, S5_GROUP
    d_ff = w_ffn_out.shape[1] * 4
    dq4 = D // 4
    chip = 2 * lax.axis_index("x") + lax.axis_index("y")

    small_in = jnp.concatenate([meta_tokens, jnp.pad(s5_d, ((0, 15), (0, 0)))], axis=0)
    (small_all,) = _all_gather_chips("ag_small", [small_in])
    meta_full = small_all[:, :N_META, :].transpose(1, 0, 2).reshape(N_META, D)
    d_skip = small_all[:, N_META, :].reshape(1, D)
    shards = [s.astype(BF16) for s in (s5_w_glu[0], w_kv, w_q[0], w_o[0],
                                       w_ffn_in.reshape(2 * D, -1), w_ffn_out.reshape(-1, D))]
    rows_out = d_ff // 4

    row = lambda v: v.reshape(1, -1)
    g_mix0, g_mix1 = row(norm_mix[0]), row(norm_mix[1])
    g_ffn = [row(norm_ffn[0]), row(norm_ffn[1])]
    g_kv, g_final = row(norm_kv), row(norm_final)

    a_re3 = s5_a_re[0].reshape(G, 1, P)
    a_im3 = s5_a_im[0].reshape(G, 1, P)
    log_dt3 = s5_log_dt[0].reshape(G, 1, 1)
    bt_re = s5_b_re[0].transpose(0, 2, 1)
    bt_im = s5_b_im[0].transpose(0, 2, 1)
    ab_re, ab_im, bb_re, bb_im = _s5_prep_fwd("s5_prep", a_re3, a_im3, log_dt3, bt_re, bt_im)
    abar_re, abar_im = ab_re.reshape(1, G * P), ab_im.reshape(1, G * P)
    bd_b_re = _block_diag(bb_re).astype(BF16)
    bd_b_im = _block_diag(bb_im).astype(BF16)
    bd_bt_re = bd_b_re.transpose(0, 2, 1)
    bd_bt_im = bd_b_im.transpose(0, 2, 1)
    bd_c_re = _block_diag(s5_c_re[0]).astype(BF16)
    bd_c_im = _block_diag(s5_c_im[0]).astype(BF16)
    bd_ct_re = bd_c_re.transpose(0, 2, 1)
    bd_ct_im = bd_c_im.transpose(0, 2, 1)

    h0 = jnp.concatenate([jnp.zeros((META_START, D), F32), meta_full, x[0]], axis=0)
    (u,) = _rmsnorm_fwd("norm_mix0", h0, [g_mix0], [F32])
    (y, z, x_re, x_im), landed = _s5_fwd(
        "s5_scan", u, bd_b_re, bd_b_im, bd_ct_re, bd_ct_im, abar_re, abar_im, d_skip,
        comm=_halves_gather_comm(shards), comm_args=shards)
    wg_glu, wg_kv, wg_q, wg_o, wg_in, wg_out = _pair_forward("ag_forward", landed, shards)
    wg_q = wg_q.reshape(1, D, D)
    wg_o = wg_o.reshape(1, D, D)
    wg_out = [wg_out[:, l * rows_out:(l + 1) * rows_out, :].reshape(1, d_ff, D) for l in range(2)]
    val, gate, h1 = _mm_glu("glu_proj", z, wg_glu, h0)

    def ffn_fwd(l, h):
        (n,) = _rmsnorm_fwd(f"norm_ffn{l}", h, [g_ffn[l]], [BF16])
        g, u, mid = _mm_swiglu(f"ffn_in{l}", n, wg_in, l)
        return n, (g, u), mid, _mm_nn(f"ffn_out{l}", mid, wg_out[l], res=h)

    n1, gu0, mid0, h2 = ffn_fwd(0, h1)
    nk, nq = _rmsnorm_fwd("norm_kv_q", h2, [g_kv, g_mix1], [BF16, BF16])
    kv = _mm_nn("kv_proj", nk, wg_kv, out_dtype=BF16)
    q = _mm_nn("q_proj", nq, wg_q, out_dtype=BF16)
    o, ltot = _attn_fwd("attn_fwd", q, kv)
    h3 = _mm_nn("o_proj", o, wg_o, res=h2)
    n3, gu1, mid1, h4 = ffn_fwd(1, h3)
    loss_part, dh4, dg_final = _final_loss("final_loss", h4, g_final, loss_target[0])

    def ffn_bwd(l, dh, h, n, gu, mid):
        dgu = _mm_nt_swiglu_bwd(f"ffn_out{l}_dx", dh, wg_out[l], *gu)
        dw_out = _mm_tn(f"ffn_out{l}_dw", mid, dh, 1)
        dw_in = _mm_tn(f"ffn_in{l}_dw", n, dgu, 4)
        dn = _mm_nt_k(f"ffn_in{l}_dx", dgu, wg_in, D, k_blk=l)
        dh_prev, (dg,) = _rmsnorm_bwd(f"norm_ffn{l}_bwd", h, [(g_ffn[l], dn)], dh)
        return dh_prev, dg, dw_in, dw_out

    dh3, dg_ffn1, dw_in1, dw_out1 = ffn_bwd(1, dh4, h3, n3, gu1, mid1)
    d_o = _mm_nt_k("o_proj_dx", dh3, wg_o, D, out_dtype=BF16)
    dw_o = _mm_tn("o_proj_dw", o, dh3, 1)
    dq, dk, dv = _attn_bwd("attn_bwd", q, kv, d_o, ltot)
    dkv = jnp.concatenate([dk, dv], axis=1)
    dw_q = _mm_tn("q_proj_dw", nq, dq, 1)
    dnq = _mm_nt_k("q_proj_dx", dq, wg_q, D)
    dw_kv = _mm_tn("kv_proj_dw", nk, dkv, 4)
    dnk = _mm_nt_k("kv_proj_dx", dkv, wg_kv, D)
    dh2, (dg_mix1, dg_kv) = _rmsnorm_bwd("norm_kv_q_bwd", h2, [(g_mix1, dnq), (g_kv, dnk)], dh3)
    dh1, dg_ffn0, dw_in0, dw_out0 = ffn_bwd(0, dh2, h1, n1, gu0, mid0)
    dvg = _glu_bwd("glu_bwd", val, gate, dh1)
    dw_glu = _mm_tn("glu_proj_dw", z, dvg, 4)
    dz = _mm_nt_k("glu_proj_dx", dvg, wg_glu, D)
    big = [dw_kv, dw_q.reshape(4, D // 4, D), dw_o.reshape(4, D // 4, D), dw_in1, dw_out1.reshape(4, rows_out, D),
           dw_glu, dw_in0, dw_out0.reshape(4, rows_out, D)]
    big_pairs = _pair_sums("big", big)
    (du, dd, dbd_b_re, dbd_b_im, dbd_c_re, dbd_c_im, dab_re, dab_im), big_parts = _s5_bwd(
        "s5_scan_bwd", dz, y, u, x_re, x_im, bd_c_re, bd_c_im, bd_bt_re, bd_bt_im, abar_re, abar_im, d_skip,
        comm=_exchange_comm(big_pairs), comm_args=big_pairs)
    big_parts = _exchange_finish(big_pairs, big_parts)
    dh0, (dg_mix0,) = _rmsnorm_bwd("norm_mix0_bwd", h0, [(g_mix0, du)], dh1)
    da_re, da_im, dlog_dt, dbt_re, dbt_im = _s5_prep_bwd(
        "s5_prep_bwd", a_re3, a_im3, log_dt3, bt_re, bt_im,
        dab_re.reshape(G, 1, P), dab_im.reshape(G, 1, P),
        _block_diag_extract(dbd_b_re, C, P), _block_diag_extract(dbd_b_im, C, P))
    grad_x = dh0[X_START:][None]

    small_parts = [
        dg_mix0, dg_mix1, dg_ffn0, dg_ffn1, da_re, da_im,
        dbt_re.transpose(0, 2, 1), dbt_im.transpose(0, 2, 1),
        _block_diag_extract(dbd_c_re, C, P), _block_diag_extract(dbd_c_im, C, P),
        dg_kv, dg_final, dh0[META_START:X_START], dd, loss_part, dlog_dt]
    small_sizes = [p.size for p in small_parts]
    unit = 4 * 2 * 8 * 128
    padded = -(-sum(small_sizes) // unit) * unit
    tail = jnp.concatenate([loss_part.reshape(-1), dlog_dt.reshape(-1)])
    small_flat = jnp.concatenate(
        [p.reshape(-1) for p in small_parts[:-2]] + [jnp.pad(tail, (0, padded - sum(small_sizes)))])
    small_blocks = small_flat.reshape(4, padded // (4 * 128), 128)
    small_exchanged = _chip_exchange("rs_chip_exchange", _pair_sums("small", [small_blocks]))
    gw_kv, gw_q, gw_o, gw_in1, gw_out1, gw_glu, gw_in0, gw_out0, small_mine = _chip_sums(big_parts + small_exchanged)
    gw_in = jnp.concatenate([gw_in0, gw_in1], axis=0)
    gw_out = jnp.concatenate([gw_out0, gw_out1], axis=0)
    (small_red,) = _all_gather_chips("ag_small_grads", [small_mine])
    small_red = small_red.reshape(-1)
    pieces, at = [], 0
    for p, size in zip(small_parts, small_sizes):
        pieces.append(small_red[at:at + size].reshape(p.shape))
        at += size
    (gn_mix0, gn_mix1, gn_ffn0, gn_ffn1, ga_re, ga_im, gb_re, gb_im, gc_re, gc_im, gn_kv, gn_final,
     gmeta_full, gd_full, loss_all, glog_dt) = pieces
    loss = loss_all[0, 0]
    gn_mix = small_red[:2 * D].reshape(2, D)
    gn_ffn = small_red[2 * D:4 * D].reshape(2, D)
    gmeta = lax.dynamic_slice_in_dim(gmeta_full, chip * dq4, dq4, axis=1)
    gd = lax.dynamic_slice_in_dim(gd_full, chip * dq4, dq4, axis=1)

    grads = {
        "meta_tokens": gmeta, "norm_mix": gn_mix, "norm_ffn": gn_ffn,
        "s5_a_re": ga_re.reshape(s5_a_re.shape), "s5_a_im": ga_im.reshape(s5_a_im.shape),
        "s5_log_dt": glog_dt.reshape(s5_log_dt.shape),
        "s5_b_re": gb_re.reshape(s5_b_re.shape), "s5_b_im": gb_im.reshape(s5_b_im.shape),
        "s5_c_re": gc_re.reshape(s5_c_re.shape), "s5_c_im": gc_im.reshape(s5_c_im.shape),
        "s5_d": gd, "s5_w_glu": gw_glu.reshape(s5_w_glu.shape), "norm_kv": gn_kv.reshape(norm_kv.shape),
        "w_kv": gw_kv, "w_q": gw_q.reshape(w_q.shape), "w_o": gw_o.reshape(w_o.shape),
        "w_ffn_in": gw_in.reshape(w_ffn_in.shape), "w_ffn_out": gw_out.reshape(w_ffn_out.shape),
        "norm_final": gn_final.reshape(norm_final.shape),
    }
    weights = {
        "meta_tokens": (meta_tokens, m_meta_tokens, v_meta_tokens), "norm_mix": (norm_mix, m_norm_mix, v_norm_mix),
        "norm_ffn": (norm_ffn, m_norm_ffn, v_norm_ffn), "s5_a_re": (s5_a_re, m_s5_a_re, v_s5_a_re),
        "s5_a_im": (s5_a_im, m_s5_a_im, v_s5_a_im), "s5_log_dt": (s5_log_dt, m_s5_log_dt, v_s5_log_dt),
        "s5_b_re": (s5_b_re, m_s5_b_re, v_s5_b_re), "s5_b_im": (s5_b_im, m_s5_b_im, v_s5_b_im),
        "s5_c_re": (s5_c_re, m_s5_c_re, v_s5_c_re), "s5_c_im": (s5_c_im, m_s5_c_im, v_s5_c_im),
        "s5_d": (s5_d, m_s5_d, v_s5_d), "s5_w_glu": (s5_w_glu, m_s5_w_glu, v_s5_w_glu),
        "norm_kv": (norm_kv, m_norm_kv, v_norm_kv), "w_kv": (w_kv, m_w_kv, v_w_kv), "w_q": (w_q, m_w_q, v_w_q),
        "w_o": (w_o, m_w_o, v_w_o), "w_ffn_in": (w_ffn_in, m_w_ffn_in, v_w_ffn_in),
        "w_ffn_out": (w_ffn_out, m_w_ffn_out, v_w_ffn_out), "norm_final": (norm_final, m_norm_final, v_norm_final),
    }
    names = list(weights)
    deltas, new_m, new_v = [], [], []
    for name in names:
        w, m, v = weights[name]
        d, nm, nv = _adamw(f"adamw_{name}", w, grads[name], m, v)
        deltas.append(d)
        new_m.append(nm)
        new_v.append(nv)
    return (loss, grad_x, *[grads[n] for n in names], *deltas, *new_m, *new_v)
```

```python
import functools
import math

import jax
import jax.numpy as jnp
from jax import lax
from jax.experimental import pallas as pl
from jax.experimental.pallas import tpu as pltpu

F32 = jnp.float32
BF16 = jnp.bfloat16

N_META = 16
X_START = 128
META_START = X_START - N_META
S5_GROUP = 16
S5_STATE = 64
HEAD_DIM = 64
KEY_BLOCK = 128
STATE_TILE = 512
CH_TILE = 128
RMS_EPS = 1e-6
ADAM_LR, ADAM_B1, ADAM_B2, ADAM_EPS, ADAM_WD, ADAM_STEP = 0.001, 0.9, 0.999, 1e-08, 0.01, 10
VMEM_LIMIT_BYTES = 48 * 1024 * 1024
MESH = pl.DeviceIdType.MESH
NT_DIMS = (((1,), (1,)), ((), ()))
TN_DIMS = (((0,), (0,)), ((), ()))


def _cparams(*sem):
    return pltpu.CompilerParams(dimension_semantics=sem, vmem_limit_bytes=VMEM_LIMIT_BYTES)


def _row_tile(rows, cap):
    for unit in (128, 8):
        best = 0
        for t in range(unit, min(rows, cap) + 1, unit):
            if rows % t == 0:
                best = t
        if best:
            return best
    return rows


def _col_tile(cols, cap):
    best = 0
    for t in range(128, min(cols, cap) + 1, 128):
        if cols % t == 0:
            best = t
    return best if best else cols


def _gelu(x):
    k = math.sqrt(2.0 / math.pi)
    return 0.5 * x * (1.0 + jnp.tanh(k * (x + 0.044715 * x * x * x)))


def _gelu_grad(x):
    k = math.sqrt(2.0 / math.pi)
    t = jnp.tanh(k * (x + 0.044715 * x * x * x))
    return 0.5 * (1.0 + t) + 0.5 * x * (1.0 - t * t) * k * (1.0 + 3.0 * 0.044715 * x * x)


def _sigmoid(x):
    return 1.0 / (1.0 + jnp.exp(-x))


def _rmsnorm_fwd(name, x, gains, out_dtypes):
    T, D = x.shape
    tm = _row_tile(T, 512)
    n = len(gains)

    def body(x_ref, *refs):
        xv = x_ref[...]
        xh = xv * lax.rsqrt(jnp.mean(xv * xv, axis=-1, keepdims=True) + RMS_EPS)
        for g_ref, o_ref in zip(refs[:n], refs[n:]):
            o_ref[...] = (xh * g_ref[...]).astype(o_ref.dtype)

    row = pl.BlockSpec((tm, D), lambda i: (i, 0))
    vec = pl.BlockSpec((1, D), lambda i: (0, 0))
    return pl.pallas_call(
        body, name=name, grid=(T // tm,),
        in_specs=[row] + [vec] * n, out_specs=[row] * n,
        out_shape=[jax.ShapeDtypeStruct((T, D), dt) for dt in out_dtypes],
        compiler_params=_cparams("parallel"),
    )(x, *gains)


def _rmsnorm_bwd(name, x, pairs, dres):
    T, D = x.shape
    tm = _row_tile(T, 512)
    n = len(pairs)

    def body(x_ref, dres_ref, *refs):
        g_refs, dy_refs = refs[:n], refs[n:2 * n]
        dx_ref, dg_refs = refs[2 * n], refs[2 * n + 1:]
        i = pl.program_id(0)
        xv = x_ref[...]
        r = lax.rsqrt(jnp.mean(xv * xv, axis=-1, keepdims=True) + RMS_EPS)
        xh = xv * r
        dxh = jnp.zeros_like(xv)
        for g_ref, dy_ref, dg_ref in zip(g_refs, dy_refs, dg_refs):
            dy = dy_ref[...].astype(F32)
            part = jnp.sum(dy * xh, axis=0, keepdims=True)

            @pl.when(i == 0)
            def _():
                dg_ref[...] = part

            @pl.when(i > 0)
            def _():
                dg_ref[...] += part

            dxh = dxh + dy * g_ref[...]
        dx = r * (dxh - xh * jnp.mean(dxh * xh, axis=-1, keepdims=True))
        dx_ref[...] = dres_ref[...] + dx

    row = pl.BlockSpec((tm, D), lambda i: (i, 0))
    vec = pl.BlockSpec((1, D), lambda i: (0, 0))
    outs = pl.pallas_call(
        body, name=name, grid=(T // tm,),
        in_specs=[row, row] + [vec] * n + [row] * n,
        out_specs=[row] + [vec] * n,
        out_shape=[jax.ShapeDtypeStruct((T, D), F32)] + [jax.ShapeDtypeStruct((1, D), F32)] * n,
        compiler_params=_cparams("arbitrary"),
    )(x, dres, *[g for g, _ in pairs], *[dy for _, dy in pairs])
    return outs[0], outs[1:]


def _mm_nn(name, a, w, k_blk=0, res=None, out_dtype=F32):
    M, K = a.shape
    S, _, Ns = w.shape
    tm = _row_tile(M, 512)
    tn = _col_tile(Ns, 1408)
    nt = Ns // tn

    def body(a_ref, w_ref, *refs):
        o_ref = refs[-1]
        acc = jnp.dot(a_ref[...].astype(BF16), w_ref[...], preferred_element_type=F32)
        if res is not None:
            acc = acc + refs[0][...]
        o_ref[...] = acc.astype(o_ref.dtype)

    in_specs = [pl.BlockSpec((tm, K), lambda j, i: (i, 0)),
                pl.BlockSpec((None, K, tn), lambda j, i: (j // nt, k_blk, j % nt))]
    args = [a, w]
    if res is not None:
        in_specs.append(pl.BlockSpec((tm, tn), lambda j, i: (i, j)))
        args.append(res)
    return pl.pallas_call(
        body, name=name, grid=(S * nt, M // tm),
        in_specs=in_specs, out_specs=pl.BlockSpec((tm, tn), lambda j, i: (i, j)),
        out_shape=jax.ShapeDtypeStruct((M, S * Ns), out_dtype),
        compiler_params=_cparams("parallel", "parallel"),
    )(*args)


def _mm_nt_k(name, dy, w, K, k_blk=0, out_dtype=F32):
    M = dy.shape[0]
    S, _, Ns = w.shape
    tm = _row_tile(M, 1408)
    tn = _col_tile(Ns, 1408)
    nt = Ns // tn
    steps = S * nt

    def body(dy_ref, w_ref, o_ref, acc_ref):
        j = pl.program_id(1)
        part = lax.dot_general(dy_ref[...].astype(BF16), w_ref[...], (((1,), (1,)), ((), ())),
                               preferred_element_type=F32)

        @pl.when(j == 0)
        def _():
            acc_ref[...] = part

        @pl.when(j > 0)
        def _():
            acc_ref[...] += part

        @pl.when(j == steps - 1)
        def _():
            o_ref[...] = acc_ref[...].astype(o_ref.dtype)

    return pl.pallas_call(
        body, name=name, grid=(M // tm, steps),
        in_specs=[pl.BlockSpec((tm, tn), lambda i, j: (i, j)),
                  pl.BlockSpec((None, K, tn), lambda i, j: (j // nt, k_blk, j % nt))],
        out_specs=pl.BlockSpec((tm, K), lambda i, j: (i, 0)),
        out_shape=jax.ShapeDtypeStruct((M, K), out_dtype),
        scratch_shapes=[pltpu.VMEM((tm, K), F32)],
        compiler_params=_cparams("parallel", "arbitrary"),
    )(dy, w)


def _mm_tn(name, a, dy, S, out_dtype=BF16):
    T, K = a.shape
    Ns = dy.shape[1] // S
    tn = _col_tile(Ns, max(128, (6 * 1024 * 1024) // (4 * K) // 128 * 128))
    nt = Ns // tn
    tt = _row_tile(T, 1408)
    steps = T // tt

    def body(a_ref, dy_ref, o_ref, acc_ref):
        t = pl.program_id(1)
        part = lax.dot_general(a_ref[...].astype(BF16), dy_ref[...].astype(BF16), (((0,), (0,)), ((), ())),
                               preferred_element_type=F32)

        @pl.when(t == 0)
        def _():
            acc_ref[...] = part

        @pl.when(t > 0)
        def _():
            acc_ref[...] += part

        @pl.when(t == steps - 1)
        def _():
            o_ref[...] = acc_ref[...].astype(o_ref.dtype)

    return pl.pallas_call(
        body, name=name, grid=(S * nt, steps),
        in_specs=[pl.BlockSpec((tt, K), lambda j, t: (t, 0)),
                  pl.BlockSpec((tt, tn), lambda j, t: (t, j))],
        out_specs=pl.BlockSpec((None, K, tn), lambda j, t: (j // nt, 0, j % nt)),
        out_shape=jax.ShapeDtypeStruct((S, K, Ns), out_dtype),
        scratch_shapes=[pltpu.VMEM((K, tn), F32)],
        compiler_params=_cparams("parallel", "arbitrary"),
    )(a, dy)


def _gated_tile(T, width):
    return _row_tile(T, max(8, (2 * 1024 * 1024) // (4 * width) // 8 * 8))


def _mm_glu(name, a, w, res):
    M, K = a.shape
    Ns = w.shape[2]
    tm = _row_tile(M, 512)

    def body(a_ref, wv_ref, wg_ref, r_ref, v_ref, g_ref, o_ref):
        ab = a_ref[...].astype(BF16)
        val = jnp.dot(ab, wv_ref[...], preferred_element_type=F32)
        gate = jnp.dot(ab, wg_ref[...], preferred_element_type=F32)
        v_ref[...] = val
        g_ref[...] = gate
        o_ref[...] = r_ref[...] + val * _sigmoid(gate)

    out = pl.BlockSpec((tm, Ns), lambda j, i: (i, j))
    return pl.pallas_call(
        body, name=name, grid=(2, M // tm),
        in_specs=[pl.BlockSpec((tm, K), lambda j, i: (i, 0)),
                  pl.BlockSpec((None, K, Ns), lambda j, i: (j, 0, 0)),
                  pl.BlockSpec((None, K, Ns), lambda j, i: (j + 2, 0, 0)), out],
        out_specs=[out, out, out],
        out_shape=[jax.ShapeDtypeStruct((M, 2 * Ns), F32)] * 3,
        compiler_params=_cparams("parallel", "parallel"),
    )(a, w, w, res)


def _glu_bwd(name, val, gate, dout):
    T, D = dout.shape
    tm = _gated_tile(T, 2 * D)

    def body(v_ref, g_ref, d_ref, o_ref):
        s = _sigmoid(g_ref[...])
        d = d_ref[...]
        o_ref[:, :D] = (d * s).astype(o_ref.dtype)
        o_ref[:, D:] = (d * v_ref[...] * s * (1.0 - s)).astype(o_ref.dtype)

    blk = pl.BlockSpec((tm, D), lambda i: (i, 0))
    return pl.pallas_call(
        body, name=name, grid=(T // tm,),
        in_specs=[blk, blk, blk],
        out_specs=pl.BlockSpec((tm, 2 * D), lambda i: (i, 0)),
        out_shape=jax.ShapeDtypeStruct((T, 2 * D), BF16),
        compiler_params=_cparams("parallel"),
    )(val, gate, dout)


def _mm_swiglu(name, a, w, k_blk):
    M, K = a.shape
    Ns = w.shape[2]
    tm = _row_tile(M, 512)

    def body(a_ref, wg_ref, wu_ref, g_ref, u_ref, mid_ref):
        ab = a_ref[...].astype(BF16)
        g = jnp.dot(ab, wg_ref[...], preferred_element_type=F32)
        u = jnp.dot(ab, wu_ref[...], preferred_element_type=F32)
        g_ref[...] = g.astype(g_ref.dtype)
        u_ref[...] = u.astype(u_ref.dtype)
        mid_ref[...] = (g * _sigmoid(g) * u).astype(mid_ref.dtype)

    out = pl.BlockSpec((tm, Ns), lambda j, i: (i, j))
    return pl.pallas_call(
        body, name=name, grid=(2, M // tm),
        in_specs=[pl.BlockSpec((tm, K), lambda j, i: (i, 0)),
                  pl.BlockSpec((None, K, Ns), lambda j, i: (j, k_blk, 0)),
                  pl.BlockSpec((None, K, Ns), lambda j, i: (j + 2, k_blk, 0))],
        out_specs=[out, out, out],
        out_shape=[jax.ShapeDtypeStruct((M, 2 * Ns), BF16)] * 3,
        compiler_params=_cparams("parallel", "parallel"),
    )(a, w, w)


def _mm_nt_swiglu_bwd(name, dh, w, g, u):
    M, D = dh.shape
    F = w.shape[1]
    tm = _row_tile(M, 512)

    def body(dh_ref, w_ref, g_ref, u_ref, o_ref):
        d = lax.dot_general(dh_ref[...].astype(BF16), w_ref[...], NT_DIMS, preferred_element_type=F32)
        gv = g_ref[...].astype(F32)
        s = _sigmoid(gv)
        o_ref[:, :F] = (d * u_ref[...].astype(F32) * s * (1.0 + gv * (1.0 - s))).astype(o_ref.dtype)
        o_ref[:, F:] = (d * gv * s).astype(o_ref.dtype)

    half = pl.BlockSpec((tm, F), lambda i: (i, 0))
    return pl.pallas_call(
        body, name=name, grid=(M // tm,),
        in_specs=[pl.BlockSpec((tm, D), lambda i: (i, 0)), pl.BlockSpec((None, F, D), lambda i: (0, 0, 0)), half, half],
        out_specs=pl.BlockSpec((tm, 2 * F), lambda i: (i, 0)),
        out_shape=jax.ShapeDtypeStruct((M, 2 * F), BF16),
        compiler_params=_cparams("parallel"),
    )(dh, w, g, u)


def _final_loss(name, h, gain, target):
    T, D = h.shape
    tm = X_START
    lead = X_START // tm

    def body(h_ref, g_ref, t_ref, loss_ref, dh_ref, dg_ref):
        i = pl.program_id(0)

        @pl.when(i == 0)
        def _():
            loss_ref[...] = jnp.zeros_like(loss_ref)
            dg_ref[...] = jnp.zeros_like(dg_ref)
            dh_ref[...] = jnp.zeros_like(dh_ref)

        @pl.when(i >= lead)
        def _():
            xv = h_ref[...]
            r = lax.rsqrt(jnp.mean(xv * xv, axis=-1, keepdims=True) + RMS_EPS)
            xh = xv * r
            g = g_ref[...]
            diff = xh * g - t_ref[...]
            loss_ref[...] += 0.5 * jnp.sum(jnp.mean(diff * diff, axis=-1, keepdims=True), axis=0, keepdims=True)
            dout = diff * (1.0 / D)
            dg_ref[...] += jnp.sum(dout * xh, axis=0, keepdims=True)
            dxh = dout * g
            dh_ref[...] = r * (dxh - xh * jnp.mean(dxh * xh, axis=-1, keepdims=True))

    return pl.pallas_call(
        body, name=name, grid=(T // tm,),
        in_specs=[pl.BlockSpec((tm, D), lambda i: (i, 0)), pl.BlockSpec((1, D), lambda i: (0, 0)),
                  pl.BlockSpec((tm, D), lambda i: (jnp.maximum(i - lead, 0), 0))],
        out_specs=[pl.BlockSpec((1, 128), lambda i: (0, 0)), pl.BlockSpec((tm, D), lambda i: (i, 0)),
                   pl.BlockSpec((1, D), lambda i: (0, 0))],
        out_shape=[jax.ShapeDtypeStruct((1, 128), F32), jax.ShapeDtypeStruct((T, D), F32),
                   jax.ShapeDtypeStruct((1, D), F32)],
        compiler_params=_cparams("arbitrary"),
    )(h, gain, target)


def _grid_call(name, body, grid, in_specs, out_specs, out_shape, scratch_shapes, args, comm=None, comm_args=()):
    params = _cparams(*(("arbitrary",) * len(grid)))
    if comm is None:
        return pl.pallas_call(body, name=name, grid=grid, in_specs=in_specs, out_specs=out_specs,
                              out_shape=out_shape, scratch_shapes=scratch_shapes, compiler_params=params)(*args), []
    outs = pl.pallas_call(
        _embed_comm(comm, body, grid, len(in_specs), len(out_specs)), name=name, grid=grid,
        in_specs=list(in_specs) + _any_specs(comm.n), out_specs=list(out_specs) + _any_specs(comm.n),
        out_shape=list(out_shape) + comm.out_shape, scratch_shapes=list(scratch_shapes) + comm.scratch,
        compiler_params=params)(*args, *comm_args)
    return outs[:len(out_specs)], outs[len(out_specs):]


def _s5_discretise(a_re, a_im, log_dt, bt_re, bt_im):
    dt = jnp.exp(log_dt)
    mag = jnp.exp(dt * a_re)
    ang = dt * a_im
    abar_re = mag * jnp.cos(ang)
    abar_im = mag * jnp.sin(ang)
    den = a_re * a_re + a_im * a_im
    coef_re = ((abar_re - 1.0) * a_re + abar_im * a_im) / den
    coef_im = (abar_im * a_re - (abar_re - 1.0) * a_im) / den
    bbar_re = coef_re * bt_re - coef_im * bt_im
    bbar_im = coef_re * bt_im + coef_im * bt_re
    return abar_re, abar_im, bbar_re, bbar_im


def _s5_prep_fwd(name, a_re, a_im, log_dt, bt_re, bt_im):
    G, _, P = a_re.shape
    C = bt_re.shape[1]

    def body(ar, ai, ld, br, bi, o_ar, o_ai, o_br, o_bi):
        outs = _s5_discretise(ar[...], ai[...], ld[...], br[...], bi[...])
        for o, v in zip((o_ar, o_ai, o_br, o_bi), outs):
            o[...] = v

    return pl.pallas_call(
        body, name=name,
        out_shape=[jax.ShapeDtypeStruct((G, 1, P), F32)] * 2 + [jax.ShapeDtypeStruct((G, C, P), F32)] * 2,
    )(a_re, a_im, log_dt, bt_re, bt_im)


def _s5_prep_bwd(name, a_re, a_im, log_dt, bt_re, bt_im, d_ar, d_ai, d_br, d_bi):
    G, _, P = a_re.shape
    C = bt_re.shape[1]

    def body(ar, ai, ld, br, bi, gar, gai, gbr, gbi, o_ar, o_ai, o_ld, o_br, o_bi):
        _, vjp = jax.vjp(_s5_discretise, ar[...], ai[...], ld[...], br[...], bi[...])
        grads = vjp((gar[...], gai[...], gbr[...], gbi[...]))
        for o, v in zip((o_ar, o_ai, o_ld, o_br, o_bi), grads):
            o[...] = v

    return pl.pallas_call(
        body, name=name,
        out_shape=[jax.ShapeDtypeStruct((G, 1, P), F32)] * 2 + [jax.ShapeDtypeStruct((G, 1, 1), F32)]
        + [jax.ShapeDtypeStruct((G, C, P), F32)] * 2,
    )(a_re, a_im, log_dt, bt_re, bt_im, d_ar, d_ai, d_br, d_bi)


def _cmul(ar, ai, br, bi):
    return ar * br - ai * bi, ar * bi + ai * br


def _power_table(a_re, a_im):
    rows_re, rows_im = [a_re], [a_im]
    for _ in range(7):
        r, m = _cmul(rows_re[-1], rows_im[-1], a_re, a_im)
        rows_re.append(r)
        rows_im.append(m)
    row = lax.broadcasted_iota(jnp.int32, (8, a_re.shape[1]), 0)
    t_re = jnp.zeros((8, a_re.shape[1]), F32)
    t_im = jnp.zeros((8, a_re.shape[1]), F32)
    for k in range(8):
        t_re = jnp.where(row == k, rows_re[k], t_re)
        t_im = jnp.where(row == k, rows_im[k], t_im)
    return t_re, t_im, rows_re, rows_im


def _s5_fwd(name, u, b_re, b_im, ct_re, ct_im, abar_re, abar_im, d_skip, comm=None, comm_args=()):
    T, D = u.shape
    n_st = D // CH_TILE
    W = STATE_TILE
    tc = _row_tile(T, 512)
    n_tiles = tc // 8

    def body(u_ref, bre_ref, bim_ref, cre_ref, cim_ref, ar_ref, ai_ref, d_ref,
             y_ref, z_ref, xr_ref, xi_ref, carry_re, carry_im, pw_re, pw_im, sh_re, sh_im):
        c = pl.program_id(1)

        @pl.when(c == 0)
        def _():
            t_re, t_im, rows_re, rows_im = _power_table(ar_ref[...], ai_ref[...])
            pw_re[...] = t_re
            pw_im[...] = t_im
            first_rows = lax.broadcasted_iota(jnp.int32, (8, W), 0)
            for n, d in enumerate((1, 2, 4)):
                sh_re[n] = jnp.where(first_rows >= d, jnp.broadcast_to(rows_re[d - 1], (8, W)), 0.0)
                sh_im[n] = jnp.where(first_rows >= d, jnp.broadcast_to(rows_im[d - 1], (8, W)), 0.0)
            carry_re[...] = jnp.zeros_like(carry_re)
            carry_im[...] = jnp.zeros_like(carry_im)

        ub = u_ref[...].astype(BF16)
        xr_ref[...] = jnp.dot(ub, bre_ref[...], preferred_element_type=F32)
        xi_ref[...] = jnp.dot(ub, bim_ref[...], preferred_element_type=F32)
        row = lax.broadcasted_iota(jnp.int32, (8, W), 0)

        def tile(i, carry):
            c_re, c_im = carry
            rows = pl.ds(pl.multiple_of(i * 8, 8), 8)
            r = xr_ref[rows, :]
            m = xi_ref[rows, :]
            for n, d in enumerate((1, 2, 4)):
                pr, pm = _cmul(sh_re[n], sh_im[n], pltpu.roll(r, d, 0), pltpu.roll(m, d, 0))
                r = r + pr
                m = m + pm
            pr, pm = _cmul(pw_re[...], pw_im[...], c_re, c_im)
            r = r + pr
            m = m + pm
            xr_ref[rows, :] = r
            xi_ref[rows, :] = m
            return jnp.broadcast_to(r[7:8, :], (8, W)), jnp.broadcast_to(m[7:8, :], (8, W))

        c_re, c_im = lax.fori_loop(0, n_tiles, tile, (carry_re[...], carry_im[...]))
        carry_re[...] = c_re
        carry_im[...] = c_im
        y = (jnp.dot(xr_ref[...].astype(BF16), cre_ref[...], preferred_element_type=F32)
             - jnp.dot(xi_ref[...].astype(BF16), cim_ref[...], preferred_element_type=F32)
             + d_ref[...] * u_ref[...])
        y_ref[...] = y
        z_ref[...] = _gelu(y).astype(z_ref.dtype)

    ch = pl.BlockSpec((tc, CH_TILE), lambda s, c: (c, s))
    st = pl.BlockSpec((tc, W), lambda s, c: (c, s))
    return _grid_call(
        name, body, (n_st, T // tc),
        in_specs=[ch,
                  pl.BlockSpec((None, CH_TILE, W), lambda s, c: (s, 0, 0)),
                  pl.BlockSpec((None, CH_TILE, W), lambda s, c: (s, 0, 0)),
                  pl.BlockSpec((None, W, CH_TILE), lambda s, c: (s, 0, 0)),
                  pl.BlockSpec((None, W, CH_TILE), lambda s, c: (s, 0, 0)),
                  pl.BlockSpec((1, W), lambda s, c: (0, s)),
                  pl.BlockSpec((1, W), lambda s, c: (0, s)),
                  pl.BlockSpec((1, CH_TILE), lambda s, c: (0, s))],
        out_specs=[ch, ch, st, st],
        out_shape=[jax.ShapeDtypeStruct((T, D), F32), jax.ShapeDtypeStruct((T, D), BF16),
                   jax.ShapeDtypeStruct((T, 4 * D), F32), jax.ShapeDtypeStruct((T, 4 * D), F32)],
        scratch_shapes=[pltpu.VMEM((8, W), F32), pltpu.VMEM((8, W), F32),
                        pltpu.VMEM((8, W), F32), pltpu.VMEM((8, W), F32),
                        pltpu.VMEM((3, 8, W), F32), pltpu.VMEM((3, 8, W), F32)],
        args=(u, b_re, b_im, ct_re, ct_im, abar_re, abar_im, d_skip), comm=comm, comm_args=comm_args)


def _s5_bwd(name, dz, y, u, x_re, x_im, c_re, c_im, bt_re, bt_im, abar_re, abar_im, d_skip, comm=None, comm_args=()):
    T, D = u.shape
    n_st = D // CH_TILE
    W = STATE_TILE
    tc = _row_tile(T, 512)
    n_chunks = T // tc
    n_tiles = tc // 8
    tiles_per_chunk = tc // 8

    def body(dz_ref, y_ref, u_ref, xr_ref, xi_ref, xpr_ref, xpi_ref, cre_ref, cim_ref, btr_ref, bti_ref,
             ar_ref, ai_ref, d_ref,
             du_ref, dd_ref, dbr_ref, dbi_ref, dcr_ref, dci_ref, dar_ref, dai_ref,
             lam_re, lam_im, xe_re, xe_im, carry_re, carry_im, pw_re, pw_im, sh_re, sh_im, acc_ar, acc_ai):
        k = pl.program_id(1)
        first_chunk = k == n_chunks - 1

        @pl.when(k == 0)
        def _():
            t_re, t_im, rows_re, rows_im = _power_table(ar_ref[...], -ai_ref[...])
            row = lax.broadcasted_iota(jnp.int32, (8, W), 0)
            r_re = jnp.zeros((8, W), F32)
            r_im = jnp.zeros((8, W), F32)
            for j in range(8):
                r_re = jnp.where(row == j, rows_re[7 - j], r_re)
                r_im = jnp.where(row == j, rows_im[7 - j], r_im)
            pw_re[...] = r_re
            pw_im[...] = r_im
            for n, d in enumerate((1, 2, 4)):
                sh_re[n] = jnp.where(row < 8 - d, jnp.broadcast_to(rows_re[d - 1], (8, W)), 0.0)
                sh_im[n] = jnp.where(row < 8 - d, jnp.broadcast_to(rows_im[d - 1], (8, W)), 0.0)
            carry_re[...] = jnp.zeros_like(carry_re)
            carry_im[...] = jnp.zeros_like(carry_im)
            acc_ar[...] = jnp.zeros_like(acc_ar)
            acc_ai[...] = jnp.zeros_like(acc_ai)
            dd_ref[...] = jnp.zeros_like(dd_ref)
            dbr_ref[...] = jnp.zeros_like(dbr_ref)
            dbi_ref[...] = jnp.zeros_like(dbi_ref)
            dcr_ref[...] = jnp.zeros_like(dcr_ref)
            dci_ref[...] = jnp.zeros_like(dci_ref)

        uv = u_ref[...]
        dy = dz_ref[...] * _gelu_grad(y_ref[...])
        dyb = dy.astype(BF16)
        lam_re[...] = jnp.dot(dyb, cre_ref[...], preferred_element_type=F32)
        lam_im[...] = -jnp.dot(dyb, cim_ref[...], preferred_element_type=F32)
        keep = jnp.where(first_chunk, 0.0, 1.0)
        xe_re[pl.ds(0, 8), :] = xpr_ref[...] * keep
        xe_im[pl.ds(0, 8), :] = xpi_ref[...] * keep
        xe_re[pl.ds(8, tc), :] = xr_ref[...]
        xe_im[pl.ds(8, tc), :] = xi_ref[...]
        row = lax.broadcasted_iota(jnp.int32, (8, W), 0)

        def tile(n, carry):
            c_re, c_im, s_ar, s_ai = carry
            i = n_tiles - 1 - n
            rows = pl.ds(pl.multiple_of(i * 8, 8), 8)
            r = lam_re[rows, :]
            m = lam_im[rows, :]
            for q, d in enumerate((1, 2, 4)):
                pr, pm = _cmul(sh_re[q], sh_im[q], pltpu.roll(r, 8 - d, 0), pltpu.roll(m, 8 - d, 0))
                r = r + pr
                m = m + pm
            pr, pm = _cmul(pw_re[...], pw_im[...], c_re, c_im)
            r = r + pr
            m = m + pm
            lam_re[rows, :] = r
            lam_im[rows, :] = m
            cur_re = xe_re[pl.ds(pl.multiple_of(i * 8 + 8, 8), 8), :]
            cur_im = xe_im[pl.ds(pl.multiple_of(i * 8 + 8, 8), 8), :]
            bef_re = xe_re[rows, :]
            bef_im = xe_im[rows, :]
            xp_re = jnp.where(row == 0, jnp.broadcast_to(bef_re[7:8, :], (8, W)), pltpu.roll(cur_re, 1, 0))
            xp_im = jnp.where(row == 0, jnp.broadcast_to(bef_im[7:8, :], (8, W)), pltpu.roll(cur_im, 1, 0))
            s_ar = s_ar + r * xp_re + m * xp_im
            s_ai = s_ai + m * xp_re - r * xp_im
            return jnp.broadcast_to(r[0:1, :], (8, W)), jnp.broadcast_to(m[0:1, :], (8, W)), s_ar, s_ai

        c_re, c_im, s_ar, s_ai = lax.fori_loop(
            0, n_tiles, tile, (carry_re[...], carry_im[...], acc_ar[...], acc_ai[...]))
        carry_re[...] = c_re
        carry_im[...] = c_im
        acc_ar[...] = s_ar
        acc_ai[...] = s_ai
        lr = lam_re[...].astype(BF16)
        li = lam_im[...].astype(BF16)
        du_ref[...] = (dy * d_ref[...] + jnp.dot(lr, btr_ref[...], preferred_element_type=F32)
                       + jnp.dot(li, bti_ref[...], preferred_element_type=F32))
        dd_ref[...] += jnp.sum(dy * uv, axis=0, keepdims=True)
        tn_dims = (((0,), (0,)), ((), ()))
        ub = uv.astype(BF16)
        dbr_ref[...] += lax.dot_general(ub, lr, tn_dims, preferred_element_type=F32)
        dbi_ref[...] += lax.dot_general(ub, li, tn_dims, preferred_element_type=F32)
        dcr_ref[...] += lax.dot_general(dyb, xr_ref[...].astype(BF16), tn_dims, preferred_element_type=F32)
        dci_ref[...] -= lax.dot_general(dyb, xi_ref[...].astype(BF16), tn_dims, preferred_element_type=F32)

        @pl.when(first_chunk)
        def _():
            dar_ref[...] = jnp.sum(acc_ar[...], axis=0, keepdims=True)
            dai_ref[...] = jnp.sum(acc_ai[...], axis=0, keepdims=True)

    rev = lambda k: n_chunks - 1 - k
    ch = pl.BlockSpec((tc, CH_TILE), lambda s, k: (rev(k), s))
    st = pl.BlockSpec((tc, W), lambda s, k: (rev(k), s))
    prev = pl.BlockSpec((8, W), lambda s, k: (jnp.maximum(rev(k) * tiles_per_chunk - 1, 0), s))
    mat_cw = pl.BlockSpec((None, CH_TILE, W), lambda s, k: (s, 0, 0))
    mat_wc = pl.BlockSpec((None, W, CH_TILE), lambda s, k: (s, 0, 0))
    vec_w = pl.BlockSpec((1, W), lambda s, k: (0, s))
    vec_c = pl.BlockSpec((1, CH_TILE), lambda s, k: (0, s))
    dense = jax.ShapeDtypeStruct((n_st, CH_TILE, W), F32)
    return _grid_call(
        name, body, (n_st, n_chunks),
        in_specs=[ch, ch, ch, st, st, prev, prev, mat_cw, mat_cw, mat_wc, mat_wc, vec_w, vec_w, vec_c],
        out_specs=[ch, vec_c, mat_cw, mat_cw, mat_cw, mat_cw, vec_w, vec_w],
        out_shape=[jax.ShapeDtypeStruct((T, D), F32), jax.ShapeDtypeStruct((1, D), F32), dense, dense, dense, dense,
                   jax.ShapeDtypeStruct((1, 4 * D), F32), jax.ShapeDtypeStruct((1, 4 * D), F32)],
        scratch_shapes=[pltpu.VMEM((tc, W), F32), pltpu.VMEM((tc, W), F32),
                        pltpu.VMEM((tc + 8, W), F32), pltpu.VMEM((tc + 8, W), F32),
                        pltpu.VMEM((8, W), F32), pltpu.VMEM((8, W), F32),
                        pltpu.VMEM((8, W), F32), pltpu.VMEM((8, W), F32),
                        pltpu.VMEM((3, 8, W), F32), pltpu.VMEM((3, 8, W), F32),
                        pltpu.VMEM((8, W), F32), pltpu.VMEM((8, W), F32)],
        args=(dz, y, u, x_re, x_im, x_re, x_im, c_re, c_im, bt_re, bt_im, abar_re, abar_im, d_skip),
        comm=comm, comm_args=comm_args)


def _sums_matrix(strictly_later, copies):
    jj = lax.broadcasted_iota(jnp.int32, (copies * KEY_BLOCK, 2 * KEY_BLOCK), 0) & (KEY_BLOCK - 1)
    ss = lax.broadcasted_iota(jnp.int32, (copies * KEY_BLOCK, 2 * KEY_BLOCK), 1)
    tri = (jj > ss) if strictly_later else (jj < ss)
    return (tri | (ss >= KEY_BLOCK)).astype(BF16)


def _split_heads(blk):
    first = lax.broadcasted_iota(jnp.int32, blk.shape, 1) < HEAD_DIM
    zero = jnp.zeros_like(blk)
    return jnp.concatenate([jnp.where(first, blk, zero), jnp.where(first, zero, blk)], axis=0)


LOG2_E = 1.4426950408889634


def _sb_scores(z, mask, later):
    z2 = z * LOG2_E
    minus_abs = lax.bitcast_convert_type(lax.bitcast_convert_type(z2, jnp.uint32) | jnp.uint32(0x80000000), F32)
    lb = jnp.minimum(z2, 0.0) - jnp.log2(1.0 + jnp.exp2(minus_abs))
    lm = lb - z2
    if mask is not None:
        lm = jnp.where(mask, lm, 0.0)
    hi = lm.astype(BF16)
    lo = (lm - hi.astype(F32)).astype(BF16)
    return lb, jnp.dot(jnp.concatenate([hi, lo], axis=1), later, preferred_element_type=F32)


def _key_rows(kb):
    return pl.ds(pl.multiple_of(kb * KEY_BLOCK, KEY_BLOCK), KEY_BLOCK)


def _sb_mask(q_row0, k_row0, tq):
    tpos = q_row0 + lax.broadcasted_iota(jnp.int32, (tq, KEY_BLOCK), 0)
    spos = k_row0 + lax.broadcasted_iota(jnp.int32, (tq, KEY_BLOCK), 1)
    return (spos < tpos) & (spos >= META_START)


def _key_block_phases(iq, per_q, block, ascending, per_step):
    first_diag = iq * per_q

    def nth(lo, n, i):
        return lo + i if ascending else lo + n - 1 - i

    def run(lo, n, masked):
        if isinstance(n, int):
            for i in range(n):
                j = i if ascending else n - 1 - i
                block(lo + j, masked, j * KEY_BLOCK)
            return

        def group(i, carry):
            for j in range(per_step):
                block(nth(lo, n, per_step * i + j), masked, 0)
            return carry

        def single(i, carry):
            block(nth(lo, n, n - rest + i), masked, 0)
            return carry

        rest = jnp.bitwise_and(n, per_step - 1)
        lax.fori_loop(0, jnp.right_shift(n, per_step.bit_length() - 1), group, 0)
        lax.fori_loop(0, rest, single, 0)

    phases = [(0, jnp.minimum(iq, 1), True), (1, jnp.maximum(first_diag - 1, 0), False), (first_diag, per_q, True)]
    for lo, n, masked in (phases if ascending else phases[::-1]):
        run(lo, n, masked)


def _attn_fwd(name, q, kv):
    T, D = q.shape
    n_hp = D // 128
    tq = _row_tile(T, 512)
    per_q = tq // KEY_BLOCK
    scale = 1.0 / math.sqrt(HEAD_DIM)

    chains = [(h, r) for h in range(2) for r in range(2)]
    cols = lambda h: slice(h * KEY_BLOCK, (h + 1) * KEY_BLOCK)

    def body(q_ref, k_ref, v_ref, o_ref, l_ref, z_buf, w_buf, acc_ref, run_ref):
        iq = pl.program_id(1)
        n_kb = (iq + 1) * per_q
        qs = q_ref[...] * jnp.asarray(scale, BF16)
        later = _sums_matrix(True, 2)

        def scores(kb):
            return lax.dot_general(qs, _split_heads(k_ref[_key_rows(kb), :]), NT_DIMS, preferred_element_type=F32)

        def weighted_values(kb):
            return jnp.dot(w_buf[...], _split_heads(v_ref[_key_rows(kb), :]), preferred_element_type=F32)

        acc_ref[...] = jnp.zeros_like(acc_ref)
        run_ref[...] = jnp.zeros_like(run_ref)
        w_buf[...] = jnp.zeros_like(w_buf)
        z_buf[...] = scores(n_kb - 1)

        def block(kb, masked, row0):
            acc_ref[...] += weighted_values(jnp.minimum(kb + 1, n_kb - 1))
            z_next = scores(jnp.maximum(kb - 1, 0))
            half = (tq - row0) // 2
            rows = lambda r: slice(row0 + r * half, row0 + (r + 1) * half)
            if row0:
                w_buf[:row0, :] = jnp.zeros((row0, 2 * KEY_BLOCK), BF16)
            masks = [_sb_mask(iq * tq + row0 + r * half, kb * KEY_BLOCK, half) if masked else None for r in range(2)]
            first = [_sb_scores(z_buf[rows(r), cols(h)], masks[r], later) for h, r in chains]
            for (h, r), (lb, sums) in zip(chains, first):
                run = run_ref[h, rows(r), :]
                w = jnp.exp2(lb + sums[:, :KEY_BLOCK] + run)
                if masked:
                    w = jnp.where(masks[r], w, 0.0)
                w_buf[rows(r), cols(h)] = w.astype(BF16)
                run_ref[h, rows(r), :] = run + sums[:, KEY_BLOCK:]
            z_buf[...] = z_next

        _key_block_phases(iq, per_q, block, ascending=False, per_step=4)
        acc_ref[...] += weighted_values(0)
        lane = lax.broadcasted_iota(jnp.int32, (tq, 128), 1)
        o_ref[...] = acc_ref[...].astype(o_ref.dtype)
        l_ref[...] = jnp.where(lane < HEAD_DIM, run_ref[0], run_ref[1])

    blk = pl.BlockSpec((tq, 128), lambda h, i: (i, h))
    return pl.pallas_call(
        body, name=name, grid=(n_hp, T // tq),
        in_specs=[blk, pl.BlockSpec((T, 128), lambda h, i: (0, h)), pl.BlockSpec((T, 128), lambda h, i: (0, n_hp + h))],
        out_specs=[blk, blk],
        out_shape=[jax.ShapeDtypeStruct((T, D), BF16), jax.ShapeDtypeStruct((T, D), F32)],
        scratch_shapes=[pltpu.VMEM((tq, 2 * KEY_BLOCK), F32), pltpu.VMEM((tq, 2 * KEY_BLOCK), BF16),
                        pltpu.VMEM((tq, 128), F32), pltpu.VMEM((2, tq, 128), F32)],
        compiler_params=_cparams("parallel", "arbitrary"),
    )(q, kv, kv)


def _attn_bwd(name, q, kv, do, ltot):
    T, D = q.shape
    n_hp = D // 128
    tq = _row_tile(T, 512)
    n_q = T // tq
    per_q = tq // KEY_BLOCK
    scale = 1.0 / math.sqrt(HEAD_DIM)

    chains = [(h, r) for h in range(2) for r in range(2)]
    head_cols = lambda h: slice(h * KEY_BLOCK, (h + 1) * KEY_BLOCK)

    def body(q_ref, k_ref, v_ref, do_ref, l_ref, dq_ref, dk_ref, dv_ref,
             dk_acc, dv_acc, dq_acc, lpre_ref, cpre_ref, z_buf, dw_buf, dz_buf, w_buf):
        iq = pl.program_id(1)
        n_kb = (iq + 1) * per_q

        @pl.when(iq == 0)
        def _():
            dk_acc[...] = jnp.zeros_like(dk_acc)
            dv_acc[...] = jnp.zeros_like(dv_acc)

        first = lax.broadcasted_iota(jnp.int32, (tq, 128), 1) < HEAD_DIM
        qs = q_ref[...] * jnp.asarray(scale, BF16)
        dov = do_ref[...]
        ltv = l_ref[...]
        swapped = pltpu.roll(ltv, HEAD_DIM, 1)
        ltot = [jnp.where(first, ltv, swapped), jnp.where(first, swapped, ltv)]
        zero = jnp.zeros_like(qs)
        q_stack = jnp.concatenate([jnp.where(first, qs, zero), jnp.where(first, zero, qs)], axis=0)
        do_stack = jnp.concatenate([jnp.where(first, dov, zero), jnp.where(first, zero, dov)], axis=0)
        later = _sums_matrix(True, 2)
        earlier = _sums_matrix(False, 1)

        def scores(kb):
            rows = _key_rows(kb)
            return (lax.dot_general(qs, _split_heads(k_ref[rows, :]), NT_DIMS, preferred_element_type=F32),
                    lax.dot_general(dov, _split_heads(v_ref[rows, :]), NT_DIMS, preferred_element_type=F32))

        def flush(kb):
            rows = _key_rows(kb)
            k_heads = _split_heads(k_ref[rows, :])
            dq_acc[...] += (jnp.dot(dz_buf[:tq, :], k_heads[:KEY_BLOCK, :], preferred_element_type=F32)
                            + jnp.dot(dz_buf[tq:, :], k_heads[KEY_BLOCK:, :], preferred_element_type=F32))
            dk_acc[rows, :] += lax.dot_general(dz_buf[...], q_stack, TN_DIMS, preferred_element_type=F32)
            dv_acc[rows, :] += lax.dot_general(w_buf[...], do_stack, TN_DIMS, preferred_element_type=F32)

        dq_acc[...] = jnp.zeros_like(dq_acc)
        lpre_ref[...] = jnp.zeros_like(lpre_ref)
        cpre_ref[...] = jnp.zeros_like(cpre_ref)
        dz_buf[...] = jnp.zeros_like(dz_buf)
        w_buf[...] = jnp.zeros_like(w_buf)
        z_buf[...], dw_buf[...] = scores(0)

        def block(kb, masked, row0):
            flush(jnp.maximum(kb - 1, 0))
            z_next, dw_next = scores(jnp.minimum(kb + 1, n_kb - 1))
            half = (tq - row0) // 2
            row_half = lambda r: slice(row0 + r * half, row0 + (r + 1) * half)
            stacked = lambda h, r: slice(h * tq + row0 + r * half, h * tq + row0 + (r + 1) * half)
            if row0:
                for h in range(2):
                    w_buf[h * tq:h * tq + row0, :] = jnp.zeros((row0, 128), BF16)
                    dz_buf[h * tq:h * tq + row0, :] = jnp.zeros((row0, 128), BF16)
            masks = [_sb_mask(iq * tq + row0 + r * half, kb * KEY_BLOCK, half) if masked else None for r in range(2)]
            first_stage = [_sb_scores(z_buf[row_half(r), head_cols(h)], masks[r], later) for h, r in chains]
            second_stage = []
            for (h, r), (lb, sums) in zip(chains, first_stage):
                after = ltot[h][row_half(r), :] - lpre_ref[h, row_half(r), :] - sums[:, KEY_BLOCK:]
                w = jnp.exp2(lb + sums[:, :KEY_BLOCK] + after)
                if masked:
                    w = jnp.where(masks[r], w, 0.0)
                da = w * dw_buf[row_half(r), head_cols(h)]
                w_buf[stacked(h, r), :] = w.astype(BF16)
                lpre_ref[h, row_half(r), :] += sums[:, KEY_BLOCK:]
                second_stage.append((da, jnp.dot(da.astype(BF16), earlier, preferred_element_type=F32)))
            for (h, r), (lb, _), (da, dsums) in zip(chains, first_stage, second_stage):
                sig = jnp.exp2(lb)
                through_later = sig * (dsums[:, :KEY_BLOCK] + cpre_ref[h, row_half(r), :])
                if masked:
                    through_later = jnp.where(masks[r], through_later, 0.0)
                dz_buf[stacked(h, r), :] = (da * (1.0 - sig) - through_later).astype(BF16)
                cpre_ref[h, row_half(r), :] += dsums[:, KEY_BLOCK:]
            z_buf[...] = z_next
            dw_buf[...] = dw_next

        _key_block_phases(iq, per_q, block, ascending=True, per_step=4)
        flush(n_kb - 1)
        dq_ref[...] = (dq_acc[...] * scale).astype(dq_ref.dtype)

        @pl.when(iq == n_q - 1)
        def _():
            dk_ref[...] = dk_acc[...].astype(dk_ref.dtype)
            dv_ref[...] = dv_acc[...].astype(dv_ref.dtype)

    blk = pl.BlockSpec((tq, 128), lambda h, i: (i, h))
    full = pl.BlockSpec((T, 128), lambda h, i: (0, h))
    return pl.pallas_call(
        body, name=name, grid=(n_hp, n_q),
        in_specs=[blk, full, pl.BlockSpec((T, 128), lambda h, i: (0, n_hp + h)), blk, blk],
        out_specs=[blk, full, full],
        out_shape=[jax.ShapeDtypeStruct((T, D), BF16)] * 3,
        scratch_shapes=[pltpu.VMEM((T, 128), F32), pltpu.VMEM((T, 128), F32), pltpu.VMEM((tq, 128), F32),
                        pltpu.VMEM((2, tq, 128), F32), pltpu.VMEM((2, tq, 128), F32),
                        pltpu.VMEM((tq, 2 * KEY_BLOCK), F32), pltpu.VMEM((tq, 2 * KEY_BLOCK), F32),
                        pltpu.VMEM((2 * tq, 128), BF16), pltpu.VMEM((2 * tq, 128), BF16)],
        compiler_params=_cparams("parallel", "arbitrary"),
    )(q, kv, kv, do, ltot)


def _adamw(name, w, g, m, v):
    shape = w.shape
    size = w.size
    if w.ndim >= 2 and shape[-1] % 128 == 0:
        cols = shape[-1]
    else:
        cols = 1024 if size % 1024 == 0 else shape[-1]
    rows = size // cols
    tm = _row_tile(rows, max(8, (1024 * 1024) // (4 * cols) // 8 * 8)) if rows % 8 == 0 else rows
    c1 = 1.0 / (1.0 - ADAM_B1 ** ADAM_STEP)
    c2 = 1.0 / (1.0 - ADAM_B2 ** ADAM_STEP)

    def body(w_ref, g_ref, m_ref, v_ref, d_ref, nm_ref, nv_ref):
        gv = g_ref[...]
        nm = ADAM_B1 * m_ref[...] + (1.0 - ADAM_B1) * gv
        nv = ADAM_B2 * v_ref[...] + (1.0 - ADAM_B2) * (gv * gv)
        d_ref[...] = -ADAM_LR * ((nm * c1) / (jnp.sqrt(nv * c2) + ADAM_EPS) + ADAM_WD * w_ref[...])
        nm_ref[...] = nm
        nv_ref[...] = nv

    blk = pl.BlockSpec((tm, cols), lambda i: (i, 0))
    outs = pl.pallas_call(
        body, name=name, grid=(rows // tm,),
        in_specs=[blk] * 4, out_specs=[blk] * 3,
        out_shape=[jax.ShapeDtypeStruct((rows, cols), F32)] * 3,
        compiler_params=_cparams("parallel"),
    )(*[t.reshape(rows, cols) for t in (w, g, m, v)])
    return tuple(o.reshape(shape) for o in outs)


def _any_specs(n):
    return [pl.BlockSpec(memory_space=pl.ANY)] * n


def _chip_index():
    return 2 * lax.axis_index("x") + lax.axis_index("y")


def _place():
    x, y, c = lax.axis_index("x"), lax.axis_index("y"), lax.axis_index("c")
    chips = [(1 - x, y), (x, 1 - y), (1 - x, 1 - y)]
    return x, y, c, chips


def _all_gather_chips(name, shards):
    n = len(shards)

    def body(*refs):
        x_refs, o_refs = refs[:n], refs[n:2 * n]
        send_sems, recv_sems = refs[2 * n:]
        x, y, c, chips = _place()
        me = 2 * x + y
        sibling = (x, y, 1 - c)

        def half(ref, i, which):
            h = shards[i].shape[0] // 2
            return ref.at[pl.ds(which * h, h)]

        def remote(k, i, src, dst, to):
            return pltpu.make_async_remote_copy(src_ref=src, dst_ref=dst, send_sem=send_sems.at[k, i],
                                                recv_sem=recv_sems.at[k, i], device_id=to, device_id_type=MESH)

        sent = []
        for j, chip in enumerate(chips):
            for i in range(n):
                cp = remote(j, i, half(x_refs[i], i, c), half(o_refs[i].at[me], i, c), (*chip, c))
                cp.start()
                sent.append(cp)
        for j, chip in enumerate(chips):
            pj = 2 * chip[0] + chip[1]
            for i in range(n):
                landed = half(o_refs[i].at[pj], i, c)
                remote(j, i, landed, landed, (*chip, c)).wait_recv()
                cp = remote(3 + j, i, landed, landed, sibling)
                cp.start()
                sent.append(cp)
        for j, chip in enumerate(chips):
            pj = 2 * chip[0] + chip[1]
            for i in range(n):
                got = half(o_refs[i].at[pj], i, 1 - c)
                remote(3 + j, i, got, got, sibling).wait_recv()
        for cp in sent:
            cp.wait_send()

    outs = pl.pallas_call(
        body, name=name,
        in_specs=_any_specs(n), out_specs=_any_specs(n),
        out_shape=[jax.ShapeDtypeStruct((4,) + s.shape, s.dtype) for s in shards],
        scratch_shapes=[pltpu.SemaphoreType.DMA((6, n)), pltpu.SemaphoreType.DMA((6, n))],
    )(*shards)
    return [lax.dynamic_update_slice(o, s[None], (_chip_index(), 0, 0)) for o, s in zip(outs, shards)]


def _pair_split(name, grads):
    n = len(grads)

    def body(*refs):
        g_refs, got_refs = refs[:n], refs[n:2 * n]
        send_sems, recv_sems = refs[2 * n:]
        x, y, c, _ = _place()
        sibling = (x, y, 1 - c)
        sent = []
        for i in range(n):
            h = grads[i].shape[1] // 2
            rc = pltpu.make_async_remote_copy(
                src_ref=g_refs[i].at[:, pl.ds((1 - c) * h, h)], dst_ref=got_refs[i],
                send_sem=send_sems.at[i], recv_sem=recv_sems.at[i], device_id=sibling, device_id_type=MESH)
            rc.start()
            sent.append(rc)
        for rc in sent:
            rc.wait()

    return pl.pallas_call(
        body, name=name,
        in_specs=_any_specs(n), out_specs=_any_specs(n),
        out_shape=[jax.ShapeDtypeStruct((4, g.shape[1] // 2, g.shape[2]), g.dtype) for g in grads],
        scratch_shapes=[pltpu.SemaphoreType.DMA((n,)), pltpu.SemaphoreType.DMA((n,))],
    )(*grads)


def _chip_exchange(name, sums):
    return _exchange_finish(sums, _run_comm(name, _exchange_comm(sums), sums))


class _Comm:
    def __init__(self, out_shape, copies):
        self.n = len(out_shape)
        self.out_shape = out_shape
        self.copies = copies
        self.scratch = [pltpu.SemaphoreType.DMA((3, self.n)), pltpu.SemaphoreType.DMA((3, self.n))]

    def start(self, *refs):
        for cp in self.copies(*refs, False):
            cp.start()

    def finish(self, *refs):
        for cp in self.copies(*refs, True):
            cp.wait_recv()
        for cp in self.copies(*refs, False):
            cp.wait_send()


def _exchange_comm(sums):
    n = len(sums)

    def copies(s_refs, o_refs, send_sems, recv_sems, mirrors):
        x, y, c, chips = _place()
        me = 2 * x + y
        out = []
        for j, chip in enumerate(chips):
            pj = 2 * chip[0] + chip[1]
            for i in range(n):
                out.append(pltpu.make_async_remote_copy(
                    src_ref=s_refs[i].at[pj], dst_ref=o_refs[i].at[pj if mirrors else me], send_sem=send_sems.at[j, i],
                    recv_sem=recv_sems.at[j, i], device_id=(*chip, c), device_id_type=MESH))
        return out

    return _Comm([jax.ShapeDtypeStruct(s.shape, s.dtype) for s in sums], copies)


def _exchange_finish(sums, outs):
    me = _chip_index()
    return [lax.dynamic_update_slice(o, lax.dynamic_index_in_dim(s, me, 0, keepdims=True), (me, 0, 0))
            for o, s in zip(outs, sums)]


def _halves_gather_comm(shards):
    n = len(shards)

    def copies(x_refs, o_refs, send_sems, recv_sems, mirrors):
        x, y, c, chips = _place()
        me = 2 * x + y
        out = []
        for j, chip in enumerate(chips):
            pj = 2 * chip[0] + chip[1]
            for i in range(n):
                h = shards[i].shape[0] // 2
                rows = pl.ds(c * h, h)
                out.append(pltpu.make_async_remote_copy(
                    src_ref=x_refs[i].at[rows], dst_ref=o_refs[i].at[pj if mirrors else me, rows],
                    send_sem=send_sems.at[j, i], recv_sem=recv_sems.at[j, i], device_id=(*chip, c), device_id_type=MESH))
        return out

    return _Comm([jax.ShapeDtypeStruct((4,) + s.shape, s.dtype) for s in shards], copies)


def _run_comm(name, comm, arrays):
    n = comm.n

    def body(*refs):
        comm.start(refs[:n], refs[n:2 * n], *refs[2 * n:])
        comm.finish(refs[:n], refs[n:2 * n], *refs[2 * n:])

    return pl.pallas_call(
        body, name=name, in_specs=_any_specs(n), out_specs=_any_specs(n),
        out_shape=comm.out_shape, scratch_shapes=comm.scratch,
    )(*arrays)


def _embed_comm(comm, body, grid, n_in, n_out):
    n = comm.n

    def wrapped(*refs):
        ins, c_in = refs[:n_in], refs[n_in:n_in + n]
        outs, c_out = refs[n_in + n:n_in + n + n_out], refs[n_in + n + n_out:n_in + 2 * n + n_out]
        scratch, sems = refs[n_in + 2 * n + n_out:-2], refs[-2:]
        ids = [pl.program_id(a) for a in range(len(grid))]
        first = functools.reduce(jnp.logical_and, [i == 0 for i in ids])
        last = functools.reduce(jnp.logical_and, [i == g - 1 for i, g in zip(ids, grid)])

        @pl.when(first)
        def _():
            comm.start(c_in, c_out, *sems)

        body(*ins, *outs, *scratch)

        @pl.when(last)
        def _():
            comm.finish(c_in, c_out, *sems)

    return wrapped


def _pair_forward(name, landed, shards):
    n = len(landed)

    def body(*refs):
        o_refs = refs[n:2 * n]
        send_sems, recv_sems = refs[2 * n:]
        x, y, c, chips = _place()
        sibling = (x, y, 1 - c)
        sent, arriving = [], []
        for j, chip in enumerate(chips):
            pj = 2 * chip[0] + chip[1]
            for i in range(n):
                h = shards[i].shape[0] // 2
                mine = o_refs[i].at[pj, pl.ds(c * h, h)]
                theirs = o_refs[i].at[pj, pl.ds((1 - c) * h, h)]
                for ref, group in ((mine, sent), (theirs, arriving)):
                    group.append(pltpu.make_async_remote_copy(
                        src_ref=ref, dst_ref=ref, send_sem=send_sems.at[j, i], recv_sem=recv_sems.at[j, i],
                        device_id=sibling, device_id_type=MESH))
        for cp in sent:
            cp.start()
        for cp in arriving:
            cp.wait_recv()
        for cp in sent:
            cp.wait_send()

    outs = pl.pallas_call(
        body, name=name, in_specs=_any_specs(n), out_specs=_any_specs(n),
        out_shape=[jax.ShapeDtypeStruct(a.shape, a.dtype) for a in landed],
        input_output_aliases={i: i for i in range(n)},
        scratch_shapes=[pltpu.SemaphoreType.DMA((3, n)), pltpu.SemaphoreType.DMA((3, n))],
    )(*landed)
    return [lax.dynamic_update_slice(o, s[None], (_chip_index(), 0, 0)) for o, s in zip(outs, shards)]


def _pair_join(name, halves):
    n = len(halves)

    def body(*refs):
        h_refs, o_refs = refs[:n], refs[n:2 * n]
        send_sems, recv_sems = refs[2 * n:]
        x, y, c, _ = _place()
        sibling = (x, y, 1 - c)
        sent = []
        for i in range(n):
            h = halves[i].shape[0]
            mine = o_refs[i].at[pl.ds(c * h, h)]
            rc = pltpu.make_async_remote_copy(
                src_ref=h_refs[i], dst_ref=mine, send_sem=send_sems.at[i], recv_sem=recv_sems.at[i],
                device_id=sibling, device_id_type=MESH)
            rc.start()
            sent.append(rc)
        for i in range(n):
            h = halves[i].shape[0]
            theirs = o_refs[i].at[pl.ds((1 - c) * h, h)]
            pltpu.make_async_remote_copy(
                src_ref=h_refs[i], dst_ref=theirs, send_sem=send_sems.at[i], recv_sem=recv_sems.at[i],
                device_id=sibling, device_id_type=MESH).wait_recv()
        for rc in sent:
            rc.wait_send()

    outs = pl.pallas_call(
        body, name=name,
        in_specs=_any_specs(n), out_specs=_any_specs(n),
        out_shape=[jax.ShapeDtypeStruct((2 * s.shape[0], s.shape[1]), s.dtype) for s in halves],
        scratch_shapes=[pltpu.SemaphoreType.DMA((n,)), pltpu.SemaphoreType.DMA((n,))],
    )(*halves)
    c = lax.axis_index("c")
    return [lax.dynamic_update_slice(o, s, (c * s.shape[0], 0)) for o, s in zip(outs, halves)]


def _add_pair(name, a, b):
    _, H, C = a.shape
    th = _row_tile(H, max(8, (1024 * 1024) // (4 * C) // 8 * 8))

    def body(a_ref, b_ref, o_ref):
        o_ref[...] = (a_ref[...].astype(F32) + b_ref[...].astype(F32)).astype(o_ref.dtype)

    blk = pl.BlockSpec((None, th, C), lambda q, i: (q, i, 0))
    return pl.pallas_call(
        body, name=name, grid=(4, H // th), in_specs=[blk, blk], out_specs=blk,
        out_shape=jax.ShapeDtypeStruct(a.shape, a.dtype), compiler_params=_cparams("parallel", "parallel"),
    )(a, b)


def _add_chips(name, parts):
    _, H, C = parts.shape
    th = _row_tile(H, max(8, (1024 * 1024) // (4 * C) // 8 * 8))

    def body(p_ref, o_ref):
        acc = p_ref[0].astype(F32)
        for q in range(1, 4):
            acc = acc + p_ref[q].astype(F32)
        o_ref[...] = acc

    return pl.pallas_call(
        body, name=name, grid=(H // th,),
        in_specs=[pl.BlockSpec((4, th, C), lambda i: (0, i, 0))], out_specs=pl.BlockSpec((th, C), lambda i: (i, 0)),
        out_shape=jax.ShapeDtypeStruct((H, C), F32), compiler_params=_cparams("parallel"),
    )(parts)


def _pair_sums(tag, grads):
    c = lax.axis_index("c")
    got = _pair_split(f"rs_pair_split_{tag}", grads)
    kept = [lax.dynamic_slice_in_dim(g, c * (g.shape[1] // 2), g.shape[1] // 2, axis=1) for g in grads]
    return [_add_pair(f"rs_add_pair_{tag}_{i}", k, g) for i, (k, g) in enumerate(zip(kept, got))]


def _chip_sums(parts):
    return _pair_join("rs_pair_join", [_add_chips(f"rs_add_chips_{i}", p) for i, p in enumerate(parts)])


def _block_diag(t):
    G, A, B = t.shape
    eye = jnp.eye(8, dtype=t.dtype)
    return jnp.einsum("sgab,gh->sgahb", t.reshape(G // 8, 8, A, B), eye).reshape(G // 8, 8 * A, 8 * B)


def _block_diag_extract(m, A, B):
    n = m.shape[0]
    eye = jnp.eye(8, dtype=m.dtype)
    return jnp.einsum("sgahb,gh->sgab", m.reshape(n, 8, A, 8, B), eye).reshape(8 * n, A, B)


def kernel(x, meta_tokens, norm_mix, norm_ffn, s5_a_re, s5_a_im, s5_log_dt, s5_b_re, s5_b_im, s5_c_re, s5_c_im, s5_d, s5_w_glu, norm_kv, w_kv, w_q, w_o, w_ffn_in, w_ffn_out, norm_final, loss_target, m_meta_tokens, m_norm_mix, m_norm_ffn, m_s5_a_re, m_s5_a_im, m_s5_log_dt, m_s5_b_re, m_s5_b_im, m_s5_c_re, m_s5_c_im, m_s5_d, m_s5_w_glu, m_norm_kv, m_w_kv, m_w_q, m_w_o, m_w_ffn_in, m_w_ffn_out, m_norm_final, v_meta_tokens, v_norm_mix, v_norm_ffn, v_s5_a_re, v_s5_a_im, v_s5_log_dt, v_s5_b_re, v_s5_b_im, v_s5_c_re, v_s5_c_im, v_s5_d, v_s5_w_glu, v_norm_kv, v_w_kv, v_w_q, v_w_o, v_w_ffn_in, v_w_ffn_out, v_norm_final):
    seq, D = x.shape[1], x.shape[2]
    T = X_START + seq
    G, P, C = s5_a_re.shape[1], S5_STATE, S5_GROUP
    d_ff = w_ffn_out.shape[1] * 4
    dq4 = D // 4
    chip = 2 * lax.axis_index("x") + lax.axis_index("y")

    small_in = jnp.concatenate([meta_tokens, jnp.pad(s5_d, ((0, 15), (0, 0)))], axis=0)
    (small_all,) = _all_gather_chips("ag_small", [small_in])
    meta_full = small_all[:, :N_META, :].transpose(1, 0, 2).reshape(N_META, D)
    d_skip = small_all[:, N_META, :].reshape(1, D)
    shards = [s.astype(BF16) for s in (s5_w_glu[0], w_kv, w_q[0], w_o[0],
                                       w_ffn_in.reshape(2 * D, -1), w_ffn_out.reshape(-1, D))]
    rows_out = d_ff // 4

    row = lambda v: v.reshape(1, -1)
    g_mix0, g_mix1 = row(norm_mix[0]), row(norm_mix[1])
    g_ffn = [row(norm_ffn[0]), row(norm_ffn[1])]
    g_kv, g_final = row(norm_kv), row(norm_final)

    a_re3 = s5_a_re[0].reshape(G, 1, P)
    a_im3 = s5_a_im[0].reshape(G, 1, P)
    log_dt3 = s5_log_dt[0].reshape(G, 1, 1)
    bt_re = s5_b_re[0].transpose(0, 2, 1)
    bt_im = s5_b_im[0].transpose(0, 2, 1)
    ab_re, ab_im, bb_re, bb_im = _s5_prep_fwd("s5_prep", a_re3, a_im3, log_dt3, bt_re, bt_im)
    abar_re, abar_im = ab_re.reshape(1, G * P), ab_im.reshape(1, G * P)
    bd_b_re = _block_diag(bb_re).astype(BF16)
    bd_b_im = _block_diag(bb_im).astype(BF16)
    bd_bt_re = bd_b_re.transpose(0, 2, 1)
    bd_bt_im = bd_b_im.transpose(0, 2, 1)
    bd_c_re = _block_diag(s5_c_re[0]).astype(BF16)
    bd_c_im = _block_diag(s5_c_im[0]).astype(BF16)
    bd_ct_re = bd_c_re.transpose(0, 2, 1)
    bd_ct_im = bd_c_im.transpose(0, 2, 1)

    h0 = jnp.concatenate([jnp.zeros((META_START, D), F32), meta_full, x[0]], axis=0)
    (u,) = _rmsnorm_fwd("norm_mix0", h0, [g_mix0], [F32])
    (y, z, x_re, x_im), landed = _s5_fwd(
        "s5_scan", u, bd_b_re, bd_b_im, bd_ct_re, bd_ct_im, abar_re, abar_im, d_skip,
        comm=_halves_gather_comm(shards), comm_args=shards)
    wg_glu, wg_kv, wg_q, wg_o, wg_in, wg_out = _pair_forward("ag_forward", landed, shards)
    wg_q = wg_q.reshape(1, D, D)
    wg_o = wg_o.reshape(1, D, D)
    wg_out = [wg_out[:, l * rows_out:(l + 1) * rows_out, :].reshape(1, d_ff, D) for l in range(2)]
    val, gate, h1 = _mm_glu("glu_proj", z, wg_glu, h0)

    def ffn_fwd(l, h):
        (n,) = _rmsnorm_fwd(f"norm_ffn{l}", h, [g_ffn[l]], [BF16])
        g, u, mid = _mm_swiglu(f"ffn_in{l}", n, wg_in, l)
        return n, (g, u), mid, _mm_nn(f"ffn_out{l}", mid, wg_out[l], res=h)

    n1, gu0, mid0, h2 = ffn_fwd(0, h1)
    nk, nq = _rmsnorm_fwd("norm_kv_q", h2, [g_kv, g_mix1], [BF16, BF16])
    kv = _mm_nn("kv_proj", nk, wg_kv, out_dtype=BF16)
    q = _mm_nn("q_proj", nq, wg_q, out_dtype=BF16)
    o, ltot = _attn_fwd("attn_fwd", q, kv)
    h3 = _mm_nn("o_proj", o, wg_o, res=h2)
    n3, gu1, mid1, h4 = ffn_fwd(1, h3)
    loss_part, dh4, dg_final = _final_loss("final_loss", h4, g_final, loss_target[0])

    def ffn_bwd(l, dh, h, n, gu, mid):
        dgu = _mm_nt_swiglu_bwd(f"ffn_out{l}_dx", dh, wg_out[l], *gu)
        dw_out = _mm_tn(f"ffn_out{l}_dw", mid, dh, 1)
        dw_in = _mm_tn(f"ffn_in{l}_dw", n, dgu, 4)
        dn = _mm_nt_k(f"ffn_in{l}_dx", dgu, wg_in, D, k_blk=l)
        dh_prev, (dg,) = _rmsnorm_bwd(f"norm_ffn{l}_bwd", h, [(g_ffn[l], dn)], dh)
        return dh_prev, dg, dw_in, dw_out

    dh3, dg_ffn1, dw_in1, dw_out1 = ffn_bwd(1, dh4, h3, n3, gu1, mid1)
    d_o = _mm_nt_k("o_proj_dx", dh3, wg_o, D, out_dtype=BF16)
    dw_o = _mm_tn("o_proj_dw", o, dh3, 1)
    dq, dk, dv = _attn_bwd("attn_bwd", q, kv, d_o, ltot)
    dkv = jnp.concatenate([dk, dv], axis=1)
    dw_q = _mm_tn("q_proj_dw", nq, dq, 1)
    dnq = _mm_nt_k("q_proj_dx", dq, wg_q, D)
    dw_kv = _mm_tn("kv_proj_dw", nk, dkv, 4)
    dnk = _mm_nt_k("kv_proj_dx", dkv, wg_kv, D)
    dh2, (dg_mix1, dg_kv) = _rmsnorm_bwd("norm_kv_q_bwd", h2, [(g_mix1, dnq), (g_kv, dnk)], dh3)
    dh1, dg_ffn0, dw_in0, dw_out0 = ffn_bwd(0, dh2, h1, n1, gu0, mid0)
    dvg = _glu_bwd("glu_bwd", val, gate, dh1)
    dw_glu = _mm_tn("glu_proj_dw", z, dvg, 4)
    dz = _mm_nt_k("glu_proj_dx", dvg, wg_glu, D)
    big = [dw_kv, dw_q.reshape(4, D // 4, D), dw_o.reshape(4, D // 4, D), dw_in1, dw_out1.reshape(4, rows_out, D),
           dw_glu, dw_in0, dw_out0.reshape(4, rows_out, D)]
    big_pairs = _pair_sums("big", big)
    (du, dd, dbd_b_re, dbd_b_im, dbd_c_re, dbd_c_im, dab_re, dab_im), big_parts = _s5_bwd(
        "s5_scan_bwd", dz, y, u, x_re, x_im, bd_c_re, bd_c_im, bd_bt_re, bd_bt_im, abar_re, abar_im, d_skip,
        comm=_exchange_comm(big_pairs), comm_args=big_pairs)
    big_parts = _exchange_finish(big_pairs, big_parts)
    dh0, (dg_mix0,) = _rmsnorm_bwd("norm_mix0_bwd", h0, [(g_mix0, du)], dh1)
    da_re, da_im, dlog_dt, dbt_re, dbt_im = _s5_prep_bwd(
        "s5_prep_bwd", a_re3, a_im3, log_dt3, bt_re, bt_im,
        dab_re.reshape(G, 1, P), dab_im.reshape(G, 1, P),
        _block_diag_extract(dbd_b_re, C, P), _block_diag_extract(dbd_b_im, C, P))
    grad_x = dh0[X_START:][None]

    small_parts = [
        dg_mix0, dg_mix1, dg_ffn0, dg_ffn1, da_re, da_im,
        dbt_re.transpose(0, 2, 1), dbt_im.transpose(0, 2, 1),
        _block_diag_extract(dbd_c_re, C, P), _block_diag_extract(dbd_c_im, C, P),
        dg_kv, dg_final, dh0[META_START:X_START], dd, loss_part, dlog_dt]
    small_sizes = [p.size for p in small_parts]
    unit = 4 * 2 * 8 * 128
    padded = -(-sum(small_sizes) // unit) * unit
    tail = jnp.concatenate([loss_part.reshape(-1), dlog_dt.reshape(-1)])
    small_flat = jnp.concatenate(
        [p.reshape(-1) for p in small_parts[:-2]] + [jnp.pad(tail, (0, padded - sum(small_sizes)))])
    small_blocks = small_flat.reshape(4, padded // (4 * 128), 128)
    small_exchanged = _chip_exchange("rs_chip_exchange", _pair_sums("small", [small_blocks]))
    gw_kv, gw_q, gw_o, gw_in1, gw_out1, gw_glu, gw_in0, gw_out0, small_mine = _chip_sums(big_parts + small_exchanged)
    gw_in = jnp.concatenate([gw_in0, gw_in1], axis=0)
    gw_out = jnp.concatenate([gw_out0, gw_out1], axis=0)
    (small_red,) = _all_gather_chips("ag_small_grads", [small_mine])
    small_red = small_red.reshape(-1)
    pieces, at = [], 0
    for p, size in zip(small_parts, small_sizes):
        pieces.append(small_red[at:at + size].reshape(p.shape))
        at += size
    (gn_mix0, gn_mix1, gn_ffn0, gn_ffn1, ga_re, ga_im, gb_re, gb_im, gc_re, gc_im, gn_kv, gn_final,
     gmeta_full, gd_full, loss_all, glog_dt) = pieces
    loss = loss_all[0, 0]
    gn_mix = small_red[:2 * D].reshape(2, D)
    gn_ffn = small_red[2 * D:4 * D].reshape(2, D)
    gmeta = lax.dynamic_slice_in_dim(gmeta_full, chip * dq4, dq4, axis=1)
    gd = lax.dynamic_slice_in_dim(gd_full, chip * dq4, dq4, axis=1)

    grads = {
        "meta_tokens": gmeta, "norm_mix": gn_mix, "norm_ffn": gn_ffn,
        "s5_a_re": ga_re.reshape(s5_a_re.shape), "s5_a_im": ga_im.reshape(s5_a_im.shape),
        "s5_log_dt": glog_dt.reshape(s5_log_dt.shape),
        "s5_b_re": gb_re.reshape(s5_b_re.shape), "s5_b_im": gb_im.reshape(s5_b_im.shape),
        "s5_c_re": gc_re.reshape(s5_c_re.shape), "s5_c_im": gc_im.reshape(s5_c_im.shape),
        "s5_d": gd, "s5_w_glu": gw_glu.reshape(s5_w_glu.shape), "norm_kv": gn_kv.reshape(norm_kv.shape),
        "w_kv": gw_kv, "w_q": gw_q.reshape(w_q.shape), "w_o": gw_o.reshape(w_o.shape),
        "w_ffn_in": gw_in.reshape(w_ffn_in.shape), "w_ffn_out": gw_out.reshape(w_ffn_out.shape),
        "norm_final": gn_final.reshape(norm_final.shape),
    }
    weights = {
        "meta_tokens": (meta_tokens, m_meta_tokens, v_meta_tokens), "norm_mix": (norm_mix, m_norm_mix, v_norm_mix),
        "norm_ffn": (norm_ffn, m_norm_ffn, v_norm_ffn), "s5_a_re": (s5_a_re, m_s5_a_re, v_s5_a_re),
        "s5_a_im": (s5_a_im, m_s5_a_im, v_s5_a_im), "s5_log_dt": (s5_log_dt, m_s5_log_dt, v_s5_log_dt),
        "s5_b_re": (s5_b_re, m_s5_b_re, v_s5_b_re), "s5_b_im": (s5_b_im, m_s5_b_im, v_s5_b_im),
        "s5_c_re": (s5_c_re, m_s5_c_re, v_s5_c_re), "s5_c_im": (s5_c_im, m_s5_c_im, v_s5_c_im),
        "s5_d": (s5_d, m_s5_d, v_s5_d), "s5_w_glu": (s5_w_glu, m_s5_w_glu, v_s5_w_glu),
        "norm_kv": (norm_kv, m_norm_kv, v_norm_kv), "w_kv": (w_kv, m_w_kv, v_w_kv), "w_q": (w_q, m_w_q, v_w_q),
        "w_o": (w_o, m_w_o, v_w_o), "w_ffn_in": (w_ffn_in, m_w_ffn_in, v_w_ffn_in),
        "w_ffn_out": (w_ffn_out, m_w_ffn_out, v_w_ffn_out), "norm_final": (norm_final, m_norm_final, v_norm_final),
    }
    names = list(weights)
    deltas, new_m, new_v = [], [], []
    for name in names:
        w, m, v = weights[name]
        d, nm, nv = _adamw(f"adamw_{name}", w, grads[name], m, v)
        deltas.append(d)
        new_m.append(nm)
        new_v.append(nv)
    return (loss, grad_x, *[grads[n] for n in names], *deltas, *new_m, *new_v)
```

```python
import functools
import math

import jax
import jax.numpy as jnp
from jax import lax
from jax.experimental import pallas as pl
from jax.experimental.pallas import tpu as pltpu

F32 = jnp.float32
BF16 = jnp.bfloat16

N_META = 16
X_START = 128
META_START = X_START - N_META
S5_GROUP = 16
S5_STATE = 64
HEAD_DIM = 64
KEY_BLOCK = 128
STATE_TILE = 512
CH_TILE = 128
RMS_EPS = 1e-6
ADAM_LR, ADAM_B1, ADAM_B2, ADAM_EPS, ADAM_WD, ADAM_STEP = 0.001, 0.9, 0.999, 1e-08, 0.01, 10
VMEM_LIMIT_BYTES = 48 * 1024 * 1024
MESH = pl.DeviceIdType.MESH
NT_DIMS = (((1,), (1,)), ((), ()))
TN_DIMS = (((0,), (0,)), ((), ()))


def _cparams(*sem):
    return pltpu.CompilerParams(dimension_semantics=sem, vmem_limit_bytes=VMEM_LIMIT_BYTES)


def _row_tile(rows, cap):
    for unit in (128, 8):
        best = 0
        for t in range(unit, min(rows, cap) + 1, unit):
            if rows % t == 0:
                best = t
        if best:
            return best
    return rows


def _col_tile(cols, cap):
    best = 0
    for t in range(128, min(cols, cap) + 1, 128):
        if cols % t == 0:
            best = t
    return best if best else cols


def _gelu(x):
    k = math.sqrt(2.0 / math.pi)
    return 0.5 * x * (1.0 + jnp.tanh(k * (x + 0.044715 * x * x * x)))


def _gelu_grad(x):
    k = math.sqrt(2.0 / math.pi)
    t = jnp.tanh(k * (x + 0.044715 * x * x * x))
    return 0.5 * (1.0 + t) + 0.5 * x * (1.0 - t * t) * k * (1.0 + 3.0 * 0.044715 * x * x)


def _sigmoid(x):
    return 0.5 + 0.5 * jnp.tanh(0.5 * x)


def _rmsnorm_fwd(name, x, gains, out_dtypes):
    T, D = x.shape
    tm = _row_tile(T, 512)
    n = len(gains)

    def body(x_ref, *refs):
        xv = x_ref[...]
        xh = xv * lax.rsqrt(jnp.mean(xv * xv, axis=-1, keepdims=True) + RMS_EPS)
        for g_ref, o_ref in zip(refs[:n], refs[n:]):
            o_ref[...] = (xh * g_ref[...]).astype(o_ref.dtype)

    row = pl.BlockSpec((tm, D), lambda i: (i, 0))
    vec = pl.BlockSpec((1, D), lambda i: (0, 0))
    return pl.pallas_call(
        body, name=name, grid=(T // tm,),
        in_specs=[row] + [vec] * n, out_specs=[row] * n,
        out_shape=[jax.ShapeDtypeStruct((T, D), dt) for dt in out_dtypes],
        compiler_params=_cparams("parallel"),
    )(x, *gains)


def _rmsnorm_bwd(name, x, pairs, dres):
    T, D = x.shape
    tm = _row_tile(T, 512)
    n = len(pairs)

    def body(x_ref, dres_ref, *refs):
        g_refs, dy_refs = refs[:n], refs[n:2 * n]
        dx_ref, dg_refs = refs[2 * n], refs[2 * n + 1:]
        i = pl.program_id(0)
        xv = x_ref[...]
        r = lax.rsqrt(jnp.mean(xv * xv, axis=-1, keepdims=True) + RMS_EPS)
        xh = xv * r
        dxh = jnp.zeros_like(xv)
        for g_ref, dy_ref, dg_ref in zip(g_refs, dy_refs, dg_refs):
            dy = dy_ref[...].astype(F32)
            part = jnp.sum(dy * xh, axis=0, keepdims=True)

            @pl.when(i == 0)
            def _():
                dg_ref[...] = part

            @pl.when(i > 0)
            def _():
                dg_ref[...] += part

            dxh = dxh + dy * g_ref[...]
        dx = r * (dxh - xh * jnp.mean(dxh * xh, axis=-1, keepdims=True))
        dx_ref[...] = dres_ref[...] + dx

    row = pl.BlockSpec((tm, D), lambda i: (i, 0))
    vec = pl.BlockSpec((1, D), lambda i: (0, 0))
    outs = pl.pallas_call(
        body, name=name, grid=(T // tm,),
        in_specs=[row, row] + [vec] * n + [row] * n,
        out_specs=[row] + [vec] * n,
        out_shape=[jax.ShapeDtypeStruct((T, D), F32)] + [jax.ShapeDtypeStruct((1, D), F32)] * n,
        compiler_params=_cparams("arbitrary"),
    )(x, dres, *[g for g, _ in pairs], *[dy for _, dy in pairs])
    return outs[0], outs[1:]


def _mm_nn(name, a, w, k_blk=0, res=None, out_dtype=F32):
    M, K = a.shape
    S, _, Ns = w.shape
    tm = _row_tile(M, 512)
    tn = _col_tile(Ns, 1408)
    nt = Ns // tn

    def body(a_ref, w_ref, *refs):
        o_ref = refs[-1]
        acc = jnp.dot(a_ref[...].astype(BF16), w_ref[...], preferred_element_type=F32)
        if res is not None:
            acc = acc + refs[0][...]
        o_ref[...] = acc.astype(o_ref.dtype)

    in_specs = [pl.BlockSpec((tm, K), lambda j, i: (i, 0)),
                pl.BlockSpec((None, K, tn), lambda j, i: (j // nt, k_blk, j % nt))]
    args = [a, w]
    if res is not None:
        in_specs.append(pl.BlockSpec((tm, tn), lambda j, i: (i, j)))
        args.append(res)
    return pl.pallas_call(
        body, name=name, grid=(S * nt, M // tm),
        in_specs=in_specs, out_specs=pl.BlockSpec((tm, tn), lambda j, i: (i, j)),
        out_shape=jax.ShapeDtypeStruct((M, S * Ns), out_dtype),
        compiler_params=_cparams("parallel", "parallel"),
    )(*args)


def _mm_nt_k(name, dy, w, K, k_blk=0, out_dtype=F32):
    M = dy.shape[0]
    S, _, Ns = w.shape
    tm = _row_tile(M, 1408)
    tn = _col_tile(Ns, 1408)
    nt = Ns // tn
    steps = S * nt

    def body(dy_ref, w_ref, o_ref, acc_ref):
        j = pl.program_id(1)
        part = lax.dot_general(dy_ref[...].astype(BF16), w_ref[...], (((1,), (1,)), ((), ())),
                               preferred_element_type=F32)

        @pl.when(j == 0)
        def _():
            acc_ref[...] = part

        @pl.when(j > 0)
        def _():
            acc_ref[...] += part

        @pl.when(j == steps - 1)
        def _():
            o_ref[...] = acc_ref[...].astype(o_ref.dtype)

    return pl.pallas_call(
        body, name=name, grid=(M // tm, steps),
        in_specs=[pl.BlockSpec((tm, tn), lambda i, j: (i, j)),
                  pl.BlockSpec((None, K, tn), lambda i, j: (j // nt, k_blk, j % nt))],
        out_specs=pl.BlockSpec((tm, K), lambda i, j: (i, 0)),
        out_shape=jax.ShapeDtypeStruct((M, K), out_dtype),
        scratch_shapes=[pltpu.VMEM((tm, K), F32)],
        compiler_params=_cparams("parallel", "arbitrary"),
    )(dy, w)


def _mm_tn(name, a, dy, S, out_dtype=BF16):
    T, K = a.shape
    Ns = dy.shape[1] // S
    tn = _col_tile(Ns, max(128, (6 * 1024 * 1024) // (4 * K) // 128 * 128))
    nt = Ns // tn
    tt = _row_tile(T, 1408)
    steps = T // tt

    def body(a_ref, dy_ref, o_ref, acc_ref):
        t = pl.program_id(1)
        part = lax.dot_general(a_ref[...].astype(BF16), dy_ref[...].astype(BF16), (((0,), (0,)), ((), ())),
                               preferred_element_type=F32)

        @pl.when(t == 0)
        def _():
            acc_ref[...] = part

        @pl.when(t > 0)
        def _():
            acc_ref[...] += part

        @pl.when(t == steps - 1)
        def _():
            o_ref[...] = acc_ref[...].astype(o_ref.dtype)

    return pl.pallas_call(
        body, name=name, grid=(S * nt, steps),
        in_specs=[pl.BlockSpec((tt, K), lambda j, t: (t, 0)),
                  pl.BlockSpec((tt, tn), lambda j, t: (t, j))],
        out_specs=pl.BlockSpec((None, K, tn), lambda j, t: (j // nt, 0, j % nt)),
        out_shape=jax.ShapeDtypeStruct((S, K, Ns), out_dtype),
        scratch_shapes=[pltpu.VMEM((K, tn), F32)],
        compiler_params=_cparams("parallel", "arbitrary"),
    )(a, dy)


def _gated_tile(T, width):
    return _row_tile(T, max(8, (2 * 1024 * 1024) // (4 * width) // 8 * 8))


def _mm_glu(name, a, w, res):
    M, K = a.shape
    Ns = w.shape[2]
    tm = _row_tile(M, 512)

    def body(a_ref, wv_ref, wg_ref, r_ref, v_ref, g_ref, o_ref):
        ab = a_ref[...].astype(BF16)
        val = jnp.dot(ab, wv_ref[...], preferred_element_type=F32)
        gate = jnp.dot(ab, wg_ref[...], preferred_element_type=F32)
        v_ref[...] = val
        g_ref[...] = gate
        o_ref[...] = r_ref[...] + val * _sigmoid(gate)

    out = pl.BlockSpec((tm, Ns), lambda j, i: (i, j))
    return pl.pallas_call(
        body, name=name, grid=(2, M // tm),
        in_specs=[pl.BlockSpec((tm, K), lambda j, i: (i, 0)),
                  pl.BlockSpec((None, K, Ns), lambda j, i: (j, 0, 0)),
                  pl.BlockSpec((None, K, Ns), lambda j, i: (j + 2, 0, 0)), out],
        out_specs=[out, out, out],
        out_shape=[jax.ShapeDtypeStruct((M, 2 * Ns), F32)] * 3,
        compiler_params=_cparams("parallel", "parallel"),
    )(a, w, w, res)


def _glu_bwd(name, val, gate, dout):
    T, D = dout.shape
    tm = _gated_tile(T, 2 * D)

    def body(v_ref, g_ref, d_ref, o_ref):
        s = _sigmoid(g_ref[...])
        d = d_ref[...]
        o_ref[:, :D] = (d * s).astype(o_ref.dtype)
        o_ref[:, D:] = (d * v_ref[...] * s * (1.0 - s)).astype(o_ref.dtype)

    blk = pl.BlockSpec((tm, D), lambda i: (i, 0))
    return pl.pallas_call(
        body, name=name, grid=(T // tm,),
        in_specs=[blk, blk, blk],
        out_specs=pl.BlockSpec((tm, 2 * D), lambda i: (i, 0)),
        out_shape=jax.ShapeDtypeStruct((T, 2 * D), BF16),
        compiler_params=_cparams("parallel"),
    )(val, gate, dout)


def _mm_swiglu(name, a, w, k_blk):
    M, K = a.shape
    Ns = w.shape[2]
    tm = _row_tile(M, 512)

    def body(a_ref, wg_ref, wu_ref, g_ref, u_ref, mid_ref):
        ab = a_ref[...].astype(BF16)
        g = jnp.dot(ab, wg_ref[...], preferred_element_type=F32)
        u = jnp.dot(ab, wu_ref[...], preferred_element_type=F32)
        g_ref[...] = g.astype(g_ref.dtype)
        u_ref[...] = u.astype(u_ref.dtype)
        mid_ref[...] = (g * _sigmoid(g) * u).astype(mid_ref.dtype)

    out = pl.BlockSpec((tm, Ns), lambda j, i: (i, j))
    return pl.pallas_call(
        body, name=name, grid=(2, M // tm),
        in_specs=[pl.BlockSpec((tm, K), lambda j, i: (i, 0)),
                  pl.BlockSpec((None, K, Ns), lambda j, i: (j, k_blk, 0)),
                  pl.BlockSpec((None, K, Ns), lambda j, i: (j + 2, k_blk, 0))],
        out_specs=[out, out, out],
        out_shape=[jax.ShapeDtypeStruct((M, 2 * Ns), BF16)] * 3,
        compiler_params=_cparams("parallel", "parallel"),
    )(a, w, w)


def _mm_nt_swiglu_bwd(name, dh, w, g, u):
    M, D = dh.shape
    F = w.shape[1]
    tm = _row_tile(M, 512)

    def body(dh_ref, w_ref, g_ref, u_ref, o_ref):
        d = lax.dot_general(dh_ref[...].astype(BF16), w_ref[...], NT_DIMS, preferred_element_type=F32)
        gv = g_ref[...].astype(F32)
        s = _sigmoid(gv)
        o_ref[:, :F] = (d * u_ref[...].astype(F32) * s * (1.0 + gv * (1.0 - s))).astype(o_ref.dtype)
        o_ref[:, F:] = (d * gv * s).astype(o_ref.dtype)

    half = pl.BlockSpec((tm, F), lambda i: (i, 0))
    return pl.pallas_call(
        body, name=name, grid=(M // tm,),
        in_specs=[pl.BlockSpec((tm, D), lambda i: (i, 0)), pl.BlockSpec((None, F, D), lambda i: (0, 0, 0)), half, half],
        out_specs=pl.BlockSpec((tm, 2 * F), lambda i: (i, 0)),
        out_shape=jax.ShapeDtypeStruct((M, 2 * F), BF16),
        compiler_params=_cparams("parallel"),
    )(dh, w, g, u)


def _final_loss(name, h, gain, target):
    T, D = h.shape
    tm = X_START
    lead = X_START // tm

    def body(h_ref, g_ref, t_ref, loss_ref, dh_ref, dg_ref):
        i = pl.program_id(0)

        @pl.when(i == 0)
        def _():
            loss_ref[...] = jnp.zeros_like(loss_ref)
            dg_ref[...] = jnp.zeros_like(dg_ref)
            dh_ref[...] = jnp.zeros_like(dh_ref)

        @pl.when(i >= lead)
        def _():
            xv = h_ref[...]
            r = lax.rsqrt(jnp.mean(xv * xv, axis=-1, keepdims=True) + RMS_EPS)
            xh = xv * r
            g = g_ref[...]
            diff = xh * g - t_ref[...]
            loss_ref[...] += 0.5 * jnp.sum(jnp.mean(diff * diff, axis=-1, keepdims=True), axis=0, keepdims=True)
            dout = diff * (1.0 / D)
            dg_ref[...] += jnp.sum(dout * xh, axis=0, keepdims=True)
            dxh = dout * g
            dh_ref[...] = r * (dxh - xh * jnp.mean(dxh * xh, axis=-1, keepdims=True))

    return pl.pallas_call(
        body, name=name, grid=(T // tm,),
        in_specs=[pl.BlockSpec((tm, D), lambda i: (i, 0)), pl.BlockSpec((1, D), lambda i: (0, 0)),
                  pl.BlockSpec((tm, D), lambda i: (jnp.maximum(i - lead, 0), 0))],
        out_specs=[pl.BlockSpec((1, 128), lambda i: (0, 0)), pl.BlockSpec((tm, D), lambda i: (i, 0)),
                   pl.BlockSpec((1, D), lambda i: (0, 0))],
        out_shape=[jax.ShapeDtypeStruct((1, 128), F32), jax.ShapeDtypeStruct((T, D), F32),
                   jax.ShapeDtypeStruct((1, D), F32)],
        compiler_params=_cparams("arbitrary"),
    )(h, gain, target)


def _grid_call(name, body, grid, in_specs, out_specs, out_shape, scratch_shapes, args, comm=None, comm_args=()):
    params = _cparams(*(("arbitrary",) * len(grid)))
    if comm is None:
        return pl.pallas_call(body, name=name, grid=grid, in_specs=in_specs, out_specs=out_specs,
                              out_shape=out_shape, scratch_shapes=scratch_shapes, compiler_params=params)(*args), []
    outs = pl.pallas_call(
        _embed_comm(comm, body, grid, len(in_specs), len(out_specs)), name=name, grid=grid,
        in_specs=list(in_specs) + _any_specs(comm.n), out_specs=list(out_specs) + _any_specs(comm.n),
        out_shape=list(out_shape) + comm.out_shape, scratch_shapes=list(scratch_shapes) + comm.scratch,
        compiler_params=params)(*args, *comm_args)
    return outs[:len(out_specs)], outs[len(out_specs):]


def _s5_discretise(a_re, a_im, log_dt, bt_re, bt_im):
    dt = jnp.exp(log_dt)
    mag = jnp.exp(dt * a_re)
    ang = dt * a_im
    abar_re = mag * jnp.cos(ang)
    abar_im = mag * jnp.sin(ang)
    den = a_re * a_re + a_im * a_im
    coef_re = ((abar_re - 1.0) * a_re + abar_im * a_im) / den
    coef_im = (abar_im * a_re - (abar_re - 1.0) * a_im) / den
    bbar_re = coef_re * bt_re - coef_im * bt_im
    bbar_im = coef_re * bt_im + coef_im * bt_re
    return abar_re, abar_im, bbar_re, bbar_im


def _s5_prep_fwd(name, a_re, a_im, log_dt, bt_re, bt_im):
    G, _, P = a_re.shape
    C = bt_re.shape[1]

    def body(ar, ai, ld, br, bi, o_ar, o_ai, o_br, o_bi):
        outs = _s5_discretise(ar[...], ai[...], ld[...], br[...], bi[...])
        for o, v in zip((o_ar, o_ai, o_br, o_bi), outs):
            o[...] = v

    return pl.pallas_call(
        body, name=name,
        out_shape=[jax.ShapeDtypeStruct((G, 1, P), F32)] * 2 + [jax.ShapeDtypeStruct((G, C, P), F32)] * 2,
    )(a_re, a_im, log_dt, bt_re, bt_im)


def _s5_prep_bwd(name, a_re, a_im, log_dt, bt_re, bt_im, d_ar, d_ai, d_br, d_bi):
    G, _, P = a_re.shape
    C = bt_re.shape[1]

    def body(ar, ai, ld, br, bi, gar, gai, gbr, gbi, o_ar, o_ai, o_ld, o_br, o_bi):
        _, vjp = jax.vjp(_s5_discretise, ar[...], ai[...], ld[...], br[...], bi[...])
        grads = vjp((gar[...], gai[...], gbr[...], gbi[...]))
        for o, v in zip((o_ar, o_ai, o_ld, o_br, o_bi), grads):
            o[...] = v

    return pl.pallas_call(
        body, name=name,
        out_shape=[jax.ShapeDtypeStruct((G, 1, P), F32)] * 2 + [jax.ShapeDtypeStruct((G, 1, 1), F32)]
        + [jax.ShapeDtypeStruct((G, C, P), F32)] * 2,
    )(a_re, a_im, log_dt, bt_re, bt_im, d_ar, d_ai, d_br, d_bi)


def _cmul(ar, ai, br, bi):
    return ar * br - ai * bi, ar * bi + ai * br


def _power_table(a_re, a_im):
    rows_re, rows_im = [a_re], [a_im]
    for _ in range(7):
        r, m = _cmul(rows_re[-1], rows_im[-1], a_re, a_im)
        rows_re.append(r)
        rows_im.append(m)
    row = lax.broadcasted_iota(jnp.int32, (8, a_re.shape[1]), 0)
    t_re = jnp.zeros((8, a_re.shape[1]), F32)
    t_im = jnp.zeros((8, a_re.shape[1]), F32)
    for k in range(8):
        t_re = jnp.where(row == k, rows_re[k], t_re)
        t_im = jnp.where(row == k, rows_im[k], t_im)
    return t_re, t_im, rows_re, rows_im


def _s5_fwd(name, u, b_re, b_im, ct_re, ct_im, abar_re, abar_im, d_skip, comm=None, comm_args=()):
    T, D = u.shape
    n_st = D // CH_TILE
    W = STATE_TILE
    tc = _row_tile(T, 512)
    n_tiles = tc // 8

    def body(u_ref, bre_ref, bim_ref, cre_ref, cim_ref, ar_ref, ai_ref, d_ref,
             y_ref, z_ref, xr_ref, xi_ref, carry_re, carry_im, pw_re, pw_im, sh_re, sh_im):
        c = pl.program_id(1)

        @pl.when(c == 0)
        def _():
            t_re, t_im, rows_re, rows_im = _power_table(ar_ref[...], ai_ref[...])
            pw_re[...] = t_re
            pw_im[...] = t_im
            first_rows = lax.broadcasted_iota(jnp.int32, (8, W), 0)
            for n, d in enumerate((1, 2, 4)):
                sh_re[n] = jnp.where(first_rows >= d, jnp.broadcast_to(rows_re[d - 1], (8, W)), 0.0)
                sh_im[n] = jnp.where(first_rows >= d, jnp.broadcast_to(rows_im[d - 1], (8, W)), 0.0)
            carry_re[...] = jnp.zeros_like(carry_re)
            carry_im[...] = jnp.zeros_like(carry_im)

        ub = u_ref[...].astype(BF16)
        xr_ref[...] = jnp.dot(ub, bre_ref[...], preferred_element_type=F32)
        xi_ref[...] = jnp.dot(ub, bim_ref[...], preferred_element_type=F32)
        row = lax.broadcasted_iota(jnp.int32, (8, W), 0)

        def tile(i, carry):
            c_re, c_im = carry
            rows = pl.ds(pl.multiple_of(i * 8, 8), 8)
            r = xr_ref[rows, :]
            m = xi_ref[rows, :]
            for n, d in enumerate((1, 2, 4)):
                pr, pm = _cmul(sh_re[n], sh_im[n], pltpu.roll(r, d, 0), pltpu.roll(m, d, 0))
                r = r + pr
                m = m + pm
            pr, pm = _cmul(pw_re[...], pw_im[...], c_re, c_im)
            r = r + pr
            m = m + pm
            xr_ref[rows, :] = r
            xi_ref[rows, :] = m
            return jnp.broadcast_to(r[7:8, :], (8, W)), jnp.broadcast_to(m[7:8, :], (8, W))

        c_re, c_im = lax.fori_loop(0, n_tiles, tile, (carry_re[...], carry_im[...]))
        carry_re[...] = c_re
        carry_im[...] = c_im
        y = (jnp.dot(xr_ref[...].astype(BF16), cre_ref[...], preferred_element_type=F32)
             - jnp.dot(xi_ref[...].astype(BF16), cim_ref[...], preferred_element_type=F32)
             + d_ref[...] * u_ref[...])
        y_ref[...] = y
        z_ref[...] = _gelu(y).astype(z_ref.dtype)

    ch = pl.BlockSpec((tc, CH_TILE), lambda s, c: (c, s))
    st = pl.BlockSpec((tc, W), lambda s, c: (c, s))
    return _grid_call(
        name, body, (n_st, T // tc),
        in_specs=[ch,
                  pl.BlockSpec((None, CH_TILE, W), lambda s, c: (s, 0, 0)),
                  pl.BlockSpec((None, CH_TILE, W), lambda s, c: (s, 0, 0)),
                  pl.BlockSpec((None, W, CH_TILE), lambda s, c: (s, 0, 0)),
                  pl.BlockSpec((None, W, CH_TILE), lambda s, c: (s, 0, 0)),
                  pl.BlockSpec((1, W), lambda s, c: (0, s)),
                  pl.BlockSpec((1, W), lambda s, c: (0, s)),
                  pl.BlockSpec((1, CH_TILE), lambda s, c: (0, s))],
        out_specs=[ch, ch, st, st],
        out_shape=[jax.ShapeDtypeStruct((T, D), F32), jax.ShapeDtypeStruct((T, D), BF16),
                   jax.ShapeDtypeStruct((T, 4 * D), F32), jax.ShapeDtypeStruct((T, 4 * D), F32)],
        scratch_shapes=[pltpu.VMEM((8, W), F32), pltpu.VMEM((8, W), F32),
                        pltpu.VMEM((8, W), F32), pltpu.VMEM((8, W), F32),
                        pltpu.VMEM((3, 8, W), F32), pltpu.VMEM((3, 8, W), F32)],
        args=(u, b_re, b_im, ct_re, ct_im, abar_re, abar_im, d_skip), comm=comm, comm_args=comm_args)


def _s5_bwd(name, dz, y, u, x_re, x_im, c_re, c_im, bt_re, bt_im, abar_re, abar_im, d_skip, comm=None, comm_args=()):
    T, D = u.shape
    n_st = D // CH_TILE
    W = STATE_TILE
    tc = _row_tile(T, 512)
    n_chunks = T // tc
    n_tiles = tc // 8
    tiles_per_chunk = tc // 8

    def body(dz_ref, y_ref, u_ref, xr_ref, xi_ref, xpr_ref, xpi_ref, cre_ref, cim_ref, btr_ref, bti_ref,
             ar_ref, ai_ref, d_ref,
             du_ref, dd_ref, dbr_ref, dbi_ref, dcr_ref, dci_ref, dar_ref, dai_ref,
             lam_re, lam_im, xe_re, xe_im, carry_re, carry_im, pw_re, pw_im, sh_re, sh_im, acc_ar, acc_ai):
        k = pl.program_id(1)
        first_chunk = k == n_chunks - 1

        @pl.when(k == 0)
        def _():
            t_re, t_im, rows_re, rows_im = _power_table(ar_ref[...], -ai_ref[...])
            row = lax.broadcasted_iota(jnp.int32, (8, W), 0)
            r_re = jnp.zeros((8, W), F32)
            r_im = jnp.zeros((8, W), F32)
            for j in range(8):
                r_re = jnp.where(row == j, rows_re[7 - j], r_re)
                r_im = jnp.where(row == j, rows_im[7 - j], r_im)
            pw_re[...] = r_re
            pw_im[...] = r_im
            for n, d in enumerate((1, 2, 4)):
                sh_re[n] = jnp.where(row < 8 - d, jnp.broadcast_to(rows_re[d - 1], (8, W)), 0.0)
                sh_im[n] = jnp.where(row < 8 - d, jnp.broadcast_to(rows_im[d - 1], (8, W)), 0.0)
            carry_re[...] = jnp.zeros_like(carry_re)
            carry_im[...] = jnp.zeros_like(carry_im)
            acc_ar[...] = jnp.zeros_like(acc_ar)
            acc_ai[...] = jnp.zeros_like(acc_ai)
            dd_ref[...] = jnp.zeros_like(dd_ref)
            dbr_ref[...] = jnp.zeros_like(dbr_ref)
            dbi_ref[...] = jnp.zeros_like(dbi_ref)
            dcr_ref[...] = jnp.zeros_like(dcr_ref)
            dci_ref[...] = jnp.zeros_like(dci_ref)

        uv = u_ref[...]
        dy = dz_ref[...] * _gelu_grad(y_ref[...])
        dyb = dy.astype(BF16)
        lam_re[...] = jnp.dot(dyb, cre_ref[...], preferred_element_type=F32)
        lam_im[...] = -jnp.dot(dyb, cim_ref[...], preferred_element_type=F32)
        keep = jnp.where(first_chunk, 0.0, 1.0)
        xe_re[pl.ds(0, 8), :] = xpr_ref[...] * keep
        xe_im[pl.ds(0, 8), :] = xpi_ref[...] * keep
        xe_re[pl.ds(8, tc), :] = xr_ref[...]
        xe_im[pl.ds(8, tc), :] = xi_ref[...]
        row = lax.broadcasted_iota(jnp.int32, (8, W), 0)

        def tile(n, carry):
            c_re, c_im, s_ar, s_ai = carry
            i = n_tiles - 1 - n
            rows = pl.ds(pl.multiple_of(i * 8, 8), 8)
            r = lam_re[rows, :]
            m = lam_im[rows, :]
            for q, d in enumerate((1, 2, 4)):
                pr, pm = _cmul(sh_re[q], sh_im[q], pltpu.roll(r, 8 - d, 0), pltpu.roll(m, 8 - d, 0))
                r = r + pr
                m = m + pm
            pr, pm = _cmul(pw_re[...], pw_im[...], c_re, c_im)
            r = r + pr
            m = m + pm
            lam_re[rows, :] = r
            lam_im[rows, :] = m
            cur_re = xe_re[pl.ds(pl.multiple_of(i * 8 + 8, 8), 8), :]
            cur_im = xe_im[pl.ds(pl.multiple_of(i * 8 + 8, 8), 8), :]
            bef_re = xe_re[rows, :]
            bef_im = xe_im[rows, :]
            xp_re = jnp.where(row == 0, jnp.broadcast_to(bef_re[7:8, :], (8, W)), pltpu.roll(cur_re, 1, 0))
            xp_im = jnp.where(row == 0, jnp.broadcast_to(bef_im[7:8, :], (8, W)), pltpu.roll(cur_im, 1, 0))
            s_ar = s_ar + r * xp_re + m * xp_im
            s_ai = s_ai + m * xp_re - r * xp_im
            return jnp.broadcast_to(r[0:1, :], (8, W)), jnp.broadcast_to(m[0:1, :], (8, W)), s_ar, s_ai

        c_re, c_im, s_ar, s_ai = lax.fori_loop(
            0, n_tiles, tile, (carry_re[...], carry_im[...], acc_ar[...], acc_ai[...]))
        carry_re[...] = c_re
        carry_im[...] = c_im
        acc_ar[...] = s_ar
        acc_ai[...] = s_ai
        lr = lam_re[...].astype(BF16)
        li = lam_im[...].astype(BF16)
        du_ref[...] = (dy * d_ref[...] + jnp.dot(lr, btr_ref[...], preferred_element_type=F32)
                       + jnp.dot(li, bti_ref[...], preferred_element_type=F32))
        dd_ref[...] += jnp.sum(dy * uv, axis=0, keepdims=True)
        tn_dims = (((0,), (0,)), ((), ()))
        ub = uv.astype(BF16)
        dbr_ref[...] += lax.dot_general(ub, lr, tn_dims, preferred_element_type=F32)
        dbi_ref[...] += lax.dot_general(ub, li, tn_dims, preferred_element_type=F32)
        dcr_ref[...] += lax.dot_general(dyb, xr_ref[...].astype(BF16), tn_dims, preferred_element_type=F32)
        dci_ref[...] -= lax.dot_general(dyb, xi_ref[...].astype(BF16), tn_dims, preferred_element_type=F32)

        @pl.when(first_chunk)
        def _():
            dar_ref[...] = jnp.sum(acc_ar[...], axis=0, keepdims=True)
            dai_ref[...] = jnp.sum(acc_ai[...], axis=0, keepdims=True)

    rev = lambda k: n_chunks - 1 - k
    ch = pl.BlockSpec((tc, CH_TILE), lambda s, k: (rev(k), s))
    st = pl.BlockSpec((tc, W), lambda s, k: (rev(k), s))
    prev = pl.BlockSpec((8, W), lambda s, k: (jnp.maximum(rev(k) * tiles_per_chunk - 1, 0), s))
    mat_cw = pl.BlockSpec((None, CH_TILE, W), lambda s, k: (s, 0, 0))
    mat_wc = pl.BlockSpec((None, W, CH_TILE), lambda s, k: (s, 0, 0))
    vec_w = pl.BlockSpec((1, W), lambda s, k: (0, s))
    vec_c = pl.BlockSpec((1, CH_TILE), lambda s, k: (0, s))
    dense = jax.ShapeDtypeStruct((n_st, CH_TILE, W), F32)
    return _grid_call(
        name, body, (n_st, n_chunks),
        in_specs=[ch, ch, ch, st, st, prev, prev, mat_cw, mat_cw, mat_wc, mat_wc, vec_w, vec_w, vec_c],
        out_specs=[ch, vec_c, mat_cw, mat_cw, mat_cw, mat_cw, vec_w, vec_w],
        out_shape=[jax.ShapeDtypeStruct((T, D), F32), jax.ShapeDtypeStruct((1, D), F32), dense, dense, dense, dense,
                   jax.ShapeDtypeStruct((1, 4 * D), F32), jax.ShapeDtypeStruct((1, 4 * D), F32)],
        scratch_shapes=[pltpu.VMEM((tc, W), F32), pltpu.VMEM((tc, W), F32),
                        pltpu.VMEM((tc + 8, W), F32), pltpu.VMEM((tc + 8, W), F32),
                        pltpu.VMEM((8, W), F32), pltpu.VMEM((8, W), F32),
                        pltpu.VMEM((8, W), F32), pltpu.VMEM((8, W), F32),
                        pltpu.VMEM((3, 8, W), F32), pltpu.VMEM((3, 8, W), F32),
                        pltpu.VMEM((8, W), F32), pltpu.VMEM((8, W), F32)],
        args=(dz, y, u, x_re, x_im, x_re, x_im, c_re, c_im, bt_re, bt_im, abar_re, abar_im, d_skip),
        comm=comm, comm_args=comm_args)


def _sums_matrix(strictly_later, copies):
    jj = lax.broadcasted_iota(jnp.int32, (copies * KEY_BLOCK, 2 * KEY_BLOCK), 0) & (KEY_BLOCK - 1)
    ss = lax.broadcasted_iota(jnp.int32, (copies * KEY_BLOCK, 2 * KEY_BLOCK), 1)
    tri = (jj > ss) if strictly_later else (jj < ss)
    return (tri | (ss >= KEY_BLOCK)).astype(BF16)


def _split_heads(blk):
    first = lax.broadcasted_iota(jnp.int32, blk.shape, 1) < HEAD_DIM
    zero = jnp.zeros_like(blk)
    return jnp.concatenate([jnp.where(first, blk, zero), jnp.where(first, zero, blk)], axis=0)


LOG2_E = 1.4426950408889634


def _sb_scores(z, mask, later):
    z2 = z * LOG2_E
    minus_abs = lax.bitcast_convert_type(lax.bitcast_convert_type(z2, jnp.uint32) | jnp.uint32(0x80000000), F32)
    lb = jnp.minimum(z2, 0.0) - jnp.log2(1.0 + jnp.exp2(minus_abs))
    lm = lb - z2
    if mask is not None:
        lm = jnp.where(mask, lm, 0.0)
    hi = lm.astype(BF16)
    lo = (lm - hi.astype(F32)).astype(BF16)
    return lb, jnp.dot(jnp.concatenate([hi, lo], axis=1), later, preferred_element_type=F32)


def _key_rows(kb):
    return pl.ds(pl.multiple_of(kb * KEY_BLOCK, KEY_BLOCK), KEY_BLOCK)


def _sb_mask(q_row0, k_row0, tq):
    tpos = q_row0 + lax.broadcasted_iota(jnp.int32, (tq, KEY_BLOCK), 0)
    spos = k_row0 + lax.broadcasted_iota(jnp.int32, (tq, KEY_BLOCK), 1)
    return (spos < tpos) & (spos >= META_START)


def _key_block_phases(iq, per_q, block, ascending, per_step):
    first_diag = iq * per_q

    def nth(lo, n, i):
        return lo + i if ascending else lo + n - 1 - i

    def run(lo, n, masked):
        if isinstance(n, int):
            for i in range(n):
                j = i if ascending else n - 1 - i
                block(lo + j, masked, j * KEY_BLOCK)
            return

        def group(i, carry):
            for j in range(per_step):
                block(nth(lo, n, per_step * i + j), masked, 0)
            return carry

        def single(i, carry):
            block(nth(lo, n, n - rest + i), masked, 0)
            return carry

        rest = jnp.bitwise_and(n, per_step - 1)
        lax.fori_loop(0, jnp.right_shift(n, per_step.bit_length() - 1), group, 0)
        lax.fori_loop(0, rest, single, 0)

    phases = [(0, jnp.minimum(iq, 1), True), (1, jnp.maximum(first_diag - 1, 0), False), (first_diag, per_q, True)]
    for lo, n, masked in (phases if ascending else phases[::-1]):
        run(lo, n, masked)


def _attn_fwd(name, q, kv):
    T, D = q.shape
    n_hp = D // 128
    tq = _row_tile(T, 512)
    per_q = tq // KEY_BLOCK
    scale = 1.0 / math.sqrt(HEAD_DIM)

    chains = [(h, r) for h in range(2) for r in range(2)]
    cols = lambda h: slice(h * KEY_BLOCK, (h + 1) * KEY_BLOCK)

    def body(q_ref, k_ref, v_ref, o_ref, l_ref, z_buf, w_buf, acc_ref, run_ref):
        iq = pl.program_id(1)
        n_kb = (iq + 1) * per_q
        qs = q_ref[...] * jnp.asarray(scale, BF16)
        later = _sums_matrix(True, 2)

        def scores(kb):
            return lax.dot_general(qs, _split_heads(k_ref[_key_rows(kb), :]), NT_DIMS, preferred_element_type=F32)

        def weighted_values(kb):
            return jnp.dot(w_buf[...], _split_heads(v_ref[_key_rows(kb), :]), preferred_element_type=F32)

        acc_ref[...] = jnp.zeros_like(acc_ref)
        run_ref[...] = jnp.zeros_like(run_ref)
        w_buf[...] = jnp.zeros_like(w_buf)
        z_buf[...] = scores(n_kb - 1)

        def block(kb, masked, row0):
            acc_ref[...] += weighted_values(jnp.minimum(kb + 1, n_kb - 1))
            z_next = scores(jnp.maximum(kb - 1, 0))
            half = (tq - row0) // 2
            rows = lambda r: slice(row0 + r * half, row0 + (r + 1) * half)
            if row0:
                w_buf[:row0, :] = jnp.zeros((row0, 2 * KEY_BLOCK), BF16)
            masks = [_sb_mask(iq * tq + row0 + r * half, kb * KEY_BLOCK, half) if masked else None for r in range(2)]
            first = [_sb_scores(z_buf[rows(r), cols(h)], masks[r], later) for h, r in chains]
            for (h, r), (lb, sums) in zip(chains, first):
                run = run_ref[h, rows(r), :]
                w = jnp.exp2(lb + sums[:, :KEY_BLOCK] + run)
                if masked:
                    w = jnp.where(masks[r], w, 0.0)
                w_buf[rows(r), cols(h)] = w.astype(BF16)
                run_ref[h, rows(r), :] = run + sums[:, KEY_BLOCK:]
            z_buf[...] = z_next

        _key_block_phases(iq, per_q, block, ascending=False, per_step=4)
        acc_ref[...] += weighted_values(0)
        lane = lax.broadcasted_iota(jnp.int32, (tq, 128), 1)
        o_ref[...] = acc_ref[...].astype(o_ref.dtype)
        l_ref[...] = jnp.where(lane < HEAD_DIM, run_ref[0], run_ref[1])

    blk = pl.BlockSpec((tq, 128), lambda h, i: (i, h))
    return pl.pallas_call(
        body, name=name, grid=(n_hp, T // tq),
        in_specs=[blk, pl.BlockSpec((T, 128), lambda h, i: (0, h)), pl.BlockSpec((T, 128), lambda h, i: (0, n_hp + h))],
        out_specs=[blk, blk],
        out_shape=[jax.ShapeDtypeStruct((T, D), BF16), jax.ShapeDtypeStruct((T, D), F32)],
        scratch_shapes=[pltpu.VMEM((tq, 2 * KEY_BLOCK), F32), pltpu.VMEM((tq, 2 * KEY_BLOCK), BF16),
                        pltpu.VMEM((tq, 128), F32), pltpu.VMEM((2, tq, 128), F32)],
        compiler_params=_cparams("parallel", "arbitrary"),
    )(q, kv, kv)


def _attn_bwd(name, q, kv, do, ltot):
    T, D = q.shape
    n_hp = D // 128
    tq = _row_tile(T, 512)
    n_q = T // tq
    per_q = tq // KEY_BLOCK
    scale = 1.0 / math.sqrt(HEAD_DIM)

    chains = [(h, r) for h in range(2) for r in range(2)]
    head_cols = lambda h: slice(h * KEY_BLOCK, (h + 1) * KEY_BLOCK)

    def body(q_ref, k_ref, v_ref, do_ref, l_ref, dq_ref, dk_ref, dv_ref,
             dk_acc, dv_acc, dq_acc, lpre_ref, cpre_ref, z_buf, dw_buf, dz_buf, w_buf):
        iq = pl.program_id(1)
        n_kb = (iq + 1) * per_q

        @pl.when(iq == 0)
        def _():
            dk_acc[...] = jnp.zeros_like(dk_acc)
            dv_acc[...] = jnp.zeros_like(dv_acc)

        first = lax.broadcasted_iota(jnp.int32, (tq, 128), 1) < HEAD_DIM
        qs = q_ref[...] * jnp.asarray(scale, BF16)
        dov = do_ref[...]
        ltv = l_ref[...]
        swapped = pltpu.roll(ltv, HEAD_DIM, 1)
        ltot = [jnp.where(first, ltv, swapped), jnp.where(first, swapped, ltv)]
        zero = jnp.zeros_like(qs)
        q_stack = jnp.concatenate([jnp.where(first, qs, zero), jnp.where(first, zero, qs)], axis=0)
        do_stack = jnp.concatenate([jnp.where(first, dov, zero), jnp.where(first, zero, dov)], axis=0)
        later = _sums_matrix(True, 2)
        earlier = _sums_matrix(False, 1)

        def scores(kb):
            rows = _key_rows(kb)
            return (lax.dot_general(qs, _split_heads(k_ref[rows, :]), NT_DIMS, preferred_element_type=F32),
                    lax.dot_general(dov, _split_heads(v_ref[rows, :]), NT_DIMS, preferred_element_type=F32))

        def flush(kb):
            rows = _key_rows(kb)
            k_heads = _split_heads(k_ref[rows, :])
            dq_acc[...] += (jnp.dot(dz_buf[:tq, :], k_heads[:KEY_BLOCK, :], preferred_element_type=F32)
                            + jnp.dot(dz_buf[tq:, :], k_heads[KEY_BLOCK:, :], preferred_element_type=F32))
            dk_acc[rows, :] += lax.dot_general(dz_buf[...], q_stack, TN_DIMS, preferred_element_type=F32)
            dv_acc[rows, :] += lax.dot_general(w_buf[...], do_stack, TN_DIMS, preferred_element_type=F32)

        dq_acc[...] = jnp.zeros_like(dq_acc)
        lpre_ref[...] = jnp.zeros_like(lpre_ref)
        cpre_ref[...] = jnp.zeros_like(cpre_ref)
        dz_buf[...] = jnp.zeros_like(dz_buf)
        w_buf[...] = jnp.zeros_like(w_buf)
        z_buf[...], dw_buf[...] = scores(0)

        def block(kb, masked, row0):
            flush(jnp.maximum(kb - 1, 0))
            z_next, dw_next = scores(jnp.minimum(kb + 1, n_kb - 1))
            half = (tq - row0) // 2
            row_half = lambda r: slice(row0 + r * half, row0 + (r + 1) * half)
            stacked = lambda h, r: slice(h * tq + row0 + r * half, h * tq + row0 + (r + 1) * half)
            if row0:
                for h in range(2):
                    w_buf[h * tq:h * tq + row0, :] = jnp.zeros((row0, 128), BF16)
                    dz_buf[h * tq:h * tq + row0, :] = jnp.zeros((row0, 128), BF16)
            masks = [_sb_mask(iq * tq + row0 + r * half, kb * KEY_BLOCK, half) if masked else None for r in range(2)]
            first_stage = [_sb_scores(z_buf[row_half(r), head_cols(h)], masks[r], later) for h, r in chains]
            second_stage = []
            for (h, r), (lb, sums) in zip(chains, first_stage):
                after = ltot[h][row_half(r), :] - lpre_ref[h, row_half(r), :] - sums[:, KEY_BLOCK:]
                w = jnp.exp2(lb + sums[:, :KEY_BLOCK] + after)
                if masked:
                    w = jnp.where(masks[r], w, 0.0)
                da = w * dw_buf[row_half(r), head_cols(h)]
                w_buf[stacked(h, r), :] = w.astype(BF16)
                lpre_ref[h, row_half(r), :] += sums[:, KEY_BLOCK:]
                second_stage.append((da, jnp.dot(da.astype(BF16), earlier, preferred_element_type=F32)))
            for (h, r), (lb, _), (da, dsums) in zip(chains, first_stage, second_stage):
                sig = jnp.exp2(lb)
                through_later = sig * (dsums[:, :KEY_BLOCK] + cpre_ref[h, row_half(r), :])
                if masked:
                    through_later = jnp.where(masks[r], through_later, 0.0)
                dz_buf[stacked(h, r), :] = (da * (1.0 - sig) - through_later).astype(BF16)
                cpre_ref[h, row_half(r), :] += dsums[:, KEY_BLOCK:]
            z_buf[...] = z_next
            dw_buf[...] = dw_next

        _key_block_phases(iq, per_q, block, ascending=True, per_step=4)
        flush(n_kb - 1)
        dq_ref[...] = (dq_acc[...] * scale).astype(dq_ref.dtype)

        @pl.when(iq == n_q - 1)
        def _():
            dk_ref[...] = dk_acc[...].astype(dk_ref.dtype)
            dv_ref[...] = dv_acc[...].astype(dv_ref.dtype)

    blk = pl.BlockSpec((tq, 128), lambda h, i: (i, h))
    full = pl.BlockSpec((T, 128), lambda h, i: (0, h))
    return pl.pallas_call(
        body, name=name, grid=(n_hp, n_q),
        in_specs=[blk, full, pl.BlockSpec((T, 128), lambda h, i: (0, n_hp + h)), blk, blk],
        out_specs=[blk, full, full],
        out_shape=[jax.ShapeDtypeStruct((T, D), BF16)] * 3,
        scratch_shapes=[pltpu.VMEM((T, 128), F32), pltpu.VMEM((T, 128), F32), pltpu.VMEM((tq, 128), F32),
                        pltpu.VMEM((2, tq, 128), F32), pltpu.VMEM((2, tq, 128), F32),
                        pltpu.VMEM((tq, 2 * KEY_BLOCK), F32), pltpu.VMEM((tq, 2 * KEY_BLOCK), F32),
                        pltpu.VMEM((2 * tq, 128), BF16), pltpu.VMEM((2 * tq, 128), BF16)],
        compiler_params=_cparams("parallel", "arbitrary"),
    )(q, kv, kv, do, ltot)


def _adamw(name, w, g, m, v):
    shape = w.shape
    size = w.size
    if w.ndim >= 2 and shape[-1] % 128 == 0:
        cols = shape[-1]
    else:
        cols = 1024 if size % 1024 == 0 else shape[-1]
    rows = size // cols
    tm = _row_tile(rows, max(8, (1024 * 1024) // (4 * cols) // 8 * 8)) if rows % 8 == 0 else rows
    c1 = 1.0 / (1.0 - ADAM_B1 ** ADAM_STEP)
    c2 = 1.0 / (1.0 - ADAM_B2 ** ADAM_STEP)

    def body(w_ref, g_ref, m_ref, v_ref, d_ref, nm_ref, nv_ref):
        gv = g_ref[...]
        nm = ADAM_B1 * m_ref[...] + (1.0 - ADAM_B1) * gv
        nv = ADAM_B2 * v_ref[...] + (1.0 - ADAM_B2) * (gv * gv)
        d_ref[...] = -ADAM_LR * ((nm * c1) / (jnp.sqrt(nv * c2) + ADAM_EPS) + ADAM_WD * w_ref[...])
        nm_ref[...] = nm
        nv_ref[...] = nv

    blk = pl.BlockSpec((tm, cols), lambda i: (i, 0))
    outs = pl.pallas_call(
        body, name=name, grid=(rows // tm,),
        in_specs=[blk] * 4, out_specs=[blk] * 3,
        out_shape=[jax.ShapeDtypeStruct((rows, cols), F32)] * 3,
        compiler_params=_cparams("parallel"),
    )(*[t.reshape(rows, cols) for t in (w, g, m, v)])
    return tuple(o.reshape(shape) for o in outs)


def _any_specs(n):
    return [pl.BlockSpec(memory_space=pl.ANY)] * n


def _chip_index():
    return 2 * lax.axis_index("x") + lax.axis_index("y")


def _place():
    x, y, c = lax.axis_index("x"), lax.axis_index("y"), lax.axis_index("c")
    chips = [(1 - x, y), (x, 1 - y), (1 - x, 1 - y)]
    return x, y, c, chips


def _all_gather_chips(name, shards):
    n = len(shards)

    def body(*refs):
        x_refs, o_refs = refs[:n], refs[n:2 * n]
        send_sems, recv_sems = refs[2 * n:]
        x, y, c, chips = _place()
        me = 2 * x + y
        sibling = (x, y, 1 - c)

        def half(ref, i, which):
            h = shards[i].shape[0] // 2
            return ref.at[pl.ds(which * h, h)]

        def remote(k, i, src, dst, to):
            return pltpu.make_async_remote_copy(src_ref=src, dst_ref=dst, send_sem=send_sems.at[k, i],
                                                recv_sem=recv_sems.at[k, i], device_id=to, device_id_type=MESH)

        sent = []
        for j, chip in enumerate(chips):
            for i in range(n):
                cp = remote(j, i, half(x_refs[i], i, c), half(o_refs[i].at[me], i, c), (*chip, c))
                cp.start()
                sent.append(cp)
        for j, chip in enumerate(chips):
            pj = 2 * chip[0] + chip[1]
            for i in range(n):
                landed = half(o_refs[i].at[pj], i, c)
                remote(j, i, landed, landed, (*chip, c)).wait_recv()
                cp = remote(3 + j, i, landed, landed, sibling)
                cp.start()
                sent.append(cp)
        for j, chip in enumerate(chips):
            pj = 2 * chip[0] + chip[1]
            for i in range(n):
                got = half(o_refs[i].at[pj], i, 1 - c)
                remote(3 + j, i, got, got, sibling).wait_recv()
        for cp in sent:
            cp.wait_send()

    outs = pl.pallas_call(
        body, name=name,
        in_specs=_any_specs(n), out_specs=_any_specs(n),
        out_shape=[jax.ShapeDtypeStruct((4,) + s.shape, s.dtype) for s in shards],
        scratch_shapes=[pltpu.SemaphoreType.DMA((6, n)), pltpu.SemaphoreType.DMA((6, n))],
    )(*shards)
    return [lax.dynamic_update_slice(o, s[None], (_chip_index(), 0, 0)) for o, s in zip(outs, shards)]


def _pair_split(name, grads):
    n = len(grads)

    def body(*refs):
        g_refs, got_refs = refs[:n], refs[n:2 * n]
        send_sems, recv_sems = refs[2 * n:]
        x, y, c, _ = _place()
        sibling = (x, y, 1 - c)
        sent = []
        for i in range(n):
            h = grads[i].shape[1] // 2
            rc = pltpu.make_async_remote_copy(
                src_ref=g_refs[i].at[:, pl.ds((1 - c) * h, h)], dst_ref=got_refs[i],
                send_sem=send_sems.at[i], recv_sem=recv_sems.at[i], device_id=sibling, device_id_type=MESH)
            rc.start()
            sent.append(rc)
        for rc in sent:
            rc.wait()

    return pl.pallas_call(
        body, name=name,
        in_specs=_any_specs(n), out_specs=_any_specs(n),
        out_shape=[jax.ShapeDtypeStruct((4, g.shape[1] // 2, g.shape[2]), g.dtype) for g in grads],
        scratch_shapes=[pltpu.SemaphoreType.DMA((n,)), pltpu.SemaphoreType.DMA((n,))],
    )(*grads)


def _chip_exchange(name, sums):
    return _exchange_finish(sums, _run_comm(name, _exchange_comm(sums), sums))


class _Comm:
    def __init__(self, out_shape, copies):
        self.n = len(out_shape)
        self.out_shape = out_shape
        self.copies = copies
        self.scratch = [pltpu.SemaphoreType.DMA((3, self.n)), pltpu.SemaphoreType.DMA((3, self.n))]

    def start(self, *refs):
        for cp in self.copies(*refs, False):
            cp.start()

    def finish(self, *refs):
        for cp in self.copies(*refs, True):
            cp.wait_recv()
        for cp in self.copies(*refs, False):
            cp.wait_send()


def _exchange_comm(sums):
    n = len(sums)

    def copies(s_refs, o_refs, send_sems, recv_sems, mirrors):
        x, y, c, chips = _place()
        me = 2 * x + y
        out = []
        for j, chip in enumerate(chips):
            pj = 2 * chip[0] + chip[1]
            for i in range(n):
                out.append(pltpu.make_async_remote_copy(
                    src_ref=s_refs[i].at[pj], dst_ref=o_refs[i].at[pj if mirrors else me], send_sem=send_sems.at[j, i],
                    recv_sem=recv_sems.at[j, i], device_id=(*chip, c), device_id_type=MESH))
        return out

    return _Comm([jax.ShapeDtypeStruct(s.shape, s.dtype) for s in sums], copies)


def _exchange_finish(sums, outs):
    me = _chip_index()
    return [lax.dynamic_update_slice(o, lax.dynamic_index_in_dim(s, me, 0, keepdims=True), (me, 0, 0))
            for o, s in zip(outs, sums)]


def _halves_gather_comm(shards):
    n = len(shards)

    def copies(x_refs, o_refs, send_sems, recv_sems, mirrors):
        x, y, c, chips = _place()
        me = 2 * x + y
        out = []
        for j, chip in enumerate(chips):
            pj = 2 * chip[0] + chip[1]
            for i in range(n):
                h = shards[i].shape[0] // 2
                rows = pl.ds(c * h, h)
                out.append(pltpu.make_async_remote_copy(
                    src_ref=x_refs[i].at[rows], dst_ref=o_refs[i].at[pj if mirrors else me, rows],
                    send_sem=send_sems.at[j, i], recv_sem=recv_sems.at[j, i], device_id=(*chip, c), device_id_type=MESH))
        return out

    return _Comm([jax.ShapeDtypeStruct((4,) + s.shape, s.dtype) for s in shards], copies)


def _run_comm(name, comm, arrays):
    n = comm.n

    def body(*refs):
        comm.start(refs[:n], refs[n:2 * n], *refs[2 * n:])
        comm.finish(refs[:n], refs[n:2 * n], *refs[2 * n:])

    return pl.pallas_call(
        body, name=name, in_specs=_any_specs(n), out_specs=_any_specs(n),
        out_shape=comm.out_shape, scratch_shapes=comm.scratch,
    )(*arrays)


def _embed_comm(comm, body, grid, n_in, n_out):
    n = comm.n

    def wrapped(*refs):
        ins, c_in = refs[:n_in], refs[n_in:n_in + n]
        outs, c_out = refs[n_in + n:n_in + n + n_out], refs[n_in + n + n_out:n_in + 2 * n + n_out]
        scratch, sems = refs[n_in + 2 * n + n_out:-2], refs[-2:]
        ids = [pl.program_id(a) for a in range(len(grid))]
        first = functools.reduce(jnp.logical_and, [i == 0 for i in ids])
        last = functools.reduce(jnp.logical_and, [i == g - 1 for i, g in zip(ids, grid)])

        @pl.when(first)
        def _():
            comm.start(c_in, c_out, *sems)

        body(*ins, *outs, *scratch)

        @pl.when(last)
        def _():
            comm.finish(c_in, c_out, *sems)

    return wrapped


def _pair_forward(name, landed, shards):
    n = len(landed)

    def body(*refs):
        o_refs = refs[n:2 * n]
        send_sems, recv_sems = refs[2 * n:]
        x, y, c, chips = _place()
        sibling = (x, y, 1 - c)
        sent, arriving = [], []
        for j, chip in enumerate(chips):
            pj = 2 * chip[0] + chip[1]
            for i in range(n):
                h = shards[i].shape[0] // 2
                mine = o_refs[i].at[pj, pl.ds(c * h, h)]
                theirs = o_refs[i].at[pj, pl.ds((1 - c) * h, h)]
                for ref, group in ((mine, sent), (theirs, arriving)):
                    group.append(pltpu.make_async_remote_copy(
                        src_ref=ref, dst_ref=ref, send_sem=send_sems.at[j, i], recv_sem=recv_sems.at[j, i],
                        device_id=sibling, device_id_type=MESH))
        for cp in sent:
            cp.start()
        for cp in arriving:
            cp.wait_recv()
        for cp in sent:
            cp.wait_send()

    outs = pl.pallas_call(
        body, name=name, in_specs=_any_specs(n), out_specs=_any_specs(n),
        out_shape=[jax.ShapeDtypeStruct(a.shape, a.dtype) for a in landed],
        input_output_aliases={i: i for i in range(n)},
        scratch_shapes=[pltpu.SemaphoreType.DMA((3, n)), pltpu.SemaphoreType.DMA((3, n))],
    )(*landed)
    return [lax.dynamic_update_slice(o, s[None], (_chip_index(), 0, 0)) for o, s in zip(outs, shards)]


def _pair_join(name, halves):
    n = len(halves)

    def body(*refs):
        h_refs, o_refs = refs[:n], refs[n:2 * n]
        send_sems, recv_sems = refs[2 * n:]
        x, y, c, _ = _place()
        sibling = (x, y, 1 - c)
        sent = []
        for i in range(n):
            h = halves[i].shape[0]
            mine = o_refs[i].at[pl.ds(c * h, h)]
            rc = pltpu.make_async_remote_copy(
                src_ref=h_refs[i], dst_ref=mine, send_sem=send_sems.at[i], recv_sem=recv_sems.at[i],
                device_id=sibling, device_id_type=MESH)
            rc.start()
            sent.append(rc)
        for i in range(n):
            h = halves[i].shape[0]
            theirs = o_refs[i].at[pl.ds((1 - c) * h, h)]
            pltpu.make_async_remote_copy(
                src_ref=h_refs[i], dst_ref=theirs, send_sem=send_sems.at[i], recv_sem=recv_sems.at[i],
                device_id=sibling, device_id_type=MESH).wait_recv()
        for rc in sent:
            rc.wait_send()

    outs = pl.pallas_call(
        body, name=name,
        in_specs=_any_specs(n), out_specs=_any_specs(n),
        out_shape=[jax.ShapeDtypeStruct((2 * s.shape[0], s.shape[1]), s.dtype) for s in halves],
        scratch_shapes=[pltpu.SemaphoreType.DMA((n,)), pltpu.SemaphoreType.DMA((n,))],
    )(*halves)
    c = lax.axis_index("c")
    return [lax.dynamic_update_slice(o, s, (c * s.shape[0], 0)) for o, s in zip(outs, halves)]


def _add_pair(name, a, b, c):
    _, H, C = b.shape
    th = _row_tile(H, max(8, (1024 * 1024) // (4 * C) // 8 * 8))
    per_half = H // th

    def body(c_ref, a_ref, b_ref, o_ref):
        o_ref[...] = (a_ref[...].astype(F32) + b_ref[...].astype(F32)).astype(o_ref.dtype)

    blk = pl.BlockSpec((None, th, C), lambda q, i, c_ref: (q, i, 0))
    return pl.pallas_call(
        body, name=name,
        grid_spec=pltpu.PrefetchScalarGridSpec(
            num_scalar_prefetch=1, grid=(4, per_half),
            in_specs=[pl.BlockSpec((None, th, C), lambda q, i, c_ref: (q, c_ref[0] * per_half + i, 0)), blk],
            out_specs=blk),
        out_shape=jax.ShapeDtypeStruct(b.shape, a.dtype), compiler_params=_cparams("parallel", "parallel"),
    )(c, a, b)


def _add_chips(name, parts):
    _, H, C = parts.shape
    th = _row_tile(H, max(8, (1024 * 1024) // (4 * C) // 8 * 8))

    def body(p_ref, o_ref):
        acc = p_ref[0].astype(F32)
        for q in range(1, 4):
            acc = acc + p_ref[q].astype(F32)
        o_ref[...] = acc

    return pl.pallas_call(
        body, name=name, grid=(H // th,),
        in_specs=[pl.BlockSpec((4, th, C), lambda i: (0, i, 0))], out_specs=pl.BlockSpec((th, C), lambda i: (i, 0)),
        out_shape=jax.ShapeDtypeStruct((H, C), F32), compiler_params=_cparams("parallel"),
    )(parts)


def _pair_sums(tag, grads):
    c = lax.axis_index("c").astype(jnp.int32).reshape(1)
    got = _pair_split(f"rs_pair_split_{tag}", grads)
    return [_add_pair(f"rs_add_pair_{tag}_{i}", g, h, c) for i, (g, h) in enumerate(zip(grads, got))]


def _chip_sums(parts):
    return _pair_join("rs_pair_join", [_add_chips(f"rs_add_chips_{i}", p) for i, p in enumerate(parts)])


def _block_diag(t):
    G, A, B = t.shape
    eye = jnp.eye(8, dtype=t.dtype)
    return jnp.einsum("sgab,gh->sgahb", t.reshape(G // 8, 8, A, B), eye).reshape(G // 8, 8 * A, 8 * B)


def _block_diag_extract(m, A, B):
    n = m.shape[0]
    eye = jnp.eye(8, dtype=m.dtype)
    return jnp.einsum("sgahb,gh->sgab", m.reshape(n, 8, A, 8, B), eye).reshape(8 * n, A, B)


def kernel(x, meta_tokens, norm_mix, norm_ffn, s5_a_re, s5_a_im, s5_log_dt, s5_b_re, s5_b_im, s5_c_re, s5_c_im, s5_d, s5_w_glu, norm_kv, w_kv, w_q, w_o, w_ffn_in, w_ffn_out, norm_final, loss_target, m_meta_tokens, m_norm_mix, m_norm_ffn, m_s5_a_re, m_s5_a_im, m_s5_log_dt, m_s5_b_re, m_s5_b_im, m_s5_c_re, m_s5_c_im, m_s5_d, m_s5_w_glu, m_norm_kv, m_w_kv, m_w_q, m_w_o, m_w_ffn_in, m_w_ffn_out, m_norm_final, v_meta_tokens, v_norm_mix, v_norm_ffn, v_s5_a_re, v_s5_a_im, v_s5_log_dt, v_s5_b_re, v_s5_b_im, v_s5_c_re, v_s5_c_im, v_s5_d, v_s5_w_glu, v_norm_kv, v_w_kv, v_w_q, v_w_o, v_w_ffn_in, v_w_ffn_out, v_norm_final):
    seq, D = x.shape[1], x.shape[2]
    T = X_START + seq
    G, P, C = s5_a_re.shape[1], S5_STATE, S5_GROUP
    d_ff = w_ffn_out.shape[1] * 4
    dq4 = D // 4
    chip = 2 * lax.axis_index("x") + lax.axis_index("y")

    small_in = jnp.concatenate([meta_tokens, jnp.pad(s5_d, ((0, 15), (0, 0)))], axis=0)
    (small_all,) = _all_gather_chips("ag_small", [small_in])
    meta_full = small_all[:, :N_META, :].transpose(1, 0, 2).reshape(N_META, D)
    d_skip = small_all[:, N_META, :].reshape(1, D)
    shards = [s.astype(BF16) for s in (s5_w_glu[0], w_kv, w_q[0], w_o[0],
                                       w_ffn_in.reshape(2 * D, -1), w_ffn_out.reshape(-1, D))]
    rows_out = d_ff // 4

    row = lambda v: v.reshape(1, -1)
    g_mix0, g_mix1 = row(norm_mix[0]), row(norm_mix[1])
    g_ffn = [row(norm_ffn[0]), row(norm_ffn[1])]
    g_kv, g_final = row(norm_kv), row(norm_final)

    a_re3 = s5_a_re[0].reshape(G, 1, P)
    a_im3 = s5_a_im[0].reshape(G, 1, P)
    log_dt3 = s5_log_dt[0].reshape(G, 1, 1)
    bt_re = s5_b_re[0].transpose(0, 2, 1)
    bt_im = s5_b_im[0].transpose(0, 2, 1)
    ab_re, ab_im, bb_re, bb_im = _s5_prep_fwd("s5_prep", a_re3, a_im3, log_dt3, bt_re, bt_im)
    abar_re, abar_im = ab_re.reshape(1, G * P), ab_im.reshape(1, G * P)
    bd_b_re = _block_diag(bb_re).astype(BF16)
    bd_b_im = _block_diag(bb_im).astype(BF16)
    bd_bt_re = bd_b_re.transpose(0, 2, 1)
    bd_bt_im = bd_b_im.transpose(0, 2, 1)
    bd_c_re = _block_diag(s5_c_re[0]).astype(BF16)
    bd_c_im = _block_diag(s5_c_im[0]).astype(BF16)
    bd_ct_re = bd_c_re.transpose(0, 2, 1)
    bd_ct_im = bd_c_im.transpose(0, 2, 1)

    h0 = jnp.concatenate([jnp.zeros((META_START, D), F32), meta_full, x[0]], axis=0)
    (u,) = _rmsnorm_fwd("norm_mix0", h0, [g_mix0], [F32])
    (y, z, x_re, x_im), landed = _s5_fwd(
        "s5_scan", u, bd_b_re, bd_b_im, bd_ct_re, bd_ct_im, abar_re, abar_im, d_skip,
        comm=_halves_gather_comm(shards), comm_args=shards)
    wg_glu, wg_kv, wg_q, wg_o, wg_in, wg_out = _pair_forward("ag_forward", landed, shards)
    wg_q = wg_q.reshape(1, D, D)
    wg_o = wg_o.reshape(1, D, D)
    wg_out = [wg_out[:, l * rows_out:(l + 1) * rows_out, :].reshape(1, d_ff, D) for l in range(2)]
    val, gate, h1 = _mm_glu("glu_proj", z, wg_glu, h0)

    def ffn_fwd(l, h):
        (n,) = _rmsnorm_fwd(f"norm_ffn{l}", h, [g_ffn[l]], [BF16])
        g, u, mid = _mm_swiglu(f"ffn_in{l}", n, wg_in, l)
        return n, (g, u), mid, _mm_nn(f"ffn_out{l}", mid, wg_out[l], res=h)

    n1, gu0, mid0, h2 = ffn_fwd(0, h1)
    nk, nq = _rmsnorm_fwd("norm_kv_q", h2, [g_kv, g_mix1], [BF16, BF16])
    kv = _mm_nn("kv_proj", nk, wg_kv, out_dtype=BF16)
    q = _mm_nn("q_proj", nq, wg_q, out_dtype=BF16)
    o, ltot = _attn_fwd("attn_fwd", q, kv)
    h3 = _mm_nn("o_proj", o, wg_o, res=h2)
    n3, gu1, mid1, h4 = ffn_fwd(1, h3)
    loss_part, dh4, dg_final = _final_loss("final_loss", h4, g_final, loss_target[0])

    def ffn_bwd(l, dh, h, n, gu, mid):
        dgu = _mm_nt_swiglu_bwd(f"ffn_out{l}_dx", dh, wg_out[l], *gu)
        dw_out = _mm_tn(f"ffn_out{l}_dw", mid, dh, 1)
        dw_in = _mm_tn(f"ffn_in{l}_dw", n, dgu, 4)
        dn = _mm_nt_k(f"ffn_in{l}_dx", dgu, wg_in, D, k_blk=l)
        dh_prev, (dg,) = _rmsnorm_bwd(f"norm_ffn{l}_bwd", h, [(g_ffn[l], dn)], dh)
        return dh_prev, dg, dw_in, dw_out

    dh3, dg_ffn1, dw_in1, dw_out1 = ffn_bwd(1, dh4, h3, n3, gu1, mid1)
    d_o = _mm_nt_k("o_proj_dx", dh3, wg_o, D, out_dtype=BF16)
    dw_o = _mm_tn("o_proj_dw", o, dh3, 1)
    dq, dk, dv = _attn_bwd("attn_bwd", q, kv, d_o, ltot)
    dkv = jnp.concatenate([dk, dv], axis=1)
    dw_q = _mm_tn("q_proj_dw", nq, dq, 1)
    dnq = _mm_nt_k("q_proj_dx", dq, wg_q, D)
    dw_kv = _mm_tn("kv_proj_dw", nk, dkv, 4)
    dnk = _mm_nt_k("kv_proj_dx", dkv, wg_kv, D)
    dh2, (dg_mix1, dg_kv) = _rmsnorm_bwd("norm_kv_q_bwd", h2, [(g_mix1, dnq), (g_kv, dnk)], dh3)
    dh1, dg_ffn0, dw_in0, dw_out0 = ffn_bwd(0, dh2, h1, n1, gu0, mid0)
    dvg = _glu_bwd("glu_bwd", val, gate, dh1)
    dw_glu = _mm_tn("glu_proj_dw", z, dvg, 4)
    dz = _mm_nt_k("glu_proj_dx", dvg, wg_glu, D)
    big = [dw_kv, dw_q.reshape(4, D // 4, D), dw_o.reshape(4, D // 4, D), dw_in1, dw_out1.reshape(4, rows_out, D),
           dw_glu, dw_in0, dw_out0.reshape(4, rows_out, D)]
    big_pairs = _pair_sums("big", big)
    (du, dd, dbd_b_re, dbd_b_im, dbd_c_re, dbd_c_im, dab_re, dab_im), big_parts = _s5_bwd(
        "s5_scan_bwd", dz, y, u, x_re, x_im, bd_c_re, bd_c_im, bd_bt_re, bd_bt_im, abar_re, abar_im, d_skip,
        comm=_exchange_comm(big_pairs), comm_args=big_pairs)
    big_parts = _exchange_finish(big_pairs, big_parts)
    dh0, (dg_mix0,) = _rmsnorm_bwd("norm_mix0_bwd", h0, [(g_mix0, du)], dh1)
    da_re, da_im, dlog_dt, dbt_re, dbt_im = _s5_prep_bwd(
        "s5_prep_bwd", a_re3, a_im3, log_dt3, bt_re, bt_im,
        dab_re.reshape(G, 1, P), dab_im.reshape(G, 1, P),
        _block_diag_extract(dbd_b_re, C, P), _block_diag_extract(dbd_b_im, C, P))
    grad_x = dh0[X_START:][None]

    small_parts = [
        dg_mix0, dg_mix1, dg_ffn0, dg_ffn1, da_re, da_im,
        dbt_re.transpose(0, 2, 1), dbt_im.transpose(0, 2, 1),
        _block_diag_extract(dbd_c_re, C, P), _block_diag_extract(dbd_c_im, C, P),
        dg_kv, dg_final, dh0[META_START:X_START], dd, loss_part, dlog_dt]
    small_sizes = [p.size for p in small_parts]
    unit = 4 * 2 * 8 * 128
    padded = -(-sum(small_sizes) // unit) * unit
    tail = jnp.concatenate([loss_part.reshape(-1), dlog_dt.reshape(-1)])
    small_flat = jnp.concatenate(
        [p.reshape(-1) for p in small_parts[:-2]] + [jnp.pad(tail, (0, padded - sum(small_sizes)))])
    small_blocks = small_flat.reshape(4, padded // (4 * 128), 128)
    small_exchanged = _chip_exchange("rs_chip_exchange", _pair_sums("small", [small_blocks]))
    gw_kv, gw_q, gw_o, gw_in1, gw_out1, gw_glu, gw_in0, gw_out0, small_mine = _chip_sums(big_parts + small_exchanged)
    gw_in = jnp.concatenate([gw_in0, gw_in1], axis=0)
    gw_out = jnp.concatenate([gw_out0, gw_out1], axis=0)
    (small_red,) = _all_gather_chips("ag_small_grads", [small_mine])
    small_red = small_red.reshape(-1)
    pieces, at = [], 0
    for p, size in zip(small_parts, small_sizes):
        pieces.append(small_red[at:at + size].reshape(p.shape))
        at += size
    (gn_mix0, gn_mix1, gn_ffn0, gn_ffn1, ga_re, ga_im, gb_re, gb_im, gc_re, gc_im, gn_kv, gn_final,
     gmeta_full, gd_full, loss_all, glog_dt) = pieces
    loss = loss_all[0, 0]
    gn_mix = small_red[:2 * D].reshape(2, D)
    gn_ffn = small_red[2 * D:4 * D].reshape(2, D)
    gmeta = lax.dynamic_slice_in_dim(gmeta_full, chip * dq4, dq4, axis=1)
    gd = lax.dynamic_slice_in_dim(gd_full, chip * dq4, dq4, axis=1)

    grads = {
        "meta_tokens": gmeta, "norm_mix": gn_mix, "norm_ffn": gn_ffn,
        "s5_a_re": ga_re.reshape(s5_a_re.shape), "s5_a_im": ga_im.reshape(s5_a_im.shape),
        "s5_log_dt": glog_dt.reshape(s5_log_dt.shape),
        "s5_b_re": gb_re.reshape(s5_b_re.shape), "s5_b_im": gb_im.reshape(s5_b_im.shape),
        "s5_c_re": gc_re.reshape(s5_c_re.shape), "s5_c_im": gc_im.reshape(s5_c_im.shape),
        "s5_d": gd, "s5_w_glu": gw_glu.reshape(s5_w_glu.shape), "norm_kv": gn_kv.reshape(norm_kv.shape),
        "w_kv": gw_kv, "w_q": gw_q.reshape(w_q.shape), "w_o": gw_o.reshape(w_o.shape),
        "w_ffn_in": gw_in.reshape(w_ffn_in.shape), "w_ffn_out": gw_out.reshape(w_ffn_out.shape),
        "norm_final": gn_final.reshape(norm_final.shape),
    }
    weights = {
        "meta_tokens": (meta_tokens, m_meta_tokens, v_meta_tokens), "norm_mix": (norm_mix, m_norm_mix, v_norm_mix),
        "norm_ffn": (norm_ffn, m_norm_ffn, v_norm_ffn), "s5_a_re": (s5_a_re, m_s5_a_re, v_s5_a_re),
        "s5_a_im": (s5_a_im, m_s5_a_im, v_s5_a_im), "s5_log_dt": (s5_log_dt, m_s5_log_dt, v_s5_log_dt),
        "s5_b_re": (s5_b_re, m_s5_b_re, v_s5_b_re), "s5_b_im": (s5_b_im, m_s5_b_im, v_s5_b_im),
        "s5_c_re": (s5_c_re, m_s5_c_re, v_s5_c_re), "s5_c_im": (s5_c_im, m_s5_c_im, v_s5_c_im),
        "s5_d": (s5_d, m_s5_d, v_s5_d), "s5_w_glu": (s5_w_glu, m_s5_w_glu, v_s5_w_glu),
        "norm_kv": (norm_kv, m_norm_kv, v_norm_kv), "w_kv": (w_kv, m_w_kv, v_w_kv), "w_q": (w_q, m_w_q, v_w_q),
        "w_o": (w_o, m_w_o, v_w_o), "w_ffn_in": (w_ffn_in, m_w_ffn_in, v_w_ffn_in),
        "w_ffn_out": (w_ffn_out, m_w_ffn_out, v_w_ffn_out), "norm_final": (norm_final, m_norm_final, v_norm_final),
    }
    names = list(weights)
    deltas, new_m, new_v = [], [], []
    for name in names:
        w, m, v = weights[name]
        d, nm, nv = _adamw(f"adamw_{name}", w, grads[name], m, v)
        deltas.append(d)
        new_m.append(nm)
        new_v.append(nv)
    return (loss, grad_x, *[grads[n] for n in names], *deltas, *new_m, *new_v)
```

```python
import functools
import math

import jax
import jax.numpy as jnp
from jax import lax
from jax.experimental import pallas as pl
from jax.experimental.pallas import tpu as pltpu

F32 = jnp.float32
BF16 = jnp.bfloat16

N_META = 16
X_START = 128
META_START = X_START - N_META
S5_GROUP = 16
S5_STATE = 64
HEAD_DIM = 64
KEY_BLOCK = 128
STATE_TILE = 512
CH_TILE = 128
RMS_EPS = 1e-6
ADAM_LR, ADAM_B1, ADAM_B2, ADAM_EPS, ADAM_WD, ADAM_STEP = 0.001, 0.9, 0.999, 1e-08, 0.01, 10
VMEM_LIMIT_BYTES = 48 * 1024 * 1024
MESH = pl.DeviceIdType.MESH
NT_DIMS = (((1,), (1,)), ((), ()))
TN_DIMS = (((0,), (0,)), ((), ()))


def _cparams(*sem):
    return pltpu.CompilerParams(dimension_semantics=sem, vmem_limit_bytes=VMEM_LIMIT_BYTES)


def _row_tile(rows, cap):
    for unit in (128, 8):
        best = 0
        for t in range(unit, min(rows, cap) + 1, unit):
            if rows % t == 0:
                best = t
        if best:
            return best
    return rows


def _col_tile(cols, cap):
    best = 0
    for t in range(128, min(cols, cap) + 1, 128):
        if cols % t == 0:
            best = t
    return best if best else cols


def _gelu(x):
    k = math.sqrt(2.0 / math.pi)
    return 0.5 * x * (1.0 + jnp.tanh(k * (x + 0.044715 * x * x * x)))


def _gelu_grad(x):
    k = math.sqrt(2.0 / math.pi)
    t = jnp.tanh(k * (x + 0.044715 * x * x * x))
    return 0.5 * (1.0 + t) + 0.5 * x * (1.0 - t * t) * k * (1.0 + 3.0 * 0.044715 * x * x)


def _sigmoid(x):
    return 0.5 + 0.5 * jnp.tanh(0.5 * x)


def _rmsnorm_fwd(name, x, gains, out_dtypes):
    T, D = x.shape
    tm = _row_tile(T, 512)
    n = len(gains)

    def body(x_ref, *refs):
        xv = x_ref[...]
        xh = xv * lax.rsqrt(jnp.mean(xv * xv, axis=-1, keepdims=True) + RMS_EPS)
        for g_ref, o_ref in zip(refs[:n], refs[n:]):
            o_ref[...] = (xh * g_ref[...]).astype(o_ref.dtype)

    row = pl.BlockSpec((tm, D), lambda i: (i, 0))
    vec = pl.BlockSpec((1, D), lambda i: (0, 0))
    return pl.pallas_call(
        body, name=name, grid=(T // tm,),
        in_specs=[row] + [vec] * n, out_specs=[row] * n,
        out_shape=[jax.ShapeDtypeStruct((T, D), dt) for dt in out_dtypes],
        compiler_params=_cparams("parallel"),
    )(x, *gains)


def _rmsnorm_bwd(name, x, pairs, dres):
    T, D = x.shape
    tm = _row_tile(T, 512)
    n = len(pairs)

    def body(x_ref, dres_ref, *refs):
        g_refs, dy_refs = refs[:n], refs[n:2 * n]
        dx_ref, dg_refs = refs[2 * n], refs[2 * n + 1:]
        i = pl.program_id(0)
        xv = x_ref[...]
        r = lax.rsqrt(jnp.mean(xv * xv, axis=-1, keepdims=True) + RMS_EPS)
        xh = xv * r
        dxh = jnp.zeros_like(xv)
        for g_ref, dy_ref, dg_ref in zip(g_refs, dy_refs, dg_refs):
            dy = dy_ref[...].astype(F32)
            part = jnp.sum(dy * xh, axis=0, keepdims=True)

            @pl.when(i == 0)
            def _():
                dg_ref[...] = part

            @pl.when(i > 0)
            def _():
                dg_ref[...] += part

            dxh = dxh + dy * g_ref[...]
        dx = r * (dxh - xh * jnp.mean(dxh * xh, axis=-1, keepdims=True))
        dx_ref[...] = dres_ref[...] + dx

    row = pl.BlockSpec((tm, D), lambda i: (i, 0))
    vec = pl.BlockSpec((1, D), lambda i: (0, 0))
    outs = pl.pallas_call(
        body, name=name, grid=(T // tm,),
        in_specs=[row, row] + [vec] * n + [row] * n,
        out_specs=[row] + [vec] * n,
        out_shape=[jax.ShapeDtypeStruct((T, D), F32)] + [jax.ShapeDtypeStruct((1, D), F32)] * n,
        compiler_params=_cparams("arbitrary"),
    )(x, dres, *[g for g, _ in pairs], *[dy for _, dy in pairs])
    return outs[0], outs[1:]


def _mm_nn(name, a, w, k_blk=0, res=None, out_dtype=F32):
    M, K = a.shape
    S, _, Ns = w.shape
    tm = _row_tile(M, 512)
    tn = _col_tile(Ns, 1408)
    nt = Ns // tn

    def body(a_ref, w_ref, *refs):
        o_ref = refs[-1]
        acc = jnp.dot(a_ref[...].astype(BF16), w_ref[...], preferred_element_type=F32)
        if res is not None:
            acc = acc + refs[0][...]
        o_ref[...] = acc.astype(o_ref.dtype)

    in_specs = [pl.BlockSpec((tm, K), lambda j, i: (i, 0)),
                pl.BlockSpec((None, K, tn), lambda j, i: (j // nt, k_blk, j % nt))]
    args = [a, w]
    if res is not None:
        in_specs.append(pl.BlockSpec((tm, tn), lambda j, i: (i, j)))
        args.append(res)
    return pl.pallas_call(
        body, name=name, grid=(S * nt, M // tm),
        in_specs=in_specs, out_specs=pl.BlockSpec((tm, tn), lambda j, i: (i, j)),
        out_shape=jax.ShapeDtypeStruct((M, S * Ns), out_dtype),
        compiler_params=_cparams("parallel", "parallel"),
    )(*args)


def _mm_nt_k(name, dy, w, K, k_blk=0, out_dtype=F32, comm=None, comm_args=()):
    M = dy.shape[0]
    S, _, Ns = w.shape
    tm = _row_tile(M, 1408)
    tn = _col_tile(Ns, 1408)
    nt = Ns // tn
    steps = S * nt

    def body(dy_ref, w_ref, o_ref, acc_ref):
        j = pl.program_id(1)
        part = lax.dot_general(dy_ref[...].astype(BF16), w_ref[...], (((1,), (1,)), ((), ())),
                               preferred_element_type=F32)

        @pl.when(j == 0)
        def _():
            acc_ref[...] = part

        @pl.when(j > 0)
        def _():
            acc_ref[...] += part

        @pl.when(j == steps - 1)
        def _():
            o_ref[...] = acc_ref[...].astype(o_ref.dtype)

    (out,), comm_results = _grid_call(
        name, body, (M // tm, steps),
        in_specs=[pl.BlockSpec((tm, tn), lambda i, j: (i, j)),
                  pl.BlockSpec((None, K, tn), lambda i, j: (j // nt, k_blk, j % nt))],
        out_specs=[pl.BlockSpec((tm, K), lambda i, j: (i, 0))],
        out_shape=[jax.ShapeDtypeStruct((M, K), out_dtype)],
        scratch_shapes=[pltpu.VMEM((tm, K), F32)],
        args=(dy, w), comm=comm, comm_args=comm_args)
    return out if comm is None else (out, comm_results)


def _mm_tn(name, a, dy, S, out_dtype=BF16):
    T, K = a.shape
    Ns = dy.shape[1] // S
    tn = _col_tile(Ns, max(128, (6 * 1024 * 1024) // (4 * K) // 128 * 128))
    nt = Ns // tn
    tt = _row_tile(T, 1408)
    steps = T // tt

    def body(a_ref, dy_ref, o_ref, acc_ref):
        t = pl.program_id(1)
        part = lax.dot_general(a_ref[...].astype(BF16), dy_ref[...].astype(BF16), (((0,), (0,)), ((), ())),
                               preferred_element_type=F32)

        @pl.when(t == 0)
        def _():
            acc_ref[...] = part

        @pl.when(t > 0)
        def _():
            acc_ref[...] += part

        @pl.when(t == steps - 1)
        def _():
            o_ref[...] = acc_ref[...].astype(o_ref.dtype)

    return pl.pallas_call(
        body, name=name, grid=(S * nt, steps),
        in_specs=[pl.BlockSpec((tt, K), lambda j, t: (t, 0)),
                  pl.BlockSpec((tt, tn), lambda j, t: (t, j))],
        out_specs=pl.BlockSpec((None, K, tn), lambda j, t: (j // nt, 0, j % nt)),
        out_shape=jax.ShapeDtypeStruct((S, K, Ns), out_dtype),
        scratch_shapes=[pltpu.VMEM((K, tn), F32)],
        compiler_params=_cparams("parallel", "arbitrary"),
    )(a, dy)


def _gated_tile(T, width):
    return _row_tile(T, max(8, (2 * 1024 * 1024) // (4 * width) // 8 * 8))


def _mm_glu(name, a, w, res):
    M, K = a.shape
    Ns = w.shape[2]
    tm = _row_tile(M, 512)

    def body(a_ref, wv_ref, wg_ref, r_ref, v_ref, g_ref, o_ref):
        ab = a_ref[...].astype(BF16)
        val = jnp.dot(ab, wv_ref[...], preferred_element_type=F32)
        gate = jnp.dot(ab, wg_ref[...], preferred_element_type=F32)
        v_ref[...] = val
        g_ref[...] = gate
        o_ref[...] = r_ref[...] + val * _sigmoid(gate)

    out = pl.BlockSpec((tm, Ns), lambda j, i: (i, j))
    return pl.pallas_call(
        body, name=name, grid=(2, M // tm),
        in_specs=[pl.BlockSpec((tm, K), lambda j, i: (i, 0)),
                  pl.BlockSpec((None, K, Ns), lambda j, i: (j, 0, 0)),
                  pl.BlockSpec((None, K, Ns), lambda j, i: (j + 2, 0, 0)), out],
        out_specs=[out, out, out],
        out_shape=[jax.ShapeDtypeStruct((M, 2 * Ns), F32)] * 3,
        compiler_params=_cparams("parallel", "parallel"),
    )(a, w, w, res)


def _glu_bwd(name, val, gate, dout):
    T, D = dout.shape
    tm = _gated_tile(T, 2 * D)

    def body(v_ref, g_ref, d_ref, o_ref):
        s = _sigmoid(g_ref[...])
        d = d_ref[...]
        o_ref[:, :D] = (d * s).astype(o_ref.dtype)
        o_ref[:, D:] = (d * v_ref[...] * s * (1.0 - s)).astype(o_ref.dtype)

    blk = pl.BlockSpec((tm, D), lambda i: (i, 0))
    return pl.pallas_call(
        body, name=name, grid=(T // tm,),
        in_specs=[blk, blk, blk],
        out_specs=pl.BlockSpec((tm, 2 * D), lambda i: (i, 0)),
        out_shape=jax.ShapeDtypeStruct((T, 2 * D), BF16),
        compiler_params=_cparams("parallel"),
    )(val, gate, dout)


def _mm_swiglu(name, a, w, k_blk):
    M, K = a.shape
    Ns = w.shape[2]
    tm = _row_tile(M, 512)

    def body(a_ref, wg_ref, wu_ref, g_ref, u_ref, mid_ref):
        ab = a_ref[...].astype(BF16)
        g = jnp.dot(ab, wg_ref[...], preferred_element_type=F32)
        u = jnp.dot(ab, wu_ref[...], preferred_element_type=F32)
        g_ref[...] = g.astype(g_ref.dtype)
        u_ref[...] = u.astype(u_ref.dtype)
        mid_ref[...] = (g * _sigmoid(g) * u).astype(mid_ref.dtype)

    out = pl.BlockSpec((tm, Ns), lambda j, i: (i, j))
    return pl.pallas_call(
        body, name=name, grid=(2, M // tm),
        in_specs=[pl.BlockSpec((tm, K), lambda j, i: (i, 0)),
                  pl.BlockSpec((None, K, Ns), lambda j, i: (j, k_blk, 0)),
                  pl.BlockSpec((None, K, Ns), lambda j, i: (j + 2, k_blk, 0))],
        out_specs=[out, out, out],
        out_shape=[jax.ShapeDtypeStruct((M, 2 * Ns), BF16)] * 3,
        compiler_params=_cparams("parallel", "parallel"),
    )(a, w, w)


def _mm_nt_swiglu_bwd(name, dh, w, g, u):
    M, D = dh.shape
    F = w.shape[1]
    tm = _row_tile(M, 512)

    def body(dh_ref, w_ref, g_ref, u_ref, o_ref):
        d = lax.dot_general(dh_ref[...].astype(BF16), w_ref[...], NT_DIMS, preferred_element_type=F32)
        gv = g_ref[...].astype(F32)
        s = _sigmoid(gv)
        o_ref[:, :F] = (d * u_ref[...].astype(F32) * s * (1.0 + gv * (1.0 - s))).astype(o_ref.dtype)
        o_ref[:, F:] = (d * gv * s).astype(o_ref.dtype)

    half = pl.BlockSpec((tm, F), lambda i: (i, 0))
    return pl.pallas_call(
        body, name=name, grid=(M // tm,),
        in_specs=[pl.BlockSpec((tm, D), lambda i: (i, 0)), pl.BlockSpec((None, F, D), lambda i: (0, 0, 0)), half, half],
        out_specs=pl.BlockSpec((tm, 2 * F), lambda i: (i, 0)),
        out_shape=jax.ShapeDtypeStruct((M, 2 * F), BF16),
        compiler_params=_cparams("parallel"),
    )(dh, w, g, u)


def _final_loss(name, h, gain, target):
    T, D = h.shape
    tm = X_START
    lead = X_START // tm

    def body(h_ref, g_ref, t_ref, loss_ref, dh_ref, dg_ref):
        i = pl.program_id(0)

        @pl.when(i == 0)
        def _():
            loss_ref[...] = jnp.zeros_like(loss_ref)
            dg_ref[...] = jnp.zeros_like(dg_ref)
            dh_ref[...] = jnp.zeros_like(dh_ref)

        @pl.when(i >= lead)
        def _():
            xv = h_ref[...]
            r = lax.rsqrt(jnp.mean(xv * xv, axis=-1, keepdims=True) + RMS_EPS)
            xh = xv * r
            g = g_ref[...]
            diff = xh * g - t_ref[...]
            loss_ref[...] += 0.5 * jnp.sum(jnp.mean(diff * diff, axis=-1, keepdims=True), axis=0, keepdims=True)
            dout = diff * (1.0 / D)
            dg_ref[...] += jnp.sum(dout * xh, axis=0, keepdims=True)
            dxh = dout * g
            dh_ref[...] = r * (dxh - xh * jnp.mean(dxh * xh, axis=-1, keepdims=True))

    return pl.pallas_call(
        body, name=name, grid=(T // tm,),
        in_specs=[pl.BlockSpec((tm, D), lambda i: (i, 0)), pl.BlockSpec((1, D), lambda i: (0, 0)),
                  pl.BlockSpec((tm, D), lambda i: (jnp.maximum(i - lead, 0), 0))],
        out_specs=[pl.BlockSpec((1, 128), lambda i: (0, 0)), pl.BlockSpec((tm, D), lambda i: (i, 0)),
                   pl.BlockSpec((1, D), lambda i: (0, 0))],
        out_shape=[jax.ShapeDtypeStruct((1, 128), F32), jax.ShapeDtypeStruct((T, D), F32),
                   jax.ShapeDtypeStruct((1, D), F32)],
        compiler_params=_cparams("arbitrary"),
    )(h, gain, target)


def _grid_call(name, body, grid, in_specs, out_specs, out_shape, scratch_shapes, args, comm=None, comm_args=()):
    params = _cparams(*(("arbitrary",) * len(grid)))
    if comm is None:
        return pl.pallas_call(body, name=name, grid=grid, in_specs=in_specs, out_specs=out_specs,
                              out_shape=out_shape, scratch_shapes=scratch_shapes, compiler_params=params)(*args), []
    outs = pl.pallas_call(
        _embed_comm(comm, body, grid, len(in_specs), len(out_specs)), name=name, grid=grid,
        in_specs=list(in_specs) + _any_specs(comm.n), out_specs=list(out_specs) + _any_specs(comm.n),
        out_shape=list(out_shape) + comm.out_shape, scratch_shapes=list(scratch_shapes) + comm.scratch,
        compiler_params=params)(*args, *comm_args)
    return outs[:len(out_specs)], outs[len(out_specs):]


def _s5_discretise(a_re, a_im, log_dt, bt_re, bt_im):
    dt = jnp.exp(log_dt)
    mag = jnp.exp(dt * a_re)
    ang = dt * a_im
    abar_re = mag * jnp.cos(ang)
    abar_im = mag * jnp.sin(ang)
    den = a_re * a_re + a_im * a_im
    coef_re = ((abar_re - 1.0) * a_re + abar_im * a_im) / den
    coef_im = (abar_im * a_re - (abar_re - 1.0) * a_im) / den
    bbar_re = coef_re * bt_re - coef_im * bt_im
    bbar_im = coef_re * bt_im + coef_im * bt_re
    return abar_re, abar_im, bbar_re, bbar_im


def _s5_prep_fwd(name, a_re, a_im, log_dt, bt_re, bt_im):
    G, _, P = a_re.shape
    C = bt_re.shape[1]

    def body(ar, ai, ld, br, bi, o_ar, o_ai, o_br, o_bi):
        outs = _s5_discretise(ar[...], ai[...], ld[...], br[...], bi[...])
        for o, v in zip((o_ar, o_ai, o_br, o_bi), outs):
            o[...] = v

    return pl.pallas_call(
        body, name=name,
        out_shape=[jax.ShapeDtypeStruct((G, 1, P), F32)] * 2 + [jax.ShapeDtypeStruct((G, C, P), F32)] * 2,
    )(a_re, a_im, log_dt, bt_re, bt_im)


def _s5_prep_bwd(name, a_re, a_im, log_dt, bt_re, bt_im, d_ar, d_ai, d_br, d_bi):
    G, _, P = a_re.shape
    C = bt_re.shape[1]

    def body(ar, ai, ld, br, bi, gar, gai, gbr, gbi, o_ar, o_ai, o_ld, o_br, o_bi):
        _, vjp = jax.vjp(_s5_discretise, ar[...], ai[...], ld[...], br[...], bi[...])
        grads = vjp((gar[...], gai[...], gbr[...], gbi[...]))
        for o, v in zip((o_ar, o_ai, o_ld, o_br, o_bi), grads):
            o[...] = v

    return pl.pallas_call(
        body, name=name,
        out_shape=[jax.ShapeDtypeStruct((G, 1, P), F32)] * 2 + [jax.ShapeDtypeStruct((G, 1, 1), F32)]
        + [jax.ShapeDtypeStruct((G, C, P), F32)] * 2,
    )(a_re, a_im, log_dt, bt_re, bt_im, d_ar, d_ai, d_br, d_bi)


def _cmul(ar, ai, br, bi):
    return ar * br - ai * bi, ar * bi + ai * br


def _power_table(a_re, a_im):
    rows_re, rows_im = [a_re], [a_im]
    for _ in range(7):
        r, m = _cmul(rows_re[-1], rows_im[-1], a_re, a_im)
        rows_re.append(r)
        rows_im.append(m)
    row = lax.broadcasted_iota(jnp.int32, (8, a_re.shape[1]), 0)
    t_re = jnp.zeros((8, a_re.shape[1]), F32)
    t_im = jnp.zeros((8, a_re.shape[1]), F32)
    for k in range(8):
        t_re = jnp.where(row == k, rows_re[k], t_re)
        t_im = jnp.where(row == k, rows_im[k], t_im)
    return t_re, t_im, rows_re, rows_im


def _s5_fwd(name, u, b_re, b_im, ct_re, ct_im, abar_re, abar_im, d_skip, comm=None, comm_args=()):
    T, D = u.shape
    n_st = D // CH_TILE
    W = STATE_TILE
    tc = _row_tile(T, 512)
    n_tiles = tc // 8

    def body(u_ref, bre_ref, bim_ref, cre_ref, cim_ref, ar_ref, ai_ref, d_ref,
             y_ref, z_ref, xr_ref, xi_ref, carry_re, carry_im, pw_re, pw_im, sh_re, sh_im):
        c = pl.program_id(1)

        @pl.when(c == 0)
        def _():
            t_re, t_im, rows_re, rows_im = _power_table(ar_ref[...], ai_ref[...])
            pw_re[...] = t_re
            pw_im[...] = t_im
            first_rows = lax.broadcasted_iota(jnp.int32, (8, W), 0)
            for n, d in enumerate((1, 2, 4)):
                sh_re[n] = jnp.where(first_rows >= d, jnp.broadcast_to(rows_re[d - 1], (8, W)), 0.0)
                sh_im[n] = jnp.where(first_rows >= d, jnp.broadcast_to(rows_im[d - 1], (8, W)), 0.0)
            carry_re[...] = jnp.zeros_like(carry_re)
            carry_im[...] = jnp.zeros_like(carry_im)

        ub = u_ref[...].astype(BF16)
        xr_ref[...] = jnp.dot(ub, bre_ref[...], preferred_element_type=F32)
        xi_ref[...] = jnp.dot(ub, bim_ref[...], preferred_element_type=F32)
        row = lax.broadcasted_iota(jnp.int32, (8, W), 0)

        def tile(i, carry):
            c_re, c_im = carry
            rows = pl.ds(pl.multiple_of(i * 8, 8), 8)
            r = xr_ref[rows, :]
            m = xi_ref[rows, :]
            for n, d in enumerate((1, 2, 4)):
                pr, pm = _cmul(sh_re[n], sh_im[n], pltpu.roll(r, d, 0), pltpu.roll(m, d, 0))
                r = r + pr
                m = m + pm
            pr, pm = _cmul(pw_re[...], pw_im[...], c_re, c_im)
            r = r + pr
            m = m + pm
            xr_ref[rows, :] = r
            xi_ref[rows, :] = m
            return jnp.broadcast_to(r[7:8, :], (8, W)), jnp.broadcast_to(m[7:8, :], (8, W))

        c_re, c_im = lax.fori_loop(0, n_tiles, tile, (carry_re[...], carry_im[...]))
        carry_re[...] = c_re
        carry_im[...] = c_im
        y = (jnp.dot(xr_ref[...].astype(BF16), cre_ref[...], preferred_element_type=F32)
             - jnp.dot(xi_ref[...].astype(BF16), cim_ref[...], preferred_element_type=F32)
             + d_ref[...] * u_ref[...])
        y_ref[...] = y
        z_ref[...] = _gelu(y).astype(z_ref.dtype)

    ch = pl.BlockSpec((tc, CH_TILE), lambda s, c: (c, s))
    st = pl.BlockSpec((tc, W), lambda s, c: (c, s))
    return _grid_call(
        name, body, (n_st, T // tc),
        in_specs=[ch,
                  pl.BlockSpec((None, CH_TILE, W), lambda s, c: (s, 0, 0)),
                  pl.BlockSpec((None, CH_TILE, W), lambda s, c: (s, 0, 0)),
                  pl.BlockSpec((None, W, CH_TILE), lambda s, c: (s, 0, 0)),
                  pl.BlockSpec((None, W, CH_TILE), lambda s, c: (s, 0, 0)),
                  pl.BlockSpec((1, W), lambda s, c: (0, s)),
                  pl.BlockSpec((1, W), lambda s, c: (0, s)),
                  pl.BlockSpec((1, CH_TILE), lambda s, c: (0, s))],
        out_specs=[ch, ch, st, st],
        out_shape=[jax.ShapeDtypeStruct((T, D), F32), jax.ShapeDtypeStruct((T, D), BF16),
                   jax.ShapeDtypeStruct((T, 4 * D), F32), jax.ShapeDtypeStruct((T, 4 * D), F32)],
        scratch_shapes=[pltpu.VMEM((8, W), F32), pltpu.VMEM((8, W), F32),
                        pltpu.VMEM((8, W), F32), pltpu.VMEM((8, W), F32),
                        pltpu.VMEM((3, 8, W), F32), pltpu.VMEM((3, 8, W), F32)],
        args=(u, b_re, b_im, ct_re, ct_im, abar_re, abar_im, d_skip), comm=comm, comm_args=comm_args)


def _s5_bwd(name, dz, y, u, x_re, x_im, c_re, c_im, bt_re, bt_im, abar_re, abar_im, d_skip, comm=None, comm_args=()):
    T, D = u.shape
    n_st = D // CH_TILE
    W = STATE_TILE
    tc = _row_tile(T, 512)
    n_chunks = T // tc
    n_tiles = tc // 8
    tiles_per_chunk = tc // 8

    def body(dz_ref, y_ref, u_ref, xr_ref, xi_ref, xpr_ref, xpi_ref, cre_ref, cim_ref, btr_ref, bti_ref,
             ar_ref, ai_ref, d_ref,
             du_ref, dd_ref, dbr_ref, dbi_ref, dcr_ref, dci_ref, dar_ref, dai_ref,
             lam_re, lam_im, xe_re, xe_im, carry_re, carry_im, pw_re, pw_im, sh_re, sh_im, acc_ar, acc_ai):
        k = pl.program_id(1)
        first_chunk = k == n_chunks - 1

        @pl.when(k == 0)
        def _():
            t_re, t_im, rows_re, rows_im = _power_table(ar_ref[...], -ai_ref[...])
            row = lax.broadcasted_iota(jnp.int32, (8, W), 0)
            r_re = jnp.zeros((8, W), F32)
            r_im = jnp.zeros((8, W), F32)
            for j in range(8):
                r_re = jnp.where(row == j, rows_re[7 - j], r_re)
                r_im = jnp.where(row == j, rows_im[7 - j], r_im)
            pw_re[...] = r_re
            pw_im[...] = r_im
            for n, d in enumerate((1, 2, 4)):
                sh_re[n] = jnp.where(row < 8 - d, jnp.broadcast_to(rows_re[d - 1], (8, W)), 0.0)
                sh_im[n] = jnp.where(row < 8 - d, jnp.broadcast_to(rows_im[d - 1], (8, W)), 0.0)
            carry_re[...] = jnp.zeros_like(carry_re)
            carry_im[...] = jnp.zeros_like(carry_im)
            acc_ar[...] = jnp.zeros_like(acc_ar)
            acc_ai[...] = jnp.zeros_like(acc_ai)
            dd_ref[...] = jnp.zeros_like(dd_ref)
            dbr_ref[...] = jnp.zeros_like(dbr_ref)
            dbi_ref[...] = jnp.zeros_like(dbi_ref)
            dcr_ref[...] = jnp.zeros_like(dcr_ref)
            dci_ref[...] = jnp.zeros_like(dci_ref)

        uv = u_ref[...]
        dy = dz_ref[...] * _gelu_grad(y_ref[...])
        dyb = dy.astype(BF16)
        lam_re[...] = jnp.dot(dyb, cre_ref[...], preferred_element_type=F32)
        lam_im[...] = -jnp.dot(dyb, cim_ref[...], preferred_element_type=F32)
        keep = jnp.where(first_chunk, 0.0, 1.0)
        xe_re[pl.ds(0, 8), :] = xpr_ref[...] * keep
        xe_im[pl.ds(0, 8), :] = xpi_ref[...] * keep
        xe_re[pl.ds(8, tc), :] = xr_ref[...]
        xe_im[pl.ds(8, tc), :] = xi_ref[...]
        row = lax.broadcasted_iota(jnp.int32, (8, W), 0)

        def tile(n, carry):
            c_re, c_im, s_ar, s_ai = carry
            i = n_tiles - 1 - n
            rows = pl.ds(pl.multiple_of(i * 8, 8), 8)
            r = lam_re[rows, :]
            m = lam_im[rows, :]
            for q, d in enumerate((1, 2, 4)):
                pr, pm = _cmul(sh_re[q], sh_im[q], pltpu.roll(r, 8 - d, 0), pltpu.roll(m, 8 - d, 0))
                r = r + pr
                m = m + pm
            pr, pm = _cmul(pw_re[...], pw_im[...], c_re, c_im)
            r = r + pr
            m = m + pm
            lam_re[rows, :] = r
            lam_im[rows, :] = m
            cur_re = xe_re[pl.ds(pl.multiple_of(i * 8 + 8, 8), 8), :]
            cur_im = xe_im[pl.ds(pl.multiple_of(i * 8 + 8, 8), 8), :]
            bef_re = xe_re[rows, :]
            bef_im = xe_im[rows, :]
            xp_re = jnp.where(row == 0, jnp.broadcast_to(bef_re[7:8, :], (8, W)), pltpu.roll(cur_re, 1, 0))
            xp_im = jnp.where(row == 0, jnp.broadcast_to(bef_im[7:8, :], (8, W)), pltpu.roll(cur_im, 1, 0))
            s_ar = s_ar + r * xp_re + m * xp_im
            s_ai = s_ai + m * xp_re - r * xp_im
            return jnp.broadcast_to(r[0:1, :], (8, W)), jnp.broadcast_to(m[0:1, :], (8, W)), s_ar, s_ai

        c_re, c_im, s_ar, s_ai = lax.fori_loop(
            0, n_tiles, tile, (carry_re[...], carry_im[...], acc_ar[...], acc_ai[...]))
        carry_re[...] = c_re
        carry_im[...] = c_im
        acc_ar[...] = s_ar
        acc_ai[...] = s_ai
        lr = lam_re[...].astype(BF16)
        li = lam_im[...].astype(BF16)
        du_ref[...] = (dy * d_ref[...] + jnp.dot(lr, btr_ref[...], preferred_element_type=F32)
                       + jnp.dot(li, bti_ref[...], preferred_element_type=F32))
        dd_ref[...] += jnp.sum(dy * uv, axis=0, keepdims=True)
        tn_dims = (((0,), (0,)), ((), ()))
        ub = uv.astype(BF16)
        dbr_ref[...] += lax.dot_general(ub, lr, tn_dims, preferred_element_type=F32)
        dbi_ref[...] += lax.dot_general(ub, li, tn_dims, preferred_element_type=F32)
        dcr_ref[...] += lax.dot_general(dyb, xr_ref[...].astype(BF16), tn_dims, preferred_element_type=F32)
        dci_ref[...] -= lax.dot_general(dyb, xi_ref[...].astype(BF16), tn_dims, preferred_element_type=F32)

        @pl.when(first_chunk)
        def _():
            dar_ref[...] = jnp.sum(acc_ar[...], axis=0, keepdims=True)
            dai_ref[...] = jnp.sum(acc_ai[...], axis=0, keepdims=True)

    rev = lambda k: n_chunks - 1 - k
    ch = pl.BlockSpec((tc, CH_TILE), lambda s, k: (rev(k), s))
    st = pl.BlockSpec((tc, W), lambda s, k: (rev(k), s))
    prev = pl.BlockSpec((8, W), lambda s, k: (jnp.maximum(rev(k) * tiles_per_chunk - 1, 0), s))
    mat_cw = pl.BlockSpec((None, CH_TILE, W), lambda s, k: (s, 0, 0))
    mat_wc = pl.BlockSpec((None, W, CH_TILE), lambda s, k: (s, 0, 0))
    vec_w = pl.BlockSpec((1, W), lambda s, k: (0, s))
    vec_c = pl.BlockSpec((1, CH_TILE), lambda s, k: (0, s))
    dense = jax.ShapeDtypeStruct((n_st, CH_TILE, W), F32)
    return _grid_call(
        name, body, (n_st, n_chunks),
        in_specs=[ch, ch, ch, st, st, prev, prev, mat_cw, mat_cw, mat_wc, mat_wc, vec_w, vec_w, vec_c],
        out_specs=[ch, vec_c, mat_cw, mat_cw, mat_cw, mat_cw, vec_w, vec_w],
        out_shape=[jax.ShapeDtypeStruct((T, D), F32), jax.ShapeDtypeStruct((1, D), F32), dense, dense, dense, dense,
                   jax.ShapeDtypeStruct((1, 4 * D), F32), jax.ShapeDtypeStruct((1, 4 * D), F32)],
        scratch_shapes=[pltpu.VMEM((tc, W), F32), pltpu.VMEM((tc, W), F32),
                        pltpu.VMEM((tc + 8, W), F32), pltpu.VMEM((tc + 8, W), F32),
                        pltpu.VMEM((8, W), F32), pltpu.VMEM((8, W), F32),
                        pltpu.VMEM((8, W), F32), pltpu.VMEM((8, W), F32),
                        pltpu.VMEM((3, 8, W), F32), pltpu.VMEM((3, 8, W), F32),
                        pltpu.VMEM((8, W), F32), pltpu.VMEM((8, W), F32)],
        args=(dz, y, u, x_re, x_im, x_re, x_im, c_re, c_im, bt_re, bt_im, abar_re, abar_im, d_skip),
        comm=comm, comm_args=comm_args)


def _sums_matrix(strictly_later, copies):
    jj = lax.broadcasted_iota(jnp.int32, (copies * KEY_BLOCK, 2 * KEY_BLOCK), 0) & (KEY_BLOCK - 1)
    ss = lax.broadcasted_iota(jnp.int32, (copies * KEY_BLOCK, 2 * KEY_BLOCK), 1)
    tri = (jj > ss) if strictly_later else (jj < ss)
    return (tri | (ss >= KEY_BLOCK)).astype(BF16)


def _split_heads(blk):
    first = lax.broadcasted_iota(jnp.int32, blk.shape, 1) < HEAD_DIM
    zero = jnp.zeros_like(blk)
    return jnp.concatenate([jnp.where(first, blk, zero), jnp.where(first, zero, blk)], axis=0)


LOG2_E = 1.4426950408889634


def _sb_scores(z, mask, later):
    z2 = z * LOG2_E
    minus_abs = lax.bitcast_convert_type(lax.bitcast_convert_type(z2, jnp.uint32) | jnp.uint32(0x80000000), F32)
    lb = jnp.minimum(z2, 0.0) - jnp.log2(1.0 + jnp.exp2(minus_abs))
    lm = lb - z2
    if mask is not None:
        lm = jnp.where(mask, lm, 0.0)
    hi = lm.astype(BF16)
    lo = (lm - hi.astype(F32)).astype(BF16)
    return lb, jnp.dot(jnp.concatenate([hi, lo], axis=1), later, preferred_element_type=F32)


def _key_rows(kb):
    return pl.ds(pl.multiple_of(kb * KEY_BLOCK, KEY_BLOCK), KEY_BLOCK)


def _sb_mask(q_row0, k_row0, tq):
    tpos = q_row0 + lax.broadcasted_iota(jnp.int32, (tq, KEY_BLOCK), 0)
    spos = k_row0 + lax.broadcasted_iota(jnp.int32, (tq, KEY_BLOCK), 1)
    return (spos < tpos) & (spos >= META_START)


def _key_block_phases(iq, per_q, block, ascending, per_step):
    first_diag = iq * per_q

    def nth(lo, n, i):
        return lo + i if ascending else lo + n - 1 - i

    def run(lo, n, masked):
        if isinstance(n, int):
            for i in range(n):
                j = i if ascending else n - 1 - i
                block(lo + j, masked, j * KEY_BLOCK)
            return

        def group(i, carry):
            for j in range(per_step):
                block(nth(lo, n, per_step * i + j), masked, 0)
            return carry

        def single(i, carry):
            block(nth(lo, n, n - rest + i), masked, 0)
            return carry

        rest = jnp.bitwise_and(n, per_step - 1)
        lax.fori_loop(0, jnp.right_shift(n, per_step.bit_length() - 1), group, 0)
        lax.fori_loop(0, rest, single, 0)

    phases = [(0, jnp.minimum(iq, 1), True), (1, jnp.maximum(first_diag - 1, 0), False), (first_diag, per_q, True)]
    for lo, n, masked in (phases if ascending else phases[::-1]):
        run(lo, n, masked)


def _attn_fwd(name, q, kv):
    T, D = q.shape
    n_hp = D // 128
    tq = _row_tile(T, 512)
    per_q = tq // KEY_BLOCK
    scale = 1.0 / math.sqrt(HEAD_DIM)

    chains = [(h, r) for h in range(2) for r in range(2)]
    cols = lambda h: slice(h * KEY_BLOCK, (h + 1) * KEY_BLOCK)

    def body(q_ref, k_ref, v_ref, o_ref, l_ref, z_buf, w_buf, acc_ref, run_ref):
        iq = pl.program_id(1)
        n_kb = (iq + 1) * per_q
        qs = q_ref[...] * jnp.asarray(scale, BF16)
        later = _sums_matrix(True, 2)

        def scores(kb):
            return lax.dot_general(qs, _split_heads(k_ref[_key_rows(kb), :]), NT_DIMS, preferred_element_type=F32)

        def weighted_values(kb):
            return jnp.dot(w_buf[...], _split_heads(v_ref[_key_rows(kb), :]), preferred_element_type=F32)

        acc_ref[...] = jnp.zeros_like(acc_ref)
        run_ref[...] = jnp.zeros_like(run_ref)
        w_buf[...] = jnp.zeros_like(w_buf)
        z_buf[...] = scores(n_kb - 1)

        def block(kb, masked, row0):
            acc_ref[...] += weighted_values(jnp.minimum(kb + 1, n_kb - 1))
            z_next = scores(jnp.maximum(kb - 1, 0))
            half = (tq - row0) // 2
            rows = lambda r: slice(row0 + r * half, row0 + (r + 1) * half)
            if row0:
                w_buf[:row0, :] = jnp.zeros((row0, 2 * KEY_BLOCK), BF16)
            masks = [_sb_mask(iq * tq + row0 + r * half, kb * KEY_BLOCK, half) if masked else None for r in range(2)]
            first = [_sb_scores(z_buf[rows(r), cols(h)], masks[r], later) for h, r in chains]
            for (h, r), (lb, sums) in zip(chains, first):
                run = run_ref[h, rows(r), :]
                w = jnp.exp2(lb + sums[:, :KEY_BLOCK] + run)
                if masked:
                    w = jnp.where(masks[r], w, 0.0)
                w_buf[rows(r), cols(h)] = w.astype(BF16)
                run_ref[h, rows(r), :] = run + sums[:, KEY_BLOCK:]
            z_buf[...] = z_next

        _key_block_phases(iq, per_q, block, ascending=False, per_step=4)
        acc_ref[...] += weighted_values(0)
        lane = lax.broadcasted_iota(jnp.int32, (tq, 128), 1)
        o_ref[...] = acc_ref[...].astype(o_ref.dtype)
        l_ref[...] = jnp.where(lane < HEAD_DIM, run_ref[0], run_ref[1])

    blk = pl.BlockSpec((tq, 128), lambda h, i: (i, h))
    return pl.pallas_call(
        body, name=name, grid=(n_hp, T // tq),
        in_specs=[blk, pl.BlockSpec((T, 128), lambda h, i: (0, h)), pl.BlockSpec((T, 128), lambda h, i: (0, n_hp + h))],
        out_specs=[blk, blk],
        out_shape=[jax.ShapeDtypeStruct((T, D), BF16), jax.ShapeDtypeStruct((T, D), F32)],
        scratch_shapes=[pltpu.VMEM((tq, 2 * KEY_BLOCK), F32), pltpu.VMEM((tq, 2 * KEY_BLOCK), BF16),
                        pltpu.VMEM((tq, 128), F32), pltpu.VMEM((2, tq, 128), F32)],
        compiler_params=_cparams("parallel", "arbitrary"),
    )(q, kv, kv)


def _attn_bwd(name, q, kv, do, ltot):
    T, D = q.shape
    n_hp = D // 128
    tq = _row_tile(T, 512)
    n_q = T // tq
    per_q = tq // KEY_BLOCK
    scale = 1.0 / math.sqrt(HEAD_DIM)

    chains = [(h, r) for h in range(2) for r in range(2)]
    head_cols = lambda h: slice(h * KEY_BLOCK, (h + 1) * KEY_BLOCK)

    def body(q_ref, k_ref, v_ref, do_ref, l_ref, dq_ref, dk_ref, dv_ref,
             dk_acc, dv_acc, dq_acc, lpre_ref, cpre_ref, z_buf, dw_buf, dz_buf, w_buf):
        iq = pl.program_id(1)
        n_kb = (iq + 1) * per_q

        @pl.when(iq == 0)
        def _():
            dk_acc[...] = jnp.zeros_like(dk_acc)
            dv_acc[...] = jnp.zeros_like(dv_acc)

        first = lax.broadcasted_iota(jnp.int32, (tq, 128), 1) < HEAD_DIM
        qs = q_ref[...] * jnp.asarray(scale, BF16)
        dov = do_ref[...]
        ltv = l_ref[...]
        swapped = pltpu.roll(ltv, HEAD_DIM, 1)
        ltot = [jnp.where(first, ltv, swapped), jnp.where(first, swapped, ltv)]
        zero = jnp.zeros_like(qs)
        q_stack = jnp.concatenate([jnp.where(first, qs, zero), jnp.where(first, zero, qs)], axis=0)
        do_stack = jnp.concatenate([jnp.where(first, dov, zero), jnp.where(first, zero, dov)], axis=0)
        later = _sums_matrix(True, 2)
        earlier = _sums_matrix(False, 1)

        def scores(kb):
            rows = _key_rows(kb)
            return (lax.dot_general(qs, _split_heads(k_ref[rows, :]), NT_DIMS, preferred_element_type=F32),
                    lax.dot_general(dov, _split_heads(v_ref[rows, :]), NT_DIMS, preferred_element_type=F32))

        def flush(kb):
            rows = _key_rows(kb)
            k_heads = _split_heads(k_ref[rows, :])
            dq_acc[...] += (jnp.dot(dz_buf[:tq, :], k_heads[:KEY_BLOCK, :], preferred_element_type=F32)
                            + jnp.dot(dz_buf[tq:, :], k_heads[KEY_BLOCK:, :], preferred_element_type=F32))
            dk_acc[rows, :] += lax.dot_general(dz_buf[...], q_stack, TN_DIMS, preferred_element_type=F32)
            dv_acc[rows, :] += lax.dot_general(w_buf[...], do_stack, TN_DIMS, preferred_element_type=F32)

        dq_acc[...] = jnp.zeros_like(dq_acc)
        lpre_ref[...] = jnp.zeros_like(lpre_ref)
        cpre_ref[...] = jnp.zeros_like(cpre_ref)
        dz_buf[...] = jnp.zeros_like(dz_buf)
        w_buf[...] = jnp.zeros_like(w_buf)
        z_buf[...], dw_buf[...] = scores(0)

        def block(kb, masked, row0):
            flush(jnp.maximum(kb - 1, 0))
            z_next, dw_next = scores(jnp.minimum(kb + 1, n_kb - 1))
            half = (tq - row0) // 2
            row_half = lambda r: slice(row0 + r * half, row0 + (r + 1) * half)
            stacked = lambda h, r: slice(h * tq + row0 + r * half, h * tq + row0 + (r + 1) * half)
            if row0:
                for h in range(2):
                    w_buf[h * tq:h * tq + row0, :] = jnp.zeros((row0, 128), BF16)
                    dz_buf[h * tq:h * tq + row0, :] = jnp.zeros((row0, 128), BF16)
            masks = [_sb_mask(iq * tq + row0 + r * half, kb * KEY_BLOCK, half) if masked else None for r in range(2)]
            first_stage = [_sb_scores(z_buf[row_half(r), head_cols(h)], masks[r], later) for h, r in chains]
            second_stage = []
            for (h, r), (lb, sums) in zip(chains, first_stage):
                after = ltot[h][row_half(r), :] - lpre_ref[h, row_half(r), :] - sums[:, KEY_BLOCK:]
                w = jnp.exp2(lb + sums[:, :KEY_BLOCK] + after)
                if masked:
                    w = jnp.where(masks[r], w, 0.0)
                da = w * dw_buf[row_half(r), head_cols(h)]
                w_buf[stacked(h, r), :] = w.astype(BF16)
                lpre_ref[h, row_half(r), :] += sums[:, KEY_BLOCK:]
                second_stage.append((da, jnp.dot(da.astype(BF16), earlier, preferred_element_type=F32)))
            for (h, r), (lb, _), (da, dsums) in zip(chains, first_stage, second_stage):
                sig = jnp.exp2(lb)
                through_later = sig * (dsums[:, :KEY_BLOCK] + cpre_ref[h, row_half(r), :])
                if masked:
                    through_later = jnp.where(masks[r], through_later, 0.0)
                dz_buf[stacked(h, r), :] = (da * (1.0 - sig) - through_later).astype(BF16)
                cpre_ref[h, row_half(r), :] += dsums[:, KEY_BLOCK:]
            z_buf[...] = z_next
            dw_buf[...] = dw_next

        _key_block_phases(iq, per_q, block, ascending=True, per_step=4)
        flush(n_kb - 1)
        dq_ref[...] = (dq_acc[...] * scale).astype(dq_ref.dtype)

        @pl.when(iq == n_q - 1)
        def _():
            dk_ref[...] = dk_acc[...].astype(dk_ref.dtype)
            dv_ref[...] = dv_acc[...].astype(dv_ref.dtype)

    blk = pl.BlockSpec((tq, 128), lambda h, i: (i, h))
    full = pl.BlockSpec((T, 128), lambda h, i: (0, h))
    return pl.pallas_call(
        body, name=name, grid=(n_hp, n_q),
        in_specs=[blk, full, pl.BlockSpec((T, 128), lambda h, i: (0, n_hp + h)), blk, blk],
        out_specs=[blk, full, full],
        out_shape=[jax.ShapeDtypeStruct((T, D), BF16)] * 3,
        scratch_shapes=[pltpu.VMEM((T, 128), F32), pltpu.VMEM((T, 128), F32), pltpu.VMEM((tq, 128), F32),
                        pltpu.VMEM((2, tq, 128), F32), pltpu.VMEM((2, tq, 128), F32),
                        pltpu.VMEM((tq, 2 * KEY_BLOCK), F32), pltpu.VMEM((tq, 2 * KEY_BLOCK), F32),
                        pltpu.VMEM((2 * tq, 128), BF16), pltpu.VMEM((2 * tq, 128), BF16)],
        compiler_params=_cparams("parallel", "arbitrary"),
    )(q, kv, kv, do, ltot)


def _adamw(name, w, g, m, v):
    shape = w.shape
    size = w.size
    if w.ndim >= 2 and shape[-1] % 128 == 0:
        cols = shape[-1]
    else:
        cols = 1024 if size % 1024 == 0 else shape[-1]
    rows = size // cols
    tm = _row_tile(rows, max(8, (1024 * 1024) // (4 * cols) // 8 * 8)) if rows % 8 == 0 else rows
    c1 = 1.0 / (1.0 - ADAM_B1 ** ADAM_STEP)
    c2 = 1.0 / (1.0 - ADAM_B2 ** ADAM_STEP)

    def body(w_ref, g_ref, m_ref, v_ref, d_ref, nm_ref, nv_ref):
        gv = g_ref[...]
        nm = ADAM_B1 * m_ref[...] + (1.0 - ADAM_B1) * gv
        nv = ADAM_B2 * v_ref[...] + (1.0 - ADAM_B2) * (gv * gv)
        d_ref[...] = -ADAM_LR * ((nm * c1) / (jnp.sqrt(nv * c2) + ADAM_EPS) + ADAM_WD * w_ref[...])
        nm_ref[...] = nm
        nv_ref[...] = nv

    blk = pl.BlockSpec((tm, cols), lambda i: (i, 0))
    outs = pl.pallas_call(
        body, name=name, grid=(rows // tm,),
        in_specs=[blk] * 4, out_specs=[blk] * 3,
        out_shape=[jax.ShapeDtypeStruct((rows, cols), F32)] * 3,
        compiler_params=_cparams("parallel"),
    )(*[t.reshape(rows, cols) for t in (w, g, m, v)])
    return tuple(o.reshape(shape) for o in outs)


def _any_specs(n):
    return [pl.BlockSpec(memory_space=pl.ANY)] * n


def _chip_index():
    return 2 * lax.axis_index("x") + lax.axis_index("y")


def _place():
    x, y, c = lax.axis_index("x"), lax.axis_index("y"), lax.axis_index("c")
    chips = [(1 - x, y), (x, 1 - y), (1 - x, 1 - y)]
    return x, y, c, chips


def _all_gather_chips(name, shards):
    n = len(shards)

    def body(*refs):
        x_refs, o_refs = refs[:n], refs[n:2 * n]
        send_sems, recv_sems = refs[2 * n:]
        x, y, c, chips = _place()
        me = 2 * x + y
        sibling = (x, y, 1 - c)

        def half(ref, i, which):
            h = shards[i].shape[0] // 2
            return ref.at[pl.ds(which * h, h)]

        def remote(k, i, src, dst, to):
            return pltpu.make_async_remote_copy(src_ref=src, dst_ref=dst, send_sem=send_sems.at[k, i],
                                                recv_sem=recv_sems.at[k, i], device_id=to, device_id_type=MESH)

        sent = []
        for j, chip in enumerate(chips):
            for i in range(n):
                cp = remote(j, i, half(x_refs[i], i, c), half(o_refs[i].at[me], i, c), (*chip, c))
                cp.start()
                sent.append(cp)
        for j, chip in enumerate(chips):
            pj = 2 * chip[0] + chip[1]
            for i in range(n):
                landed = half(o_refs[i].at[pj], i, c)
                remote(j, i, landed, landed, (*chip, c)).wait_recv()
                cp = remote(3 + j, i, landed, landed, sibling)
                cp.start()
                sent.append(cp)
        for j, chip in enumerate(chips):
            pj = 2 * chip[0] + chip[1]
            for i in range(n):
                got = half(o_refs[i].at[pj], i, 1 - c)
                remote(3 + j, i, got, got, sibling).wait_recv()
        for cp in sent:
            cp.wait_send()

    outs = pl.pallas_call(
        body, name=name,
        in_specs=_any_specs(n), out_specs=_any_specs(n),
        out_shape=[jax.ShapeDtypeStruct((4,) + s.shape, s.dtype) for s in shards],
        scratch_shapes=[pltpu.SemaphoreType.DMA((6, n)), pltpu.SemaphoreType.DMA((6, n))],
    )(*shards)
    return [lax.dynamic_update_slice(o, s[None], (_chip_index(), 0, 0)) for o, s in zip(outs, shards)]


def _pair_split(name, grads):
    n = len(grads)

    def body(*refs):
        g_refs, got_refs = refs[:n], refs[n:2 * n]
        send_sems, recv_sems = refs[2 * n:]
        x, y, c, _ = _place()
        sibling = (x, y, 1 - c)
        sent = []
        for i in range(n):
            h = grads[i].shape[1] // 2
            rc = pltpu.make_async_remote_copy(
                src_ref=g_refs[i].at[:, pl.ds((1 - c) * h, h)], dst_ref=got_refs[i],
                send_sem=send_sems.at[i], recv_sem=recv_sems.at[i], device_id=sibling, device_id_type=MESH)
            rc.start()
            sent.append(rc)
        for rc in sent:
            rc.wait()

    return pl.pallas_call(
        body, name=name,
        in_specs=_any_specs(n), out_specs=_any_specs(n),
        out_shape=[jax.ShapeDtypeStruct((4, g.shape[1] // 2, g.shape[2]), g.dtype) for g in grads],
        scratch_shapes=[pltpu.SemaphoreType.DMA((n,)), pltpu.SemaphoreType.DMA((n,))],
    )(*grads)


def _chip_exchange(name, sums):
    return _exchange_finish(sums, _run_comm(name, _exchange_comm(sums), sums))


class _Comm:
    def __init__(self, out_shape, copies):
        self.n = len(out_shape)
        self.out_shape = out_shape
        self.copies = copies
        self.scratch = [pltpu.SemaphoreType.DMA((3, self.n)), pltpu.SemaphoreType.DMA((3, self.n))]

    def start(self, *refs):
        for cp in self.copies(*refs, False):
            cp.start()

    def finish(self, *refs):
        for cp in self.copies(*refs, True):
            cp.wait_recv()
        for cp in self.copies(*refs, False):
            cp.wait_send()


def _exchange_comm(sums):
    n = len(sums)

    def copies(s_refs, o_refs, send_sems, recv_sems, mirrors):
        x, y, c, chips = _place()
        me = 2 * x + y
        out = []
        for j, chip in enumerate(chips):
            pj = 2 * chip[0] + chip[1]
            for i in range(n):
                out.append(pltpu.make_async_remote_copy(
                    src_ref=s_refs[i].at[pj], dst_ref=o_refs[i].at[pj if mirrors else me], send_sem=send_sems.at[j, i],
                    recv_sem=recv_sems.at[j, i], device_id=(*chip, c), device_id_type=MESH))
        return out

    return _Comm([jax.ShapeDtypeStruct(s.shape, s.dtype) for s in sums], copies)


def _pair_split_comm(grads):
    n = len(grads)

    def copies(g_refs, o_refs, send_sems, recv_sems, mirrors):
        x, y, c, _ = _place()
        out = []
        for i in range(n):
            h = grads[i].shape[1] // 2
            out.append(pltpu.make_async_remote_copy(
                src_ref=g_refs[i].at[:, pl.ds((1 - c) * h, h)], dst_ref=o_refs[i], send_sem=send_sems.at[0, i],
                recv_sem=recv_sems.at[0, i], device_id=(x, y, 1 - c), device_id_type=MESH))
        return out

    return _Comm([jax.ShapeDtypeStruct((4, g.shape[1] // 2, g.shape[2]), g.dtype) for g in grads], copies)


def _exchange_finish(sums, outs):
    me = _chip_index()
    return [lax.dynamic_update_slice(o, lax.dynamic_index_in_dim(s, me, 0, keepdims=True), (me, 0, 0))
            for o, s in zip(outs, sums)]


def _halves_gather_comm(shards):
    n = len(shards)

    def copies(x_refs, o_refs, send_sems, recv_sems, mirrors):
        x, y, c, chips = _place()
        me = 2 * x + y
        out = []
        for j, chip in enumerate(chips):
            pj = 2 * chip[0] + chip[1]
            for i in range(n):
                h = shards[i].shape[0] // 2
                rows = pl.ds(c * h, h)
                out.append(pltpu.make_async_remote_copy(
                    src_ref=x_refs[i].at[rows], dst_ref=o_refs[i].at[pj if mirrors else me, rows],
                    send_sem=send_sems.at[j, i], recv_sem=recv_sems.at[j, i], device_id=(*chip, c), device_id_type=MESH))
        return out

    return _Comm([jax.ShapeDtypeStruct((4,) + s.shape, s.dtype) for s in shards], copies)


def _run_comm(name, comm, arrays):
    n = comm.n

    def body(*refs):
        comm.start(refs[:n], refs[n:2 * n], *refs[2 * n:])
        comm.finish(refs[:n], refs[n:2 * n], *refs[2 * n:])

    return pl.pallas_call(
        body, name=name, in_specs=_any_specs(n), out_specs=_any_specs(n),
        out_shape=comm.out_shape, scratch_shapes=comm.scratch,
    )(*arrays)


def _embed_comm(comm, body, grid, n_in, n_out):
    n = comm.n

    def wrapped(*refs):
        ins, c_in = refs[:n_in], refs[n_in:n_in + n]
        outs, c_out = refs[n_in + n:n_in + n + n_out], refs[n_in + n + n_out:n_in + 2 * n + n_out]
        scratch, sems = refs[n_in + 2 * n + n_out:-2], refs[-2:]
        ids = [pl.program_id(a) for a in range(len(grid))]
        first = functools.reduce(jnp.logical_and, [i == 0 for i in ids])
        last = functools.reduce(jnp.logical_and, [i == g - 1 for i, g in zip(ids, grid)])

        @pl.when(first)
        def _():
            comm.start(c_in, c_out, *sems)

        body(*ins, *outs, *scratch)

        @pl.when(last)
        def _():
            comm.finish(c_in, c_out, *sems)

    return wrapped


def _pair_forward(name, landed, shards):
    n = len(landed)

    def body(*refs):
        o_refs = refs[n:2 * n]
        send_sems, recv_sems = refs[2 * n:]
        x, y, c, chips = _place()
        sibling = (x, y, 1 - c)
        sent, arriving = [], []
        for j, chip in enumerate(chips):
            pj = 2 * chip[0] + chip[1]
            for i in range(n):
                h = shards[i].shape[0] // 2
                mine = o_refs[i].at[pj, pl.ds(c * h, h)]
                theirs = o_refs[i].at[pj, pl.ds((1 - c) * h, h)]
                for ref, group in ((mine, sent), (theirs, arriving)):
                    group.append(pltpu.make_async_remote_copy(
                        src_ref=ref, dst_ref=ref, send_sem=send_sems.at[j, i], recv_sem=recv_sems.at[j, i],
                        device_id=sibling, device_id_type=MESH))
        for cp in sent:
            cp.start()
        for cp in arriving:
            cp.wait_recv()
        for cp in sent:
            cp.wait_send()

    outs = pl.pallas_call(
        body, name=name, in_specs=_any_specs(n), out_specs=_any_specs(n),
        out_shape=[jax.ShapeDtypeStruct(a.shape, a.dtype) for a in landed],
        input_output_aliases={i: i for i in range(n)},
        scratch_shapes=[pltpu.SemaphoreType.DMA((3, n)), pltpu.SemaphoreType.DMA((3, n))],
    )(*landed)
    return [lax.dynamic_update_slice(o, s[None], (_chip_index(), 0, 0)) for o, s in zip(outs, shards)]


def _pair_join(name, halves):
    n = len(halves)

    def body(*refs):
        h_refs, o_refs = refs[:n], refs[n:2 * n]
        send_sems, recv_sems = refs[2 * n:]
        x, y, c, _ = _place()
        sibling = (x, y, 1 - c)
        sent = []
        for i in range(n):
            h = halves[i].shape[0]
            mine = o_refs[i].at[pl.ds(c * h, h)]
            rc = pltpu.make_async_remote_copy(
                src_ref=h_refs[i], dst_ref=mine, send_sem=send_sems.at[i], recv_sem=recv_sems.at[i],
                device_id=sibling, device_id_type=MESH)
            rc.start()
            sent.append(rc)
        for i in range(n):
            h = halves[i].shape[0]
            theirs = o_refs[i].at[pl.ds((1 - c) * h, h)]
            pltpu.make_async_remote_copy(
                src_ref=h_refs[i], dst_ref=theirs, send_sem=send_sems.at[i], recv_sem=recv_sems.at[i],
                device_id=sibling, device_id_type=MESH).wait_recv()
        for rc in sent:
            rc.wait_send()

    outs = pl.pallas_call(
        body, name=name,
        in_specs=_any_specs(n), out_specs=_any_specs(n),
        out_shape=[jax.ShapeDtypeStruct((2 * s.shape[0], s.shape[1]), s.dtype) for s in halves],
        scratch_shapes=[pltpu.SemaphoreType.DMA((n,)), pltpu.SemaphoreType.DMA((n,))],
    )(*halves)
    c = lax.axis_index("c")
    return [lax.dynamic_update_slice(o, s, (c * s.shape[0], 0)) for o, s in zip(outs, halves)]


def _add_pair(name, a, b, c):
    _, H, C = b.shape
    th = _row_tile(H, max(8, (1024 * 1024) // (4 * C) // 8 * 8))
    per_half = H // th

    def body(c_ref, a_ref, b_ref, o_ref):
        o_ref[...] = (a_ref[...].astype(F32) + b_ref[...].astype(F32)).astype(o_ref.dtype)

    blk = pl.BlockSpec((None, th, C), lambda q, i, c_ref: (q, i, 0))
    return pl.pallas_call(
        body, name=name,
        grid_spec=pltpu.PrefetchScalarGridSpec(
            num_scalar_prefetch=1, grid=(4, per_half),
            in_specs=[pl.BlockSpec((None, th, C), lambda q, i, c_ref: (q, c_ref[0] * per_half + i, 0)), blk],
            out_specs=blk),
        out_shape=jax.ShapeDtypeStruct(b.shape, a.dtype), compiler_params=_cparams("parallel", "parallel"),
    )(c, a, b)


def _add_chips(name, parts):
    _, H, C = parts.shape
    th = _row_tile(H, max(8, (1024 * 1024) // (4 * C) // 8 * 8))

    def body(p_ref, o_ref):
        acc = p_ref[0].astype(F32)
        for q in range(1, 4):
            acc = acc + p_ref[q].astype(F32)
        o_ref[...] = acc

    return pl.pallas_call(
        body, name=name, grid=(H // th,),
        in_specs=[pl.BlockSpec((4, th, C), lambda i: (0, i, 0))], out_specs=pl.BlockSpec((th, C), lambda i: (i, 0)),
        out_shape=jax.ShapeDtypeStruct((H, C), F32), compiler_params=_cparams("parallel"),
    )(parts)


def _pair_sums(tag, grads, got=None):
    c = lax.axis_index("c").astype(jnp.int32).reshape(1)
    if got is None:
        got = _pair_split(f"rs_pair_split_{tag}", grads)
    return [_add_pair(f"rs_add_pair_{tag}_{i}", g, h, c) for i, (g, h) in enumerate(zip(grads, got))]


def _chip_sums(parts):
    return _pair_join("rs_pair_join", [_add_chips(f"rs_add_chips_{i}", p) for i, p in enumerate(parts)])


def _block_diag(t):
    G, A, B = t.shape
    eye = jnp.eye(8, dtype=t.dtype)
    return jnp.einsum("sgab,gh->sgahb", t.reshape(G // 8, 8, A, B), eye).reshape(G // 8, 8 * A, 8 * B)


def _block_diag_extract(m, A, B):
    n = m.shape[0]
    eye = jnp.eye(8, dtype=m.dtype)
    return jnp.einsum("sgahb,gh->sgab", m.reshape(n, 8, A, 8, B), eye).reshape(8 * n, A, B)


def kernel(x, meta_tokens, norm_mix, norm_ffn, s5_a_re, s5_a_im, s5_log_dt, s5_b_re, s5_b_im, s5_c_re, s5_c_im, s5_d, s5_w_glu, norm_kv, w_kv, w_q, w_o, w_ffn_in, w_ffn_out, norm_final, loss_target, m_meta_tokens, m_norm_mix, m_norm_ffn, m_s5_a_re, m_s5_a_im, m_s5_log_dt, m_s5_b_re, m_s5_b_im, m_s5_c_re, m_s5_c_im, m_s5_d, m_s5_w_glu, m_norm_kv, m_w_kv, m_w_q, m_w_o, m_w_ffn_in, m_w_ffn_out, m_norm_final, v_meta_tokens, v_norm_mix, v_norm_ffn, v_s5_a_re, v_s5_a_im, v_s5_log_dt, v_s5_b_re, v_s5_b_im, v_s5_c_re, v_s5_c_im, v_s5_d, v_s5_w_glu, v_norm_kv, v_w_kv, v_w_q, v_w_o, v_w_ffn_in, v_w_ffn_out, v_norm_final):
    seq, D = x.shape[1], x.shape[2]
    T = X_START + seq
    G, P, C = s5_a_re.shape[1], S5_STATE, S5_GROUP
    d_ff = w_ffn_out.shape[1] * 4
    dq4 = D // 4
    chip = 2 * lax.axis_index("x") + lax.axis_index("y")

    small_in = jnp.concatenate([meta_tokens, jnp.pad(s5_d, ((0, 15), (0, 0)))], axis=0)
    (small_all,) = _all_gather_chips("ag_small", [small_in])
    meta_full = small_all[:, :N_META, :].transpose(1, 0, 2).reshape(N_META, D)
    d_skip = small_all[:, N_META, :].reshape(1, D)
    shards = [s.astype(BF16) for s in (s5_w_glu[0], w_kv, w_q[0], w_o[0],
                                       w_ffn_in.reshape(2 * D, -1), w_ffn_out.reshape(-1, D))]
    rows_out = d_ff // 4

    row = lambda v: v.reshape(1, -1)
    g_mix0, g_mix1 = row(norm_mix[0]), row(norm_mix[1])
    g_ffn = [row(norm_ffn[0]), row(norm_ffn[1])]
    g_kv, g_final = row(norm_kv), row(norm_final)

    a_re3 = s5_a_re[0].reshape(G, 1, P)
    a_im3 = s5_a_im[0].reshape(G, 1, P)
    log_dt3 = s5_log_dt[0].reshape(G, 1, 1)
    bt_re = s5_b_re[0].transpose(0, 2, 1)
    bt_im = s5_b_im[0].transpose(0, 2, 1)
    ab_re, ab_im, bb_re, bb_im = _s5_prep_fwd("s5_prep", a_re3, a_im3, log_dt3, bt_re, bt_im)
    abar_re, abar_im = ab_re.reshape(1, G * P), ab_im.reshape(1, G * P)
    bd_b_re = _block_diag(bb_re).astype(BF16)
    bd_b_im = _block_diag(bb_im).astype(BF16)
    bd_bt_re = bd_b_re.transpose(0, 2, 1)
    bd_bt_im = bd_b_im.transpose(0, 2, 1)
    bd_c_re = _block_diag(s5_c_re[0]).astype(BF16)
    bd_c_im = _block_diag(s5_c_im[0]).astype(BF16)
    bd_ct_re = bd_c_re.transpose(0, 2, 1)
    bd_ct_im = bd_c_im.transpose(0, 2, 1)

    h0 = jnp.concatenate([jnp.zeros((META_START, D), F32), meta_full, x[0]], axis=0)
    (u,) = _rmsnorm_fwd("norm_mix0", h0, [g_mix0], [F32])
    (y, z, x_re, x_im), landed = _s5_fwd(
        "s5_scan", u, bd_b_re, bd_b_im, bd_ct_re, bd_ct_im, abar_re, abar_im, d_skip,
        comm=_halves_gather_comm(shards), comm_args=shards)
    wg_glu, wg_kv, wg_q, wg_o, wg_in, wg_out = _pair_forward("ag_forward", landed, shards)
    wg_q = wg_q.reshape(1, D, D)
    wg_o = wg_o.reshape(1, D, D)
    wg_out = [wg_out[:, l * rows_out:(l + 1) * rows_out, :].reshape(1, d_ff, D) for l in range(2)]
    val, gate, h1 = _mm_glu("glu_proj", z, wg_glu, h0)

    def ffn_fwd(l, h):
        (n,) = _rmsnorm_fwd(f"norm_ffn{l}", h, [g_ffn[l]], [BF16])
        g, u, mid = _mm_swiglu(f"ffn_in{l}", n, wg_in, l)
        return n, (g, u), mid, _mm_nn(f"ffn_out{l}", mid, wg_out[l], res=h)

    n1, gu0, mid0, h2 = ffn_fwd(0, h1)
    nk, nq = _rmsnorm_fwd("norm_kv_q", h2, [g_kv, g_mix1], [BF16, BF16])
    kv = _mm_nn("kv_proj", nk, wg_kv, out_dtype=BF16)
    q = _mm_nn("q_proj", nq, wg_q, out_dtype=BF16)
    o, ltot = _attn_fwd("attn_fwd", q, kv)
    h3 = _mm_nn("o_proj", o, wg_o, res=h2)
    n3, gu1, mid1, h4 = ffn_fwd(1, h3)
    loss_part, dh4, dg_final = _final_loss("final_loss", h4, g_final, loss_target[0])

    def ffn_bwd(l, dh, h, n, gu, mid):
        dgu = _mm_nt_swiglu_bwd(f"ffn_out{l}_dx", dh, wg_out[l], *gu)
        dw_out = _mm_tn(f"ffn_out{l}_dw", mid, dh, 1)
        dw_in = _mm_tn(f"ffn_in{l}_dw", n, dgu, 4)
        dn = _mm_nt_k(f"ffn_in{l}_dx", dgu, wg_in, D, k_blk=l)
        dh_prev, (dg,) = _rmsnorm_bwd(f"norm_ffn{l}_bwd", h, [(g_ffn[l], dn)], dh)
        return dh_prev, dg, dw_in, dw_out

    dh3, dg_ffn1, dw_in1, dw_out1 = ffn_bwd(1, dh4, h3, n3, gu1, mid1)
    d_o = _mm_nt_k("o_proj_dx", dh3, wg_o, D, out_dtype=BF16)
    dw_o = _mm_tn("o_proj_dw", o, dh3, 1)
    dq, dk, dv = _attn_bwd("attn_bwd", q, kv, d_o, ltot)
    dkv = jnp.concatenate([dk, dv], axis=1)
    dw_q = _mm_tn("q_proj_dw", nq, dq, 1)
    dnq = _mm_nt_k("q_proj_dx", dq, wg_q, D)
    dw_kv = _mm_tn("kv_proj_dw", nk, dkv, 4)
    dnk = _mm_nt_k("kv_proj_dx", dkv, wg_kv, D)
    dh2, (dg_mix1, dg_kv) = _rmsnorm_bwd("norm_kv_q_bwd", h2, [(g_mix1, dnq), (g_kv, dnk)], dh3)
    dh1, dg_ffn0, dw_in0, dw_out0 = ffn_bwd(0, dh2, h1, n1, gu0, mid0)
    dvg = _glu_bwd("glu_bwd", val, gate, dh1)
    dw_glu = _mm_tn("glu_proj_dw", z, dvg, 4)
    big = [dw_kv, dw_q.reshape(4, D // 4, D), dw_o.reshape(4, D // 4, D), dw_in1, dw_out1.reshape(4, rows_out, D),
           dw_glu, dw_in0, dw_out0.reshape(4, rows_out, D)]
    dz, got_big = _mm_nt_k("glu_proj_dx", dvg, wg_glu, D, comm=_pair_split_comm(big), comm_args=big)
    big_pairs = _pair_sums("big", big, got_big)
    (du, dd, dbd_b_re, dbd_b_im, dbd_c_re, dbd_c_im, dab_re, dab_im), big_parts = _s5_bwd(
        "s5_scan_bwd", dz, y, u, x_re, x_im, bd_c_re, bd_c_im, bd_bt_re, bd_bt_im, abar_re, abar_im, d_skip,
        comm=_exchange_comm(big_pairs), comm_args=big_pairs)
    big_parts = _exchange_finish(big_pairs, big_parts)
    dh0, (dg_mix0,) = _rmsnorm_bwd("norm_mix0_bwd", h0, [(g_mix0, du)], dh1)
    da_re, da_im, dlog_dt, dbt_re, dbt_im = _s5_prep_bwd(
        "s5_prep_bwd", a_re3, a_im3, log_dt3, bt_re, bt_im,
        dab_re.reshape(G, 1, P), dab_im.reshape(G, 1, P),
        _block_diag_extract(dbd_b_re, C, P), _block_diag_extract(dbd_b_im, C, P))
    grad_x = dh0[X_START:][None]

    small_parts = [
        dg_mix0, dg_mix1, dg_ffn0, dg_ffn1, da_re, da_im,
        dbt_re.transpose(0, 2, 1), dbt_im.transpose(0, 2, 1),
        _block_diag_extract(dbd_c_re, C, P), _block_diag_extract(dbd_c_im, C, P),
        dg_kv, dg_final, dh0[META_START:X_START], dd, loss_part, dlog_dt]
    small_sizes = [p.size for p in small_parts]
    unit = 4 * 2 * 8 * 128
    padded = -(-sum(small_sizes) // unit) * unit
    tail = jnp.concatenate([loss_part.reshape(-1), dlog_dt.reshape(-1)])
    small_flat = jnp.concatenate(
        [p.reshape(-1) for p in small_parts[:-2]] + [jnp.pad(tail, (0, padded - sum(small_sizes)))])
    small_blocks = small_flat.reshape(4, padded // (4 * 128), 128)
    small_exchanged = _chip_exchange("rs_chip_exchange", _pair_sums("small", [small_blocks]))
    gw_kv, gw_q, gw_o, gw_in1, gw_out1, gw_glu, gw_in0, gw_out0, small_mine = _chip_sums(big_parts + small_exchanged)
    gw_in = jnp.concatenate([gw_in0, gw_in1], axis=0)
    gw_out = jnp.concatenate([gw_out0, gw_out1], axis=0)
    (small_red,) = _all_gather_chips("ag_small_grads", [small_mine])
    small_red = small_red.reshape(-1)
    pieces, at = [], 0
    for p, size in zip(small_parts, small_sizes):
        pieces.append(small_red[at:at + size].reshape(p.shape))
        at += size
    (gn_mix0, gn_mix1, gn_ffn0, gn_ffn1, ga_re, ga_im, gb_re, gb_im, gc_re, gc_im, gn_kv, gn_final,
     gmeta_full, gd_full, loss_all, glog_dt) = pieces
    loss = loss_all[0, 0]
    gn_mix = small_red[:2 * D].reshape(2, D)
    gn_ffn = small_red[2 * D:4 * D].reshape(2, D)
    gmeta = lax.dynamic_slice_in_dim(gmeta_full, chip * dq4, dq4, axis=1)
    gd = lax.dynamic_slice_in_dim(gd_full, chip * dq4, dq4, axis=1)

    grads = {
        "meta_tokens": gmeta, "norm_mix": gn_mix, "norm_ffn": gn_ffn,
        "s5_a_re": ga_re.reshape(s5_a_re.shape), "s5_a_im": ga_im.reshape(s5_a_im.shape),
        "s5_log_dt": glog_dt.reshape(s5_log_dt.shape),
        "s5_b_re": gb_re.reshape(s5_b_re.shape), "s5_b_im": gb_im.reshape(s5_b_im.shape),
        "s5_c_re": gc_re.reshape(s5_c_re.shape), "s5_c_im": gc_im.reshape(s5_c_im.shape),
        "s5_d": gd, "s5_w_glu": gw_glu.reshape(s5_w_glu.shape), "norm_kv": gn_kv.reshape(norm_kv.shape),
        "w_kv": gw_kv, "w_q": gw_q.reshape(w_q.shape), "w_o": gw_o.reshape(w_o.shape),
        "w_ffn_in": gw_in.reshape(w_ffn_in.shape), "w_ffn_out": gw_out.reshape(w_ffn_out.shape),
        "norm_final": gn_final.reshape(norm_final.shape),
    }
    weights = {
        "meta_tokens": (meta_tokens, m_meta_tokens, v_meta_tokens), "norm_mix": (norm_mix, m_norm_mix, v_norm_mix),
        "norm_ffn": (norm_ffn, m_norm_ffn, v_norm_ffn), "s5_a_re": (s5_a_re, m_s5_a_re, v_s5_a_re),
        "s5_a_im": (s5_a_im, m_s5_a_im, v_s5_a_im), "s5_log_dt": (s5_log_dt, m_s5_log_dt, v_s5_log_dt),
        "s5_b_re": (s5_b_re, m_s5_b_re, v_s5_b_re), "s5_b_im": (s5_b_im, m_s5_b_im, v_s5_b_im),
        "s5_c_re": (s5_c_re, m_s5_c_re, v_s5_c_re), "s5_c_im": (s5_c_im, m_s5_c_im, v_s5_c_im),
        "s5_d": (s5_d, m_s5_d, v_s5_d), "s5_w_glu": (s5_w_glu, m_s5_w_glu, v_s5_w_glu),
        "norm_kv": (norm_kv, m_norm_kv, v_norm_kv), "w_kv": (w_kv, m_w_kv, v_w_kv), "w_q": (w_q, m_w_q, v_w_q),
        "w_o": (w_o, m_w_o, v_w_o), "w_ffn_in": (w_ffn_in, m_w_ffn_in, v_w_ffn_in),
        "w_ffn_out": (w_ffn_out, m_w_ffn_out, v_w_ffn_out), "norm_final": (norm_final, m_norm_final, v_norm_final),
    }
    names = list(weights)
    deltas, new_m, new_v = [], [], []
    for name in names:
        w, m, v = weights[name]
        d, nm, nv = _adamw(f"adamw_{name}", w, grads[name], m, v)
        deltas.append(d)
        new_m.append(nm)
        new_v.append(nv)
    return (loss, grad_x, *[grads[n] for n in names], *deltas, *new_m, *new_v)
```
